```python
import jax, jax.numpy as jnp
from jax import lax
import numpy as np

D_MODEL = 1024
BATCH = 4
SEQ = 4096
DEPTH = 1
DEC_BATCH = 16
DEC_SEQ = 32
PAST_LEN = 4096

CHUNK = 64
MIX_WIDTH = D_MODEL
GM_WIDTH = MIX_WIDTH // 2
GM_GROUPS = 4
GM_GROUP_DIM = GM_WIDTH // GM_GROUPS
GM_CHUNK = 128
SB_WIDTH = MIX_WIDTH - GM_WIDTH
SB_HEAD_DIM = 64
SB_HEADS = SB_WIDTH // SB_HEAD_DIM
SB_BLOCK = 128
IN_WIDTH = 2 * GM_WIDTH + 3 * SB_WIDTH
N_EXPERT_GROUPS = 4
EXPERTS_PER_GROUP = 4
N_EXPERTS = N_EXPERT_GROUPS * EXPERTS_PER_GROUP
TOP_K = 2
EXPERT_FF = D_MODEL // 4
EPS = 1e-6

kernel_name = "hybrid_gmlp_stickbreaking_hmoe_stream_step"


def rmsnorm(x, g):
    xf = x.astype(jnp.float32)
    y = xf * lax.rsqrt(jnp.mean(xf * xf, axis=-1, keepdims=True) + EPS)
    return (y * g.astype(jnp.float32)).astype(x.dtype)


def mixer_inputs(xn, w_in, gm_v_norm_g, q_norm_g, k_norm_g):
    b, t, _ = xn.shape
    proj = xn @ w_in
    gm_u, gm_v, q, k, v = jnp.split(
        proj, [GM_WIDTH, 2 * GM_WIDTH, 2 * GM_WIDTH + SB_WIDTH, 2 * GM_WIDTH + 2 * SB_WIDTH], axis=-1)
    gm_u = jax.nn.gelu(gm_u).reshape(b, t, GM_GROUPS, GM_GROUP_DIM)
    gm_v = rmsnorm(jax.nn.gelu(gm_v), gm_v_norm_g).reshape(b, t, GM_GROUPS, GM_GROUP_DIM)
    q = rmsnorm(q.reshape(b, t, SB_HEADS, SB_HEAD_DIM), q_norm_g)
    k = rmsnorm(k.reshape(b, t, SB_HEADS, SB_HEAD_DIM), k_norm_g)
    v = v.reshape(b, t, SB_HEADS, SB_HEAD_DIM)
    return gm_u, gm_v, q, k, v


def gmlp_chunk(u, v, w_s, b_s):
    tc = v.shape[-3]
    mask = jnp.tril(jnp.ones((GM_CHUNK, GM_CHUNK), dtype=bool))
    w = jnp.where(mask, w_s, 0)[:, :tc, :tc]
    mixed = jnp.einsum('gts,...sgd->...tgd', w, v) + b_s[:, :tc].T[:, :, None]
    return u * mixed


def stick_breaking(q, k, v, q_pos, k_pos):
    z = jnp.einsum('bqhd,bshd->bhqs', q, k).astype(jnp.float32) * (SB_HEAD_DIM ** -0.5)
    causal = k_pos[None, :] < q_pos[:, None]
    log_beta = jax.nn.log_sigmoid(z)
    log_keep = jnp.where(causal, log_beta - z, 0.0)
    rest = lax.cumsum(log_keep, axis=3, reverse=True) - log_keep
    w = jnp.where(causal, jnp.exp(log_beta + rest), 0.0)
    return jnp.einsum('bhqs,bshd->bqhd', w.astype(v.dtype), v)


def hier_moe(x, w_router_group, b_router_group, w_router_expert, b_router_expert,
             w_gate, w_up, w_down):
    shp = x.shape
    xf = x.reshape(-1, shp[-1])
    lg = (xf @ w_router_group).astype(jnp.float32) + b_router_group.astype(jnp.float32)
    p_g = jax.nn.softmax(lg, axis=-1)
    g_sel = jnp.argmax(lg, axis=-1)
    p_sel = jnp.take_along_axis(p_g, g_sel[:, None], axis=-1)[:, 0]
    le = jnp.einsum('nd,gde->nge', xf, w_router_expert).astype(jnp.float32) \
        + b_router_expert.astype(jnp.float32)
    le = jnp.take_along_axis(le, g_sel[:, None, None], axis=1)[:, 0]
    p_e = jax.nn.softmax(le, axis=-1)
    top_v, top_i = lax.top_k(p_e, TOP_K)
    gates = p_sel[:, None] * top_v / jnp.sum(top_v, axis=-1, keepdims=True)
    expert_ids = g_sel[:, None] * EXPERTS_PER_GROUP + top_i
    combine = jnp.sum(jax.nn.one_hot(expert_ids, N_EXPERTS, dtype=jnp.float32)
                      * gates[..., None], axis=1).astype(x.dtype)
    hg = jnp.einsum('nd,edf->nef', xf, w_gate)
    hu = jnp.einsum('nd,edf->nef', xf, w_up)
    a = jax.nn.silu(hg) * hu * combine[:, :, None]
    y = jnp.einsum('nef,efd->nd', a, w_down)
    return y.reshape(shp)


def setup_inputs(seed: int = 0) -> dict:
    key = jax.random.key(seed)
    ks = jax.random.split(key, 24)
    f32 = jnp.float32
    nrm = lambda k, s, sc: jax.random.normal(k, s, f32) * sc
    L = DEPTH
    return {
        "x_prompt": nrm(ks[0], (BATCH, SEQ, D_MODEL), 1.0),
        "x_sample": nrm(ks[1], (DEC_BATCH, DEC_SEQ, D_MODEL), 1.0),
        "cache_sb_k": nrm(ks[2], (L, DEC_BATCH, PAST_LEN, SB_HEADS, SB_HEAD_DIM), 1.0),
        "cache_sb_v": nrm(ks[3], (L, DEC_BATCH, PAST_LEN, SB_HEADS, SB_HEAD_DIM), 1.0),
        "norm1_g": 1.0 + nrm(ks[4], (L, D_MODEL), 0.02),
        "w_in": nrm(ks[5], (L, D_MODEL, IN_WIDTH), D_MODEL ** -0.5),
        "gm_v_norm_g": 1.0 + nrm(ks[6], (L, GM_WIDTH), 0.02),
        "gm_w_s": nrm(ks[7], (L, GM_GROUPS, GM_CHUNK, GM_CHUNK), GM_CHUNK ** -0.5),
        "gm_b_s": nrm(ks[8], (L, GM_GROUPS, GM_CHUNK), 0.02),
        "q_norm_g": 1.0 + nrm(ks[9], (L, SB_HEAD_DIM), 0.02),
        "k_norm_g": 1.0 + nrm(ks[10], (L, SB_HEAD_DIM), 0.02),
        "w_out": nrm(ks[11], (L, MIX_WIDTH, D_MODEL), MIX_WIDTH ** -0.5),
        "norm2_g": 1.0 + nrm(ks[12], (L, D_MODEL), 0.02),
        "w_router_group": nrm(ks[13], (L, D_MODEL, N_EXPERT_GROUPS), D_MODEL ** -0.5),
        "b_router_group": nrm(ks[14], (L, N_EXPERT_GROUPS), 0.01),
        "w_router_expert": nrm(ks[15], (L, N_EXPERT_GROUPS, D_MODEL, EXPERTS_PER_GROUP), D_MODEL ** -0.5),
        "b_router_expert": nrm(ks[16], (L, N_EXPERT_GROUPS, EXPERTS_PER_GROUP), 0.01),
        "w_gate": nrm(ks[17], (L, N_EXPERTS, D_MODEL, EXPERT_FF), D_MODEL ** -0.5),
        "w_up": nrm(ks[18], (L, N_EXPERTS, D_MODEL, EXPERT_FF), D_MODEL ** -0.5),
        "w_down": nrm(ks[19], (L, N_EXPERTS, EXPERT_FF, D_MODEL), EXPERT_FF ** -0.5),
    }


def reference(x_prompt, x_sample, cache_sb_k, cache_sb_v, norm1_g, w_in, gm_v_norm_g,
              gm_w_s, gm_b_s, q_norm_g, k_norm_g, w_out, norm2_g, w_router_group,
              b_router_group, w_router_expert, b_router_expert, w_gate, w_up, w_down):
    yp, ys = x_prompt, x_sample
    kp_list, vp_list, ks_list, vs_list, gv_list = [], [], [], [], []
    for l in range(DEPTH):
        moe_args = (w_router_group[l], b_router_group[l], w_router_expert[l],
                    b_router_expert[l], w_gate[l], w_up[l], w_down[l])
        b, s, _ = yp.shape
        xn = rmsnorm(yp, norm1_g[l])
        gu, gv, q, k, v = mixer_inputs(xn, w_in[l], gm_v_norm_g[l], q_norm_g[l], k_norm_g[l])
        nc = s // GM_CHUNK
        gm = gmlp_chunk(gu.reshape(b, nc, GM_CHUNK, GM_GROUPS, GM_GROUP_DIM),
                        gv.reshape(b, nc, GM_CHUNK, GM_GROUPS, GM_GROUP_DIM),
                        gm_w_s[l], gm_b_s[l]).reshape(b, s, GM_WIDTH)
        nb = s // SB_BLOCK
        k_pos = jnp.arange(s, dtype=jnp.int32)
        q_blocks = q.reshape(b, nb, SB_BLOCK, SB_HEADS, SB_HEAD_DIM).transpose(1, 0, 2, 3, 4)
        pos_blocks = k_pos.reshape(nb, SB_BLOCK)
        sb = lax.map(lambda a: stick_breaking(a[0], k, v, a[1], k_pos), (q_blocks, pos_blocks))
        sb = sb.transpose(1, 0, 2, 3, 4).reshape(b, s, SB_WIDTH)
        h = yp + jnp.concatenate([gm, sb], axis=-1) @ w_out[l]
        yp = h + hier_moe(rmsnorm(h, norm2_g[l]), *moe_args)
        kp_list.append(k)
        vp_list.append(v)
        db, n, _ = ys.shape
        xn = rmsnorm(ys, norm1_g[l])
        gu, gv, q, k, v = mixer_inputs(xn, w_in[l], gm_v_norm_g[l], q_norm_g[l], k_norm_g[l])
        gm = gmlp_chunk(gu, gv, gm_w_s[l], gm_b_s[l]).reshape(db, n, GM_WIDTH)
        k_all = jnp.concatenate([cache_sb_k[l], k], axis=1)
        v_all = jnp.concatenate([cache_sb_v[l], v], axis=1)
        past = cache_sb_k.shape[2]
        k_pos = jnp.arange(past + n, dtype=jnp.int32)
        q_pos = past + jnp.arange(n, dtype=jnp.int32)
        sb = stick_breaking(q, k_all, v_all, q_pos, k_pos).reshape(db, n, SB_WIDTH)
        h = ys + jnp.concatenate([gm, sb], axis=-1) @ w_out[l]
        ys = h + hier_moe(rmsnorm(h, norm2_g[l]), *moe_args)
        ks_list.append(k)
        vs_list.append(v)
        gv_list.append(gv)
    sb_k_prompt = jnp.stack(kp_list, axis=0)
    sb_v_prompt = jnp.stack(vp_list, axis=0)
    sb_k_sample = jnp.stack(ks_list, axis=0)
    sb_v_sample = jnp.stack(vs_list, axis=0)
    gm_v_sample = jnp.stack(gv_list, axis=0)
    return (yp, ys, sb_k_prompt, sb_v_prompt, sb_k_sample, sb_v_sample, gm_v_sample)
```

```python
import functools

import jax
import jax.numpy as jnp
from jax import lax
from jax.experimental import pallas as pl
from jax.experimental.pallas import tpu as pltpu

D_MODEL = 1024
GM_WIDTH = 512
GM_GROUPS = 4
GM_GROUP_DIM = 128
GM_CHUNK = 128
SB_WIDTH = 512
SB_HEAD_DIM = 64
SB_HEADS = 8
IN_WIDTH = 2 * GM_WIDTH + 3 * SB_WIDTH
N_EXPERT_GROUPS = 4
EXPERTS_PER_GROUP = 4
N_EXPERTS = 16
EXPERT_FF = 256
GROUP_FF = EXPERTS_PER_GROUP * EXPERT_FF
EPS = 1e-6

LANES = 128
SB_BLOCK = 128
CACHE_BLOCK = 512
CACHE_SUB = 256
VMEM_LIMIT = 56 * 1024 * 1024

F32 = jnp.float32
BF16 = jnp.bfloat16
_NT = (((1,), (1,)), ((), ()))


def _gelu_tanh(x):
    return 0.5 * x * (1.0 + jnp.tanh(0.7978845608028654 * (x + 0.044715 * (x * x * x))))


def _softplus(z):
    return jnp.maximum(z, 0.0) + jnp.log(1.0 + jnp.exp(-jnp.abs(z)))


def _suffix_matrix(n):
    r = lax.broadcasted_iota(jnp.int32, (n, n), 0)
    c = lax.broadcasted_iota(jnp.int32, (n, n), 1)
    return jnp.where(r > c, 1.0, 0.0).astype(BF16)


def _mixer_in_body(x_ref, g1_ref, w_in_ref, gvg_ref, qg_ref, kg_ref, ws_ref, bst_ref, bd_ref,
                   gm_ref, qs_ref, kb_ref, vb_ref, kf_ref, vf_ref, *gv_refs, chunk):
    x = x_ref[...]
    tm = x.shape[0]
    xn = x * lax.rsqrt(jnp.mean(x * x, axis=-1, keepdims=True) + EPS) * g1_ref[...]
    proj = jnp.dot(xn.astype(BF16), w_in_ref[...], preferred_element_type=F32)

    gu = _gelu_tanh(proj[:, :GM_WIDTH])
    gvr = _gelu_tanh(proj[:, GM_WIDTH:2 * GM_WIDTH])
    gv = gvr * lax.rsqrt(jnp.mean(gvr * gvr, axis=-1, keepdims=True) + EPS) * gvg_ref[...]
    if gv_refs:
        gv_refs[0][...] = gv

    o = 2 * GM_WIDTH
    q = proj[:, o:o + SB_WIDTH]
    k = proj[:, o + SB_WIDTH:o + 2 * SB_WIDTH]
    v = proj[:, o + 2 * SB_WIDTH:]
    bd = bd_ref[...]
    qms = jnp.dot((q * q).astype(BF16), bd, preferred_element_type=F32)
    kms = jnp.dot((k * k).astype(BF16), bd, preferred_element_type=F32)
    qn = q * lax.rsqrt(qms + EPS) * qg_ref[...]
    kn = k * lax.rsqrt(kms + EPS) * kg_ref[...]
    kf_ref[...] = kn
    vf_ref[...] = v
    qs_ref[...] = (qn * (SB_HEAD_DIM ** -0.5)).astype(BF16)
    kb_ref[...] = kn.astype(BF16)
    vb_ref[...] = v.astype(BF16)

    r = lax.broadcasted_iota(jnp.int32, (chunk, chunk), 0)
    c = lax.broadcasted_iota(jnp.int32, (chunk, chunk), 1)
    gvb = gv.astype(BF16)
    for g in range(GM_GROUPS):
        w = jnp.where(r >= c, ws_ref[g, :chunk, :chunk], 0.0).astype(BF16)
        b = bst_ref[:chunk, g:g + 1]
        lo = g * GM_GROUP_DIM
        for ci in range(tm // chunk):
            rows = slice(ci * chunk, (ci + 1) * chunk)
            mixed = jnp.dot(w, gvb[rows, lo:lo + GM_GROUP_DIM], preferred_element_type=F32) + b
            gm_ref[rows, lo:lo + GM_GROUP_DIM] = (gu[rows, lo:lo + GM_GROUP_DIM] * mixed).astype(BF16)


def _mixer_in(x2d, params, *, tm, chunk, want_gv):
    n = x2d.shape[0]
    row = lambda w: pl.BlockSpec((tm, w), lambda i: (i, 0))
    full = lambda a: pl.BlockSpec(a.shape, lambda i: (0,) * a.ndim)
    g1, w_in, gvg, qg, kg, ws, bst, bd = params
    out_shape = [
        jax.ShapeDtypeStruct((n, GM_WIDTH), BF16),
        jax.ShapeDtypeStruct((n, SB_WIDTH), BF16),
        jax.ShapeDtypeStruct((n, SB_WIDTH), BF16),
        jax.ShapeDtypeStruct((n, SB_WIDTH), BF16),
        jax.ShapeDtypeStruct((n, SB_WIDTH), F32),
        jax.ShapeDtypeStruct((n, SB_WIDTH), F32),
    ]
    if want_gv:
        out_shape.append(jax.ShapeDtypeStruct((n, GM_WIDTH), F32))
    return pl.pallas_call(
        functools.partial(_mixer_in_body, chunk=chunk),
        grid=(n // tm,),
        in_specs=[row(D_MODEL)] + [full(a) for a in params],
        out_specs=[row(s.shape[1]) for s in out_shape],
        out_shape=out_shape,
        compiler_params=pltpu.CompilerParams(
            dimension_semantics=("parallel",), vmem_limit_bytes=VMEM_LIMIT),
        name="mixer_in_gv" if want_gv else "mixer_in",
    )(x2d, *params)


def _sb_step(qst, kj, vj, suffix, c, causal=None):
    z = lax.dot_general(qst, kj, _NT, preferred_element_type=F32)
    sp = _softplus(z)
    lk = -sp
    if causal is not None:
        lk = jnp.where(causal, lk, 0.0)
    rest = jnp.dot(lk.astype(BF16), suffix, preferred_element_type=F32) + c
    w = jnp.exp(z - sp + rest)
    if causal is not None:
        w = jnp.where(causal, w, 0.0)
    pv = jnp.dot(w.astype(BF16), vj, preferred_element_type=F32)
    return pv, c + jnp.sum(lk, axis=-1, keepdims=True)


def _prompt_attn_body(q_ref, k_ref, v_ref, o_ref):
    i = pl.program_id(2)
    q = q_ref[...]
    lane = lax.broadcasted_iota(jnp.int32, q.shape, 1)
    first = lane < SB_HEAD_DIM
    zero = jnp.zeros_like(q)
    qst = jnp.concatenate([jnp.where(first, q, zero), jnp.where(first, zero, q)], axis=0)
    suffix = _suffix_matrix(SB_BLOCK)
    m = 2 * SB_BLOCK
    t = lax.broadcasted_iota(jnp.int32, (m, SB_BLOCK), 0) % SB_BLOCK
    s = lax.broadcasted_iota(jnp.int32, (m, SB_BLOCK), 1)

    def block(j):
        rows = pl.ds(pl.multiple_of(j * SB_BLOCK, SB_BLOCK), SB_BLOCK)
        return k_ref[rows, :], v_ref[rows, :]

    kd, vd = block(i)
    acc, c = _sb_step(qst, kd, vd, suffix, jnp.zeros((m, 1), F32), causal=s < t)

    def body(jj, carry):
        acc, c = carry
        kj, vj = block(i - 1 - jj)
        pv, c = _sb_step(qst, kj, vj, suffix, c)
        return acc + pv, c

    acc, c = lax.fori_loop(0, i, body, (acc, c))
    o_ref[...] = jnp.where(first, acc[:SB_BLOCK], acc[SB_BLOCK:]).astype(o_ref.dtype)


def _prompt_attn(qs, kb, vb):
    b, s, _ = qs.shape
    pairs = SB_WIDTH // LANES
    q_spec = pl.BlockSpec((None, SB_BLOCK, LANES), lambda bi, hp, i: (bi, i, hp))
    kv_spec = pl.BlockSpec((None, s, LANES), lambda bi, hp, i: (bi, 0, hp))
    return pl.pallas_call(
        _prompt_attn_body,
        grid=(b, pairs, s // SB_BLOCK),
        in_specs=[q_spec, kv_spec, kv_spec],
        out_specs=q_spec,
        out_shape=jax.ShapeDtypeStruct((b, s, SB_WIDTH), BF16),
        compiler_params=pltpu.CompilerParams(
            dimension_semantics=("parallel", "parallel", "arbitrary"),
            vmem_limit_bytes=VMEM_LIMIT),
        name="prompt_attn",
    )(qs, kb, vb)


def _sample_attn_body(q_ref, kn_ref, vn_ref, ck_ref, cv_ref, o_ref, qst_ref, c_ref, acc_ref):
    j = pl.program_id(1)
    n = q_ref.shape[0]
    m = SB_HEADS * n

    @pl.when(j == 0)
    def _():
        q = q_ref[...]
        qt = jnp.concatenate([q] * SB_HEADS, axis=0)
        row = lax.broadcasted_iota(jnp.int32, qt.shape, 0)
        lane = lax.broadcasted_iota(jnp.int32, qt.shape, 1)
        qst = jnp.where(row // n == lane // SB_HEAD_DIM, qt, jnp.zeros_like(qt))
        qst_ref[...] = qst
        pad = jnp.zeros((LANES - n, SB_WIDTH), BF16)
        kn = jnp.concatenate([kn_ref[...], pad], axis=0)
        vn = jnp.concatenate([vn_ref[...], pad], axis=0)
        t = lax.broadcasted_iota(jnp.int32, (m, LANES), 0) % n
        s = lax.broadcasted_iota(jnp.int32, (m, LANES), 1)
        pv, c = _sb_step(qst, kn, vn, _suffix_matrix(LANES), jnp.zeros((m, 1), F32), causal=s < t)
        acc_ref[...] = pv
        c_ref[...] = c

    qst = qst_ref[...]
    suffix = _suffix_matrix(CACHE_SUB)
    for sub in reversed(range(CACHE_BLOCK // CACHE_SUB)):
        rows = slice(sub * CACHE_SUB, (sub + 1) * CACHE_SUB)
        pv, c = _sb_step(qst, ck_ref[rows, :].astype(BF16), cv_ref[rows, :].astype(BF16),
                         suffix, c_ref[...])
        acc_ref[...] += pv
        c_ref[...] = c

    @pl.when(j == pl.num_programs(1) - 1)
    def _():
        lane = lax.broadcasted_iota(jnp.int32, (n, SB_WIDTH), 1)
        out = jnp.zeros((n, SB_WIDTH), F32)
        for h in range(SB_HEADS):
            out = jnp.where(lane // SB_HEAD_DIM == h, acc_ref[h * n:(h + 1) * n, :], out)
        o_ref[...] = out.astype(o_ref.dtype)


def _sample_attn(qs, kb, vb, cache_k, cache_v):
    db, n, _ = qs.shape
    past = cache_k.shape[1]
    nblk = past // CACHE_BLOCK
    new_spec = pl.BlockSpec((None, n, SB_WIDTH), lambda b, j: (b, 0, 0))
    cache_spec = pl.BlockSpec((None, CACHE_BLOCK, SB_WIDTH), lambda b, j: (b, nblk - 1 - j, 0))
    m = SB_HEADS * n
    return pl.pallas_call(
        _sample_attn_body,
        grid=(db, nblk),
        in_specs=[new_spec, new_spec, new_spec, cache_spec, cache_spec],
        out_specs=new_spec,
        out_shape=jax.ShapeDtypeStruct((db, n, SB_WIDTH), BF16),
        scratch_shapes=[pltpu.VMEM((m, SB_WIDTH), BF16),
                        pltpu.VMEM((m, 1), F32),
                        pltpu.VMEM((m, SB_WIDTH), F32)],
        compiler_params=pltpu.CompilerParams(
            dimension_semantics=("parallel", "arbitrary"), vmem_limit_bytes=VMEM_LIMIT),
        name="sample_attn",
    )(qs, kb, vb, cache_k, cache_v)


def _route(logits):
    lane = lax.broadcasted_iota(jnp.int32, logits.shape, 1)
    neg = jnp.float32(-jnp.inf)
    big = jnp.int32(1 << 20)
    rmax = lambda a: jnp.max(a, axis=-1, keepdims=True)
    rmin = lambda a: jnp.min(a, axis=-1, keepdims=True)
    rsum = lambda a: jnp.sum(a, axis=-1, keepdims=True)

    gmask = lane < N_EXPERT_GROUPS
    gl = jnp.where(gmask, logits, neg)
    gmax = rmax(gl)
    g_sel = rmin(jnp.where(gl == gmax, lane, big))
    p_sel = 1.0 / rsum(jnp.where(gmask, jnp.exp(gl - gmax), 0.0))

    e_lo = N_EXPERT_GROUPS + g_sel * EXPERTS_PER_GROUP
    emask = (lane >= e_lo) & (lane < e_lo + EXPERTS_PER_GROUP)
    el = jnp.where(emask, logits, neg)
    emax = rmax(el)
    ex = jnp.where(emask, jnp.exp(el - emax), 0.0)
    p_e = ex / rsum(ex)
    pm = jnp.where(emask, p_e, -1.0)
    v1 = rmax(pm)
    i1 = rmin(jnp.where(pm == v1, lane, big))
    pm2 = jnp.where(lane == i1, -1.0, pm)
    v2 = rmax(pm2)
    i2 = rmin(jnp.where(pm2 == v2, lane, big))
    tot = v1 + v2
    return (jnp.where(lane == i1, p_sel * v1 / tot, 0.0)
            + jnp.where(lane == i2, p_sel * v2 / tot, 0.0))


def _out_moe_body(x_ref, gm_ref, sb_ref, wo_ref, g2_ref, wrh_ref, wrl_ref, br_ref,
                  wgu_ref, wd_ref, o_ref, hn_ref, comb_ref):
    g = pl.program_id(1)

    @pl.when(g == 0)
    def _():
        h = (x_ref[...]
             + jnp.dot(gm_ref[...], wo_ref[:GM_WIDTH, :], preferred_element_type=F32)
             + jnp.dot(sb_ref[...], wo_ref[GM_WIDTH:, :], preferred_element_type=F32))
        o_ref[...] = h
        hn = h * lax.rsqrt(jnp.mean(h * h, axis=-1, keepdims=True) + EPS) * g2_ref[...]
        hi = hn.astype(BF16)
        hn_ref[...] = hi
        lo = (hn - hi.astype(F32)).astype(BF16)
        logits = (jnp.dot(hi, wrh_ref[...], preferred_element_type=F32)
                  + jnp.dot(lo, wrh_ref[...], preferred_element_type=F32)
                  + jnp.dot(hi, wrl_ref[...], preferred_element_type=F32)) + br_ref[...]
        comb_ref[...] = _route(logits)

    hgu = jnp.dot(hn_ref[...], wgu_ref[...], preferred_element_type=F32)
    comb = comb_ref[...]
    lane = lax.broadcasted_iota(jnp.int32, comb.shape, 1)
    parts = []
    for e in range(EXPERTS_PER_GROUP):
        gate = jnp.sum(jnp.where(lane == N_EXPERT_GROUPS + g * EXPERTS_PER_GROUP + e, comb, 0.0),
                       axis=-1, keepdims=True)
        hg = hgu[:, e * EXPERT_FF:(e + 1) * EXPERT_FF]
        hu = hgu[:, GROUP_FF + e * EXPERT_FF:GROUP_FF + (e + 1) * EXPERT_FF]
        parts.append((hg / (1.0 + jnp.exp(-hg)) * hu * gate).astype(BF16))
    a = jnp.concatenate(parts, axis=-1)
    o_ref[...] += jnp.dot(a, wd_ref[...], preferred_element_type=F32)


def _out_moe(x2d, gm, sb, params, *, tm):
    n = x2d.shape[0]
    wo, g2, wrh, wrl, br, wgu, wd = params
    row = lambda w: pl.BlockSpec((tm, w), lambda i, g: (i, 0))
    full = lambda a: pl.BlockSpec(a.shape, lambda i, g: (0,) * a.ndim)
    return pl.pallas_call(
        _out_moe_body,
        grid=(n // tm, N_EXPERT_GROUPS),
        in_specs=[row(D_MODEL), row(GM_WIDTH), row(SB_WIDTH),
                  full(wo), full(g2), full(wrh), full(wrl), full(br),
                  pl.BlockSpec((None, D_MODEL, 2 * GROUP_FF), lambda i, g: (g, 0, 0)),
                  pl.BlockSpec((None, GROUP_FF, D_MODEL), lambda i, g: (g, 0, 0))],
        out_specs=row(D_MODEL),
        out_shape=jax.ShapeDtypeStruct((n, D_MODEL), F32),
        scratch_shapes=[pltpu.VMEM((tm, D_MODEL), BF16), pltpu.VMEM((tm, LANES), F32)],
        compiler_params=pltpu.CompilerParams(
            dimension_semantics=("parallel", "arbitrary"), vmem_limit_bytes=VMEM_LIMIT),
        name="out_moe",
    )(x2d, gm, sb, *params)


def _layer(xp, xs, cache_k, cache_v, norm1_g, w_in, gm_v_norm_g, gm_w_s, gm_b_s, q_norm_g,
           k_norm_g, w_out, norm2_g, w_router_group, b_router_group, w_router_expert,
           b_router_expert, w_gate, w_up, w_down):
    b, s, d = xp.shape
    db, n, _ = xs.shape
    head_of = jnp.arange(SB_WIDTH) // SB_HEAD_DIM
    bd = jnp.where(head_of[:, None] == head_of[None, :], 1.0 / SB_HEAD_DIM, 0.0).astype(BF16)
    in_params = (norm1_g[None, :], w_in.astype(BF16), gm_v_norm_g[None, :],
                 jnp.tile(q_norm_g, SB_HEADS)[None, :], jnp.tile(k_norm_g, SB_HEADS)[None, :],
                 gm_w_s, gm_b_s.T, bd)

    wr = jnp.concatenate(
        [w_router_group, jnp.transpose(w_router_expert, (1, 0, 2)).reshape(d, N_EXPERTS)], axis=1)
    wr = jnp.pad(wr, ((0, 0), (0, LANES - wr.shape[1])))
    wr_hi = wr.astype(BF16)
    wr_lo = (wr - wr_hi.astype(F32)).astype(BF16)
    br = jnp.pad(jnp.concatenate([b_router_group, b_router_expert.reshape(-1)]),
                 (0, LANES - N_EXPERT_GROUPS - N_EXPERTS))[None, :]

    def group_cols(w):
        w = w.astype(BF16).reshape(N_EXPERT_GROUPS, EXPERTS_PER_GROUP, d, EXPERT_FF)
        return jnp.transpose(w, (0, 2, 1, 3)).reshape(N_EXPERT_GROUPS, d, GROUP_FF)

    wgu = jnp.concatenate([group_cols(w_gate), group_cols(w_up)], axis=-1)
    wd = w_down.astype(BF16).reshape(N_EXPERT_GROUPS, GROUP_FF, d)
    moe_params = (w_out.astype(BF16), norm2_g[None, :], wr_hi, wr_lo, br, wgu, wd)

    xp2 = xp.reshape(b * s, d)
    gm, qs, kb, vb, kf, vf = _mixer_in(xp2, in_params, tm=512, chunk=GM_CHUNK, want_gv=False)
    shp = (b, s, SB_WIDTH)
    sb = _prompt_attn(qs.reshape(shp), kb.reshape(shp), vb.reshape(shp))
    yp = _out_moe(xp2, gm, sb.reshape(b * s, SB_WIDTH), moe_params, tm=512).reshape(b, s, d)

    xs2 = xs.reshape(db * n, d)
    gm_s, qs_s, kb_s, vb_s, kf_s, vf_s, gv_s = _mixer_in(
        xs2, in_params, tm=db * n, chunk=n, want_gv=True)
    shs = (db, n, SB_WIDTH)
    sb_s = _sample_attn(qs_s.reshape(shs), kb_s.reshape(shs), vb_s.reshape(shs),
                        cache_k.reshape(db, -1, SB_WIDTH), cache_v.reshape(db, -1, SB_WIDTH))
    ys = _out_moe(xs2, gm_s, sb_s.reshape(db * n, SB_WIDTH), moe_params, tm=db * n).reshape(db, n, d)

    heads = (SB_HEADS, SB_HEAD_DIM)
    return (yp, ys, kf.reshape(b, s, *heads), vf.reshape(b, s, *heads),
            kf_s.reshape(db, n, *heads), vf_s.reshape(db, n, *heads),
            gv_s.reshape(db, n, GM_GROUPS, GM_GROUP_DIM))


def kernel(x_prompt, x_sample, cache_sb_k, cache_sb_v, norm1_g, w_in, gm_v_norm_g, gm_w_s, gm_b_s, q_norm_g, k_norm_g, w_out, norm2_g, w_router_group, b_router_group, w_router_expert, b_router_expert, w_gate, w_up, w_down):
    depth = w_in.shape[0]
    yp, ys = x_prompt, x_sample
    outs = [[] for _ in range(5)]
    for l in range(depth):
        yp, ys, *rest = _layer(
            yp, ys, cache_sb_k[l], cache_sb_v[l], norm1_g[l], w_in[l], gm_v_norm_g[l], gm_w_s[l],
            gm_b_s[l], q_norm_g[l], k_norm_g[l], w_out[l], norm2_g[l], w_router_group[l],
            b_router_group[l], w_router_expert[l], b_router_expert[l], w_gate[l], w_up[l],
            w_down[l])
        for acc, r in zip(outs, rest):
            acc.append(r)
    return (yp, ys) + tuple(jnp.stack(o, axis=0) for o in outs)
```

```python
import functools

import jax
import jax.numpy as jnp
from jax import lax
from jax.experimental import pallas as pl
from jax.experimental.pallas import tpu as pltpu

D_MODEL = 1024
GM_WIDTH = 512
GM_GROUPS = 4
GM_GROUP_DIM = 128
GM_CHUNK = 128
SB_WIDTH = 512
SB_HEAD_DIM = 64
SB_HEADS = 8
IN_WIDTH = 2 * GM_WIDTH + 3 * SB_WIDTH
N_EXPERT_GROUPS = 4
EXPERTS_PER_GROUP = 4
N_EXPERTS = 16
EXPERT_FF = 256
GROUP_FF = EXPERTS_PER_GROUP * EXPERT_FF
EPS = 1e-6

LANES = 128
HEAD_PAIRS = SB_WIDTH // LANES
SB_BLOCK = 128
CACHE_BLOCK = 512
CACHE_SUB = 256
VMEM_LIMIT = 56 * 1024 * 1024

LOG2_E = 1.4426950408889634
Q_SCALE = SB_HEAD_DIM ** -0.5 * LOG2_E
SB_UNDERFLOW_BITS = 160.0

F32 = jnp.float32
BF16 = jnp.bfloat16
_NT = (((1,), (1,)), ((), ()))


def _gelu_tanh(x):
    return 0.5 * x * (1.0 + jnp.tanh(0.7978845608028654 * (x + 0.044715 * (x * x * x))))


def _softplus2(z):
    return jnp.maximum(z, 0.0) + jnp.log2(1.0 + jnp.exp2(-jnp.abs(z)))


def _suffix_matrix(n):
    r = lax.broadcasted_iota(jnp.int32, (n, n), 0)
    c = lax.broadcasted_iota(jnp.int32, (n, n), 1)
    return jnp.where(r > c, 1.0, 0.0).astype(BF16)


def _mixer_in_body(x_ref, g1_ref, w_in_ref, gvg_ref, qg_ref, kg_ref, ws_ref, bst_ref, bd_ref,
                   gm_ref, qs_ref, kb_ref, vb_ref, kf_ref, vf_ref, *gv_refs, chunk):
    x = x_ref[...]
    tm = x.shape[0]
    xn = x * lax.rsqrt(jnp.mean(x * x, axis=-1, keepdims=True) + EPS) * g1_ref[...]
    proj = jnp.dot(xn.astype(BF16), w_in_ref[...], preferred_element_type=F32)

    gu = _gelu_tanh(proj[:, :GM_WIDTH])
    gvr = _gelu_tanh(proj[:, GM_WIDTH:2 * GM_WIDTH])
    gv = gvr * lax.rsqrt(jnp.mean(gvr * gvr, axis=-1, keepdims=True) + EPS) * gvg_ref[...]
    if gv_refs:
        gv_refs[0][...] = gv

    o = 2 * GM_WIDTH
    q = proj[:, o:o + SB_WIDTH]
    k = proj[:, o + SB_WIDTH:o + 2 * SB_WIDTH]
    v = proj[:, o + 2 * SB_WIDTH:]
    bd = bd_ref[...]
    qms = jnp.dot((q * q).astype(BF16), bd, preferred_element_type=F32)
    kms = jnp.dot((k * k).astype(BF16), bd, preferred_element_type=F32)
    qn = q * lax.rsqrt(qms + EPS) * qg_ref[...]
    kn = k * lax.rsqrt(kms + EPS) * kg_ref[...]
    kf_ref[...] = kn
    vf_ref[...] = v
    qs_ref[...] = (qn * Q_SCALE).astype(BF16)
    kb_ref[...] = kn.astype(BF16)
    vb_ref[...] = v.astype(BF16)

    r = lax.broadcasted_iota(jnp.int32, (chunk, chunk), 0)
    c = lax.broadcasted_iota(jnp.int32, (chunk, chunk), 1)
    gvb = gv.astype(BF16)
    for g in range(GM_GROUPS):
        w = jnp.where(r >= c, ws_ref[g, :chunk, :chunk], 0.0).astype(BF16)
        b = bst_ref[:chunk, g:g + 1]
        lo = g * GM_GROUP_DIM
        for ci in range(tm // chunk):
            rows = slice(ci * chunk, (ci + 1) * chunk)
            mixed = jnp.dot(w, gvb[rows, lo:lo + GM_GROUP_DIM], preferred_element_type=F32) + b
            gm_ref[rows, lo:lo + GM_GROUP_DIM] = (gu[rows, lo:lo + GM_GROUP_DIM] * mixed).astype(BF16)


def _mixer_in(x2d, params, *, tm, chunk, want_gv):
    n = x2d.shape[0]
    row = lambda w: pl.BlockSpec((tm, w), lambda i: (i, 0))
    full = lambda a: pl.BlockSpec(a.shape, lambda i: (0,) * a.ndim)
    g1, w_in, gvg, qg, kg, ws, bst, bd = params
    out_shape = [
        jax.ShapeDtypeStruct((n, GM_WIDTH), BF16),
        jax.ShapeDtypeStruct((n, SB_WIDTH), BF16),
        jax.ShapeDtypeStruct((n, SB_WIDTH), BF16),
        jax.ShapeDtypeStruct((n, SB_WIDTH), BF16),
        jax.ShapeDtypeStruct((n, SB_WIDTH), F32),
        jax.ShapeDtypeStruct((n, SB_WIDTH), F32),
    ]
    if want_gv:
        out_shape.append(jax.ShapeDtypeStruct((n, GM_WIDTH), F32))
    return pl.pallas_call(
        functools.partial(_mixer_in_body, chunk=chunk),
        grid=(n // tm,),
        in_specs=[row(D_MODEL)] + [full(a) for a in params],
        out_specs=[row(s.shape[1]) for s in out_shape],
        out_shape=out_shape,
        compiler_params=pltpu.CompilerParams(
            dimension_semantics=("parallel",), vmem_limit_bytes=VMEM_LIMIT),
        name="mixer_in_gv" if want_gv else "mixer_in",
    )(x2d, *params)


def _sb_step(qst, kj, vj, suffix, c, causal=None):
    z = lax.dot_general(qst, kj, _NT, preferred_element_type=F32)
    sp = _softplus2(z)
    if causal is not None:
        sp = jnp.where(causal, sp, 0.0)
    later = jnp.dot(sp.astype(BF16), suffix, preferred_element_type=F32) + c
    w = jnp.exp2(z - sp - later)
    if causal is not None:
        w = jnp.where(causal, w, 0.0)
    pv = jnp.dot(w.astype(BF16), vj, preferred_element_type=F32)
    return pv, c + jnp.sum(sp, axis=-1, keepdims=True)


def _prompt_attn_body(q_ref, k_ref, v_ref, o_ref, qst_ref, c_ref, acc_ref):
    i = pl.program_id(1)
    lane = lax.broadcasted_iota(jnp.int32, (SB_BLOCK, LANES), 1)
    first = lane < SB_HEAD_DIM
    for hp in range(HEAD_PAIRS):
        q = q_ref[:, hp * LANES:(hp + 1) * LANES]
        zero = jnp.zeros_like(q)
        qst_ref[hp, :SB_BLOCK, :] = jnp.where(first, q, zero)
        qst_ref[hp, SB_BLOCK:, :] = jnp.where(first, zero, q)
    suffix = _suffix_matrix(SB_BLOCK)
    m = 2 * SB_BLOCK

    def sweep(j, causal, fresh):
        rows = pl.ds(pl.multiple_of(j * SB_BLOCK, SB_BLOCK), SB_BLOCK)
        for hp in range(HEAD_PAIRS):
            cols = slice(hp * LANES, (hp + 1) * LANES)
            c0 = jnp.zeros((m, 1), F32) if fresh else c_ref[hp]
            pv, c = _sb_step(qst_ref[hp], k_ref[rows, cols], v_ref[rows, cols], suffix, c0, causal)
            acc_ref[hp] = pv if fresh else acc_ref[hp] + pv
            c_ref[hp] = c

    t = lax.broadcasted_iota(jnp.int32, (m, SB_BLOCK), 0) % SB_BLOCK
    s = lax.broadcasted_iota(jnp.int32, (m, SB_BLOCK), 1)
    sweep(i, s < t, True)

    def least_carry():
        c = c_ref[0]
        for hp in range(1, HEAD_PAIRS):
            c = jnp.minimum(c, c_ref[hp])
        return jnp.min(c)

    def more(state):
        jj, carry = state
        return (jj < i) & (carry < SB_UNDERFLOW_BITS)

    def body(state):
        jj, _ = state
        sweep(i - 1 - jj, None, False)
        return jj + 1, least_carry()

    lax.while_loop(more, body, (jnp.int32(0), least_carry()))
    for hp in range(HEAD_PAIRS):
        o_ref[:, hp * LANES:(hp + 1) * LANES] = jnp.where(
            first, acc_ref[hp, :SB_BLOCK, :], acc_ref[hp, SB_BLOCK:, :]).astype(o_ref.dtype)


def _prompt_attn(qs, kb, vb):
    b, s, _ = qs.shape
    q_spec = pl.BlockSpec((None, SB_BLOCK, SB_WIDTH), lambda bi, i: (bi, i, 0))
    kv_spec = pl.BlockSpec((None, s, SB_WIDTH), lambda bi, i: (bi, 0, 0))
    m = 2 * SB_BLOCK
    return pl.pallas_call(
        _prompt_attn_body,
        grid=(b, s // SB_BLOCK),
        in_specs=[q_spec, kv_spec, kv_spec],
        out_specs=q_spec,
        out_shape=jax.ShapeDtypeStruct((b, s, SB_WIDTH), BF16),
        scratch_shapes=[pltpu.VMEM((HEAD_PAIRS, m, LANES), BF16),
                        pltpu.VMEM((HEAD_PAIRS, m, 1), F32),
                        pltpu.VMEM((HEAD_PAIRS, m, LANES), F32)],
        compiler_params=pltpu.CompilerParams(
            dimension_semantics=("parallel", "arbitrary"), vmem_limit_bytes=VMEM_LIMIT),
        name="prompt_attn",
    )(qs, kb, vb)


def _sample_attn_body(q_ref, kn_ref, vn_ref, ck_ref, cv_ref, o_ref, qst_ref, c_ref, acc_ref):
    j = pl.program_id(1)
    n = q_ref.shape[0]
    m = SB_HEADS * n

    @pl.when(j == 0)
    def _():
        q = q_ref[...]
        qt = jnp.concatenate([q] * SB_HEADS, axis=0)
        row = lax.broadcasted_iota(jnp.int32, qt.shape, 0)
        lane = lax.broadcasted_iota(jnp.int32, qt.shape, 1)
        qst = jnp.where(row // n == lane // SB_HEAD_DIM, qt, jnp.zeros_like(qt))
        qst_ref[...] = qst
        pad = jnp.zeros((LANES - n, SB_WIDTH), BF16)
        kn = jnp.concatenate([kn_ref[...], pad], axis=0)
        vn = jnp.concatenate([vn_ref[...], pad], axis=0)
        t = lax.broadcasted_iota(jnp.int32, (m, LANES), 0) % n
        s = lax.broadcasted_iota(jnp.int32, (m, LANES), 1)
        pv, c = _sb_step(qst, kn, vn, _suffix_matrix(LANES), jnp.zeros((m, 1), F32), causal=s < t)
        acc_ref[...] = pv
        c_ref[...] = c

    qst = qst_ref[...]
    suffix = _suffix_matrix(CACHE_SUB)
    for sub in reversed(range(CACHE_BLOCK // CACHE_SUB)):
        rows = slice(sub * CACHE_SUB, (sub + 1) * CACHE_SUB)
        pv, c = _sb_step(qst, ck_ref[rows, :].astype(BF16), cv_ref[rows, :].astype(BF16),
                         suffix, c_ref[...])
        acc_ref[...] += pv
        c_ref[...] = c

    @pl.when(j == pl.num_programs(1) - 1)
    def _():
        lane = lax.broadcasted_iota(jnp.int32, (n, SB_WIDTH), 1)
        out = jnp.zeros((n, SB_WIDTH), F32)
        for h in range(SB_HEADS):
            out = jnp.where(lane // SB_HEAD_DIM == h, acc_ref[h * n:(h + 1) * n, :], out)
        o_ref[...] = out.astype(o_ref.dtype)


def _sample_attn(qs, kb, vb, cache_k, cache_v):
    db, n, _ = qs.shape
    past = cache_k.shape[1]
    nblk = past // CACHE_BLOCK
    new_spec = pl.BlockSpec((None, n, SB_WIDTH), lambda b, j: (b, 0, 0))
    cache_spec = pl.BlockSpec((None, CACHE_BLOCK, SB_WIDTH), lambda b, j: (b, nblk - 1 - j, 0))
    m = SB_HEADS * n
    return pl.pallas_call(
        _sample_attn_body,
        grid=(db, nblk),
        in_specs=[new_spec, new_spec, new_spec, cache_spec, cache_spec],
        out_specs=new_spec,
        out_shape=jax.ShapeDtypeStruct((db, n, SB_WIDTH), BF16),
        scratch_shapes=[pltpu.VMEM((m, SB_WIDTH), BF16),
                        pltpu.VMEM((m, 1), F32),
                        pltpu.VMEM((m, SB_WIDTH), F32)],
        compiler_params=pltpu.CompilerParams(
            dimension_semantics=("parallel", "arbitrary"), vmem_limit_bytes=VMEM_LIMIT),
        name="sample_attn",
    )(qs, kb, vb, cache_k, cache_v)


def _route(logits):
    lane = lax.broadcasted_iota(jnp.int32, logits.shape, 1)
    neg = jnp.float32(-jnp.inf)
    big = jnp.int32(1 << 20)
    rmax = lambda a: jnp.max(a, axis=-1, keepdims=True)
    rmin = lambda a: jnp.min(a, axis=-1, keepdims=True)
    rsum = lambda a: jnp.sum(a, axis=-1, keepdims=True)

    gmask = lane < N_EXPERT_GROUPS
    gl = jnp.where(gmask, logits, neg)
    gmax = rmax(gl)
    g_sel = rmin(jnp.where(gl == gmax, lane, big))
    p_sel = 1.0 / rsum(jnp.where(gmask, jnp.exp(gl - gmax), 0.0))

    e_lo = N_EXPERT_GROUPS + g_sel * EXPERTS_PER_GROUP
    emask = (lane >= e_lo) & (lane < e_lo + EXPERTS_PER_GROUP)
    el = jnp.where(emask, logits, neg)
    emax = rmax(el)
    ex = jnp.where(emask, jnp.exp(el - emax), 0.0)
    p_e = ex / rsum(ex)
    pm = jnp.where(emask, p_e, -1.0)
    v1 = rmax(pm)
    i1 = rmin(jnp.where(pm == v1, lane, big))
    pm2 = jnp.where(lane == i1, -1.0, pm)
    v2 = rmax(pm2)
    i2 = rmin(jnp.where(pm2 == v2, lane, big))
    tot = v1 + v2
    return (jnp.where(lane == i1, p_sel * v1 / tot, 0.0)
            + jnp.where(lane == i2, p_sel * v2 / tot, 0.0))


def _out_moe_body(x_ref, gm_ref, sb_ref, wo_ref, g2_ref, wrh_ref, wrl_ref, br_ref,
                  wgu_ref, wd_ref, o_ref, hn_ref, comb_ref):
    g = pl.program_id(1)

    @pl.when(g == 0)
    def _():
        h = (x_ref[...]
             + jnp.dot(gm_ref[...], wo_ref[:GM_WIDTH, :], preferred_element_type=F32)
             + jnp.dot(sb_ref[...], wo_ref[GM_WIDTH:, :], preferred_element_type=F32))
        o_ref[...] = h
        hn = h * lax.rsqrt(jnp.mean(h * h, axis=-1, keepdims=True) + EPS) * g2_ref[...]
        hi = hn.astype(BF16)
        hn_ref[...] = hi
        lo = (hn - hi.astype(F32)).astype(BF16)
        logits = (jnp.dot(hi, wrh_ref[...], preferred_element_type=F32)
                  + jnp.dot(lo, wrh_ref[...], preferred_element_type=F32)
                  + jnp.dot(hi, wrl_ref[...], preferred_element_type=F32)) + br_ref[...]
        comb_ref[...] = _route(logits)

    hgu = jnp.dot(hn_ref[...], wgu_ref[...], preferred_element_type=F32)
    comb = comb_ref[...]
    lane = lax.broadcasted_iota(jnp.int32, comb.shape, 1)
    parts = []
    for e in range(EXPERTS_PER_GROUP):
        gate = jnp.sum(jnp.where(lane == N_EXPERT_GROUPS + g * EXPERTS_PER_GROUP + e, comb, 0.0),
                       axis=-1, keepdims=True)
        hg = hgu[:, e * EXPERT_FF:(e + 1) * EXPERT_FF]
        hu = hgu[:, GROUP_FF + e * EXPERT_FF:GROUP_FF + (e + 1) * EXPERT_FF]
        parts.append((hg / (1.0 + jnp.exp(-hg)) * hu * gate).astype(BF16))
    a = jnp.concatenate(parts, axis=-1)
    o_ref[...] += jnp.dot(a, wd_ref[...], preferred_element_type=F32)


def _out_moe(x2d, gm, sb, params, *, tm):
    n = x2d.shape[0]
    wo, g2, wrh, wrl, br, wgu, wd = params
    row = lambda w: pl.BlockSpec((tm, w), lambda i, g: (i, 0))
    full = lambda a: pl.BlockSpec(a.shape, lambda i, g: (0,) * a.ndim)
    return pl.pallas_call(
        _out_moe_body,
        grid=(n // tm, N_EXPERT_GROUPS),
        in_specs=[row(D_MODEL), row(GM_WIDTH), row(SB_WIDTH),
                  full(wo), full(g2), full(wrh), full(wrl), full(br),
                  pl.BlockSpec((None, D_MODEL, 2 * GROUP_FF), lambda i, g: (g, 0, 0)),
                  pl.BlockSpec((None, GROUP_FF, D_MODEL), lambda i, g: (g, 0, 0))],
        out_specs=row(D_MODEL),
        out_shape=jax.ShapeDtypeStruct((n, D_MODEL), F32),
        scratch_shapes=[pltpu.VMEM((tm, D_MODEL), BF16), pltpu.VMEM((tm, LANES), F32)],
        compiler_params=pltpu.CompilerParams(
            dimension_semantics=("parallel", "arbitrary"), vmem_limit_bytes=VMEM_LIMIT),
        name="out_moe",
    )(x2d, gm, sb, *params)


def _layer(xp, xs, cache_k, cache_v, norm1_g, w_in, gm_v_norm_g, gm_w_s, gm_b_s, q_norm_g,
           k_norm_g, w_out, norm2_g, w_router_group, b_router_group, w_router_expert,
           b_router_expert, w_gate, w_up, w_down):
    b, s, d = xp.shape
    db, n, _ = xs.shape
    head_of = jnp.arange(SB_WIDTH) // SB_HEAD_DIM
    bd = jnp.where(head_of[:, None] == head_of[None, :], 1.0 / SB_HEAD_DIM, 0.0).astype(BF16)
    in_params = (norm1_g[None, :], w_in.astype(BF16), gm_v_norm_g[None, :],
                 jnp.tile(q_norm_g, SB_HEADS)[None, :], jnp.tile(k_norm_g, SB_HEADS)[None, :],
                 gm_w_s, gm_b_s.T, bd)

    wr = jnp.concatenate(
        [w_router_group, jnp.transpose(w_router_expert, (1, 0, 2)).reshape(d, N_EXPERTS)], axis=1)
    wr = jnp.pad(wr, ((0, 0), (0, LANES - wr.shape[1])))
    wr_hi = wr.astype(BF16)
    wr_lo = (wr - wr_hi.astype(F32)).astype(BF16)
    br = jnp.pad(jnp.concatenate([b_router_group, b_router_expert.reshape(-1)]),
                 (0, LANES - N_EXPERT_GROUPS - N_EXPERTS))[None, :]

    def group_cols(w):
        w = w.astype(BF16).reshape(N_EXPERT_GROUPS, EXPERTS_PER_GROUP, d, EXPERT_FF)
        return jnp.transpose(w, (0, 2, 1, 3)).reshape(N_EXPERT_GROUPS, d, GROUP_FF)

    wgu = jnp.concatenate([group_cols(w_gate), group_cols(w_up)], axis=-1)
    wd = w_down.astype(BF16).reshape(N_EXPERT_GROUPS, GROUP_FF, d)
    moe_params = (w_out.astype(BF16), norm2_g[None, :], wr_hi, wr_lo, br, wgu, wd)

    xp2 = xp.reshape(b * s, d)
    gm, qs, kb, vb, kf, vf = _mixer_in(xp2, in_params, tm=512, chunk=GM_CHUNK, want_gv=False)
    shp = (b, s, SB_WIDTH)
    sb = _prompt_attn(qs.reshape(shp), kb.reshape(shp), vb.reshape(shp))
    yp = _out_moe(xp2, gm, sb.reshape(b * s, SB_WIDTH), moe_params, tm=512).reshape(b, s, d)

    xs2 = xs.reshape(db * n, d)
    gm_s, qs_s, kb_s, vb_s, kf_s, vf_s, gv_s = _mixer_in(
        xs2, in_params, tm=db * n, chunk=n, want_gv=True)
    shs = (db, n, SB_WIDTH)
    sb_s = _sample_attn(qs_s.reshape(shs), kb_s.reshape(shs), vb_s.reshape(shs),
                        cache_k.reshape(db, -1, SB_WIDTH), cache_v.reshape(db, -1, SB_WIDTH))
    ys = _out_moe(xs2, gm_s, sb_s.reshape(db * n, SB_WIDTH), moe_params, tm=db * n).reshape(db, n, d)

    heads = (SB_HEADS, SB_HEAD_DIM)
    return (yp, ys, kf.reshape(b, s, *heads), vf.reshape(b, s, *heads),
            kf_s.reshape(db, n, *heads), vf_s.reshape(db, n, *heads),
            gv_s.reshape(db, n, GM_GROUPS, GM_GROUP_DIM))


def kernel(x_prompt, x_sample, cache_sb_k, cache_sb_v, norm1_g, w_in, gm_v_norm_g, gm_w_s, gm_b_s, q_norm_g, k_norm_g, w_out, norm2_g, w_router_group, b_router_group, w_router_expert, b_router_expert, w_gate, w_up, w_down):
    depth = w_in.shape[0]
    yp, ys = x_prompt, x_sample
    outs = [[] for _ in range(5)]
    for l in range(depth):
        yp, ys, *rest = _layer(
            yp, ys, cache_sb_k[l], cache_sb_v[l], norm1_g[l], w_in[l], gm_v_norm_g[l], gm_w_s[l],
            gm_b_s[l], q_norm_g[l], k_norm_g[l], w_out[l], norm2_g[l], w_router_group[l],
            b_router_group[l], w_router_expert[l], b_router_expert[l], w_gate[l], w_up[l],
            w_down[l])
        for acc, r in zip(outs, rest):
            acc.append(r)
    return (yp, ys) + tuple(jnp.stack(o, axis=0) for o in outs)
```

```python
import functools

import jax
import jax.numpy as jnp
from jax import lax
from jax.experimental import pallas as pl
from jax.experimental.pallas import tpu as pltpu

D_MODEL = 1024
GM_WIDTH = 512
GM_GROUPS = 4
GM_GROUP_DIM = 128
GM_CHUNK = 128
SB_WIDTH = 512
SB_HEAD_DIM = 64
SB_HEADS = 8
IN_WIDTH = 2 * GM_WIDTH + 3 * SB_WIDTH
N_EXPERT_GROUPS = 4
EXPERTS_PER_GROUP = 4
N_EXPERTS = 16
EXPERT_FF = 256
GROUP_FF = EXPERTS_PER_GROUP * EXPERT_FF
EPS = 1e-6

LANES = 128
HEAD_PAIRS = SB_WIDTH // LANES
SB_BLOCK = 128
CACHE_BLOCK = 256
VMEM_LIMIT = 56 * 1024 * 1024

LOG2_E = 1.4426950408889634
Q_SCALE = SB_HEAD_DIM ** -0.5 * LOG2_E
SB_UNDERFLOW_BITS = 160.0

F32 = jnp.float32
BF16 = jnp.bfloat16
_NT = (((1,), (1,)), ((), ()))


def _gelu_tanh(x):
    return 0.5 * x * (1.0 + jnp.tanh(0.7978845608028654 * (x + 0.044715 * (x * x * x))))


def _softplus2(z):
    return jnp.maximum(z, 0.0) + jnp.log2(1.0 + jnp.exp2(-jnp.abs(z)))


def _suffix_matrix(n):
    r = lax.broadcasted_iota(jnp.int32, (n, n), 0)
    c = lax.broadcasted_iota(jnp.int32, (n, n), 1)
    return jnp.where(r > c, 1.0, 0.0).astype(BF16)


def _mixer_in_body(x_ref, g1_ref, w_in_ref, gvg_ref, qg_ref, kg_ref, ws_ref, bst_ref, bd_ref,
                   gm_ref, kf_ref, vf_ref, *rest, chunk, stream):
    x = x_ref[...]
    tm = x.shape[0]
    xn = x * lax.rsqrt(jnp.mean(x * x, axis=-1, keepdims=True) + EPS) * g1_ref[...]
    proj = jnp.dot(xn.astype(BF16), w_in_ref[...], preferred_element_type=F32)

    gu = _gelu_tanh(proj[:, :GM_WIDTH])
    gvr = _gelu_tanh(proj[:, GM_WIDTH:2 * GM_WIDTH])
    gv = gvr * lax.rsqrt(jnp.mean(gvr * gvr, axis=-1, keepdims=True) + EPS) * gvg_ref[...]

    o = 2 * GM_WIDTH
    q = proj[:, o:o + SB_WIDTH]
    k = proj[:, o + SB_WIDTH:o + 2 * SB_WIDTH]
    v = proj[:, o + 2 * SB_WIDTH:]
    bd = bd_ref[...]
    qms = jnp.dot((q * q).astype(BF16), bd, preferred_element_type=F32)
    kms = jnp.dot((k * k).astype(BF16), bd, preferred_element_type=F32)
    qs = q * lax.rsqrt(qms + EPS) * qg_ref[...] * Q_SCALE
    kn = k * lax.rsqrt(kms + EPS) * kg_ref[...]
    for h in range(SB_HEADS):
        lanes = slice(h * SB_HEAD_DIM, (h + 1) * SB_HEAD_DIM)
        kf_ref[:, h, :] = kn[:, lanes]
        vf_ref[:, h, :] = v[:, lanes]
        if stream:
            rest[0][:, h, :] = qs[:, lanes]
    if stream:
        rest[1][...] = gv
    else:
        qs_ref, kb_ref, vb_ref = rest
        qs_ref[...] = qs.astype(BF16)
        kb_ref[...] = kn.astype(BF16)
        vb_ref[...] = v.astype(BF16)

    r = lax.broadcasted_iota(jnp.int32, (chunk, chunk), 0)
    c = lax.broadcasted_iota(jnp.int32, (chunk, chunk), 1)
    gvb = gv.astype(BF16)
    for g in range(GM_GROUPS):
        w = jnp.where(r >= c, ws_ref[g, :chunk, :chunk], 0.0).astype(BF16)
        b = bst_ref[:chunk, g:g + 1]
        lo = g * GM_GROUP_DIM
        for ci in range(tm // chunk):
            rows = slice(ci * chunk, (ci + 1) * chunk)
            mixed = jnp.dot(w, gvb[rows, lo:lo + GM_GROUP_DIM], preferred_element_type=F32) + b
            gm_ref[rows, lo:lo + GM_GROUP_DIM] = (gu[rows, lo:lo + GM_GROUP_DIM] * mixed).astype(BF16)


def _mixer_in(x2d, params, *, tm, chunk, stream):
    n = x2d.shape[0]
    full = lambda a: pl.BlockSpec(a.shape, lambda i: (0,) * a.ndim)
    heads = (n, SB_HEADS, SB_HEAD_DIM)
    out_shape = [
        jax.ShapeDtypeStruct((n, GM_WIDTH), BF16),
        jax.ShapeDtypeStruct(heads, F32),
        jax.ShapeDtypeStruct(heads, F32),
    ]
    if stream:
        out_shape += [jax.ShapeDtypeStruct(heads, F32),
                      jax.ShapeDtypeStruct((n, GM_WIDTH), F32)]
    else:
        out_shape += [jax.ShapeDtypeStruct((n, SB_WIDTH), BF16)] * 3
    spec = lambda s: pl.BlockSpec((tm,) + s.shape[1:], lambda i: (i,) + (0,) * (len(s.shape) - 1))
    return pl.pallas_call(
        functools.partial(_mixer_in_body, chunk=chunk, stream=stream),
        grid=(n // tm,),
        in_specs=[pl.BlockSpec((tm, D_MODEL), lambda i: (i, 0))] + [full(a) for a in params],
        out_specs=[spec(s) for s in out_shape],
        out_shape=out_shape,
        compiler_params=pltpu.CompilerParams(
            dimension_semantics=("parallel",), vmem_limit_bytes=VMEM_LIMIT),
        name="mixer_in_stream" if stream else "mixer_in",
    )(x2d, *params)


def _sb_step(qst, kj, vj, suffix, c, causal=None):
    z = lax.dot_general(qst, kj, _NT, preferred_element_type=F32)
    sp = _softplus2(z)
    if causal is not None:
        sp = jnp.where(causal, sp, 0.0)
    later = jnp.dot(sp.astype(BF16), suffix, preferred_element_type=F32) + c
    w = jnp.exp2(z - sp - later)
    if causal is not None:
        w = jnp.where(causal, w, 0.0)
    pv = jnp.dot(w.astype(BF16), vj, preferred_element_type=F32)
    return pv, c + jnp.sum(sp, axis=-1, keepdims=True)


def _prompt_attn_body(q_ref, k_ref, v_ref, o_ref, qst_ref, c_ref, acc_ref):
    i = pl.program_id(1)
    lane = lax.broadcasted_iota(jnp.int32, (SB_BLOCK, LANES), 1)
    first = lane < SB_HEAD_DIM
    for hp in range(HEAD_PAIRS):
        q = q_ref[:, hp * LANES:(hp + 1) * LANES]
        zero = jnp.zeros_like(q)
        qst_ref[hp, :SB_BLOCK, :] = jnp.where(first, q, zero)
        qst_ref[hp, SB_BLOCK:, :] = jnp.where(first, zero, q)
    suffix = _suffix_matrix(SB_BLOCK)
    m = 2 * SB_BLOCK

    def sweep(j, causal, fresh):
        rows = pl.ds(pl.multiple_of(j * SB_BLOCK, SB_BLOCK), SB_BLOCK)
        for hp in range(HEAD_PAIRS):
            cols = slice(hp * LANES, (hp + 1) * LANES)
            c0 = jnp.zeros((m, 1), F32) if fresh else c_ref[hp]
            pv, c = _sb_step(qst_ref[hp], k_ref[rows, cols], v_ref[rows, cols], suffix, c0, causal)
            acc_ref[hp] = pv if fresh else acc_ref[hp] + pv
            c_ref[hp] = c

    t = lax.broadcasted_iota(jnp.int32, (m, SB_BLOCK), 0) % SB_BLOCK
    s = lax.broadcasted_iota(jnp.int32, (m, SB_BLOCK), 1)
    sweep(i, s < t, True)

    def least_carry():
        c = c_ref[0]
        for hp in range(1, HEAD_PAIRS):
            c = jnp.minimum(c, c_ref[hp])
        return jnp.min(c)

    def more(state):
        jj, carry = state
        return (jj < i) & (carry < SB_UNDERFLOW_BITS)

    def body(state):
        jj, _ = state
        sweep(i - 1 - jj, None, False)
        return jj + 1, least_carry()

    lax.while_loop(more, body, (jnp.int32(0), least_carry()))
    for hp in range(HEAD_PAIRS):
        o_ref[:, hp * LANES:(hp + 1) * LANES] = jnp.where(
            first, acc_ref[hp, :SB_BLOCK, :], acc_ref[hp, SB_BLOCK:, :]).astype(o_ref.dtype)


def _prompt_attn(qs, kb, vb):
    b, s, _ = qs.shape
    q_spec = pl.BlockSpec((None, SB_BLOCK, SB_WIDTH), lambda bi, i: (bi, i, 0))
    kv_spec = pl.BlockSpec((None, s, SB_WIDTH), lambda bi, i: (bi, 0, 0))
    m = 2 * SB_BLOCK
    return pl.pallas_call(
        _prompt_attn_body,
        grid=(b, s // SB_BLOCK),
        in_specs=[q_spec, kv_spec, kv_spec],
        out_specs=q_spec,
        out_shape=jax.ShapeDtypeStruct((b, s, SB_WIDTH), BF16),
        scratch_shapes=[pltpu.VMEM((HEAD_PAIRS, m, LANES), BF16),
                        pltpu.VMEM((HEAD_PAIRS, m, 1), F32),
                        pltpu.VMEM((HEAD_PAIRS, m, LANES), F32)],
        compiler_params=pltpu.CompilerParams(
            dimension_semantics=("parallel", "arbitrary"), vmem_limit_bytes=VMEM_LIMIT),
        name="prompt_attn",
    )(qs, kb, vb)


def _sample_attn_body(q_ref, kn_ref, vn_ref, ck_hbm, cv_hbm, o_ref, kbuf, vbuf, sems, c_ref, acc_ref,
                      *, layer):
    b = pl.program_id(0)
    n = q_ref.shape[0]
    m = SB_HEADS * n
    nblk = ck_hbm.shape[2] // CACHE_BLOCK

    def copies(stream, blk, slot):
        rows = pl.ds(blk * CACHE_BLOCK, CACHE_BLOCK)
        return (pltpu.make_async_copy(ck_hbm.at[layer, stream, rows], kbuf.at[slot], sems.at[slot, 0]),
                pltpu.make_async_copy(cv_hbm.at[layer, stream, rows], vbuf.at[slot], sems.at[slot, 1]))

    def start(stream, blk, slot):
        for cp in copies(stream, blk, slot):
            cp.start()

    def wait(stream, blk, slot):
        for cp in copies(stream, blk, slot):
            cp.wait()

    def slot_of(jj):
        return jnp.where(jj == 0, b % 2, 2 + jj % 2)

    @pl.when(b == 0)
    def _():
        start(0, nblk - 1, 0)

    @pl.when(b + 1 < pl.num_programs(0))
    def _():
        start(b + 1, nblk - 1, (b + 1) % 2)

    q = [q_ref[:, h, :].astype(BF16) for h in range(SB_HEADS)]

    def block(k_of, v_of, suffix, causal, fresh):
        z = jnp.concatenate(
            [lax.dot_general(q[h], k_of(h), _NT, preferred_element_type=F32)
             for h in range(SB_HEADS)], axis=0)
        sp = _softplus2(z)
        if causal is not None:
            sp = jnp.where(causal, sp, 0.0)
        c0 = jnp.zeros((m, 1), F32) if fresh else c_ref[...]
        later = jnp.dot(sp.astype(BF16), suffix, preferred_element_type=F32) + c0
        w = jnp.exp2(z - sp - later)
        if causal is not None:
            w = jnp.where(causal, w, 0.0)
        wb = w.astype(BF16)
        for h in range(SB_HEADS):
            pv = jnp.dot(wb[h * n:(h + 1) * n, :], v_of(h), preferred_element_type=F32)
            acc_ref[h] = pv if fresh else acc_ref[h] + pv
        c_ref[...] = c0 + jnp.sum(sp, axis=-1, keepdims=True)

    pad = jnp.zeros((LANES - n, SB_HEAD_DIM), BF16)
    t = lax.broadcasted_iota(jnp.int32, (m, LANES), 0) % n
    s = lax.broadcasted_iota(jnp.int32, (m, LANES), 1)
    block(lambda h: jnp.concatenate([kn_ref[:, h, :].astype(BF16), pad], axis=0),
          lambda h: jnp.concatenate([vn_ref[:, h, :].astype(BF16), pad], axis=0),
          _suffix_matrix(LANES), s < t, True)

    suffix = _suffix_matrix(CACHE_BLOCK)

    def more(state):
        jj, carry = state
        return (jj < nblk) & (carry < SB_UNDERFLOW_BITS)

    def body(state):
        jj, _ = state
        blk = nblk - 1 - jj
        slot = slot_of(jj)
        wait(b, blk, slot)

        @pl.when(blk > 0)
        def _():
            start(b, blk - 1, slot_of(jj + 1))

        block(lambda h: kbuf[slot, :, h, :].astype(BF16), lambda h: vbuf[slot, :, h, :].astype(BF16),
              suffix, None, False)
        return jj + 1, jnp.min(c_ref[...])

    done, _ = lax.while_loop(more, body, (jnp.int32(0), jnp.min(c_ref[...])))

    @pl.when(done < nblk)
    def _():
        wait(b, nblk - 1 - done, slot_of(done))

    for h in range(SB_HEADS):
        o_ref[:, h, :] = acc_ref[h]


def _sample_attn(q4, kn4, vn4, cache_k, cache_v, layer):
    db, n = q4.shape[:2]
    m = SB_HEADS * n
    new_spec = pl.BlockSpec((None, n, SB_HEADS, SB_HEAD_DIM), lambda b: (b, 0, 0, 0))
    hbm = pl.BlockSpec(memory_space=pl.ANY)
    buf = pltpu.VMEM((4, CACHE_BLOCK, SB_HEADS, SB_HEAD_DIM), F32)
    return pl.pallas_call(
        functools.partial(_sample_attn_body, layer=layer),
        grid=(db,),
        in_specs=[new_spec, new_spec, new_spec, hbm, hbm],
        out_specs=new_spec,
        out_shape=jax.ShapeDtypeStruct(q4.shape, F32),
        scratch_shapes=[buf, buf, pltpu.SemaphoreType.DMA((4, 2)),
                        pltpu.VMEM((m, 1), F32),
                        pltpu.VMEM((SB_HEADS, n, SB_HEAD_DIM), F32)],
        compiler_params=pltpu.CompilerParams(
            dimension_semantics=("arbitrary",), vmem_limit_bytes=VMEM_LIMIT),
        name="sample_attn",
    )(q4, kn4, vn4, cache_k, cache_v)


def _route(logits):
    lane = lax.broadcasted_iota(jnp.int32, logits.shape, 1)
    neg = jnp.float32(-jnp.inf)
    big = jnp.int32(1 << 20)
    rmax = lambda a: jnp.max(a, axis=-1, keepdims=True)
    rmin = lambda a: jnp.min(a, axis=-1, keepdims=True)
    rsum = lambda a: jnp.sum(a, axis=-1, keepdims=True)

    gmask = lane < N_EXPERT_GROUPS
    gl = jnp.where(gmask, logits, neg)
    gmax = rmax(gl)
    g_sel = rmin(jnp.where(gl == gmax, lane, big))
    p_sel = 1.0 / rsum(jnp.where(gmask, jnp.exp(gl - gmax), 0.0))

    e_lo = N_EXPERT_GROUPS + g_sel * EXPERTS_PER_GROUP
    emask = (lane >= e_lo) & (lane < e_lo + EXPERTS_PER_GROUP)
    el = jnp.where(emask, logits, neg)
    emax = rmax(el)
    ex = jnp.where(emask, jnp.exp(el - emax), 0.0)
    p_e = ex / rsum(ex)
    pm = jnp.where(emask, p_e, -1.0)
    v1 = rmax(pm)
    i1 = rmin(jnp.where(pm == v1, lane, big))
    pm2 = jnp.where(lane == i1, -1.0, pm)
    v2 = rmax(pm2)
    i2 = rmin(jnp.where(pm2 == v2, lane, big))
    tot = v1 + v2
    return (jnp.where(lane == i1, p_sel * v1 / tot, 0.0)
            + jnp.where(lane == i2, p_sel * v2 / tot, 0.0))


def _out_moe_body(x_ref, gm_ref, sb_ref, wo_ref, g2_ref, wrh_ref, wrl_ref, br_ref,
                  wgu_ref, wd_ref, o_ref, hn_ref, comb_ref):
    g = pl.program_id(1)

    @pl.when(g == 0)
    def _():
        h = (x_ref[...]
             + jnp.dot(gm_ref[...], wo_ref[:GM_WIDTH, :], preferred_element_type=F32)
             + jnp.dot(sb_ref[...], wo_ref[GM_WIDTH:, :], preferred_element_type=F32))
        o_ref[...] = h
        hn = h * lax.rsqrt(jnp.mean(h * h, axis=-1, keepdims=True) + EPS) * g2_ref[...]
        hi = hn.astype(BF16)
        hn_ref[...] = hi
        lo = (hn - hi.astype(F32)).astype(BF16)
        logits = (jnp.dot(hi, wrh_ref[...], preferred_element_type=F32)
                  + jnp.dot(lo, wrh_ref[...], preferred_element_type=F32)
                  + jnp.dot(hi, wrl_ref[...], preferred_element_type=F32)) + br_ref[...]
        comb_ref[...] = _route(logits)

    hgu = jnp.dot(hn_ref[...], wgu_ref[...], preferred_element_type=F32)
    comb = comb_ref[...]
    lane = lax.broadcasted_iota(jnp.int32, comb.shape, 1)
    parts = []
    for e in range(EXPERTS_PER_GROUP):
        gate = jnp.sum(jnp.where(lane == N_EXPERT_GROUPS + g * EXPERTS_PER_GROUP + e, comb, 0.0),
                       axis=-1, keepdims=True)
        hg = hgu[:, e * EXPERT_FF:(e + 1) * EXPERT_FF]
        hu = hgu[:, GROUP_FF + e * EXPERT_FF:GROUP_FF + (e + 1) * EXPERT_FF]
        parts.append((hg / (1.0 + jnp.exp(-hg)) * hu * gate).astype(BF16))
    a = jnp.concatenate(parts, axis=-1)
    o_ref[...] += jnp.dot(a, wd_ref[...], preferred_element_type=F32)


def _out_moe(x2d, gm, sb, params, *, tm):
    n = x2d.shape[0]
    wo, g2, wrh, wrl, br, wgu, wd = params
    row = lambda w: pl.BlockSpec((tm, w), lambda i, g: (i, 0))
    full = lambda a: pl.BlockSpec(a.shape, lambda i, g: (0,) * a.ndim)
    return pl.pallas_call(
        _out_moe_body,
        grid=(n // tm, N_EXPERT_GROUPS),
        in_specs=[row(D_MODEL), row(GM_WIDTH), row(SB_WIDTH),
                  full(wo), full(g2), full(wrh), full(wrl), full(br),
                  pl.BlockSpec((None, D_MODEL, 2 * GROUP_FF), lambda i, g: (g, 0, 0)),
                  pl.BlockSpec((None, GROUP_FF, D_MODEL), lambda i, g: (g, 0, 0))],
        out_specs=row(D_MODEL),
        out_shape=jax.ShapeDtypeStruct((n, D_MODEL), F32),
        scratch_shapes=[pltpu.VMEM((tm, D_MODEL), BF16), pltpu.VMEM((tm, LANES), F32)],
        compiler_params=pltpu.CompilerParams(
            dimension_semantics=("parallel", "arbitrary"), vmem_limit_bytes=VMEM_LIMIT),
        name="out_moe",
    )(x2d, gm, sb, *params)


def _layer(layer, xp, xs, cache_k, cache_v, norm1_g, w_in, gm_v_norm_g, gm_w_s, gm_b_s, q_norm_g,
           k_norm_g, w_out, norm2_g, w_router_group, b_router_group, w_router_expert,
           b_router_expert, w_gate, w_up, w_down):
    b, s, d = xp.shape
    db, n, _ = xs.shape
    head_of = jnp.arange(SB_WIDTH) // SB_HEAD_DIM
    bd = jnp.where(head_of[:, None] == head_of[None, :], 1.0 / SB_HEAD_DIM, 0.0).astype(BF16)
    in_params = (norm1_g[None, :], w_in.astype(BF16), gm_v_norm_g[None, :],
                 jnp.tile(q_norm_g, SB_HEADS)[None, :], jnp.tile(k_norm_g, SB_HEADS)[None, :],
                 gm_w_s, gm_b_s.T, bd)

    wr = jnp.concatenate(
        [w_router_group, jnp.transpose(w_router_expert, (1, 0, 2)).reshape(d, N_EXPERTS)], axis=1)
    wr = jnp.pad(wr, ((0, 0), (0, LANES - wr.shape[1])))
    wr_hi = wr.astype(BF16)
    wr_lo = (wr - wr_hi.astype(F32)).astype(BF16)
    br = jnp.pad(jnp.concatenate([b_router_group, b_router_expert.reshape(-1)]),
                 (0, LANES - N_EXPERT_GROUPS - N_EXPERTS))[None, :]

    def group_cols(w):
        w = w.astype(BF16).reshape(N_EXPERT_GROUPS, EXPERTS_PER_GROUP, d, EXPERT_FF)
        return jnp.transpose(w, (0, 2, 1, 3)).reshape(N_EXPERT_GROUPS, d, GROUP_FF)

    wgu = jnp.concatenate([group_cols(w_gate), group_cols(w_up)], axis=-1)
    wd = w_down.astype(BF16).reshape(N_EXPERT_GROUPS, GROUP_FF, d)
    moe_params = (w_out.astype(BF16), norm2_g[None, :], wr_hi, wr_lo, br, wgu, wd)

    xp2 = xp.reshape(b * s, d)
    gm, kf, vf, qs, kb, vb = _mixer_in(xp2, in_params, tm=512, chunk=GM_CHUNK, stream=False)
    shp = (b, s, SB_WIDTH)
    sb = _prompt_attn(qs.reshape(shp), kb.reshape(shp), vb.reshape(shp))
    yp = _out_moe(xp2, gm, sb.reshape(b * s, SB_WIDTH), moe_params, tm=512).reshape(b, s, d)

    xs2 = xs.reshape(db * n, d)
    gm_s, kf_s, vf_s, q_s, gv_s = _mixer_in(xs2, in_params, tm=db * n, chunk=n, stream=True)
    shs = (db, n, SB_HEADS, SB_HEAD_DIM)
    sb_s = _sample_attn(q_s.reshape(shs), kf_s.reshape(shs), vf_s.reshape(shs), cache_k, cache_v, layer)
    sb_s = sb_s.reshape(db * n, SB_WIDTH).astype(BF16)
    ys = _out_moe(xs2, gm_s, sb_s, moe_params, tm=db * n).reshape(db, n, d)

    heads = (SB_HEADS, SB_HEAD_DIM)
    return (yp, ys, kf.reshape(b, s, *heads), vf.reshape(b, s, *heads),
            kf_s.reshape(db, n, *heads), vf_s.reshape(db, n, *heads),
            gv_s.reshape(db, n, GM_GROUPS, GM_GROUP_DIM))


def kernel(x_prompt, x_sample, cache_sb_k, cache_sb_v, norm1_g, w_in, gm_v_norm_g, gm_w_s, gm_b_s, q_norm_g, k_norm_g, w_out, norm2_g, w_router_group, b_router_group, w_router_expert, b_router_expert, w_gate, w_up, w_down):
    depth = w_in.shape[0]
    yp, ys = x_prompt, x_sample
    outs = [[] for _ in range(5)]
    for l in range(depth):
        yp, ys, *rest = _layer(
            l, yp, ys, cache_sb_k, cache_sb_v, norm1_g[l], w_in[l], gm_v_norm_g[l], gm_w_s[l],
            gm_b_s[l], q_norm_g[l], k_norm_g[l], w_out[l], norm2_g[l], w_router_group[l],
            b_router_group[l], w_router_expert[l], b_router_expert[l], w_gate[l], w_up[l],
            w_down[l])
        for acc, r in zip(outs, rest):
            acc.append(r)
    return (yp, ys) + tuple(jnp.stack(o, axis=0) for o in outs)
```

```python
import functools

import jax
import jax.numpy as jnp
from jax import lax
from jax.experimental import pallas as pl
from jax.experimental.pallas import tpu as pltpu

D_MODEL = 1024
GM_WIDTH = 512
GM_GROUPS = 4
GM_GROUP_DIM = 128
GM_CHUNK = 128
SB_WIDTH = 512
SB_HEAD_DIM = 64
SB_HEADS = 8
IN_WIDTH = 2 * GM_WIDTH + 3 * SB_WIDTH
N_EXPERT_GROUPS = 4
EXPERTS_PER_GROUP = 4
N_EXPERTS = 16
EXPERT_FF = 256
GROUP_FF = EXPERTS_PER_GROUP * EXPERT_FF
EPS = 1e-6

LANES = 128
HEAD_PAIRS = SB_WIDTH // LANES
SB_BLOCK = 128
CACHE_BLOCK = 256
VMEM_LIMIT = 56 * 1024 * 1024

LOG2_E = 1.4426950408889634
Q_SCALE = SB_HEAD_DIM ** -0.5 * LOG2_E
SB_UNDERFLOW_BITS = 160.0

F32 = jnp.float32
BF16 = jnp.bfloat16
_NT = (((1,), (1,)), ((), ()))


def _gelu_tanh(x):
    return 0.5 * x * (1.0 + jnp.tanh(0.7978845608028654 * (x + 0.044715 * (x * x * x))))


def _softplus2(z):
    return jnp.maximum(z, 0.0) + jnp.log2(1.0 + jnp.exp2(-jnp.abs(z)))


def _suffix_matrix(n):
    r = lax.broadcasted_iota(jnp.int32, (n, n), 0)
    c = lax.broadcasted_iota(jnp.int32, (n, n), 1)
    return jnp.where(r > c, 1.0, 0.0).astype(BF16)


def _mixer_in_body(x_ref, g1_ref, w_in_ref, gvg_ref, qg_ref, kg_ref, ws_ref, bst_ref, bd_ref,
                   gm_ref, kf_ref, vf_ref, *rest, chunk, stream):
    x = x_ref[...]
    tm = x.shape[0]
    xn = x * lax.rsqrt(jnp.mean(x * x, axis=-1, keepdims=True) + EPS) * g1_ref[...]
    proj = jnp.dot(xn.astype(BF16), w_in_ref[...], preferred_element_type=F32)

    gu = _gelu_tanh(proj[:, :GM_WIDTH])
    gvr = _gelu_tanh(proj[:, GM_WIDTH:2 * GM_WIDTH])
    gv = gvr * lax.rsqrt(jnp.mean(gvr * gvr, axis=-1, keepdims=True) + EPS) * gvg_ref[...]

    o = 2 * GM_WIDTH
    q = proj[:, o:o + SB_WIDTH]
    k = proj[:, o + SB_WIDTH:o + 2 * SB_WIDTH]
    v = proj[:, o + 2 * SB_WIDTH:]
    bd = bd_ref[...]
    qms = jnp.dot((q * q).astype(BF16), bd, preferred_element_type=F32)
    kms = jnp.dot((k * k).astype(BF16), bd, preferred_element_type=F32)
    qs = q * lax.rsqrt(qms + EPS) * qg_ref[...] * Q_SCALE
    kn = k * lax.rsqrt(kms + EPS) * kg_ref[...]
    if stream:
        for h in range(SB_HEADS):
            lanes = slice(h * SB_HEAD_DIM, (h + 1) * SB_HEAD_DIM)
            kf_ref[:, h, :] = kn[:, lanes]
            vf_ref[:, h, :] = v[:, lanes]
            rest[0][:, h, :] = qs[:, lanes]
        rest[1][...] = gv
    else:
        kf_ref[...] = kn.T
        vf_ref[...] = v.T
        qs_ref, kb_ref, vb_ref = rest
        qs_ref[...] = qs.astype(BF16)
        kb_ref[...] = kn.astype(BF16)
        vb_ref[...] = v.astype(BF16)

    r = lax.broadcasted_iota(jnp.int32, (chunk, chunk), 0)
    c = lax.broadcasted_iota(jnp.int32, (chunk, chunk), 1)
    gvb = gv.astype(BF16)
    for g in range(GM_GROUPS):
        w = jnp.where(r >= c, ws_ref[g, :chunk, :chunk], 0.0).astype(BF16)
        b = bst_ref[:chunk, g:g + 1]
        lo = g * GM_GROUP_DIM
        for ci in range(tm // chunk):
            rows = slice(ci * chunk, (ci + 1) * chunk)
            mixed = jnp.dot(w, gvb[rows, lo:lo + GM_GROUP_DIM], preferred_element_type=F32) + b
            gm_ref[rows, lo:lo + GM_GROUP_DIM] = (gu[rows, lo:lo + GM_GROUP_DIM] * mixed).astype(BF16)


def _mixer_in(x2d, params, *, tm, chunk, stream, seq=None):
    n = x2d.shape[0]
    full = lambda a: pl.BlockSpec(a.shape, lambda i: (0,) * a.ndim)
    spec = lambda s: pl.BlockSpec((tm,) + s.shape[1:], lambda i: (i,) + (0,) * (len(s.shape) - 1))
    gm = jax.ShapeDtypeStruct((n, GM_WIDTH), BF16)
    if stream:
        heads = jax.ShapeDtypeStruct((n, SB_HEADS, SB_HEAD_DIM), F32)
        out_shape = [gm, heads, heads,
                     heads,
                     jax.ShapeDtypeStruct((n, GM_WIDTH), F32)]
        out_specs = [spec(s) for s in out_shape]
    else:
        per_seq = seq // tm
        rows_t = jax.ShapeDtypeStruct((n // seq, SB_WIDTH, seq), F32)
        spec_t = pl.BlockSpec((None, SB_WIDTH, tm), lambda i: (i // per_seq, 0, i % per_seq))
        dense = jax.ShapeDtypeStruct((n, SB_WIDTH), BF16)
        out_shape = [gm, rows_t, rows_t, dense, dense, dense]
        out_specs = [spec(gm), spec_t, spec_t] + [spec(dense)] * 3
    return pl.pallas_call(
        functools.partial(_mixer_in_body, chunk=chunk, stream=stream),
        grid=(n // tm,),
        in_specs=[pl.BlockSpec((tm, D_MODEL), lambda i: (i, 0))] + [full(a) for a in params],
        out_specs=out_specs,
        out_shape=out_shape,
        compiler_params=pltpu.CompilerParams(
            dimension_semantics=("parallel",), vmem_limit_bytes=VMEM_LIMIT),
        name="mixer_in_stream" if stream else "mixer_in",
    )(x2d, *params)


def _sb_step(qst, kj, vj, suffix, c, causal=None):
    z = lax.dot_general(qst, kj, _NT, preferred_element_type=F32)
    sp = _softplus2(z)
    if causal is not None:
        sp = jnp.where(causal, sp, 0.0)
    later = jnp.dot(sp.astype(BF16), suffix, preferred_element_type=F32) + c
    w = jnp.exp2(z - sp - later)
    if causal is not None:
        w = jnp.where(causal, w, 0.0)
    pv = jnp.dot(w.astype(BF16), vj, preferred_element_type=F32)
    return pv, c + jnp.sum(sp, axis=-1, keepdims=True)


def _prompt_attn_body(q_ref, k_ref, v_ref, o_ref, qst_ref, c_ref, acc_ref):
    i = pl.program_id(1)
    lane = lax.broadcasted_iota(jnp.int32, (SB_BLOCK, LANES), 1)
    first = lane < SB_HEAD_DIM
    for hp in range(HEAD_PAIRS):
        q = q_ref[:, hp * LANES:(hp + 1) * LANES]
        zero = jnp.zeros_like(q)
        qst_ref[hp, :SB_BLOCK, :] = jnp.where(first, q, zero)
        qst_ref[hp, SB_BLOCK:, :] = jnp.where(first, zero, q)
    suffix = _suffix_matrix(SB_BLOCK)
    m = 2 * SB_BLOCK

    def sweep(j, causal, fresh):
        rows = pl.ds(pl.multiple_of(j * SB_BLOCK, SB_BLOCK), SB_BLOCK)
        for hp in range(HEAD_PAIRS):
            cols = slice(hp * LANES, (hp + 1) * LANES)
            c0 = jnp.zeros((m, 1), F32) if fresh else c_ref[hp]
            pv, c = _sb_step(qst_ref[hp], k_ref[rows, cols], v_ref[rows, cols], suffix, c0, causal)
            acc_ref[hp] = pv if fresh else acc_ref[hp] + pv
            c_ref[hp] = c

    t = lax.broadcasted_iota(jnp.int32, (m, SB_BLOCK), 0) % SB_BLOCK
    s = lax.broadcasted_iota(jnp.int32, (m, SB_BLOCK), 1)
    sweep(i, s < t, True)

    def least_carry():
        c = c_ref[0]
        for hp in range(1, HEAD_PAIRS):
            c = jnp.minimum(c, c_ref[hp])
        return jnp.min(c)

    def more(state):
        jj, carry = state
        return (jj < i) & (carry < SB_UNDERFLOW_BITS)

    def body(state):
        jj, _ = state
        sweep(i - 1 - jj, None, False)
        return jj + 1, least_carry()

    lax.while_loop(more, body, (jnp.int32(0), least_carry()))
    for hp in range(HEAD_PAIRS):
        o_ref[:, hp * LANES:(hp + 1) * LANES] = jnp.where(
            first, acc_ref[hp, :SB_BLOCK, :], acc_ref[hp, SB_BLOCK:, :]).astype(o_ref.dtype)


def _prompt_attn(qs, kb, vb):
    b, s, _ = qs.shape
    q_spec = pl.BlockSpec((None, SB_BLOCK, SB_WIDTH), lambda bi, i: (bi, i, 0))
    kv_spec = pl.BlockSpec((None, s, SB_WIDTH), lambda bi, i: (bi, 0, 0))
    m = 2 * SB_BLOCK
    return pl.pallas_call(
        _prompt_attn_body,
        grid=(b, s // SB_BLOCK),
        in_specs=[q_spec, kv_spec, kv_spec],
        out_specs=q_spec,
        out_shape=jax.ShapeDtypeStruct((b, s, SB_WIDTH), BF16),
        scratch_shapes=[pltpu.VMEM((HEAD_PAIRS, m, LANES), BF16),
                        pltpu.VMEM((HEAD_PAIRS, m, 1), F32),
                        pltpu.VMEM((HEAD_PAIRS, m, LANES), F32)],
        compiler_params=pltpu.CompilerParams(
            dimension_semantics=("parallel", "arbitrary"), vmem_limit_bytes=VMEM_LIMIT),
        name="prompt_attn",
    )(qs, kb, vb)


def _sample_attn_body(q_ref, kn_ref, vn_ref, ck_hbm, cv_hbm, o_ref, kbuf, vbuf, sems, c_ref, acc_ref,
                      *, layer):
    b = pl.program_id(0)
    n = q_ref.shape[0]
    m = SB_HEADS * n
    nblk = ck_hbm.shape[-1] // CACHE_BLOCK

    def copies(stream, blk, slot):
        cols = pl.ds(blk * CACHE_BLOCK, CACHE_BLOCK)
        return (pltpu.make_async_copy(ck_hbm.at[layer, stream, :, :, cols], kbuf.at[slot], sems.at[slot, 0]),
                pltpu.make_async_copy(cv_hbm.at[layer, stream, :, :, cols], vbuf.at[slot], sems.at[slot, 1]))

    def start(stream, blk, slot):
        for cp in copies(stream, blk, slot):
            cp.start()

    def wait(stream, blk, slot):
        for cp in copies(stream, blk, slot):
            cp.wait()

    def slot_of(jj):
        return jnp.where(jj == 0, b % 2, 2 + jj % 2)

    @pl.when(b == 0)
    def _():
        start(0, nblk - 1, 0)

    @pl.when(b + 1 < pl.num_programs(0))
    def _():
        start(b + 1, nblk - 1, (b + 1) % 2)

    q = [q_ref[:, h, :].astype(BF16) for h in range(SB_HEADS)]

    def block(k_of, v_of, suffix, causal, fresh, dim_major):
        qk = lambda a, k: (jnp.dot(a, k, preferred_element_type=F32) if dim_major
                           else lax.dot_general(a, k, _NT, preferred_element_type=F32))
        wv = lambda a, v: (lax.dot_general(a, v, _NT, preferred_element_type=F32) if dim_major
                           else jnp.dot(a, v, preferred_element_type=F32))
        z = jnp.concatenate([qk(q[h], k_of(h)) for h in range(SB_HEADS)], axis=0)
        sp = _softplus2(z)
        if causal is not None:
            sp = jnp.where(causal, sp, 0.0)
        c0 = jnp.zeros((m, 1), F32) if fresh else c_ref[...]
        later = jnp.dot(sp.astype(BF16), suffix, preferred_element_type=F32) + c0
        w = jnp.exp2(z - sp - later)
        if causal is not None:
            w = jnp.where(causal, w, 0.0)
        wb = w.astype(BF16)
        for h in range(SB_HEADS):
            pv = wv(wb[h * n:(h + 1) * n, :], v_of(h))
            acc_ref[h] = pv if fresh else acc_ref[h] + pv
        c_ref[...] = c0 + jnp.sum(sp, axis=-1, keepdims=True)

    pad = jnp.zeros((LANES - n, SB_HEAD_DIM), BF16)
    t = lax.broadcasted_iota(jnp.int32, (m, LANES), 0) % n
    s = lax.broadcasted_iota(jnp.int32, (m, LANES), 1)
    block(lambda h: jnp.concatenate([kn_ref[:, h, :].astype(BF16), pad], axis=0),
          lambda h: jnp.concatenate([vn_ref[:, h, :].astype(BF16), pad], axis=0),
          _suffix_matrix(LANES), s < t, True, False)

    suffix = _suffix_matrix(CACHE_BLOCK)

    def more(state):
        jj, carry = state
        return (jj < nblk) & (carry < SB_UNDERFLOW_BITS)

    def body(state):
        jj, _ = state
        blk = nblk - 1 - jj
        slot = slot_of(jj)
        wait(b, blk, slot)

        @pl.when(blk > 0)
        def _():
            start(b, blk - 1, slot_of(jj + 1))

        block(lambda h: kbuf[slot, h].astype(BF16), lambda h: vbuf[slot, h].astype(BF16),
              suffix, None, False, True)
        return jj + 1, jnp.min(c_ref[...])

    done, _ = lax.while_loop(more, body, (jnp.int32(0), jnp.min(c_ref[...])))

    @pl.when(done < nblk)
    def _():
        wait(b, nblk - 1 - done, slot_of(done))

    for h in range(SB_HEADS):
        o_ref[:, h, :] = acc_ref[h]


def _sample_attn(q4, kn4, vn4, cache_k, cache_v, layer):
    db, n = q4.shape[:2]
    m = SB_HEADS * n
    new_spec = pl.BlockSpec((None, n, SB_HEADS, SB_HEAD_DIM), lambda b: (b, 0, 0, 0))
    hbm = pl.BlockSpec(memory_space=pl.ANY)
    buf = pltpu.VMEM((4, SB_HEADS, SB_HEAD_DIM, CACHE_BLOCK), F32)
    return pl.pallas_call(
        functools.partial(_sample_attn_body, layer=layer),
        grid=(db,),
        in_specs=[new_spec, new_spec, new_spec, hbm, hbm],
        out_specs=new_spec,
        out_shape=jax.ShapeDtypeStruct(q4.shape, F32),
        scratch_shapes=[buf, buf, pltpu.SemaphoreType.DMA((4, 2)),
                        pltpu.VMEM((m, 1), F32),
                        pltpu.VMEM((SB_HEADS, n, SB_HEAD_DIM), F32)],
        compiler_params=pltpu.CompilerParams(
            dimension_semantics=("arbitrary",), vmem_limit_bytes=VMEM_LIMIT),
        name="sample_attn",
    )(q4, kn4, vn4, cache_k, cache_v)


def _route(logits):
    lane = lax.broadcasted_iota(jnp.int32, logits.shape, 1)
    neg = jnp.float32(-jnp.inf)
    big = jnp.int32(1 << 20)
    rmax = lambda a: jnp.max(a, axis=-1, keepdims=True)
    rmin = lambda a: jnp.min(a, axis=-1, keepdims=True)
    rsum = lambda a: jnp.sum(a, axis=-1, keepdims=True)

    gmask = lane < N_EXPERT_GROUPS
    gl = jnp.where(gmask, logits, neg)
    gmax = rmax(gl)
    g_sel = rmin(jnp.where(gl == gmax, lane, big))
    p_sel = 1.0 / rsum(jnp.where(gmask, jnp.exp(gl - gmax), 0.0))

    e_lo = N_EXPERT_GROUPS + g_sel * EXPERTS_PER_GROUP
    emask = (lane >= e_lo) & (lane < e_lo + EXPERTS_PER_GROUP)
    el = jnp.where(emask, logits, neg)
    emax = rmax(el)
    ex = jnp.where(emask, jnp.exp(el - emax), 0.0)
    p_e = ex / rsum(ex)
    pm = jnp.where(emask, p_e, -1.0)
    v1 = rmax(pm)
    i1 = rmin(jnp.where(pm == v1, lane, big))
    pm2 = jnp.where(lane == i1, -1.0, pm)
    v2 = rmax(pm2)
    i2 = rmin(jnp.where(pm2 == v2, lane, big))
    tot = v1 + v2
    return (jnp.where(lane == i1, p_sel * v1 / tot, 0.0)
            + jnp.where(lane == i2, p_sel * v2 / tot, 0.0))


def _out_moe_body(x_ref, gm_ref, sb_ref, wo_ref, g2_ref, wrh_ref, wrl_ref, br_ref,
                  wgu_ref, wd_ref, o_ref, hn_ref, comb_ref):
    g = pl.program_id(1)

    @pl.when(g == 0)
    def _():
        h = (x_ref[...]
             + jnp.dot(gm_ref[...], wo_ref[:GM_WIDTH, :], preferred_element_type=F32)
             + jnp.dot(sb_ref[...], wo_ref[GM_WIDTH:, :], preferred_element_type=F32))
        o_ref[...] = h
        hn = h * lax.rsqrt(jnp.mean(h * h, axis=-1, keepdims=True) + EPS) * g2_ref[...]
        hi = hn.astype(BF16)
        hn_ref[...] = hi
        lo = (hn - hi.astype(F32)).astype(BF16)
        logits = (jnp.dot(hi, wrh_ref[...], preferred_element_type=F32)
                  + jnp.dot(lo, wrh_ref[...], preferred_element_type=F32)
                  + jnp.dot(hi, wrl_ref[...], preferred_element_type=F32)) + br_ref[...]
        comb_ref[...] = _route(logits)

    hgu = jnp.dot(hn_ref[...], wgu_ref[...], preferred_element_type=F32)
    comb = comb_ref[...]
    lane = lax.broadcasted_iota(jnp.int32, comb.shape, 1)
    parts = []
    for e in range(EXPERTS_PER_GROUP):
        gate = jnp.sum(jnp.where(lane == N_EXPERT_GROUPS + g * EXPERTS_PER_GROUP + e, comb, 0.0),
                       axis=-1, keepdims=True)
        hg = hgu[:, e * EXPERT_FF:(e + 1) * EXPERT_FF]
        hu = hgu[:, GROUP_FF + e * EXPERT_FF:GROUP_FF + (e + 1) * EXPERT_FF]
        parts.append((hg / (1.0 + jnp.exp(-hg)) * hu * gate).astype(BF16))
    a = jnp.concatenate(parts, axis=-1)
    o_ref[...] += jnp.dot(a, wd_ref[...], preferred_element_type=F32)


def _out_moe(x2d, gm, sb, params, *, tm):
    n = x2d.shape[0]
    wo, g2, wrh, wrl, br, wgu, wd = params
    row = lambda w: pl.BlockSpec((tm, w), lambda i, g: (i, 0))
    full = lambda a: pl.BlockSpec(a.shape, lambda i, g: (0,) * a.ndim)
    return pl.pallas_call(
        _out_moe_body,
        grid=(n // tm, N_EXPERT_GROUPS),
        in_specs=[row(D_MODEL), row(GM_WIDTH), row(SB_WIDTH),
                  full(wo), full(g2), full(wrh), full(wrl), full(br),
                  pl.BlockSpec((None, D_MODEL, 2 * GROUP_FF), lambda i, g: (g, 0, 0)),
                  pl.BlockSpec((None, GROUP_FF, D_MODEL), lambda i, g: (g, 0, 0))],
        out_specs=row(D_MODEL),
        out_shape=jax.ShapeDtypeStruct((n, D_MODEL), F32),
        scratch_shapes=[pltpu.VMEM((tm, D_MODEL), BF16), pltpu.VMEM((tm, LANES), F32)],
        compiler_params=pltpu.CompilerParams(
            dimension_semantics=("parallel", "arbitrary"), vmem_limit_bytes=VMEM_LIMIT),
        name="out_moe",
    )(x2d, gm, sb, *params)


def _layer(layer, xp, xs, cache_k, cache_v, norm1_g, w_in, gm_v_norm_g, gm_w_s, gm_b_s, q_norm_g,
           k_norm_g, w_out, norm2_g, w_router_group, b_router_group, w_router_expert,
           b_router_expert, w_gate, w_up, w_down):
    b, s, d = xp.shape
    db, n, _ = xs.shape
    head_of = jnp.arange(SB_WIDTH) // SB_HEAD_DIM
    bd = jnp.where(head_of[:, None] == head_of[None, :], 1.0 / SB_HEAD_DIM, 0.0).astype(BF16)
    in_params = (norm1_g[None, :], w_in.astype(BF16), gm_v_norm_g[None, :],
                 jnp.tile(q_norm_g, SB_HEADS)[None, :], jnp.tile(k_norm_g, SB_HEADS)[None, :],
                 gm_w_s, gm_b_s.T, bd)

    wr = jnp.concatenate(
        [w_router_group, jnp.transpose(w_router_expert, (1, 0, 2)).reshape(d, N_EXPERTS)], axis=1)
    wr = jnp.pad(wr, ((0, 0), (0, LANES - wr.shape[1])))
    wr_hi = wr.astype(BF16)
    wr_lo = (wr - wr_hi.astype(F32)).astype(BF16)
    br = jnp.pad(jnp.concatenate([b_router_group, b_router_expert.reshape(-1)]),
                 (0, LANES - N_EXPERT_GROUPS - N_EXPERTS))[None, :]

    def group_cols(w):
        w = w.astype(BF16).reshape(N_EXPERT_GROUPS, EXPERTS_PER_GROUP, d, EXPERT_FF)
        return jnp.transpose(w, (0, 2, 1, 3)).reshape(N_EXPERT_GROUPS, d, GROUP_FF)

    wgu = jnp.concatenate([group_cols(w_gate), group_cols(w_up)], axis=-1)
    wd = w_down.astype(BF16).reshape(N_EXPERT_GROUPS, GROUP_FF, d)
    moe_params = (w_out.astype(BF16), norm2_g[None, :], wr_hi, wr_lo, br, wgu, wd)

    xp2 = xp.reshape(b * s, d)
    gm, kf, vf, qs, kb, vb = _mixer_in(xp2, in_params, tm=512, chunk=GM_CHUNK, stream=False, seq=s)
    shp = (b, s, SB_WIDTH)
    sb = _prompt_attn(qs.reshape(shp), kb.reshape(shp), vb.reshape(shp))
    yp = _out_moe(xp2, gm, sb.reshape(b * s, SB_WIDTH), moe_params, tm=512).reshape(b, s, d)

    xs2 = xs.reshape(db * n, d)
    gm_s, kf_s, vf_s, q_s, gv_s = _mixer_in(xs2, in_params, tm=db * n, chunk=n, stream=True)
    shs = (db, n, SB_HEADS, SB_HEAD_DIM)
    frame_minor = lambda c: jnp.transpose(c, (0, 1, 3, 4, 2))
    sb_s = _sample_attn(q_s.reshape(shs), kf_s.reshape(shs), vf_s.reshape(shs),
                        frame_minor(cache_k), frame_minor(cache_v), layer)
    sb_s = sb_s.reshape(db * n, SB_WIDTH).astype(BF16)
    ys = _out_moe(xs2, gm_s, sb_s, moe_params, tm=db * n).reshape(db, n, d)

    heads = (SB_HEADS, SB_HEAD_DIM)
    rows = lambda a: jnp.transpose(a.reshape(b, *heads, s), (0, 3, 1, 2))
    return (yp, ys, rows(kf), rows(vf),
            kf_s.reshape(db, n, *heads), vf_s.reshape(db, n, *heads),
            gv_s.reshape(db, n, GM_GROUPS, GM_GROUP_DIM))


def kernel(x_prompt, x_sample, cache_sb_k, cache_sb_v, norm1_g, w_in, gm_v_norm_g, gm_w_s, gm_b_s, q_norm_g, k_norm_g, w_out, norm2_g, w_router_group, b_router_group, w_router_expert, b_router_expert, w_gate, w_up, w_down):
    depth = w_in.shape[0]
    yp, ys = x_prompt, x_sample
    outs = [[] for _ in range(5)]
    for l in range(depth):
        yp, ys, *rest = _layer(
            l, yp, ys, cache_sb_k, cache_sb_v, norm1_g[l], w_in[l], gm_v_norm_g[l], gm_w_s[l],
            gm_b_s[l], q_norm_g[l], k_norm_g[l], w_out[l], norm2_g[l], w_router_group[l],
            b_router_group[l], w_router_expert[l], b_router_expert[l], w_gate[l], w_up[l],
            w_down[l])
        for acc, r in zip(outs, rest):
            acc.append(r)
    return (yp, ys) + tuple(jnp.stack(o, axis=0) for o in outs)
```

```python
import functools

import jax
import jax.numpy as jnp
from jax import lax
from jax.experimental import pallas as pl
from jax.experimental.pallas import tpu as pltpu

D_MODEL = 1024
GM_WIDTH = 512
GM_GROUPS = 4
GM_GROUP_DIM = 128
GM_CHUNK = 128
SB_WIDTH = 512
SB_HEAD_DIM = 64
SB_HEADS = 8
IN_WIDTH = 2 * GM_WIDTH + 3 * SB_WIDTH
N_EXPERT_GROUPS = 4
EXPERTS_PER_GROUP = 4
N_EXPERTS = 16
EXPERT_FF = 256
GROUP_FF = EXPERTS_PER_GROUP * EXPERT_FF
EPS = 1e-6

LANES = 128
HEAD_PAIRS = SB_WIDTH // LANES
SB_BLOCK = 128
SB_WIDE = 2 * SB_BLOCK
CACHE_BLOCK = 256
VMEM_LIMIT = 56 * 1024 * 1024

LOG2_E = 1.4426950408889634
Q_SCALE = SB_HEAD_DIM ** -0.5 * LOG2_E
SB_UNDERFLOW_BITS = 160.0

F32 = jnp.float32
BF16 = jnp.bfloat16
_NT = (((1,), (1,)), ((), ()))


def _gelu_tanh(x):
    return 0.5 * x * (1.0 + jnp.tanh(0.7978845608028654 * (x + 0.044715 * (x * x * x))))


def _softplus2(z):
    return jnp.maximum(z, 0.0) + jnp.log2(1.0 + jnp.exp2(-jnp.abs(z)))


def _suffix_matrix(n):
    r = lax.broadcasted_iota(jnp.int32, (n, n), 0)
    c = lax.broadcasted_iota(jnp.int32, (n, n), 1)
    return jnp.where(r > c, 1.0, 0.0).astype(BF16)


def _mixer_in_body(x_ref, g1_ref, w_in_ref, gvg_ref, qg_ref, kg_ref, ws_ref, bst_ref, bd_ref,
                   gm_ref, kf_ref, vf_ref, *rest, chunk, stream):
    x = x_ref[...]
    tm = x.shape[0]
    xn = x * lax.rsqrt(jnp.mean(x * x, axis=-1, keepdims=True) + EPS) * g1_ref[...]
    proj = jnp.dot(xn.astype(BF16), w_in_ref[...], preferred_element_type=F32)

    gu = _gelu_tanh(proj[:, :GM_WIDTH])
    gvr = _gelu_tanh(proj[:, GM_WIDTH:2 * GM_WIDTH])
    gv = gvr * lax.rsqrt(jnp.mean(gvr * gvr, axis=-1, keepdims=True) + EPS) * gvg_ref[...]

    o = 2 * GM_WIDTH
    q = proj[:, o:o + SB_WIDTH]
    k = proj[:, o + SB_WIDTH:o + 2 * SB_WIDTH]
    v = proj[:, o + 2 * SB_WIDTH:]
    bd = bd_ref[...]
    qms = jnp.dot((q * q).astype(BF16), bd, preferred_element_type=F32)
    kms = jnp.dot((k * k).astype(BF16), bd, preferred_element_type=F32)
    qs = q * lax.rsqrt(qms + EPS) * qg_ref[...] * Q_SCALE
    kn = k * lax.rsqrt(kms + EPS) * kg_ref[...]
    if stream:
        for h in range(SB_HEADS):
            lanes = slice(h * SB_HEAD_DIM, (h + 1) * SB_HEAD_DIM)
            kf_ref[:, h, :] = kn[:, lanes]
            vf_ref[:, h, :] = v[:, lanes]
            rest[0][:, h, :] = qs[:, lanes]
        rest[1][...] = gv
    else:
        kf_ref[...] = kn.T
        vf_ref[...] = v.T
        qs_ref, kb_ref, vb_ref = rest
        qs_ref[...] = qs.astype(BF16)
        kb_ref[...] = kn.astype(BF16)
        vb_ref[...] = v.astype(BF16)

    r = lax.broadcasted_iota(jnp.int32, (chunk, chunk), 0)
    c = lax.broadcasted_iota(jnp.int32, (chunk, chunk), 1)
    gvb = gv.astype(BF16)
    for g in range(GM_GROUPS):
        w = jnp.where(r >= c, ws_ref[g, :chunk, :chunk], 0.0).astype(BF16)
        b = bst_ref[:chunk, g:g + 1]
        lo = g * GM_GROUP_DIM
        for ci in range(tm // chunk):
            rows = slice(ci * chunk, (ci + 1) * chunk)
            mixed = jnp.dot(w, gvb[rows, lo:lo + GM_GROUP_DIM], preferred_element_type=F32) + b
            gm_ref[rows, lo:lo + GM_GROUP_DIM] = (gu[rows, lo:lo + GM_GROUP_DIM] * mixed).astype(BF16)


def _mixer_in(x2d, params, *, tm, chunk, stream, seq=None):
    n = x2d.shape[0]
    full = lambda a: pl.BlockSpec(a.shape, lambda i: (0,) * a.ndim)
    spec = lambda s: pl.BlockSpec((tm,) + s.shape[1:], lambda i: (i,) + (0,) * (len(s.shape) - 1))
    gm = jax.ShapeDtypeStruct((n, GM_WIDTH), BF16)
    if stream:
        heads = jax.ShapeDtypeStruct((n, SB_HEADS, SB_HEAD_DIM), F32)
        out_shape = [gm, heads, heads,
                     heads,
                     jax.ShapeDtypeStruct((n, GM_WIDTH), F32)]
        out_specs = [spec(s) for s in out_shape]
    else:
        per_seq = seq // tm
        rows_t = jax.ShapeDtypeStruct((n // seq, SB_WIDTH, seq), F32)
        spec_t = pl.BlockSpec((None, SB_WIDTH, tm), lambda i: (i // per_seq, 0, i % per_seq))
        dense = jax.ShapeDtypeStruct((n, SB_WIDTH), BF16)
        out_shape = [gm, rows_t, rows_t, dense, dense, dense]
        out_specs = [spec(gm), spec_t, spec_t] + [spec(dense)] * 3
    return pl.pallas_call(
        functools.partial(_mixer_in_body, chunk=chunk, stream=stream),
        grid=(n // tm,),
        in_specs=[pl.BlockSpec((tm, D_MODEL), lambda i: (i, 0))] + [full(a) for a in params],
        out_specs=out_specs,
        out_shape=out_shape,
        compiler_params=pltpu.CompilerParams(
            dimension_semantics=("parallel",), vmem_limit_bytes=VMEM_LIMIT),
        name="mixer_in_stream" if stream else "mixer_in",
    )(x2d, *params)


def _sb_step(qst, kj, vj, suffix, c, causal=None):
    z = lax.dot_general(qst, kj, _NT, preferred_element_type=F32)
    sp = _softplus2(z)
    if causal is not None:
        sp = jnp.where(causal, sp, 0.0)
    later = jnp.dot(sp.astype(BF16), suffix, preferred_element_type=F32) + c
    w = jnp.exp2(z - sp - later)
    if causal is not None:
        w = jnp.where(causal, w, 0.0)
    pv = jnp.dot(w.astype(BF16), vj, preferred_element_type=F32)
    return pv, c + jnp.sum(sp, axis=-1, keepdims=True)


def _prompt_attn_body(q_ref, k_ref, v_ref, o_ref, qst_ref, *state_refs):
    c_refs, acc_refs = state_refs[:HEAD_PAIRS], state_refs[HEAD_PAIRS:]
    i = pl.program_id(1)
    lane = lax.broadcasted_iota(jnp.int32, (SB_BLOCK, LANES), 1)
    first = lane < SB_HEAD_DIM
    for hp in range(HEAD_PAIRS):
        q = q_ref[:, hp * LANES:(hp + 1) * LANES]
        zero = jnp.zeros_like(q)
        qst_ref[hp, :SB_BLOCK, :] = jnp.where(first, q, zero)
        qst_ref[hp, SB_BLOCK:, :] = jnp.where(first, zero, q)
    suffix = _suffix_matrix(SB_BLOCK)
    m = 2 * SB_BLOCK

    wide = _suffix_matrix(SB_WIDE)

    def sweep(j, keys, causal, fresh):
        rows = pl.ds(pl.multiple_of(j * SB_BLOCK, SB_BLOCK), keys)
        for hp in range(HEAD_PAIRS):
            cols = slice(hp * LANES, (hp + 1) * LANES)
            c0 = jnp.zeros((m, 1), F32) if fresh else c_refs[hp][...]
            pv, c = _sb_step(qst_ref[hp], k_ref[rows, cols], v_ref[rows, cols],
                             suffix if keys == SB_BLOCK else wide, c0, causal)
            acc_refs[hp][...] = pv if fresh else acc_refs[hp][...] + pv
            c_refs[hp][...] = c

    t = lax.broadcasted_iota(jnp.int32, (m, SB_BLOCK), 0) % SB_BLOCK
    s = lax.broadcasted_iota(jnp.int32, (m, SB_BLOCK), 1)
    sweep(i, SB_BLOCK, s < t, True)

    def least_carry():
        c = c_refs[0][...]
        for hp in range(1, HEAD_PAIRS):
            c = jnp.minimum(c, c_refs[hp][...])
        return jnp.min(c)

    per_wide = SB_WIDE // SB_BLOCK
    n_wide = i // per_wide

    def more(state):
        jj, carry = state
        return (jj < n_wide) & (carry < SB_UNDERFLOW_BITS)

    def body(state):
        jj, _ = state
        sweep(i - per_wide * (jj + 1), SB_WIDE, None, False)
        return jj + 1, least_carry()

    done, carry = lax.while_loop(more, body, (jnp.int32(0), least_carry()))

    @pl.when((done == n_wide) & (i % per_wide == 1) & (carry < SB_UNDERFLOW_BITS))
    def _():
        sweep(0, SB_BLOCK, None, False)

    for hp in range(HEAD_PAIRS):
        o_ref[:, hp * LANES:(hp + 1) * LANES] = jnp.where(
            first, acc_refs[hp][:SB_BLOCK, :], acc_refs[hp][SB_BLOCK:, :]).astype(o_ref.dtype)


def _prompt_attn(qs, kb, vb):
    b, s, _ = qs.shape
    q_spec = pl.BlockSpec((None, SB_BLOCK, SB_WIDTH), lambda bi, i: (bi, i, 0))
    kv_spec = pl.BlockSpec((None, s, SB_WIDTH), lambda bi, i: (bi, 0, 0))
    m = 2 * SB_BLOCK
    return pl.pallas_call(
        _prompt_attn_body,
        grid=(b, s // SB_BLOCK),
        in_specs=[q_spec, kv_spec, kv_spec],
        out_specs=q_spec,
        out_shape=jax.ShapeDtypeStruct((b, s, SB_WIDTH), BF16),
        scratch_shapes=([pltpu.VMEM((HEAD_PAIRS, m, LANES), BF16)]
                        + [pltpu.VMEM((m, 1), F32)] * HEAD_PAIRS
                        + [pltpu.VMEM((m, LANES), F32)] * HEAD_PAIRS),
        compiler_params=pltpu.CompilerParams(
            dimension_semantics=("parallel", "arbitrary"), vmem_limit_bytes=VMEM_LIMIT),
        name="prompt_attn",
    )(qs, kb, vb)


def _sample_attn_body(q_ref, kn_ref, vn_ref, ck_hbm, cv_hbm, o_ref, kbuf, vbuf, sems, c_ref, acc_ref,
                      *, layer):
    b = pl.program_id(0)
    n = q_ref.shape[0]
    m = SB_HEADS * n
    nblk = ck_hbm.shape[-1] // CACHE_BLOCK

    def copies(stream, blk, slot):
        cols = pl.ds(blk * CACHE_BLOCK, CACHE_BLOCK)
        return (pltpu.make_async_copy(ck_hbm.at[layer, stream, :, :, cols], kbuf.at[slot], sems.at[slot, 0]),
                pltpu.make_async_copy(cv_hbm.at[layer, stream, :, :, cols], vbuf.at[slot], sems.at[slot, 1]))

    def start(stream, blk, slot):
        for cp in copies(stream, blk, slot):
            cp.start()

    def wait(stream, blk, slot):
        for cp in copies(stream, blk, slot):
            cp.wait()

    def slot_of(jj):
        return jnp.where(jj == 0, b % 2, 2 + jj % 2)

    @pl.when(b == 0)
    def _():
        start(0, nblk - 1, 0)

    @pl.when(b + 1 < pl.num_programs(0))
    def _():
        start(b + 1, nblk - 1, (b + 1) % 2)

    q = [q_ref[:, h, :].astype(BF16) for h in range(SB_HEADS)]

    def block(k_of, v_of, suffix, causal, fresh, dim_major):
        qk = lambda a, k: (jnp.dot(a, k, preferred_element_type=F32) if dim_major
                           else lax.dot_general(a, k, _NT, preferred_element_type=F32))
        wv = lambda a, v: (lax.dot_general(a, v, _NT, preferred_element_type=F32) if dim_major
                           else jnp.dot(a, v, preferred_element_type=F32))
        z = jnp.concatenate([qk(q[h], k_of(h)) for h in range(SB_HEADS)], axis=0)
        sp = _softplus2(z)
        if causal is not None:
            sp = jnp.where(causal, sp, 0.0)
        c0 = jnp.zeros((m, 1), F32) if fresh else c_ref[...]
        later = jnp.dot(sp.astype(BF16), suffix, preferred_element_type=F32) + c0
        w = jnp.exp2(z - sp - later)
        if causal is not None:
            w = jnp.where(causal, w, 0.0)
        wb = w.astype(BF16)
        for h in range(SB_HEADS):
            pv = wv(wb[h * n:(h + 1) * n, :], v_of(h))
            acc_ref[h] = pv if fresh else acc_ref[h] + pv
        c_ref[...] = c0 + jnp.sum(sp, axis=-1, keepdims=True)

    pad = jnp.zeros((LANES - n, SB_HEAD_DIM), BF16)
    t = lax.broadcasted_iota(jnp.int32, (m, LANES), 0) % n
    s = lax.broadcasted_iota(jnp.int32, (m, LANES), 1)
    block(lambda h: jnp.concatenate([kn_ref[:, h, :].astype(BF16), pad], axis=0),
          lambda h: jnp.concatenate([vn_ref[:, h, :].astype(BF16), pad], axis=0),
          _suffix_matrix(LANES), s < t, True, False)

    suffix = _suffix_matrix(CACHE_BLOCK)

    def more(state):
        jj, carry = state
        return (jj < nblk) & (carry < SB_UNDERFLOW_BITS)

    def body(state):
        jj, _ = state
        blk = nblk - 1 - jj
        slot = slot_of(jj)
        wait(b, blk, slot)

        @pl.when(blk > 0)
        def _():
            start(b, blk - 1, slot_of(jj + 1))

        block(lambda h: kbuf[slot, h].astype(BF16), lambda h: vbuf[slot, h].astype(BF16),
              suffix, None, False, True)
        return jj + 1, jnp.min(c_ref[...])

    done, _ = lax.while_loop(more, body, (jnp.int32(0), jnp.min(c_ref[...])))

    @pl.when(done < nblk)
    def _():
        wait(b, nblk - 1 - done, slot_of(done))

    for h in range(SB_HEADS):
        o_ref[:, h, :] = acc_ref[h]


def _sample_attn(q4, kn4, vn4, cache_k, cache_v, layer):
    db, n = q4.shape[:2]
    m = SB_HEADS * n
    new_spec = pl.BlockSpec((None, n, SB_HEADS, SB_HEAD_DIM), lambda b: (b, 0, 0, 0))
    hbm = pl.BlockSpec(memory_space=pl.ANY)
    buf = pltpu.VMEM((4, SB_HEADS, SB_HEAD_DIM, CACHE_BLOCK), F32)
    return pl.pallas_call(
        functools.partial(_sample_attn_body, layer=layer),
        grid=(db,),
        in_specs=[new_spec, new_spec, new_spec, hbm, hbm],
        out_specs=new_spec,
        out_shape=jax.ShapeDtypeStruct(q4.shape, F32),
        scratch_shapes=[buf, buf, pltpu.SemaphoreType.DMA((4, 2)),
                        pltpu.VMEM((m, 1), F32),
                        pltpu.VMEM((SB_HEADS, n, SB_HEAD_DIM), F32)],
        compiler_params=pltpu.CompilerParams(
            dimension_semantics=("arbitrary",), vmem_limit_bytes=VMEM_LIMIT),
        name="sample_attn",
    )(q4, kn4, vn4, cache_k, cache_v)


def _route(logits):
    lane = lax.broadcasted_iota(jnp.int32, logits.shape, 1)
    neg = jnp.float32(-jnp.inf)
    big = jnp.int32(1 << 20)
    rmax = lambda a: jnp.max(a, axis=-1, keepdims=True)
    rmin = lambda a: jnp.min(a, axis=-1, keepdims=True)
    rsum = lambda a: jnp.sum(a, axis=-1, keepdims=True)

    gmask = lane < N_EXPERT_GROUPS
    gl = jnp.where(gmask, logits, neg)
    gmax = rmax(gl)
    g_sel = rmin(jnp.where(gl == gmax, lane, big))
    p_sel = 1.0 / rsum(jnp.where(gmask, jnp.exp(gl - gmax), 0.0))

    e_lo = N_EXPERT_GROUPS + g_sel * EXPERTS_PER_GROUP
    emask = (lane >= e_lo) & (lane < e_lo + EXPERTS_PER_GROUP)
    el = jnp.where(emask, logits, neg)
    emax = rmax(el)
    ex = jnp.where(emask, jnp.exp(el - emax), 0.0)
    p_e = ex / rsum(ex)
    pm = jnp.where(emask, p_e, -1.0)
    v1 = rmax(pm)
    i1 = rmin(jnp.where(pm == v1, lane, big))
    pm2 = jnp.where(lane == i1, -1.0, pm)
    v2 = rmax(pm2)
    i2 = rmin(jnp.where(pm2 == v2, lane, big))
    tot = v1 + v2
    return (jnp.where(lane == i1, p_sel * v1 / tot, 0.0)
            + jnp.where(lane == i2, p_sel * v2 / tot, 0.0))


def _out_moe_body(x_ref, gm_ref, sb_ref, wo_ref, g2_ref, wrh_ref, wrl_ref, br_ref,
                  wg_ref, wu_ref, wd_ref, o_ref, hn_ref, comb_ref):
    g = pl.program_id(1)

    @pl.when(g == 0)
    def _():
        h = (x_ref[...]
             + jnp.dot(gm_ref[...], wo_ref[:GM_WIDTH, :], preferred_element_type=F32)
             + jnp.dot(sb_ref[...], wo_ref[GM_WIDTH:, :], preferred_element_type=F32))
        o_ref[...] = h
        hn = h * lax.rsqrt(jnp.mean(h * h, axis=-1, keepdims=True) + EPS) * g2_ref[...]
        hi = hn.astype(BF16)
        hn_ref[...] = hi
        lo = (hn - hi.astype(F32)).astype(BF16)
        logits = (jnp.dot(hi, wrh_ref[...], preferred_element_type=F32)
                  + jnp.dot(lo, wrh_ref[...], preferred_element_type=F32)
                  + jnp.dot(hi, wrl_ref[...], preferred_element_type=F32)) + br_ref[...]
        comb_ref[...] = _route(logits)

    hn = hn_ref[...]
    comb = comb_ref[...]
    lane = lax.broadcasted_iota(jnp.int32, comb.shape, 1)
    parts = []
    for e in range(EXPERTS_PER_GROUP):
        gate = jnp.sum(jnp.where(lane == N_EXPERT_GROUPS + g * EXPERTS_PER_GROUP + e, comb, 0.0),
                       axis=-1, keepdims=True)
        hg = jnp.dot(hn, wg_ref[e], preferred_element_type=F32)
        hu = jnp.dot(hn, wu_ref[e], preferred_element_type=F32)
        parts.append((hg / (1.0 + jnp.exp(-hg)) * hu * gate).astype(BF16))
    a = jnp.concatenate(parts, axis=-1)
    o_ref[...] += jnp.dot(a, wd_ref[...].reshape(GROUP_FF, D_MODEL), preferred_element_type=F32)


def _out_moe(x2d, gm, sb, params, *, tm):
    n = x2d.shape[0]
    wo, g2, wrh, wrl, br, wg, wu, wd = params
    row = lambda w: pl.BlockSpec((tm, w), lambda i, g: (i, 0))
    full = lambda a: pl.BlockSpec(a.shape, lambda i, g: (0,) * a.ndim)
    group = lambda a: pl.BlockSpec((EXPERTS_PER_GROUP,) + a.shape[1:], lambda i, g: (g, 0, 0))
    return pl.pallas_call(
        _out_moe_body,
        grid=(n // tm, N_EXPERT_GROUPS),
        in_specs=[row(D_MODEL), row(GM_WIDTH), row(SB_WIDTH),
                  full(wo), full(g2), full(wrh), full(wrl), full(br),
                  group(wg), group(wu), group(wd)],
        out_specs=row(D_MODEL),
        out_shape=jax.ShapeDtypeStruct((n, D_MODEL), F32),
        scratch_shapes=[pltpu.VMEM((tm, D_MODEL), BF16), pltpu.VMEM((tm, LANES), F32)],
        compiler_params=pltpu.CompilerParams(
            dimension_semantics=("parallel", "arbitrary"), vmem_limit_bytes=VMEM_LIMIT),
        name="out_moe",
    )(x2d, gm, sb, *params)


def _layer(layer, xp, xs, cache_k, cache_v, norm1_g, w_in, gm_v_norm_g, gm_w_s, gm_b_s, q_norm_g,
           k_norm_g, w_out, norm2_g, w_router_group, b_router_group, w_router_expert,
           b_router_expert, w_gate, w_up, w_down):
    b, s, d = xp.shape
    db, n, _ = xs.shape
    head_of = jnp.arange(SB_WIDTH) // SB_HEAD_DIM
    bd = jnp.where(head_of[:, None] == head_of[None, :], 1.0 / SB_HEAD_DIM, 0.0).astype(BF16)
    in_params = (norm1_g[None, :], w_in.astype(BF16), gm_v_norm_g[None, :],
                 jnp.tile(q_norm_g, SB_HEADS)[None, :], jnp.tile(k_norm_g, SB_HEADS)[None, :],
                 gm_w_s, gm_b_s.T, bd)

    wr = jnp.concatenate(
        [w_router_group, jnp.transpose(w_router_expert, (1, 0, 2)).reshape(d, N_EXPERTS)], axis=1)
    wr = jnp.pad(wr, ((0, 0), (0, LANES - wr.shape[1])))
    wr_hi = wr.astype(BF16)
    wr_lo = (wr - wr_hi.astype(F32)).astype(BF16)
    br = jnp.pad(jnp.concatenate([b_router_group, b_router_expert.reshape(-1)]),
                 (0, LANES - N_EXPERT_GROUPS - N_EXPERTS))[None, :]

    moe_params = (w_out.astype(BF16), norm2_g[None, :], wr_hi, wr_lo, br,
                  w_gate.astype(BF16), w_up.astype(BF16), w_down.astype(BF16))

    xp2 = xp.reshape(b * s, d)
    gm, kf, vf, qs, kb, vb = _mixer_in(xp2, in_params, tm=512, chunk=GM_CHUNK, stream=False, seq=s)
    shp = (b, s, SB_WIDTH)
    sb = _prompt_attn(qs.reshape(shp), kb.reshape(shp), vb.reshape(shp))
    yp = _out_moe(xp2, gm, sb.reshape(b * s, SB_WIDTH), moe_params, tm=512).reshape(b, s, d)

    xs2 = xs.reshape(db * n, d)
    gm_s, kf_s, vf_s, q_s, gv_s = _mixer_in(xs2, in_params, tm=db * n, chunk=n, stream=True)
    shs = (db, n, SB_HEADS, SB_HEAD_DIM)
    frame_minor = lambda c: jnp.transpose(c, (0, 1, 3, 4, 2))
    sb_s = _sample_attn(q_s.reshape(shs), kf_s.reshape(shs), vf_s.reshape(shs),
                        frame_minor(cache_k), frame_minor(cache_v), layer)
    sb_s = sb_s.reshape(db * n, SB_WIDTH).astype(BF16)
    ys = _out_moe(xs2, gm_s, sb_s, moe_params, tm=db * n).reshape(db, n, d)

    heads = (SB_HEADS, SB_HEAD_DIM)
    rows = lambda a: jnp.transpose(a.reshape(b, *heads, s), (0, 3, 1, 2))
    return (yp, ys, rows(kf), rows(vf),
            kf_s.reshape(db, n, *heads), vf_s.reshape(db, n, *heads),
            gv_s.reshape(db, n, GM_GROUPS, GM_GROUP_DIM))


def kernel(x_prompt, x_sample, cache_sb_k, cache_sb_v, norm1_g, w_in, gm_v_norm_g, gm_w_s, gm_b_s, q_norm_g, k_norm_g, w_out, norm2_g, w_router_group, b_router_group, w_router_expert, b_router_expert, w_gate, w_up, w_down):
    depth = w_in.shape[0]
    yp, ys = x_prompt, x_sample
    outs = [[] for _ in range(5)]
    for l in range(depth):
        yp, ys, *rest = _layer(
            l, yp, ys, cache_sb_k, cache_sb_v, norm1_g[l], w_in[l], gm_v_norm_g[l], gm_w_s[l],
            gm_b_s[l], q_norm_g[l], k_norm_g[l], w_out[l], norm2_g[l], w_router_group[l],
            b_router_group[l], w_router_expert[l], b_router_expert[l], w_gate[l], w_up[l],
            w_down[l])
        for acc, r in zip(outs, rest):
            acc.append(r)
    return (yp, ys) + tuple(jnp.stack(o, axis=0) for o in outs)
```

```python
import functools

import jax
import jax.numpy as jnp
from jax import lax
from jax.experimental import pallas as pl
from jax.experimental.pallas import tpu as pltpu

D_MODEL = 1024
GM_WIDTH = 512
GM_GROUPS = 4
GM_GROUP_DIM = 128
GM_CHUNK = 128
SB_WIDTH = 512
SB_HEAD_DIM = 64
SB_HEADS = 8
IN_WIDTH = 2 * GM_WIDTH + 3 * SB_WIDTH
N_EXPERT_GROUPS = 4
EXPERTS_PER_GROUP = 4
N_EXPERTS = 16
EXPERT_FF = 256
GROUP_FF = EXPERTS_PER_GROUP * EXPERT_FF
EPS = 1e-6

LANES = 128
HEAD_PAIRS = SB_WIDTH // LANES
SB_BLOCK = 128
SB_WIDE = 2 * SB_BLOCK
CACHE_BLOCK = 256
MOE_SUB = 256
MOE_CAP = 96
MOE_SLOTS = N_EXPERT_GROUPS * MOE_CAP
MOE_SUPER = 8
SLOT_LANE = 0
VMEM_LIMIT = 56 * 1024 * 1024

LOG2_E = 1.4426950408889634
Q_SCALE = SB_HEAD_DIM ** -0.5 * LOG2_E
SB_UNDERFLOW_BITS = 160.0

F32 = jnp.float32
BF16 = jnp.bfloat16
_NT = (((1,), (1,)), ((), ()))


def _gelu_tanh(x):
    return 0.5 * x * (1.0 + jnp.tanh(0.7978845608028654 * (x + 0.044715 * (x * x * x))))


def _softplus2(z):
    return jnp.maximum(z, 0.0) + jnp.log2(1.0 + jnp.exp2(-jnp.abs(z)))


def _suffix_matrix(n):
    r = lax.broadcasted_iota(jnp.int32, (n, n), 0)
    c = lax.broadcasted_iota(jnp.int32, (n, n), 1)
    return jnp.where(r > c, 1.0, 0.0).astype(BF16)


def _mixer_in_body(x_ref, g1_ref, w_in_ref, gvg_ref, qg_ref, kg_ref, ws_ref, bst_ref, bd_ref,
                   gm_ref, kf_ref, vf_ref, *rest, chunk, stream):
    x = x_ref[...]
    tm = x.shape[0]
    xn = x * lax.rsqrt(jnp.mean(x * x, axis=-1, keepdims=True) + EPS) * g1_ref[...]
    proj = jnp.dot(xn.astype(BF16), w_in_ref[...], preferred_element_type=F32)

    gu = _gelu_tanh(proj[:, :GM_WIDTH])
    gvr = _gelu_tanh(proj[:, GM_WIDTH:2 * GM_WIDTH])
    gv = gvr * lax.rsqrt(jnp.mean(gvr * gvr, axis=-1, keepdims=True) + EPS) * gvg_ref[...]

    o = 2 * GM_WIDTH
    q = proj[:, o:o + SB_WIDTH]
    k = proj[:, o + SB_WIDTH:o + 2 * SB_WIDTH]
    v = proj[:, o + 2 * SB_WIDTH:]
    bd = bd_ref[...]
    qms = jnp.dot((q * q).astype(BF16), bd, preferred_element_type=F32)
    kms = jnp.dot((k * k).astype(BF16), bd, preferred_element_type=F32)
    qs = q * lax.rsqrt(qms + EPS) * qg_ref[...] * Q_SCALE
    kn = k * lax.rsqrt(kms + EPS) * kg_ref[...]
    if stream:
        for h in range(SB_HEADS):
            lanes = slice(h * SB_HEAD_DIM, (h + 1) * SB_HEAD_DIM)
            kf_ref[:, h, :] = kn[:, lanes]
            vf_ref[:, h, :] = v[:, lanes]
            rest[0][:, h, :] = qs[:, lanes]
        rest[1][...] = gv
    else:
        kf_ref[...] = kn.T
        vf_ref[...] = v.T
        qs_ref, kb_ref, vb_ref = rest
        qs_ref[...] = qs.astype(BF16)
        kb_ref[...] = kn.astype(BF16)
        vb_ref[...] = v.astype(BF16)

    r = lax.broadcasted_iota(jnp.int32, (chunk, chunk), 0)
    c = lax.broadcasted_iota(jnp.int32, (chunk, chunk), 1)
    gvb = gv.astype(BF16)
    for g in range(GM_GROUPS):
        w = jnp.where(r >= c, ws_ref[g, :chunk, :chunk], 0.0).astype(BF16)
        b = bst_ref[:chunk, g:g + 1]
        lo = g * GM_GROUP_DIM
        for ci in range(tm // chunk):
            rows = slice(ci * chunk, (ci + 1) * chunk)
            mixed = jnp.dot(w, gvb[rows, lo:lo + GM_GROUP_DIM], preferred_element_type=F32) + b
            gm_ref[rows, lo:lo + GM_GROUP_DIM] = (gu[rows, lo:lo + GM_GROUP_DIM] * mixed).astype(BF16)


def _mixer_in(x2d, params, *, tm, chunk, stream, seq=None):
    n = x2d.shape[0]
    full = lambda a: pl.BlockSpec(a.shape, lambda i: (0,) * a.ndim)
    spec = lambda s: pl.BlockSpec((tm,) + s.shape[1:], lambda i: (i,) + (0,) * (len(s.shape) - 1))
    gm = jax.ShapeDtypeStruct((n, GM_WIDTH), BF16)
    if stream:
        heads = jax.ShapeDtypeStruct((n, SB_HEADS, SB_HEAD_DIM), F32)
        out_shape = [gm, heads, heads,
                     heads,
                     jax.ShapeDtypeStruct((n, GM_WIDTH), F32)]
        out_specs = [spec(s) for s in out_shape]
    else:
        per_seq = seq // tm
        rows_t = jax.ShapeDtypeStruct((n // seq, SB_WIDTH, seq), F32)
        spec_t = pl.BlockSpec((None, SB_WIDTH, tm), lambda i: (i // per_seq, 0, i % per_seq))
        dense = jax.ShapeDtypeStruct((n, SB_WIDTH), BF16)
        out_shape = [gm, rows_t, rows_t, dense, dense, dense]
        out_specs = [spec(gm), spec_t, spec_t] + [spec(dense)] * 3
    return pl.pallas_call(
        functools.partial(_mixer_in_body, chunk=chunk, stream=stream),
        grid=(n // tm,),
        in_specs=[pl.BlockSpec((tm, D_MODEL), lambda i: (i, 0))] + [full(a) for a in params],
        out_specs=out_specs,
        out_shape=out_shape,
        compiler_params=pltpu.CompilerParams(
            dimension_semantics=("parallel",), vmem_limit_bytes=VMEM_LIMIT),
        name="mixer_in_stream" if stream else "mixer_in",
    )(x2d, *params)


def _sb_step(qst, kj, vj, suffix, c, causal=None):
    z = lax.dot_general(qst, kj, _NT, preferred_element_type=F32)
    sp = _softplus2(z)
    if causal is not None:
        sp = jnp.where(causal, sp, 0.0)
    later = jnp.dot(sp.astype(BF16), suffix, preferred_element_type=F32) + c
    w = jnp.exp2(z - sp - later)
    if causal is not None:
        w = jnp.where(causal, w, 0.0)
    pv = jnp.dot(w.astype(BF16), vj, preferred_element_type=F32)
    return pv, c + jnp.sum(sp, axis=-1, keepdims=True)


def _prompt_attn_body(q_ref, k_ref, v_ref, o_ref, qst_ref, *state_refs):
    c_refs, acc_refs = state_refs[:HEAD_PAIRS], state_refs[HEAD_PAIRS:]
    i = pl.program_id(1)
    lane = lax.broadcasted_iota(jnp.int32, (SB_BLOCK, LANES), 1)
    first = lane < SB_HEAD_DIM
    for hp in range(HEAD_PAIRS):
        q = q_ref[:, hp * LANES:(hp + 1) * LANES]
        zero = jnp.zeros_like(q)
        qst_ref[hp, :SB_BLOCK, :] = jnp.where(first, q, zero)
        qst_ref[hp, SB_BLOCK:, :] = jnp.where(first, zero, q)
    suffix = _suffix_matrix(SB_BLOCK)
    m = 2 * SB_BLOCK

    wide = _suffix_matrix(SB_WIDE)

    def sweep(j, keys, causal, fresh):
        rows = pl.ds(pl.multiple_of(j * SB_BLOCK, SB_BLOCK), keys)
        for hp in range(HEAD_PAIRS):
            cols = slice(hp * LANES, (hp + 1) * LANES)
            c0 = jnp.zeros((m, 1), F32) if fresh else c_refs[hp][...]
            pv, c = _sb_step(qst_ref[hp], k_ref[rows, cols], v_ref[rows, cols],
                             suffix if keys == SB_BLOCK else wide, c0, causal)
            acc_refs[hp][...] = pv if fresh else acc_refs[hp][...] + pv
            c_refs[hp][...] = c

    t = lax.broadcasted_iota(jnp.int32, (m, SB_BLOCK), 0) % SB_BLOCK
    s = lax.broadcasted_iota(jnp.int32, (m, SB_BLOCK), 1)
    sweep(i, SB_BLOCK, s < t, True)

    def least_carry():
        c = c_refs[0][...]
        for hp in range(1, HEAD_PAIRS):
            c = jnp.minimum(c, c_refs[hp][...])
        return jnp.min(c)

    per_wide = SB_WIDE // SB_BLOCK
    n_wide = i // per_wide

    def more(state):
        jj, carry = state
        return (jj < n_wide) & (carry < SB_UNDERFLOW_BITS)

    def body(state):
        jj, _ = state
        sweep(i - per_wide * (jj + 1), SB_WIDE, None, False)
        return jj + 1, least_carry()

    done, carry = lax.while_loop(more, body, (jnp.int32(0), least_carry()))

    @pl.when((done == n_wide) & (i % per_wide == 1) & (carry < SB_UNDERFLOW_BITS))
    def _():
        sweep(0, SB_BLOCK, None, False)

    for hp in range(HEAD_PAIRS):
        o_ref[:, hp * LANES:(hp + 1) * LANES] = jnp.where(
            first, acc_refs[hp][:SB_BLOCK, :], acc_refs[hp][SB_BLOCK:, :]).astype(o_ref.dtype)


def _prompt_attn(qs, kb, vb):
    b, s, _ = qs.shape
    q_spec = pl.BlockSpec((None, SB_BLOCK, SB_WIDTH), lambda bi, i: (bi, i, 0))
    kv_spec = pl.BlockSpec((None, s, SB_WIDTH), lambda bi, i: (bi, 0, 0))
    m = 2 * SB_BLOCK
    return pl.pallas_call(
        _prompt_attn_body,
        grid=(b, s // SB_BLOCK),
        in_specs=[q_spec, kv_spec, kv_spec],
        out_specs=q_spec,
        out_shape=jax.ShapeDtypeStruct((b, s, SB_WIDTH), BF16),
        scratch_shapes=([pltpu.VMEM((HEAD_PAIRS, m, LANES), BF16)]
                        + [pltpu.VMEM((m, 1), F32)] * HEAD_PAIRS
                        + [pltpu.VMEM((m, LANES), F32)] * HEAD_PAIRS),
        compiler_params=pltpu.CompilerParams(
            dimension_semantics=("parallel", "arbitrary"), vmem_limit_bytes=VMEM_LIMIT),
        name="prompt_attn",
    )(qs, kb, vb)


def _sample_attn_body(q_ref, kn_ref, vn_ref, ck_hbm, cv_hbm, o_ref, kbuf, vbuf, sems, c_ref, acc_ref,
                      *, layer):
    b = pl.program_id(0)
    n = q_ref.shape[0]
    m = SB_HEADS * n
    nblk = ck_hbm.shape[-1] // CACHE_BLOCK

    def copies(stream, blk, slot):
        cols = pl.ds(blk * CACHE_BLOCK, CACHE_BLOCK)
        return (pltpu.make_async_copy(ck_hbm.at[layer, stream, :, :, cols], kbuf.at[slot], sems.at[slot, 0]),
                pltpu.make_async_copy(cv_hbm.at[layer, stream, :, :, cols], vbuf.at[slot], sems.at[slot, 1]))

    def start(stream, blk, slot):
        for cp in copies(stream, blk, slot):
            cp.start()

    def wait(stream, blk, slot):
        for cp in copies(stream, blk, slot):
            cp.wait()

    def slot_of(jj):
        return jnp.where(jj == 0, b % 2, 2 + jj % 2)

    @pl.when(b == 0)
    def _():
        start(0, nblk - 1, 0)

    @pl.when(b + 1 < pl.num_programs(0))
    def _():
        start(b + 1, nblk - 1, (b + 1) % 2)

    q = [q_ref[:, h, :].astype(BF16) for h in range(SB_HEADS)]

    def block(k_of, v_of, suffix, causal, fresh, dim_major):
        qk = lambda a, k: (jnp.dot(a, k, preferred_element_type=F32) if dim_major
                           else lax.dot_general(a, k, _NT, preferred_element_type=F32))
        wv = lambda a, v: (lax.dot_general(a, v, _NT, preferred_element_type=F32) if dim_major
                           else jnp.dot(a, v, preferred_element_type=F32))
        z = jnp.concatenate([qk(q[h], k_of(h)) for h in range(SB_HEADS)], axis=0)
        sp = _softplus2(z)
        if causal is not None:
            sp = jnp.where(causal, sp, 0.0)
        c0 = jnp.zeros((m, 1), F32) if fresh else c_ref[...]
        later = jnp.dot(sp.astype(BF16), suffix, preferred_element_type=F32) + c0
        w = jnp.exp2(z - sp - later)
        if causal is not None:
            w = jnp.where(causal, w, 0.0)
        wb = w.astype(BF16)
        for h in range(SB_HEADS):
            pv = wv(wb[h * n:(h + 1) * n, :], v_of(h))
            acc_ref[h] = pv if fresh else acc_ref[h] + pv
        c_ref[...] = c0 + jnp.sum(sp, axis=-1, keepdims=True)

    pad = jnp.zeros((LANES - n, SB_HEAD_DIM), BF16)
    t = lax.broadcasted_iota(jnp.int32, (m, LANES), 0) % n
    s = lax.broadcasted_iota(jnp.int32, (m, LANES), 1)
    block(lambda h: jnp.concatenate([kn_ref[:, h, :].astype(BF16), pad], axis=0),
          lambda h: jnp.concatenate([vn_ref[:, h, :].astype(BF16), pad], axis=0),
          _suffix_matrix(LANES), s < t, True, False)

    suffix = _suffix_matrix(CACHE_BLOCK)

    def more(state):
        jj, carry = state
        return (jj < nblk) & (carry < SB_UNDERFLOW_BITS)

    def body(state):
        jj, _ = state
        blk = nblk - 1 - jj
        slot = slot_of(jj)
        wait(b, blk, slot)

        @pl.when(blk > 0)
        def _():
            start(b, blk - 1, slot_of(jj + 1))

        block(lambda h: kbuf[slot, h].astype(BF16), lambda h: vbuf[slot, h].astype(BF16),
              suffix, None, False, True)
        return jj + 1, jnp.min(c_ref[...])

    done, _ = lax.while_loop(more, body, (jnp.int32(0), jnp.min(c_ref[...])))

    @pl.when(done < nblk)
    def _():
        wait(b, nblk - 1 - done, slot_of(done))

    for h in range(SB_HEADS):
        o_ref[:, h, :] = acc_ref[h]


def _sample_attn(q4, kn4, vn4, cache_k, cache_v, layer):
    db, n = q4.shape[:2]
    m = SB_HEADS * n
    new_spec = pl.BlockSpec((None, n, SB_HEADS, SB_HEAD_DIM), lambda b: (b, 0, 0, 0))
    hbm = pl.BlockSpec(memory_space=pl.ANY)
    buf = pltpu.VMEM((4, SB_HEADS, SB_HEAD_DIM, CACHE_BLOCK), F32)
    return pl.pallas_call(
        functools.partial(_sample_attn_body, layer=layer),
        grid=(db,),
        in_specs=[new_spec, new_spec, new_spec, hbm, hbm],
        out_specs=new_spec,
        out_shape=jax.ShapeDtypeStruct(q4.shape, F32),
        scratch_shapes=[buf, buf, pltpu.SemaphoreType.DMA((4, 2)),
                        pltpu.VMEM((m, 1), F32),
                        pltpu.VMEM((SB_HEADS, n, SB_HEAD_DIM), F32)],
        compiler_params=pltpu.CompilerParams(
            dimension_semantics=("arbitrary",), vmem_limit_bytes=VMEM_LIMIT),
        name="sample_attn",
    )(q4, kn4, vn4, cache_k, cache_v)


def _route(logits):
    lane = lax.broadcasted_iota(jnp.int32, logits.shape, 1)
    neg = jnp.float32(-jnp.inf)
    big = jnp.int32(1 << 20)
    rmax = lambda a: jnp.max(a, axis=-1, keepdims=True)
    rmin = lambda a: jnp.min(a, axis=-1, keepdims=True)
    rsum = lambda a: jnp.sum(a, axis=-1, keepdims=True)

    gmask = lane < N_EXPERT_GROUPS
    gl = jnp.where(gmask, logits, neg)
    gmax = rmax(gl)
    g_sel = rmin(jnp.where(gl == gmax, lane, big))
    p_sel = 1.0 / rsum(jnp.where(gmask, jnp.exp(gl - gmax), 0.0))

    e_lo = N_EXPERT_GROUPS + g_sel * EXPERTS_PER_GROUP
    emask = (lane >= e_lo) & (lane < e_lo + EXPERTS_PER_GROUP)
    el = jnp.where(emask, logits, neg)
    emax = rmax(el)
    ex = jnp.where(emask, jnp.exp(el - emax), 0.0)
    p_e = ex / rsum(ex)
    pm = jnp.where(emask, p_e, -1.0)
    v1 = rmax(pm)
    i1 = rmin(jnp.where(pm == v1, lane, big))
    pm2 = jnp.where(lane == i1, -1.0, pm)
    v2 = rmax(pm2)
    i2 = rmin(jnp.where(pm2 == v2, lane, big))
    tot = v1 + v2
    comb = (jnp.where(lane == i1, p_sel * v1 / tot, 0.0)
            + jnp.where(lane == i2, p_sel * v2 / tot, 0.0))
    gates = jnp.zeros_like(comb)
    for e in range(EXPERTS_PER_GROUP):
        gates = jnp.where(lane == e, rsum(jnp.where(lane == e_lo + e, comb, 0.0)), gates)
    return comb, g_sel, gates


def _rmsnorm_rows(h, g):
    return h * lax.rsqrt(jnp.mean(h * h, axis=-1, keepdims=True) + EPS) * g


def _slot_matrix(slot):
    lane = lax.broadcasted_iota(jnp.int32, (slot.shape[0], MOE_SLOTS), 1)
    return jnp.where(lane == slot.astype(jnp.int32), 1.0, 0.0)


def _post_attn_body(x_ref, gm_ref, sb_ref, wo_ref, g2_ref, wrh_ref, wrl_ref, br_ref,
                    h_ref, route_ref, bucket_ref, gate_ref, cnt_ref):
    h = (x_ref[...]
         + jnp.dot(gm_ref[...], wo_ref[:GM_WIDTH, :], preferred_element_type=F32)
         + jnp.dot(sb_ref[...], wo_ref[GM_WIDTH:, :], preferred_element_type=F32))
    h_ref[...] = h
    hn = _rmsnorm_rows(h, g2_ref[...])
    hi = hn.astype(BF16)
    lo = (hn - hi.astype(F32)).astype(BF16)
    logits = (jnp.dot(hi, wrh_ref[...], preferred_element_type=F32)
              + jnp.dot(lo, wrh_ref[...], preferred_element_type=F32)
              + jnp.dot(hi, wrl_ref[...], preferred_element_type=F32)) + br_ref[...]
    comb, g_sel, gates = _route(logits)

    lane = lax.broadcasted_iota(jnp.int32, (MOE_SUB, LANES), 1)
    r = lax.broadcasted_iota(jnp.int32, (MOE_SUB, MOE_SUB), 0)
    c = lax.broadcasted_iota(jnp.int32, (MOE_SUB, MOE_SUB), 1)
    earlier = jnp.where(r > c, 1.0, 0.0).astype(BF16)
    for st in range(h.shape[0] // MOE_SUB):
        rows = slice(st * MOE_SUB, (st + 1) * MOE_SUB)
        g = g_sel[rows]
        onehot = jnp.where(lane == g, 1.0, 0.0)
        before = jnp.dot(earlier, onehot.astype(BF16), preferred_element_type=F32)
        rank = jnp.sum(onehot * before, axis=-1, keepdims=True)
        slot = jnp.where(rank < MOE_CAP, g.astype(F32) * MOE_CAP + rank, -1.0)
        place = _slot_matrix(slot).T.astype(BF16)
        bucket_ref[st] = jnp.dot(place, hi[rows], preferred_element_type=F32).astype(BF16)
        gt = gates[rows]
        g1 = gt.astype(BF16)
        g2 = (gt - g1.astype(F32)).astype(BF16)
        g3 = (gt - g1.astype(F32) - g2.astype(F32)).astype(BF16)
        gate_ref[st] = (jnp.dot(place, g1, preferred_element_type=F32)
                        + jnp.dot(place, g2, preferred_element_type=F32)
                        + jnp.dot(place, g3, preferred_element_type=F32))
        route_ref[rows, :] = jnp.where(lane == SLOT_LANE, slot, comb[rows])
        cnt_ref[st] = jnp.broadcast_to(jnp.sum(onehot, axis=0, keepdims=True), cnt_ref.shape[1:])


def _moe_ffn_body(x_ref, gate_ref, wg_ref, wu_ref, wd_ref, y_ref):
    x = x_ref[...].reshape(-1, D_MODEL)
    gates = gate_ref[...].reshape(-1, LANES)
    parts = []
    for e in range(EXPERTS_PER_GROUP):
        hg = jnp.dot(x, wg_ref[e], preferred_element_type=F32)
        hu = jnp.dot(x, wu_ref[e], preferred_element_type=F32)
        parts.append((hg / (1.0 + jnp.exp(-hg)) * hu * gates[:, e:e + 1]).astype(BF16))
    a = jnp.concatenate(parts, axis=-1)
    y = jnp.dot(a, wd_ref[...].reshape(GROUP_FF, D_MODEL), preferred_element_type=F32)
    y_ref[...] = y.reshape(y_ref.shape)


def _moe_combine_body(h_ref, route_ref, ys_ref, o_ref):
    for st in range(h_ref.shape[0] // MOE_SUB):
        rows = slice(st * MOE_SUB, (st + 1) * MOE_SUB)
        pick = _slot_matrix(route_ref[rows, SLOT_LANE:SLOT_LANE + 1]).astype(BF16)
        ys = ys_ref[st]
        hi = ys.astype(BF16)
        lo = (ys - hi.astype(F32)).astype(BF16)
        o_ref[rows, :] = (h_ref[rows, :] + jnp.dot(pick, hi, preferred_element_type=F32)
                          + jnp.dot(pick, lo, preferred_element_type=F32))


def _moe_dense_body(h_ref, route_ref, g2_ref, wg_ref, wu_ref, wd_ref, o_ref, hn_ref):
    g = pl.program_id(1)

    @pl.when(g == 0)
    def _():
        h = h_ref[...]
        o_ref[...] = h
        hn_ref[...] = _rmsnorm_rows(h, g2_ref[...]).astype(BF16)

    hn = hn_ref[...]
    comb = route_ref[...]
    lane = lax.broadcasted_iota(jnp.int32, comb.shape, 1)
    parts = []
    for e in range(EXPERTS_PER_GROUP):
        gate = jnp.sum(jnp.where(lane == N_EXPERT_GROUPS + g * EXPERTS_PER_GROUP + e, comb, 0.0),
                       axis=-1, keepdims=True)
        hg = jnp.dot(hn, wg_ref[e], preferred_element_type=F32)
        hu = jnp.dot(hn, wu_ref[e], preferred_element_type=F32)
        parts.append((hg / (1.0 + jnp.exp(-hg)) * hu * gate).astype(BF16))
    a = jnp.concatenate(parts, axis=-1)
    o_ref[...] += jnp.dot(a, wd_ref[...].reshape(GROUP_FF, D_MODEL), preferred_element_type=F32)


def _group_spec(a, index):
    return pl.BlockSpec((EXPERTS_PER_GROUP,) + a.shape[1:], index)


def _out_moe(x2d, gm, sb, params, *, tm):
    n = x2d.shape[0]
    wo, g2, wrh, wrl, br, wg, wu, wd = params
    n_sub = n // MOE_SUB
    sub = tm // MOE_SUB
    sup = min(MOE_SUPER, n_sub)
    params_1d = pltpu.CompilerParams(dimension_semantics=("parallel",), vmem_limit_bytes=VMEM_LIMIT)
    params_2d = pltpu.CompilerParams(
        dimension_semantics=("parallel", "arbitrary"), vmem_limit_bytes=VMEM_LIMIT)

    row = lambda w: pl.BlockSpec((tm, w), lambda i: (i, 0))
    full = lambda a: pl.BlockSpec(a.shape, lambda i: (0,) * a.ndim)
    per_sub = lambda rows, w: pl.BlockSpec((sub, rows, w), lambda i: (i, 0, 0))
    h, route, buckets, gates, cnt = pl.pallas_call(
        _post_attn_body,
        grid=(n // tm,),
        in_specs=[row(D_MODEL), row(GM_WIDTH), row(SB_WIDTH),
                  full(wo), full(g2), full(wrh), full(wrl), full(br)],
        out_specs=[row(D_MODEL), row(LANES), per_sub(MOE_SLOTS, D_MODEL),
                   per_sub(MOE_SLOTS, LANES), per_sub(8, LANES)],
        out_shape=[jax.ShapeDtypeStruct((n, D_MODEL), F32),
                   jax.ShapeDtypeStruct((n, LANES), F32),
                   jax.ShapeDtypeStruct((n_sub, MOE_SLOTS, D_MODEL), BF16),
                   jax.ShapeDtypeStruct((n_sub, MOE_SLOTS, LANES), F32),
                   jax.ShapeDtypeStruct((n_sub, 8, LANES), F32)],
        compiler_params=params_1d,
        name="post_attn",
    )(x2d, gm, sb, wo, g2, wrh, wrl, br)

    def sparse():
        by_group = lambda a: a.reshape(n_sub, N_EXPERT_GROUPS, MOE_CAP, a.shape[-1])
        bucket = lambda w: pl.BlockSpec((sup, None, MOE_CAP, w), lambda g, s: (s, g, 0, 0))
        ys = pl.pallas_call(
            _moe_ffn_body,
            grid=(N_EXPERT_GROUPS, n_sub // sup),
            in_specs=[bucket(D_MODEL), bucket(LANES)]
                     + [_group_spec(w, lambda g, s: (g, 0, 0)) for w in (wg, wu, wd)],
            out_specs=bucket(D_MODEL),
            out_shape=jax.ShapeDtypeStruct((n_sub, N_EXPERT_GROUPS, MOE_CAP, D_MODEL), F32),
            compiler_params=params_2d,
            name="moe_ffn",
        )(by_group(buckets), by_group(gates), wg, wu, wd)
        return pl.pallas_call(
            _moe_combine_body,
            grid=(n // tm,),
            in_specs=[row(D_MODEL), row(LANES), per_sub(MOE_SLOTS, D_MODEL)],
            out_specs=row(D_MODEL),
            out_shape=jax.ShapeDtypeStruct((n, D_MODEL), F32),
            compiler_params=params_1d,
            name="moe_combine",
        )(h, route, ys.reshape(n_sub, MOE_SLOTS, D_MODEL))

    def dense():
        row2 = lambda w: pl.BlockSpec((tm, w), lambda i, g: (i, 0))
        return pl.pallas_call(
            _moe_dense_body,
            grid=(n // tm, N_EXPERT_GROUPS),
            in_specs=[row2(D_MODEL), row2(LANES), pl.BlockSpec(g2.shape, lambda i, g: (0, 0))]
                     + [_group_spec(w, lambda i, g: (g, 0, 0)) for w in (wg, wu, wd)],
            out_specs=row2(D_MODEL),
            out_shape=jax.ShapeDtypeStruct((n, D_MODEL), F32),
            scratch_shapes=[pltpu.VMEM((tm, D_MODEL), BF16)],
            compiler_params=params_2d,
            name="moe_dense",
        )(h, route, g2, wg, wu, wd)

    return lax.cond(jnp.max(cnt) <= MOE_CAP, sparse, dense)


def _layer(layer, xp, xs, cache_k, cache_v, norm1_g, w_in, gm_v_norm_g, gm_w_s, gm_b_s, q_norm_g,
           k_norm_g, w_out, norm2_g, w_router_group, b_router_group, w_router_expert,
           b_router_expert, w_gate, w_up, w_down):
    b, s, d = xp.shape
    db, n, _ = xs.shape
    head_of = jnp.arange(SB_WIDTH) // SB_HEAD_DIM
    bd = jnp.where(head_of[:, None] == head_of[None, :], 1.0 / SB_HEAD_DIM, 0.0).astype(BF16)
    in_params = (norm1_g[None, :], w_in.astype(BF16), gm_v_norm_g[None, :],
                 jnp.tile(q_norm_g, SB_HEADS)[None, :], jnp.tile(k_norm_g, SB_HEADS)[None, :],
                 gm_w_s, gm_b_s.T, bd)

    wr = jnp.concatenate(
        [w_router_group, jnp.transpose(w_router_expert, (1, 0, 2)).reshape(d, N_EXPERTS)], axis=1)
    wr = jnp.pad(wr, ((0, 0), (0, LANES - wr.shape[1])))
    wr_hi = wr.astype(BF16)
    wr_lo = (wr - wr_hi.astype(F32)).astype(BF16)
    br = jnp.pad(jnp.concatenate([b_router_group, b_router_expert.reshape(-1)]),
                 (0, LANES - N_EXPERT_GROUPS - N_EXPERTS))[None, :]

    moe_params = (w_out.astype(BF16), norm2_g[None, :], wr_hi, wr_lo, br,
                  w_gate.astype(BF16), w_up.astype(BF16), w_down.astype(BF16))

    xp2 = xp.reshape(b * s, d)
    gm, kf, vf, qs, kb, vb = _mixer_in(xp2, in_params, tm=512, chunk=GM_CHUNK, stream=False, seq=s)
    shp = (b, s, SB_WIDTH)
    sb = _prompt_attn(qs.reshape(shp), kb.reshape(shp), vb.reshape(shp))
    yp = _out_moe(xp2, gm, sb.reshape(b * s, SB_WIDTH), moe_params, tm=512).reshape(b, s, d)

    xs2 = xs.reshape(db * n, d)
    gm_s, kf_s, vf_s, q_s, gv_s = _mixer_in(xs2, in_params, tm=db * n, chunk=n, stream=True)
    shs = (db, n, SB_HEADS, SB_HEAD_DIM)
    frame_minor = lambda c: jnp.transpose(c, (0, 1, 3, 4, 2))
    sb_s = _sample_attn(q_s.reshape(shs), kf_s.reshape(shs), vf_s.reshape(shs),
                        frame_minor(cache_k), frame_minor(cache_v), layer)
    sb_s = sb_s.reshape(db * n, SB_WIDTH).astype(BF16)
    ys = _out_moe(xs2, gm_s, sb_s, moe_params, tm=db * n).reshape(db, n, d)

    heads = (SB_HEADS, SB_HEAD_DIM)
    rows = lambda a: jnp.transpose(a.reshape(b, *heads, s), (0, 3, 1, 2))
    return (yp, ys, rows(kf), rows(vf),
            kf_s.reshape(db, n, *heads), vf_s.reshape(db, n, *heads),
            gv_s.reshape(db, n, GM_GROUPS, GM_GROUP_DIM))


def kernel(x_prompt, x_sample, cache_sb_k, cache_sb_v, norm1_g, w_in, gm_v_norm_g, gm_w_s, gm_b_s, q_norm_g, k_norm_g, w_out, norm2_g, w_router_group, b_router_group, w_router_expert, b_router_expert, w_gate, w_up, w_down):
    depth = w_in.shape[0]
    yp, ys = x_prompt, x_sample
    outs = [[] for _ in range(5)]
    for l in range(depth):
        yp, ys, *rest = _layer(
            l, yp, ys, cache_sb_k, cache_sb_v, norm1_g[l], w_in[l], gm_v_norm_g[l], gm_w_s[l],
            gm_b_s[l], q_norm_g[l], k_norm_g[l], w_out[l], norm2_g[l], w_router_group[l],
            b_router_group[l], w_router_expert[l], b_router_expert[l], w_gate[l], w_up[l],
            w_down[l])
        for acc, r in zip(outs, rest):
            acc.append(r)
    return (yp, ys) + tuple(jnp.stack(o, axis=0) for o in outs)
```

```python
import functools

import jax
import jax.numpy as jnp
from jax import lax
from jax.experimental import pallas as pl
from jax.experimental.pallas import tpu as pltpu

D_MODEL = 1024
GM_WIDTH = 512
GM_GROUPS = 4
GM_GROUP_DIM = 128
GM_CHUNK = 128
SB_WIDTH = 512
SB_HEAD_DIM = 64
SB_HEADS = 8
IN_WIDTH = 2 * GM_WIDTH + 3 * SB_WIDTH
N_EXPERT_GROUPS = 4
EXPERTS_PER_GROUP = 4
N_EXPERTS = 16
EXPERT_FF = 256
GROUP_FF = EXPERTS_PER_GROUP * EXPERT_FF
EPS = 1e-6

LANES = 128
HEAD_PAIRS = SB_WIDTH // LANES
SB_BLOCK = 128
SB_WIDE = 2 * SB_BLOCK
CACHE_BLOCK = 256
MOE_SUB = 256
MOE_CAP = 96
MOE_SLOTS = N_EXPERT_GROUPS * MOE_CAP
MOE_SUPER = 8
SLOT_LANE = 0
VMEM_LIMIT = 56 * 1024 * 1024

LOG2_E = 1.4426950408889634
Q_SCALE = SB_HEAD_DIM ** -0.5 * LOG2_E
SB_UNDERFLOW_BITS = 160.0

F32 = jnp.float32
BF16 = jnp.bfloat16
_NT = (((1,), (1,)), ((), ()))


def _gelu_tanh(x):
    return 0.5 * x * (1.0 + jnp.tanh(0.7978845608028654 * (x + 0.044715 * (x * x * x))))


def _softplus2(z):
    return jnp.maximum(z, 0.0) + jnp.log2(1.0 + jnp.exp2(-jnp.abs(z)))


def _suffix_matrix(n):
    r = lax.broadcasted_iota(jnp.int32, (n, n), 0)
    c = lax.broadcasted_iota(jnp.int32, (n, n), 1)
    return jnp.where(r > c, 1.0, 0.0).astype(BF16)


def _mixer_in_body(x_ref, g1_ref, w_in_ref, gvg_ref, qg_ref, kg_ref, ws_ref, bst_ref, bd_ref,
                   gm_ref, kf_ref, vf_ref, *rest, chunk, stream):
    x = x_ref[...]
    tm = x.shape[0]
    xn = x * lax.rsqrt(jnp.mean(x * x, axis=-1, keepdims=True) + EPS) * g1_ref[...]
    proj = jnp.dot(xn.astype(BF16), w_in_ref[...], preferred_element_type=F32)

    gu = _gelu_tanh(proj[:, :GM_WIDTH])
    gvr = _gelu_tanh(proj[:, GM_WIDTH:2 * GM_WIDTH])
    gv = gvr * lax.rsqrt(jnp.mean(gvr * gvr, axis=-1, keepdims=True) + EPS) * gvg_ref[...]

    o = 2 * GM_WIDTH
    q = proj[:, o:o + SB_WIDTH]
    k = proj[:, o + SB_WIDTH:o + 2 * SB_WIDTH]
    v = proj[:, o + 2 * SB_WIDTH:]
    bd = bd_ref[...]
    qms = jnp.dot((q * q).astype(BF16), bd, preferred_element_type=F32)
    kms = jnp.dot((k * k).astype(BF16), bd, preferred_element_type=F32)
    qs = q * lax.rsqrt(qms + EPS) * qg_ref[...] * Q_SCALE
    kn = k * lax.rsqrt(kms + EPS) * kg_ref[...]
    if stream:
        for h in range(SB_HEADS):
            lanes = slice(h * SB_HEAD_DIM, (h + 1) * SB_HEAD_DIM)
            kf_ref[:, h, :] = kn[:, lanes]
            vf_ref[:, h, :] = v[:, lanes]
            rest[0][:, h, :] = qs[:, lanes]
        rest[1][...] = gv
    else:
        kf_ref[...] = kn.T
        vf_ref[...] = v.T
        qs_ref, kb_ref, vb_ref = rest
        qs_ref[...] = qs.astype(BF16)
        kb_ref[...] = kn.astype(BF16)
        vb_ref[...] = v.astype(BF16)

    r = lax.broadcasted_iota(jnp.int32, (chunk, chunk), 0)
    c = lax.broadcasted_iota(jnp.int32, (chunk, chunk), 1)
    gvb = gv.astype(BF16)
    for g in range(GM_GROUPS):
        w = jnp.where(r >= c, ws_ref[g, :chunk, :chunk], 0.0).astype(BF16)
        b = bst_ref[:chunk, g:g + 1]
        lo = g * GM_GROUP_DIM
        for ci in range(tm // chunk):
            rows = slice(ci * chunk, (ci + 1) * chunk)
            mixed = jnp.dot(w, gvb[rows, lo:lo + GM_GROUP_DIM], preferred_element_type=F32) + b
            gm_ref[rows, lo:lo + GM_GROUP_DIM] = (gu[rows, lo:lo + GM_GROUP_DIM] * mixed).astype(BF16)


def _mixer_in(x2d, params, *, tm, chunk, stream, seq=None):
    n = x2d.shape[0]
    full = lambda a: pl.BlockSpec(a.shape, lambda i: (0,) * a.ndim)
    spec = lambda s: pl.BlockSpec((tm,) + s.shape[1:], lambda i: (i,) + (0,) * (len(s.shape) - 1))
    gm = jax.ShapeDtypeStruct((n, GM_WIDTH), BF16)
    if stream:
        heads = jax.ShapeDtypeStruct((n, SB_HEADS, SB_HEAD_DIM), F32)
        out_shape = [gm, heads, heads,
                     heads,
                     jax.ShapeDtypeStruct((n, GM_WIDTH), F32)]
        out_specs = [spec(s) for s in out_shape]
    else:
        per_seq = seq // tm
        rows_t = jax.ShapeDtypeStruct((n // seq, SB_WIDTH, seq), F32)
        spec_t = pl.BlockSpec((None, SB_WIDTH, tm), lambda i: (i // per_seq, 0, i % per_seq))
        dense = jax.ShapeDtypeStruct((n, SB_WIDTH), BF16)
        out_shape = [gm, rows_t, rows_t, dense, dense, dense]
        out_specs = [spec(gm), spec_t, spec_t] + [spec(dense)] * 3
    return pl.pallas_call(
        functools.partial(_mixer_in_body, chunk=chunk, stream=stream),
        grid=(n // tm,),
        in_specs=[pl.BlockSpec((tm, D_MODEL), lambda i: (i, 0))] + [full(a) for a in params],
        out_specs=out_specs,
        out_shape=out_shape,
        compiler_params=pltpu.CompilerParams(
            dimension_semantics=("parallel",), vmem_limit_bytes=VMEM_LIMIT),
        name="mixer_in_stream" if stream else "mixer_in",
    )(x2d, *params)


def _sb_step(qst, kj, vj, suffix, c, causal=None):
    z = lax.dot_general(qst, kj, _NT, preferred_element_type=F32)
    sp = _softplus2(z)
    if causal is not None:
        sp = jnp.where(causal, sp, 0.0)
    later = jnp.dot(sp.astype(BF16), suffix, preferred_element_type=F32) + c
    w = jnp.exp2(z - sp - later)
    if causal is not None:
        w = jnp.where(causal, w, 0.0)
    pv = jnp.dot(w.astype(BF16), vj, preferred_element_type=F32)
    return pv, c + jnp.sum(sp, axis=-1, keepdims=True)


def _prompt_attn_body(q_ref, k_ref, v_ref, o_ref, qst_ref, c_ref, acc_ref):
    i = pl.program_id(1)
    lane = lax.broadcasted_iota(jnp.int32, (SB_BLOCK, LANES), 1)
    first = lane < SB_HEAD_DIM
    for hp in range(HEAD_PAIRS):
        q = q_ref[:, hp * LANES:(hp + 1) * LANES]
        zero = jnp.zeros_like(q)
        qst_ref[hp, :SB_BLOCK, :] = jnp.where(first, q, zero)
        qst_ref[hp, SB_BLOCK:, :] = jnp.where(first, zero, q)
    suffix = _suffix_matrix(SB_BLOCK)
    m = 2 * SB_BLOCK

    wide = _suffix_matrix(SB_WIDE)

    def sweep(j, keys, causal, fresh):
        rows = pl.ds(pl.multiple_of(j * SB_BLOCK, SB_BLOCK), keys)
        cols = lambda hp: slice(hp * LANES, (hp + 1) * LANES)
        z = jnp.concatenate(
            [lax.dot_general(qst_ref[hp], k_ref[rows, cols(hp)], _NT, preferred_element_type=F32)
             for hp in range(HEAD_PAIRS)], axis=0)
        sp = _softplus2(z)
        if causal is not None:
            sp = jnp.where(causal, sp, 0.0)
        c0 = jnp.zeros((HEAD_PAIRS * m, 1), F32) if fresh else c_ref[...]
        later = jnp.dot(sp.astype(BF16), suffix if keys == SB_BLOCK else wide,
                        preferred_element_type=F32) + c0
        w = jnp.exp2(z - sp - later)
        if causal is not None:
            w = jnp.where(causal, w, 0.0)
        wb = w.astype(BF16)
        pv = jnp.concatenate(
            [jnp.dot(wb[hp * m:(hp + 1) * m, :], v_ref[rows, cols(hp)], preferred_element_type=F32)
             for hp in range(HEAD_PAIRS)], axis=0)
        acc_ref[...] = pv if fresh else acc_ref[...] + pv
        c_ref[...] = c0 + jnp.sum(sp, axis=-1, keepdims=True)

    t = lax.broadcasted_iota(jnp.int32, (HEAD_PAIRS * m, SB_BLOCK), 0) % SB_BLOCK
    s = lax.broadcasted_iota(jnp.int32, (HEAD_PAIRS * m, SB_BLOCK), 1)
    sweep(i, SB_BLOCK, s < t, True)

    def least_carry():
        return jnp.min(c_ref[...])

    per_wide = SB_WIDE // SB_BLOCK
    n_wide = i // per_wide

    def more(state):
        jj, carry = state
        return (jj < n_wide) & (carry < SB_UNDERFLOW_BITS)

    def body(state):
        jj, _ = state
        sweep(i - per_wide * (jj + 1), SB_WIDE, None, False)
        return jj + 1, least_carry()

    done, carry = lax.while_loop(more, body, (jnp.int32(0), least_carry()))

    @pl.when((done == n_wide) & (i % per_wide == 1) & (carry < SB_UNDERFLOW_BITS))
    def _():
        sweep(0, SB_BLOCK, None, False)

    for hp in range(HEAD_PAIRS):
        lo = hp * m
        o_ref[:, hp * LANES:(hp + 1) * LANES] = jnp.where(
            first, acc_ref[lo:lo + SB_BLOCK, :], acc_ref[lo + SB_BLOCK:lo + m, :]).astype(o_ref.dtype)


def _prompt_attn(qs, kb, vb):
    b, s, _ = qs.shape
    q_spec = pl.BlockSpec((None, SB_BLOCK, SB_WIDTH), lambda bi, i: (bi, i, 0))
    kv_spec = pl.BlockSpec((None, s, SB_WIDTH), lambda bi, i: (bi, 0, 0))
    m = 2 * SB_BLOCK
    return pl.pallas_call(
        _prompt_attn_body,
        grid=(b, s // SB_BLOCK),
        in_specs=[q_spec, kv_spec, kv_spec],
        out_specs=q_spec,
        out_shape=jax.ShapeDtypeStruct((b, s, SB_WIDTH), BF16),
        scratch_shapes=[pltpu.VMEM((HEAD_PAIRS, m, LANES), BF16),
                        pltpu.VMEM((HEAD_PAIRS * m, 1), F32),
                        pltpu.VMEM((HEAD_PAIRS * m, LANES), F32)],
        compiler_params=pltpu.CompilerParams(
            dimension_semantics=("parallel", "arbitrary"), vmem_limit_bytes=VMEM_LIMIT),
        name="prompt_attn",
    )(qs, kb, vb)


def _sample_attn_body(q_ref, kn_ref, vn_ref, ck_hbm, cv_hbm, o_ref, kbuf, vbuf, sems, c_ref, acc_ref,
                      *, layer):
    b = pl.program_id(0)
    n = q_ref.shape[0]
    m = SB_HEADS * n
    nblk = ck_hbm.shape[-1] // CACHE_BLOCK

    def copies(stream, blk, slot):
        cols = pl.ds(blk * CACHE_BLOCK, CACHE_BLOCK)
        return (pltpu.make_async_copy(ck_hbm.at[layer, stream, :, :, cols], kbuf.at[slot], sems.at[slot, 0]),
                pltpu.make_async_copy(cv_hbm.at[layer, stream, :, :, cols], vbuf.at[slot], sems.at[slot, 1]))

    def start(stream, blk, slot):
        for cp in copies(stream, blk, slot):
            cp.start()

    def wait(stream, blk, slot):
        for cp in copies(stream, blk, slot):
            cp.wait()

    def slot_of(jj):
        return jnp.where(jj == 0, b % 2, 2 + jj % 2)

    @pl.when(b == 0)
    def _():
        start(0, nblk - 1, 0)

    @pl.when(b + 1 < pl.num_programs(0))
    def _():
        start(b + 1, nblk - 1, (b + 1) % 2)

    q = [q_ref[:, h, :].astype(BF16) for h in range(SB_HEADS)]

    def block(k_of, v_of, suffix, causal, fresh, dim_major):
        qk = lambda a, k: (jnp.dot(a, k, preferred_element_type=F32) if dim_major
                           else lax.dot_general(a, k, _NT, preferred_element_type=F32))
        wv = lambda a, v: (lax.dot_general(a, v, _NT, preferred_element_type=F32) if dim_major
                           else jnp.dot(a, v, preferred_element_type=F32))
        z = jnp.concatenate([qk(q[h], k_of(h)) for h in range(SB_HEADS)], axis=0)
        sp = _softplus2(z)
        if causal is not None:
            sp = jnp.where(causal, sp, 0.0)
        c0 = jnp.zeros((m, 1), F32) if fresh else c_ref[...]
        later = jnp.dot(sp.astype(BF16), suffix, preferred_element_type=F32) + c0
        w = jnp.exp2(z - sp - later)
        if causal is not None:
            w = jnp.where(causal, w, 0.0)
        wb = w.astype(BF16)
        for h in range(SB_HEADS):
            pv = wv(wb[h * n:(h + 1) * n, :], v_of(h))
            acc_ref[h] = pv if fresh else acc_ref[h] + pv
        c_ref[...] = c0 + jnp.sum(sp, axis=-1, keepdims=True)

    pad = jnp.zeros((LANES - n, SB_HEAD_DIM), BF16)
    t = lax.broadcasted_iota(jnp.int32, (m, LANES), 0) % n
    s = lax.broadcasted_iota(jnp.int32, (m, LANES), 1)
    block(lambda h: jnp.concatenate([kn_ref[:, h, :].astype(BF16), pad], axis=0),
          lambda h: jnp.concatenate([vn_ref[:, h, :].astype(BF16), pad], axis=0),
          _suffix_matrix(LANES), s < t, True, False)

    suffix = _suffix_matrix(CACHE_BLOCK)

    def more(state):
        jj, carry = state
        return (jj < nblk) & (carry < SB_UNDERFLOW_BITS)

    def body(state):
        jj, _ = state
        blk = nblk - 1 - jj
        slot = slot_of(jj)
        wait(b, blk, slot)

        @pl.when(blk > 0)
        def _():
            start(b, blk - 1, slot_of(jj + 1))

        block(lambda h: kbuf[slot, h].astype(BF16), lambda h: vbuf[slot, h].astype(BF16),
              suffix, None, False, True)
        return jj + 1, jnp.min(c_ref[...])

    done, _ = lax.while_loop(more, body, (jnp.int32(0), jnp.min(c_ref[...])))

    @pl.when(done < nblk)
    def _():
        wait(b, nblk - 1 - done, slot_of(done))

    for h in range(SB_HEADS):
        o_ref[:, h, :] = acc_ref[h]


def _sample_attn(q4, kn4, vn4, cache_k, cache_v, layer):
    db, n = q4.shape[:2]
    m = SB_HEADS * n
    new_spec = pl.BlockSpec((None, n, SB_HEADS, SB_HEAD_DIM), lambda b: (b, 0, 0, 0))
    hbm = pl.BlockSpec(memory_space=pl.ANY)
    buf = pltpu.VMEM((4, SB_HEADS, SB_HEAD_DIM, CACHE_BLOCK), F32)
    return pl.pallas_call(
        functools.partial(_sample_attn_body, layer=layer),
        grid=(db,),
        in_specs=[new_spec, new_spec, new_spec, hbm, hbm],
        out_specs=new_spec,
        out_shape=jax.ShapeDtypeStruct(q4.shape, F32),
        scratch_shapes=[buf, buf, pltpu.SemaphoreType.DMA((4, 2)),
                        pltpu.VMEM((m, 1), F32),
                        pltpu.VMEM((SB_HEADS, n, SB_HEAD_DIM), F32)],
        compiler_params=pltpu.CompilerParams(
            dimension_semantics=("arbitrary",), vmem_limit_bytes=VMEM_LIMIT),
        name="sample_attn",
    )(q4, kn4, vn4, cache_k, cache_v)


def _route(logits):
    lane = lax.broadcasted_iota(jnp.int32, logits.shape, 1)
    neg = jnp.float32(-jnp.inf)
    big = jnp.int32(1 << 20)
    rmax = lambda a: jnp.max(a, axis=-1, keepdims=True)
    rmin = lambda a: jnp.min(a, axis=-1, keepdims=True)
    rsum = lambda a: jnp.sum(a, axis=-1, keepdims=True)

    gmask = lane < N_EXPERT_GROUPS
    gl = jnp.where(gmask, logits, neg)
    gmax = rmax(gl)
    g_sel = rmin(jnp.where(gl == gmax, lane, big))
    p_sel = 1.0 / rsum(jnp.where(gmask, jnp.exp(gl - gmax), 0.0))

    e_lo = N_EXPERT_GROUPS + g_sel * EXPERTS_PER_GROUP
    emask = (lane >= e_lo) & (lane < e_lo + EXPERTS_PER_GROUP)
    el = jnp.where(emask, logits, neg)
    emax = rmax(el)
    ex = jnp.where(emask, jnp.exp(el - emax), 0.0)
    p_e = ex / rsum(ex)
    pm = jnp.where(emask, p_e, -1.0)
    v1 = rmax(pm)
    i1 = rmin(jnp.where(pm == v1, lane, big))
    pm2 = jnp.where(lane == i1, -1.0, pm)
    v2 = rmax(pm2)
    i2 = rmin(jnp.where(pm2 == v2, lane, big))
    tot = v1 + v2
    comb = (jnp.where(lane == i1, p_sel * v1 / tot, 0.0)
            + jnp.where(lane == i2, p_sel * v2 / tot, 0.0))
    gates = jnp.zeros_like(comb)
    for e in range(EXPERTS_PER_GROUP):
        gates = jnp.where(lane == e, rsum(jnp.where(lane == e_lo + e, comb, 0.0)), gates)
    return comb, g_sel, gates


def _rmsnorm_rows(h, g):
    return h * lax.rsqrt(jnp.mean(h * h, axis=-1, keepdims=True) + EPS) * g


def _slot_matrix(slot):
    lane = lax.broadcasted_iota(jnp.int32, (slot.shape[0], MOE_SLOTS), 1)
    return jnp.where(lane == slot.astype(jnp.int32), 1.0, 0.0)


def _post_attn_body(x_ref, gm_ref, sb_ref, wo_ref, g2_ref, wrh_ref, wrl_ref, br_ref,
                    h_ref, route_ref, bucket_ref, gate_ref, cnt_ref):
    h = (x_ref[...]
         + jnp.dot(gm_ref[...], wo_ref[:GM_WIDTH, :], preferred_element_type=F32)
         + jnp.dot(sb_ref[...], wo_ref[GM_WIDTH:, :], preferred_element_type=F32))
    h_ref[...] = h
    hn = _rmsnorm_rows(h, g2_ref[...])
    hi = hn.astype(BF16)
    lo = (hn - hi.astype(F32)).astype(BF16)
    logits = (jnp.dot(hi, wrh_ref[...], preferred_element_type=F32)
              + jnp.dot(lo, wrh_ref[...], preferred_element_type=F32)
              + jnp.dot(hi, wrl_ref[...], preferred_element_type=F32)) + br_ref[...]
    comb, g_sel, gates = _route(logits)

    lane = lax.broadcasted_iota(jnp.int32, (MOE_SUB, LANES), 1)
    r = lax.broadcasted_iota(jnp.int32, (MOE_SUB, MOE_SUB), 0)
    c = lax.broadcasted_iota(jnp.int32, (MOE_SUB, MOE_SUB), 1)
    earlier = jnp.where(r > c, 1.0, 0.0).astype(BF16)
    for st in range(h.shape[0] // MOE_SUB):
        rows = slice(st * MOE_SUB, (st + 1) * MOE_SUB)
        g = g_sel[rows]
        onehot = jnp.where(lane == g, 1.0, 0.0)
        before = jnp.dot(earlier, onehot.astype(BF16), preferred_element_type=F32)
        rank = jnp.sum(onehot * before, axis=-1, keepdims=True)
        slot = jnp.where(rank < MOE_CAP, g.astype(F32) * MOE_CAP + rank, -1.0)
        place = _slot_matrix(slot).T.astype(BF16)
        bucket_ref[st] = jnp.dot(place, hi[rows], preferred_element_type=F32).astype(BF16)
        gt = gates[rows]
        g1 = gt.astype(BF16)
        g2 = (gt - g1.astype(F32)).astype(BF16)
        g3 = (gt - g1.astype(F32) - g2.astype(F32)).astype(BF16)
        gate_ref[st] = (jnp.dot(place, g1, preferred_element_type=F32)
                        + jnp.dot(place, g2, preferred_element_type=F32)
                        + jnp.dot(place, g3, preferred_element_type=F32))
        route_ref[rows, :] = jnp.where(lane == SLOT_LANE, slot, comb[rows])
        cnt_ref[st] = jnp.broadcast_to(jnp.sum(onehot, axis=0, keepdims=True), cnt_ref.shape[1:])


def _moe_ffn_body(x_ref, gate_ref, wg_ref, wu_ref, wd_ref, y_ref):
    x = x_ref[...].reshape(-1, D_MODEL)
    gates = gate_ref[...].reshape(-1, LANES)
    parts = []
    for e in range(EXPERTS_PER_GROUP):
        hg = jnp.dot(x, wg_ref[e], preferred_element_type=F32)
        hu = jnp.dot(x, wu_ref[e], preferred_element_type=F32)
        parts.append((hg / (1.0 + jnp.exp(-hg)) * hu * gates[:, e:e + 1]).astype(BF16))
    a = jnp.concatenate(parts, axis=-1)
    y = jnp.dot(a, wd_ref[...].reshape(GROUP_FF, D_MODEL), preferred_element_type=F32)
    y_ref[...] = y.reshape(y_ref.shape)


def _moe_combine_body(h_ref, route_ref, ys_ref, o_ref):
    for st in range(h_ref.shape[0] // MOE_SUB):
        rows = slice(st * MOE_SUB, (st + 1) * MOE_SUB)
        pick = _slot_matrix(route_ref[rows, SLOT_LANE:SLOT_LANE + 1]).astype(BF16)
        ys = ys_ref[st]
        hi = ys.astype(BF16)
        lo = (ys - hi.astype(F32)).astype(BF16)
        o_ref[rows, :] = (h_ref[rows, :] + jnp.dot(pick, hi, preferred_element_type=F32)
                          + jnp.dot(pick, lo, preferred_element_type=F32))


def _moe_dense_body(h_ref, route_ref, g2_ref, wg_ref, wu_ref, wd_ref, o_ref, hn_ref):
    g = pl.program_id(1)

    @pl.when(g == 0)
    def _():
        h = h_ref[...]
        o_ref[...] = h
        hn_ref[...] = _rmsnorm_rows(h, g2_ref[...]).astype(BF16)

    hn = hn_ref[...]
    comb = route_ref[...]
    lane = lax.broadcasted_iota(jnp.int32, comb.shape, 1)
    parts = []
    for e in range(EXPERTS_PER_GROUP):
        gate = jnp.sum(jnp.where(lane == N_EXPERT_GROUPS + g * EXPERTS_PER_GROUP + e, comb, 0.0),
                       axis=-1, keepdims=True)
        hg = jnp.dot(hn, wg_ref[e], preferred_element_type=F32)
        hu = jnp.dot(hn, wu_ref[e], preferred_element_type=F32)
        parts.append((hg / (1.0 + jnp.exp(-hg)) * hu * gate).astype(BF16))
    a = jnp.concatenate(parts, axis=-1)
    o_ref[...] += jnp.dot(a, wd_ref[...].reshape(GROUP_FF, D_MODEL), preferred_element_type=F32)


def _group_spec(a, index):
    return pl.BlockSpec((EXPERTS_PER_GROUP,) + a.shape[1:], index)


def _out_moe(x2d, gm, sb, params, *, tm):
    n = x2d.shape[0]
    wo, g2, wrh, wrl, br, wg, wu, wd = params
    assert n % tm == 0 and tm % MOE_SUB == 0, (n, tm)
    n_sub = n // MOE_SUB
    sub = tm // MOE_SUB
    sup = min(MOE_SUPER, n_sub)
    params_1d = pltpu.CompilerParams(dimension_semantics=("parallel",), vmem_limit_bytes=VMEM_LIMIT)
    params_2d = pltpu.CompilerParams(
        dimension_semantics=("parallel", "arbitrary"), vmem_limit_bytes=VMEM_LIMIT)

    row = lambda w: pl.BlockSpec((tm, w), lambda i: (i, 0))
    full = lambda a: pl.BlockSpec(a.shape, lambda i: (0,) * a.ndim)
    per_sub = lambda rows, w: pl.BlockSpec((sub, rows, w), lambda i: (i, 0, 0))
    h, route, buckets, gates, cnt = pl.pallas_call(
        _post_attn_body,
        grid=(n // tm,),
        in_specs=[row(D_MODEL), row(GM_WIDTH), row(SB_WIDTH),
                  full(wo), full(g2), full(wrh), full(wrl), full(br)],
        out_specs=[row(D_MODEL), row(LANES), per_sub(MOE_SLOTS, D_MODEL),
                   per_sub(MOE_SLOTS, LANES), per_sub(8, LANES)],
        out_shape=[jax.ShapeDtypeStruct((n, D_MODEL), F32),
                   jax.ShapeDtypeStruct((n, LANES), F32),
                   jax.ShapeDtypeStruct((n_sub, MOE_SLOTS, D_MODEL), BF16),
                   jax.ShapeDtypeStruct((n_sub, MOE_SLOTS, LANES), F32),
                   jax.ShapeDtypeStruct((n_sub, 8, LANES), F32)],
        compiler_params=params_1d,
        name="post_attn",
    )(x2d, gm, sb, wo, g2, wrh, wrl, br)

    def sparse():
        by_group = lambda a: a.reshape(n_sub, N_EXPERT_GROUPS, MOE_CAP, a.shape[-1])
        bucket = lambda w: pl.BlockSpec((sup, 1, MOE_CAP, w), lambda g, s: (s, g, 0, 0))
        ys = pl.pallas_call(
            _moe_ffn_body,
            grid=(N_EXPERT_GROUPS, n_sub // sup),
            in_specs=[bucket(D_MODEL), bucket(LANES)]
                     + [_group_spec(w, lambda g, s: (g, 0, 0)) for w in (wg, wu, wd)],
            out_specs=bucket(D_MODEL),
            out_shape=jax.ShapeDtypeStruct((n_sub, N_EXPERT_GROUPS, MOE_CAP, D_MODEL), F32),
            compiler_params=params_2d,
            name="moe_ffn",
        )(by_group(buckets), by_group(gates), wg, wu, wd)
        return pl.pallas_call(
            _moe_combine_body,
            grid=(n // tm,),
            in_specs=[row(D_MODEL), row(LANES), per_sub(MOE_SLOTS, D_MODEL)],
            out_specs=row(D_MODEL),
            out_shape=jax.ShapeDtypeStruct((n, D_MODEL), F32),
            compiler_params=params_1d,
            name="moe_combine",
        )(h, route, ys.reshape(n_sub, MOE_SLOTS, D_MODEL))

    def dense():
        row2 = lambda w: pl.BlockSpec((tm, w), lambda i, g: (i, 0))
        return pl.pallas_call(
            _moe_dense_body,
            grid=(n // tm, N_EXPERT_GROUPS),
            in_specs=[row2(D_MODEL), row2(LANES), pl.BlockSpec(g2.shape, lambda i, g: (0, 0))]
                     + [_group_spec(w, lambda i, g: (g, 0, 0)) for w in (wg, wu, wd)],
            out_specs=row2(D_MODEL),
            out_shape=jax.ShapeDtypeStruct((n, D_MODEL), F32),
            scratch_shapes=[pltpu.VMEM((tm, D_MODEL), BF16)],
            compiler_params=params_2d,
            name="moe_dense",
        )(h, route, g2, wg, wu, wd)

    return lax.cond(jnp.max(cnt) <= MOE_CAP, sparse, dense)


def _layer(layer, xp, xs, cache_k, cache_v, norm1_g, w_in, gm_v_norm_g, gm_w_s, gm_b_s, q_norm_g,
           k_norm_g, w_out, norm2_g, w_router_group, b_router_group, w_router_expert,
           b_router_expert, w_gate, w_up, w_down):
    b, s, d = xp.shape
    db, n, _ = xs.shape
    head_of = jnp.arange(SB_WIDTH) // SB_HEAD_DIM
    bd = jnp.where(head_of[:, None] == head_of[None, :], 1.0 / SB_HEAD_DIM, 0.0).astype(BF16)
    in_params = (norm1_g[None, :], w_in.astype(BF16), gm_v_norm_g[None, :],
                 jnp.tile(q_norm_g, SB_HEADS)[None, :], jnp.tile(k_norm_g, SB_HEADS)[None, :],
                 gm_w_s, gm_b_s.T, bd)

    wr = jnp.concatenate(
        [w_router_group, jnp.transpose(w_router_expert, (1, 0, 2)).reshape(d, N_EXPERTS)], axis=1)
    wr = jnp.pad(wr, ((0, 0), (0, LANES - wr.shape[1])))
    wr_hi = wr.astype(BF16)
    wr_lo = (wr - wr_hi.astype(F32)).astype(BF16)
    br = jnp.pad(jnp.concatenate([b_router_group, b_router_expert.reshape(-1)]),
                 (0, LANES - N_EXPERT_GROUPS - N_EXPERTS))[None, :]

    moe_params = (w_out.astype(BF16), norm2_g[None, :], wr_hi, wr_lo, br,
                  w_gate.astype(BF16), w_up.astype(BF16), w_down.astype(BF16))

    xp2 = xp.reshape(b * s, d)
    gm, kf, vf, qs, kb, vb = _mixer_in(xp2, in_params, tm=512, chunk=GM_CHUNK, stream=False, seq=s)
    shp = (b, s, SB_WIDTH)
    sb = _prompt_attn(qs.reshape(shp), kb.reshape(shp), vb.reshape(shp))
    yp = _out_moe(xp2, gm, sb.reshape(b * s, SB_WIDTH), moe_params, tm=512).reshape(b, s, d)

    xs2 = xs.reshape(db * n, d)
    gm_s, kf_s, vf_s, q_s, gv_s = _mixer_in(xs2, in_params, tm=db * n, chunk=n, stream=True)
    shs = (db, n, SB_HEADS, SB_HEAD_DIM)
    frame_minor = lambda c: jnp.transpose(c, (0, 1, 3, 4, 2))
    sb_s = _sample_attn(q_s.reshape(shs), kf_s.reshape(shs), vf_s.reshape(shs),
                        frame_minor(cache_k), frame_minor(cache_v), layer)
    sb_s = sb_s.reshape(db * n, SB_WIDTH).astype(BF16)
    ys = _out_moe(xs2, gm_s, sb_s, moe_params, tm=db * n).reshape(db, n, d)

    heads = (SB_HEADS, SB_HEAD_DIM)
    rows = lambda a: jnp.transpose(a.reshape(b, *heads, s), (0, 3, 1, 2))
    return (yp, ys, rows(kf), rows(vf),
            kf_s.reshape(db, n, *heads), vf_s.reshape(db, n, *heads),
            gv_s.reshape(db, n, GM_GROUPS, GM_GROUP_DIM))


def kernel(x_prompt, x_sample, cache_sb_k, cache_sb_v, norm1_g, w_in, gm_v_norm_g, gm_w_s, gm_b_s, q_norm_g, k_norm_g, w_out, norm2_g, w_router_group, b_router_group, w_router_expert, b_router_expert, w_gate, w_up, w_down):
    depth = w_in.shape[0]
    yp, ys = x_prompt, x_sample
    outs = [[] for _ in range(5)]
    for l in range(depth):
        yp, ys, *rest = _layer(
            l, yp, ys, cache_sb_k, cache_sb_v, norm1_g[l], w_in[l], gm_v_norm_g[l], gm_w_s[l],
            gm_b_s[l], q_norm_g[l], k_norm_g[l], w_out[l], norm2_g[l], w_router_group[l],
            b_router_group[l], w_router_expert[l], b_router_expert[l], w_gate[l], w_up[l],
            w_down[l])
        for acc, r in zip(outs, rest):
            acc.append(r)
    return (yp, ys) + tuple(jnp.stack(o, axis=0) for o in outs)
```

```python
import functools

import jax
import jax.numpy as jnp
from jax import lax
from jax.experimental import pallas as pl
from jax.experimental.pallas import tpu as pltpu

D_MODEL = 1024
GM_WIDTH = 512
GM_GROUPS = 4
GM_GROUP_DIM = 128
GM_CHUNK = 128
SB_WIDTH = 512
SB_HEAD_DIM = 64
SB_HEADS = 8
IN_WIDTH = 2 * GM_WIDTH + 3 * SB_WIDTH
N_EXPERT_GROUPS = 4
EXPERTS_PER_GROUP = 4
N_EXPERTS = 16
EXPERT_FF = 256
GROUP_FF = EXPERTS_PER_GROUP * EXPERT_FF
EPS = 1e-6

LANES = 128
HEAD_PAIRS = SB_WIDTH // LANES
SB_BLOCK = 128
SB_WIDE = 2 * SB_BLOCK
CACHE_BLOCK = 256
MOE_SUB = 256
MOE_CAP = 96
MOE_SLOTS = N_EXPERT_GROUPS * MOE_CAP
MOE_SUPER = 8
SLOT_LANE = 0
VMEM_LIMIT = 56 * 1024 * 1024

LOG2_E = 1.4426950408889634
Q_SCALE = SB_HEAD_DIM ** -0.5 * LOG2_E
SB_UNDERFLOW_BITS = 160.0

F32 = jnp.float32
BF16 = jnp.bfloat16
_NT = (((1,), (1,)), ((), ()))


def _gelu_tanh(x):
    return 0.5 * x * (1.0 + jnp.tanh(0.7978845608028654 * (x + 0.044715 * (x * x * x))))


def _softplus2(z):
    return jnp.maximum(z, 0.0) + jnp.log2(1.0 + jnp.exp2(-jnp.abs(z)))


def _suffix_matrix(n):
    r = lax.broadcasted_iota(jnp.int32, (n, n), 0)
    c = lax.broadcasted_iota(jnp.int32, (n, n), 1)
    return jnp.where(r > c, 1.0, 0.0).astype(BF16)


def _mixer_in_body(x_ref, g1_ref, w_in_ref, gvg_ref, qg_ref, kg_ref, ws_ref, bst_ref, bd_ref,
                   gm_ref, kf_ref, vf_ref, *rest, chunk, stream):
    x = x_ref[...]
    tm = x.shape[0]
    xn = x * lax.rsqrt(jnp.mean(x * x, axis=-1, keepdims=True) + EPS) * g1_ref[...]
    proj = jnp.dot(xn.astype(BF16), w_in_ref[...], preferred_element_type=F32)

    gu = _gelu_tanh(proj[:, :GM_WIDTH])
    gvr = _gelu_tanh(proj[:, GM_WIDTH:2 * GM_WIDTH])
    gv = gvr * lax.rsqrt(jnp.mean(gvr * gvr, axis=-1, keepdims=True) + EPS) * gvg_ref[...]

    o = 2 * GM_WIDTH
    q = proj[:, o:o + SB_WIDTH]
    k = proj[:, o + SB_WIDTH:o + 2 * SB_WIDTH]
    v = proj[:, o + 2 * SB_WIDTH:]
    bd = bd_ref[...]
    qms = jnp.dot((q * q).astype(BF16), bd, preferred_element_type=F32)
    kms = jnp.dot((k * k).astype(BF16), bd, preferred_element_type=F32)
    qs = q * lax.rsqrt(qms + EPS) * qg_ref[...] * Q_SCALE
    kn = k * lax.rsqrt(kms + EPS) * kg_ref[...]
    if stream:
        for h in range(SB_HEADS):
            lanes = slice(h * SB_HEAD_DIM, (h + 1) * SB_HEAD_DIM)
            kf_ref[:, h, :] = kn[:, lanes]
            vf_ref[:, h, :] = v[:, lanes]
            rest[0][:, h, :] = qs[:, lanes]
        rest[1][...] = gv
    else:
        kf_ref[...] = kn.T
        vf_ref[...] = v.T
        qs_ref, kb_ref, vb_ref = rest
        qs_ref[...] = qs.astype(BF16)
        kb_ref[...] = kn.astype(BF16)
        vb_ref[...] = v.astype(BF16)

    r = lax.broadcasted_iota(jnp.int32, (chunk, chunk), 0)
    c = lax.broadcasted_iota(jnp.int32, (chunk, chunk), 1)
    gvb = gv.astype(BF16)
    for g in range(GM_GROUPS):
        w = jnp.where(r >= c, ws_ref[g, :chunk, :chunk], 0.0).astype(BF16)
        b = bst_ref[:chunk, g:g + 1]
        lo = g * GM_GROUP_DIM
        for ci in range(tm // chunk):
            rows = slice(ci * chunk, (ci + 1) * chunk)
            mixed = jnp.dot(w, gvb[rows, lo:lo + GM_GROUP_DIM], preferred_element_type=F32) + b
            gm_ref[rows, lo:lo + GM_GROUP_DIM] = (gu[rows, lo:lo + GM_GROUP_DIM] * mixed).astype(BF16)


def _mixer_in(x2d, params, *, tm, chunk, stream, seq=None):
    n = x2d.shape[0]
    full = lambda a: pl.BlockSpec(a.shape, lambda i: (0,) * a.ndim)
    spec = lambda s: pl.BlockSpec((tm,) + s.shape[1:], lambda i: (i,) + (0,) * (len(s.shape) - 1))
    gm = jax.ShapeDtypeStruct((n, GM_WIDTH), BF16)
    if stream:
        heads = jax.ShapeDtypeStruct((n, SB_HEADS, SB_HEAD_DIM), F32)
        out_shape = [gm, heads, heads,
                     heads,
                     jax.ShapeDtypeStruct((n, GM_WIDTH), F32)]
        out_specs = [spec(s) for s in out_shape]
    else:
        per_seq = seq // tm
        rows_t = jax.ShapeDtypeStruct((n // seq, SB_WIDTH, seq), F32)
        spec_t = pl.BlockSpec((None, SB_WIDTH, tm), lambda i: (i // per_seq, 0, i % per_seq))
        dense = jax.ShapeDtypeStruct((n, SB_WIDTH), BF16)
        out_shape = [gm, rows_t, rows_t, dense, dense, dense]
        out_specs = [spec(gm), spec_t, spec_t] + [spec(dense)] * 3
    return pl.pallas_call(
        functools.partial(_mixer_in_body, chunk=chunk, stream=stream),
        grid=(n // tm,),
        in_specs=[pl.BlockSpec((tm, D_MODEL), lambda i: (i, 0))] + [full(a) for a in params],
        out_specs=out_specs,
        out_shape=out_shape,
        compiler_params=pltpu.CompilerParams(
            dimension_semantics=("parallel",), vmem_limit_bytes=VMEM_LIMIT),
        name="mixer_in_stream" if stream else "mixer_in",
    )(x2d, *params)


def _sb_step(qst, kj, vj, suffix, c, causal=None):
    z = lax.dot_general(qst, kj, _NT, preferred_element_type=F32)
    sp = _softplus2(z)
    if causal is not None:
        sp = jnp.where(causal, sp, 0.0)
    later = jnp.dot(sp.astype(BF16), suffix, preferred_element_type=F32) + c
    w = jnp.exp2(z - sp - later)
    if causal is not None:
        w = jnp.where(causal, w, 0.0)
    pv = jnp.dot(w.astype(BF16), vj, preferred_element_type=F32)
    return pv, c + jnp.sum(sp, axis=-1, keepdims=True)


def _prompt_attn_body(q_ref, k_ref, v_ref, o_ref, qst_ref, c_ref, acc_ref):
    i = pl.program_id(1)
    lane = lax.broadcasted_iota(jnp.int32, (SB_BLOCK, LANES), 1)
    first = lane < SB_HEAD_DIM
    for hp in range(HEAD_PAIRS):
        q = q_ref[:, hp * LANES:(hp + 1) * LANES]
        zero = jnp.zeros_like(q)
        qst_ref[hp, :SB_BLOCK, :] = jnp.where(first, q, zero)
        qst_ref[hp, SB_BLOCK:, :] = jnp.where(first, zero, q)
    suffix = _suffix_matrix(SB_BLOCK)
    m = 2 * SB_BLOCK

    wide = _suffix_matrix(SB_WIDE)

    def sweep(j, keys, causal, fresh):
        rows = pl.ds(pl.multiple_of(j * SB_BLOCK, SB_BLOCK), keys)
        cols = lambda hp: slice(hp * LANES, (hp + 1) * LANES)
        z = jnp.concatenate(
            [lax.dot_general(qst_ref[hp], k_ref[rows, cols(hp)], _NT, preferred_element_type=F32)
             for hp in range(HEAD_PAIRS)], axis=0)
        sp = _softplus2(z)
        if causal is not None:
            sp = jnp.where(causal, sp, 0.0)
        c0 = jnp.zeros((HEAD_PAIRS * m, 1), F32) if fresh else c_ref[...]
        later = jnp.dot(sp.astype(BF16), suffix if keys == SB_BLOCK else wide,
                        preferred_element_type=F32) + c0
        w = jnp.exp2(z - sp - later)
        if causal is not None:
            w = jnp.where(causal, w, 0.0)
        wb = w.astype(BF16)
        pv = jnp.concatenate(
            [jnp.dot(wb[hp * m:(hp + 1) * m, :], v_ref[rows, cols(hp)], preferred_element_type=F32)
             for hp in range(HEAD_PAIRS)], axis=0)
        acc_ref[...] = pv if fresh else acc_ref[...] + pv
        c_ref[...] = c0 + jnp.sum(sp, axis=-1, keepdims=True)

    t = lax.broadcasted_iota(jnp.int32, (HEAD_PAIRS * m, SB_BLOCK), 0) % SB_BLOCK
    s = lax.broadcasted_iota(jnp.int32, (HEAD_PAIRS * m, SB_BLOCK), 1)
    sweep(i, SB_BLOCK, s < t, True)

    def least_carry():
        return jnp.min(c_ref[...])

    per_wide = SB_WIDE // SB_BLOCK
    n_wide = i // per_wide

    def more(state):
        jj, carry = state
        return (jj < n_wide) & (carry < SB_UNDERFLOW_BITS)

    def body(state):
        jj, _ = state
        sweep(i - per_wide * (jj + 1), SB_WIDE, None, False)
        return jj + 1, least_carry()

    done, carry = lax.while_loop(more, body, (jnp.int32(0), least_carry()))

    @pl.when((done == n_wide) & (i % per_wide == 1) & (carry < SB_UNDERFLOW_BITS))
    def _():
        sweep(0, SB_BLOCK, None, False)

    for hp in range(HEAD_PAIRS):
        lo = hp * m
        o_ref[:, hp * LANES:(hp + 1) * LANES] = jnp.where(
            first, acc_ref[lo:lo + SB_BLOCK, :], acc_ref[lo + SB_BLOCK:lo + m, :]).astype(o_ref.dtype)


def _prompt_attn(qs, kb, vb):
    b, s, _ = qs.shape
    q_spec = pl.BlockSpec((None, SB_BLOCK, SB_WIDTH), lambda bi, i: (bi, i, 0))
    kv_spec = pl.BlockSpec((None, s, SB_WIDTH), lambda bi, i: (bi, 0, 0))
    m = 2 * SB_BLOCK
    return pl.pallas_call(
        _prompt_attn_body,
        grid=(b, s // SB_BLOCK),
        in_specs=[q_spec, kv_spec, kv_spec],
        out_specs=q_spec,
        out_shape=jax.ShapeDtypeStruct((b, s, SB_WIDTH), BF16),
        scratch_shapes=[pltpu.VMEM((HEAD_PAIRS, m, LANES), BF16),
                        pltpu.VMEM((HEAD_PAIRS * m, 1), F32),
                        pltpu.VMEM((HEAD_PAIRS * m, LANES), F32)],
        compiler_params=pltpu.CompilerParams(
            dimension_semantics=("parallel", "arbitrary"), vmem_limit_bytes=VMEM_LIMIT),
        name="prompt_attn",
    )(qs, kb, vb)


def _sample_attn_body(q_ref, kn_ref, vn_ref, ck_hbm, cv_hbm, o_ref, kbuf, vbuf, sems, c_ref, acc_ref,
                      *, layer):
    b = pl.program_id(0)
    n = q_ref.shape[0]
    m = SB_HEADS * n
    nblk = ck_hbm.shape[-1] // CACHE_BLOCK

    def copies(stream, blk, slot):
        cols = pl.ds(blk * CACHE_BLOCK, CACHE_BLOCK)
        return (pltpu.make_async_copy(ck_hbm.at[layer, stream, :, :, cols], kbuf.at[slot], sems.at[slot, 0]),
                pltpu.make_async_copy(cv_hbm.at[layer, stream, :, :, cols], vbuf.at[slot], sems.at[slot, 1]))

    def start(stream, blk, slot):
        for cp in copies(stream, blk, slot):
            cp.start()

    def wait(stream, blk, slot):
        for cp in copies(stream, blk, slot):
            cp.wait()

    def slot_of(jj):
        return jnp.where(jj == 0, b % 2, 2 + jj % 2)

    @pl.when(b == 0)
    def _():
        start(0, nblk - 1, 0)

    @pl.when(b + 1 < pl.num_programs(0))
    def _():
        start(b + 1, nblk - 1, (b + 1) % 2)

    q = [q_ref[:, h, :].astype(BF16) for h in range(SB_HEADS)]

    def block(k_of, v_of, suffix, causal, fresh, dim_major):
        qk = lambda a, k: (jnp.dot(a, k, preferred_element_type=F32) if dim_major
                           else lax.dot_general(a, k, _NT, preferred_element_type=F32))
        wv = lambda a, v: (lax.dot_general(a, v, _NT, preferred_element_type=F32) if dim_major
                           else jnp.dot(a, v, preferred_element_type=F32))
        z = jnp.concatenate([qk(q[h], k_of(h)) for h in range(SB_HEADS)], axis=0)
        sp = _softplus2(z)
        if causal is not None:
            sp = jnp.where(causal, sp, 0.0)
        c0 = jnp.zeros((m, 1), F32) if fresh else c_ref[...]
        later = jnp.dot(sp.astype(BF16), suffix, preferred_element_type=F32) + c0
        w = jnp.exp2(z - sp - later)
        if causal is not None:
            w = jnp.where(causal, w, 0.0)
        wb = w.astype(BF16)
        for h in range(SB_HEADS):
            pv = wv(wb[h * n:(h + 1) * n, :], v_of(h))
            acc_ref[h] = pv if fresh else acc_ref[h] + pv
        c_ref[...] = c0 + jnp.sum(sp, axis=-1, keepdims=True)

    pad = jnp.zeros((LANES - n, SB_HEAD_DIM), BF16)
    t = lax.broadcasted_iota(jnp.int32, (m, LANES), 0) % n
    s = lax.broadcasted_iota(jnp.int32, (m, LANES), 1)
    block(lambda h: jnp.concatenate([kn_ref[:, h, :].astype(BF16), pad], axis=0),
          lambda h: jnp.concatenate([vn_ref[:, h, :].astype(BF16), pad], axis=0),
          _suffix_matrix(LANES), s < t, True, False)

    suffix = _suffix_matrix(CACHE_BLOCK)

    def more(state):
        jj, carry = state
        return (jj < nblk) & (carry < SB_UNDERFLOW_BITS)

    def body(state):
        jj, _ = state
        blk = nblk - 1 - jj
        slot = slot_of(jj)
        wait(b, blk, slot)

        @pl.when(blk > 0)
        def _():
            start(b, blk - 1, slot_of(jj + 1))

        block(lambda h: kbuf[slot, h].astype(BF16), lambda h: vbuf[slot, h].astype(BF16),
              suffix, None, False, True)
        return jj + 1, jnp.min(c_ref[...])

    done, _ = lax.while_loop(more, body, (jnp.int32(0), jnp.min(c_ref[...])))

    @pl.when(done < nblk)
    def _():
        wait(b, nblk - 1 - done, slot_of(done))

    for h in range(SB_HEADS):
        o_ref[:, h, :] = acc_ref[h]


def _sample_attn(q4, kn4, vn4, cache_k, cache_v, layer):
    db, n = q4.shape[:2]
    m = SB_HEADS * n
    new_spec = pl.BlockSpec((None, n, SB_HEADS, SB_HEAD_DIM), lambda b: (b, 0, 0, 0))
    hbm = pl.BlockSpec(memory_space=pl.ANY)
    buf = pltpu.VMEM((4, SB_HEADS, SB_HEAD_DIM, CACHE_BLOCK), F32)
    return pl.pallas_call(
        functools.partial(_sample_attn_body, layer=layer),
        grid=(db,),
        in_specs=[new_spec, new_spec, new_spec, hbm, hbm],
        out_specs=new_spec,
        out_shape=jax.ShapeDtypeStruct(q4.shape, F32),
        scratch_shapes=[buf, buf, pltpu.SemaphoreType.DMA((4, 2)),
                        pltpu.VMEM((m, 1), F32),
                        pltpu.VMEM((SB_HEADS, n, SB_HEAD_DIM), F32)],
        compiler_params=pltpu.CompilerParams(
            dimension_semantics=("arbitrary",), vmem_limit_bytes=VMEM_LIMIT),
        name="sample_attn",
    )(q4, kn4, vn4, cache_k, cache_v)


def _route(logits):
    lane = lax.broadcasted_iota(jnp.int32, logits.shape, 1)
    neg = jnp.float32(-jnp.inf)
    big = jnp.int32(1 << 20)
    rmax = lambda a: jnp.max(a, axis=-1, keepdims=True)
    rmin = lambda a: jnp.min(a, axis=-1, keepdims=True)
    rsum = lambda a: jnp.sum(a, axis=-1, keepdims=True)

    gmask = lane < N_EXPERT_GROUPS
    gl = jnp.where(gmask, logits, neg)
    gmax = rmax(gl)
    g_sel = rmin(jnp.where(gl == gmax, lane, big))
    p_sel = 1.0 / rsum(jnp.where(gmask, jnp.exp(gl - gmax), 0.0))

    e_lo = N_EXPERT_GROUPS + g_sel * EXPERTS_PER_GROUP
    emask = (lane >= e_lo) & (lane < e_lo + EXPERTS_PER_GROUP)
    el = jnp.where(emask, logits, neg)
    emax = rmax(el)
    ex = jnp.where(emask, jnp.exp(el - emax), 0.0)
    p_e = ex / rsum(ex)
    pm = jnp.where(emask, p_e, -1.0)
    v1 = rmax(pm)
    i1 = rmin(jnp.where(pm == v1, lane, big))
    pm2 = jnp.where(lane == i1, -1.0, pm)
    v2 = rmax(pm2)
    i2 = rmin(jnp.where(pm2 == v2, lane, big))
    tot = v1 + v2
    comb = (jnp.where(lane == i1, p_sel * v1 / tot, 0.0)
            + jnp.where(lane == i2, p_sel * v2 / tot, 0.0))
    gates = jnp.zeros_like(comb)
    for e in range(EXPERTS_PER_GROUP):
        gates = jnp.where(lane == e, rsum(jnp.where(lane == e_lo + e, comb, 0.0)), gates)
    return comb, g_sel, gates


def _rmsnorm_rows(h, g):
    return h * lax.rsqrt(jnp.mean(h * h, axis=-1, keepdims=True) + EPS) * g


def _slot_matrix(slot):
    lane = lax.broadcasted_iota(jnp.int32, (slot.shape[0], MOE_SLOTS), 1)
    return jnp.where(lane == slot.astype(jnp.int32), 1.0, 0.0)


def _post_attn_body(x_ref, gm_ref, sb_ref, wo_ref, g2_ref, wr_ref, br_ref,
                    h_ref, route_ref, bucket_ref, gate_ref, cnt_ref):
    h = (x_ref[...]
         + jnp.dot(gm_ref[...], wo_ref[:GM_WIDTH, :], preferred_element_type=F32)
         + jnp.dot(sb_ref[...], wo_ref[GM_WIDTH:, :], preferred_element_type=F32))
    h_ref[...] = h
    hn = _rmsnorm_rows(h, g2_ref[...])
    hi = hn.astype(BF16)
    lo = (hn - hi.astype(F32)).astype(BF16)
    both = jnp.dot(hi, wr_ref[...], preferred_element_type=F32)
    logits = (both[:, :LANES] + jnp.dot(lo, wr_ref[:, :LANES], preferred_element_type=F32)
              + both[:, LANES:]) + br_ref[...]
    comb, g_sel, gates = _route(logits)

    lane = lax.broadcasted_iota(jnp.int32, (MOE_SUB, LANES), 1)
    r = lax.broadcasted_iota(jnp.int32, (MOE_SUB, MOE_SUB), 0)
    c = lax.broadcasted_iota(jnp.int32, (MOE_SUB, MOE_SUB), 1)
    earlier = jnp.where(r > c, 1.0, 0.0).astype(BF16)
    for st in range(h.shape[0] // MOE_SUB):
        rows = slice(st * MOE_SUB, (st + 1) * MOE_SUB)
        g = g_sel[rows]
        onehot = jnp.where(lane == g, 1.0, 0.0)
        before = jnp.dot(earlier, onehot.astype(BF16), preferred_element_type=F32)
        rank = jnp.sum(onehot * before, axis=-1, keepdims=True)
        slot = jnp.where(rank < MOE_CAP, g.astype(F32) * MOE_CAP + rank, -1.0)
        place = _slot_matrix(slot).T.astype(BF16)
        bucket_ref[st] = jnp.dot(place, hi[rows], preferred_element_type=F32).astype(BF16)
        gt = gates[rows]
        g1 = gt.astype(BF16)
        g2 = (gt - g1.astype(F32)).astype(BF16)
        g3 = (gt - g1.astype(F32) - g2.astype(F32)).astype(BF16)
        gate_ref[st] = (jnp.dot(place, g1, preferred_element_type=F32)
                        + jnp.dot(place, g2, preferred_element_type=F32)
                        + jnp.dot(place, g3, preferred_element_type=F32))
        route_ref[rows, :] = jnp.where(lane == SLOT_LANE, slot, comb[rows])
        cnt_ref[st] = jnp.broadcast_to(jnp.sum(onehot, axis=0, keepdims=True), cnt_ref.shape[1:])


def _moe_ffn_body(x_ref, gate_ref, wg_ref, wu_ref, wd_ref, y_ref):
    x = x_ref[...].reshape(-1, D_MODEL)
    gates = gate_ref[...].reshape(-1, LANES)
    parts = []
    for e in range(EXPERTS_PER_GROUP):
        hg = jnp.dot(x, wg_ref[e], preferred_element_type=F32)
        hu = jnp.dot(x, wu_ref[e], preferred_element_type=F32)
        parts.append((hg / (1.0 + jnp.exp(-hg)) * hu * gates[:, e:e + 1]).astype(BF16))
    a = jnp.concatenate(parts, axis=-1)
    y = jnp.dot(a, wd_ref[...].reshape(GROUP_FF, D_MODEL), preferred_element_type=F32)
    y_ref[...] = y.reshape(y_ref.shape)


def _moe_combine_body(h_ref, route_ref, ys_ref, o_ref):
    for st in range(h_ref.shape[0] // MOE_SUB):
        rows = slice(st * MOE_SUB, (st + 1) * MOE_SUB)
        pick = _slot_matrix(route_ref[rows, SLOT_LANE:SLOT_LANE + 1]).astype(BF16)
        ys = ys_ref[st]
        hi = ys.astype(BF16)
        lo = (ys - hi.astype(F32)).astype(BF16)
        o_ref[rows, :] = (h_ref[rows, :] + jnp.dot(pick, hi, preferred_element_type=F32)
                          + jnp.dot(pick, lo, preferred_element_type=F32))


def _moe_dense_body(h_ref, route_ref, g2_ref, wg_ref, wu_ref, wd_ref, o_ref, hn_ref):
    g = pl.program_id(1)

    @pl.when(g == 0)
    def _():
        h = h_ref[...]
        o_ref[...] = h
        hn_ref[...] = _rmsnorm_rows(h, g2_ref[...]).astype(BF16)

    hn = hn_ref[...]
    comb = route_ref[...]
    lane = lax.broadcasted_iota(jnp.int32, comb.shape, 1)
    parts = []
    for e in range(EXPERTS_PER_GROUP):
        gate = jnp.sum(jnp.where(lane == N_EXPERT_GROUPS + g * EXPERTS_PER_GROUP + e, comb, 0.0),
                       axis=-1, keepdims=True)
        hg = jnp.dot(hn, wg_ref[e], preferred_element_type=F32)
        hu = jnp.dot(hn, wu_ref[e], preferred_element_type=F32)
        parts.append((hg / (1.0 + jnp.exp(-hg)) * hu * gate).astype(BF16))
    a = jnp.concatenate(parts, axis=-1)
    o_ref[...] += jnp.dot(a, wd_ref[...].reshape(GROUP_FF, D_MODEL), preferred_element_type=F32)


def _group_spec(a, index):
    return pl.BlockSpec((EXPERTS_PER_GROUP,) + a.shape[1:], index)


def _out_moe(x2d, gm, sb, params, *, tm):
    n = x2d.shape[0]
    wo, g2, wr, br, wg, wu, wd = params
    assert n % tm == 0 and tm % MOE_SUB == 0, (n, tm)
    n_sub = n // MOE_SUB
    sub = tm // MOE_SUB
    sup = min(MOE_SUPER, n_sub)
    params_1d = pltpu.CompilerParams(dimension_semantics=("parallel",), vmem_limit_bytes=VMEM_LIMIT)
    params_2d = pltpu.CompilerParams(
        dimension_semantics=("parallel", "arbitrary"), vmem_limit_bytes=VMEM_LIMIT)

    row = lambda w: pl.BlockSpec((tm, w), lambda i: (i, 0))
    full = lambda a: pl.BlockSpec(a.shape, lambda i: (0,) * a.ndim)
    per_sub = lambda rows, w: pl.BlockSpec((sub, rows, w), lambda i: (i, 0, 0))
    h, route, buckets, gates, cnt = pl.pallas_call(
        _post_attn_body,
        grid=(n // tm,),
        in_specs=[row(D_MODEL), row(GM_WIDTH), row(SB_WIDTH),
                  full(wo), full(g2), full(wr), full(br)],
        out_specs=[row(D_MODEL), row(LANES), per_sub(MOE_SLOTS, D_MODEL),
                   per_sub(MOE_SLOTS, LANES), per_sub(8, LANES)],
        out_shape=[jax.ShapeDtypeStruct((n, D_MODEL), F32),
                   jax.ShapeDtypeStruct((n, LANES), F32),
                   jax.ShapeDtypeStruct((n_sub, MOE_SLOTS, D_MODEL), BF16),
                   jax.ShapeDtypeStruct((n_sub, MOE_SLOTS, LANES), F32),
                   jax.ShapeDtypeStruct((n_sub, 8, LANES), F32)],
        compiler_params=params_1d,
        name="post_attn",
    )(x2d, gm, sb, wo, g2, wr, br)

    def sparse():
        by_group = lambda a: a.reshape(n_sub, N_EXPERT_GROUPS, MOE_CAP, a.shape[-1])
        bucket = lambda w: pl.BlockSpec((sup, 1, MOE_CAP, w), lambda g, s: (s, g, 0, 0))
        ys = pl.pallas_call(
            _moe_ffn_body,
            grid=(N_EXPERT_GROUPS, n_sub // sup),
            in_specs=[bucket(D_MODEL), bucket(LANES)]
                     + [_group_spec(w, lambda g, s: (g, 0, 0)) for w in (wg, wu, wd)],
            out_specs=bucket(D_MODEL),
            out_shape=jax.ShapeDtypeStruct((n_sub, N_EXPERT_GROUPS, MOE_CAP, D_MODEL), F32),
            compiler_params=params_2d,
            name="moe_ffn",
        )(by_group(buckets), by_group(gates), wg, wu, wd)
        return pl.pallas_call(
            _moe_combine_body,
            grid=(n // tm,),
            in_specs=[row(D_MODEL), row(LANES), per_sub(MOE_SLOTS, D_MODEL)],
            out_specs=row(D_MODEL),
            out_shape=jax.ShapeDtypeStruct((n, D_MODEL), F32),
            compiler_params=params_1d,
            name="moe_combine",
        )(h, route, ys.reshape(n_sub, MOE_SLOTS, D_MODEL))

    def dense():
        row2 = lambda w: pl.BlockSpec((tm, w), lambda i, g: (i, 0))
        return pl.pallas_call(
            _moe_dense_body,
            grid=(n // tm, N_EXPERT_GROUPS),
            in_specs=[row2(D_MODEL), row2(LANES), pl.BlockSpec(g2.shape, lambda i, g: (0, 0))]
                     + [_group_spec(w, lambda i, g: (g, 0, 0)) for w in (wg, wu, wd)],
            out_specs=row2(D_MODEL),
            out_shape=jax.ShapeDtypeStruct((n, D_MODEL), F32),
            scratch_shapes=[pltpu.VMEM((tm, D_MODEL), BF16)],
            compiler_params=params_2d,
            name="moe_dense",
        )(h, route, g2, wg, wu, wd)

    return lax.cond(jnp.max(cnt) <= MOE_CAP, sparse, dense)


def _layer(layer, xp, xs, cache_k, cache_v, norm1_g, w_in, gm_v_norm_g, gm_w_s, gm_b_s, q_norm_g,
           k_norm_g, w_out, norm2_g, w_router_group, b_router_group, w_router_expert,
           b_router_expert, w_gate, w_up, w_down):
    b, s, d = xp.shape
    db, n, _ = xs.shape
    head_of = jnp.arange(SB_WIDTH) // SB_HEAD_DIM
    bd = jnp.where(head_of[:, None] == head_of[None, :], 1.0 / SB_HEAD_DIM, 0.0).astype(BF16)
    in_params = (norm1_g[None, :], w_in.astype(BF16), gm_v_norm_g[None, :],
                 jnp.tile(q_norm_g, SB_HEADS)[None, :], jnp.tile(k_norm_g, SB_HEADS)[None, :],
                 gm_w_s, gm_b_s.T, bd)

    wr = jnp.concatenate(
        [w_router_group, jnp.transpose(w_router_expert, (1, 0, 2)).reshape(d, N_EXPERTS)], axis=1)
    wr = jnp.pad(wr, ((0, 0), (0, LANES - wr.shape[1])))
    wr_hi = wr.astype(BF16)
    wr_lo = (wr - wr_hi.astype(F32)).astype(BF16)
    br = jnp.pad(jnp.concatenate([b_router_group, b_router_expert.reshape(-1)]),
                 (0, LANES - N_EXPERT_GROUPS - N_EXPERTS))[None, :]

    moe_params = (w_out.astype(BF16), norm2_g[None, :], jnp.concatenate([wr_hi, wr_lo], axis=1), br,
                  w_gate.astype(BF16), w_up.astype(BF16), w_down.astype(BF16))

    xp2 = xp.reshape(b * s, d)
    gm, kf, vf, qs, kb, vb = _mixer_in(xp2, in_params, tm=512, chunk=GM_CHUNK, stream=False, seq=s)
    shp = (b, s, SB_WIDTH)
    sb = _prompt_attn(qs.reshape(shp), kb.reshape(shp), vb.reshape(shp))
    yp = _out_moe(xp2, gm, sb.reshape(b * s, SB_WIDTH), moe_params, tm=1024).reshape(b, s, d)

    xs2 = xs.reshape(db * n, d)
    gm_s, kf_s, vf_s, q_s, gv_s = _mixer_in(xs2, in_params, tm=db * n, chunk=n, stream=True)
    shs = (db, n, SB_HEADS, SB_HEAD_DIM)
    frame_minor = lambda c: jnp.transpose(c, (0, 1, 3, 4, 2))
    sb_s = _sample_attn(q_s.reshape(shs), kf_s.reshape(shs), vf_s.reshape(shs),
                        frame_minor(cache_k), frame_minor(cache_v), layer)
    sb_s = sb_s.reshape(db * n, SB_WIDTH).astype(BF16)
    ys = _out_moe(xs2, gm_s, sb_s, moe_params, tm=db * n).reshape(db, n, d)

    heads = (SB_HEADS, SB_HEAD_DIM)
    rows = lambda a: jnp.transpose(a.reshape(b, *heads, s), (0, 3, 1, 2))
    return (yp, ys, rows(kf), rows(vf),
            kf_s.reshape(db, n, *heads), vf_s.reshape(db, n, *heads),
            gv_s.reshape(db, n, GM_GROUPS, GM_GROUP_DIM))


def kernel(x_prompt, x_sample, cache_sb_k, cache_sb_v, norm1_g, w_in, gm_v_norm_g, gm_w_s, gm_b_s, q_norm_g, k_norm_g, w_out, norm2_g, w_router_group, b_router_group, w_router_expert, b_router_expert, w_gate, w_up, w_down):
    depth = w_in.shape[0]
    yp, ys = x_prompt, x_sample
    outs = [[] for _ in range(5)]
    for l in range(depth):
        yp, ys, *rest = _layer(
            l, yp, ys, cache_sb_k, cache_sb_v, norm1_g[l], w_in[l], gm_v_norm_g[l], gm_w_s[l],
            gm_b_s[l], q_norm_g[l], k_norm_g[l], w_out[l], norm2_g[l], w_router_group[l],
            b_router_group[l], w_router_expert[l], b_router_expert[l], w_gate[l], w_up[l],
            w_down[l])
        for acc, r in zip(outs, rest):
            acc.append(r)
    return (yp, ys) + tuple(jnp.stack(o, axis=0) for o in outs)
```

```python
import functools

import jax
import jax.numpy as jnp
from jax import lax
from jax.experimental import pallas as pl
from jax.experimental.pallas import tpu as pltpu

D_MODEL = 1024
GM_WIDTH = 512
GM_GROUPS = 4
GM_GROUP_DIM = 128
GM_CHUNK = 128
SB_WIDTH = 512
SB_HEAD_DIM = 64
SB_HEADS = 8
IN_WIDTH = 2 * GM_WIDTH + 3 * SB_WIDTH
N_EXPERT_GROUPS = 4
EXPERTS_PER_GROUP = 4
N_EXPERTS = 16
EXPERT_FF = 256
GROUP_FF = EXPERTS_PER_GROUP * EXPERT_FF
EPS = 1e-6

LANES = 128
HEAD_PAIRS = SB_WIDTH // LANES
SB_BLOCK = 128
SB_WIDE = 2 * SB_BLOCK
CACHE_BLOCK = 256
MOE_SUB = 256
MOE_CAP = 96
MOE_SLOTS = N_EXPERT_GROUPS * MOE_CAP
MOE_SUPER = 8
SLOT_LANE = 0
VMEM_LIMIT = 56 * 1024 * 1024

LOG2_E = 1.4426950408889634
Q_SCALE = SB_HEAD_DIM ** -0.5 * LOG2_E
SB_UNDERFLOW_BITS = 160.0

F32 = jnp.float32
BF16 = jnp.bfloat16
_NT = (((1,), (1,)), ((), ()))


def _gelu_tanh(x):
    return 0.5 * x * (1.0 + jnp.tanh(0.7978845608028654 * (x + 0.044715 * (x * x * x))))


def _softplus2(z):
    return jnp.maximum(z, 0.0) + jnp.log2(1.0 + jnp.exp2(-jnp.abs(z)))


def _suffix_matrix(n):
    r = lax.broadcasted_iota(jnp.int32, (n, n), 0)
    c = lax.broadcasted_iota(jnp.int32, (n, n), 1)
    return jnp.where(r > c, 1.0, 0.0).astype(BF16)


def _mixer_in_body(x_ref, g1_ref, w_in_ref, gvg_ref, qg_ref, kg_ref, ws_ref, bst_ref, bd_ref,
                   gm_ref, kf_ref, vf_ref, *rest, chunk, stream):
    x = x_ref[...]
    tm = x.shape[0]
    xn = x * lax.rsqrt(jnp.mean(x * x, axis=-1, keepdims=True) + EPS) * g1_ref[...]
    proj = jnp.dot(xn.astype(BF16), w_in_ref[...], preferred_element_type=F32)

    gu = _gelu_tanh(proj[:, :GM_WIDTH])
    gvr = _gelu_tanh(proj[:, GM_WIDTH:2 * GM_WIDTH])
    gv = gvr * lax.rsqrt(jnp.mean(gvr * gvr, axis=-1, keepdims=True) + EPS) * gvg_ref[...]

    o = 2 * GM_WIDTH
    q = proj[:, o:o + SB_WIDTH]
    k = proj[:, o + SB_WIDTH:o + 2 * SB_WIDTH]
    v = proj[:, o + 2 * SB_WIDTH:]
    bd = bd_ref[...]
    qms = jnp.dot((q * q).astype(BF16), bd, preferred_element_type=F32)
    kms = jnp.dot((k * k).astype(BF16), bd, preferred_element_type=F32)
    qs = q * lax.rsqrt(qms + EPS) * qg_ref[...] * Q_SCALE
    kn = k * lax.rsqrt(kms + EPS) * kg_ref[...]
    if stream:
        for h in range(SB_HEADS):
            lanes = slice(h * SB_HEAD_DIM, (h + 1) * SB_HEAD_DIM)
            kf_ref[:, h, :] = kn[:, lanes]
            vf_ref[:, h, :] = v[:, lanes]
            rest[0][:, h, :] = qs[:, lanes]
        rest[1][...] = gv
    else:
        kf_ref[...] = kn.T
        vf_ref[...] = v.T
        qs_ref, kb_ref, vb_ref = rest
        qs_ref[...] = qs.astype(BF16)
        kb_ref[...] = kn.astype(BF16)
        vb_ref[...] = v.astype(BF16)

    r = lax.broadcasted_iota(jnp.int32, (chunk, chunk), 0)
    c = lax.broadcasted_iota(jnp.int32, (chunk, chunk), 1)
    gvb = gv.astype(BF16)
    for g in range(GM_GROUPS):
        w = jnp.where(r >= c, ws_ref[g, :chunk, :chunk], 0.0).astype(BF16)
        b = bst_ref[:chunk, g:g + 1]
        lo = g * GM_GROUP_DIM
        for ci in range(tm // chunk):
            rows = slice(ci * chunk, (ci + 1) * chunk)
            mixed = jnp.dot(w, gvb[rows, lo:lo + GM_GROUP_DIM], preferred_element_type=F32) + b
            gm_ref[rows, lo:lo + GM_GROUP_DIM] = (gu[rows, lo:lo + GM_GROUP_DIM] * mixed).astype(BF16)


def _mixer_in(x2d, params, *, tm, chunk, stream, seq=None):
    n = x2d.shape[0]
    full = lambda a: pl.BlockSpec(a.shape, lambda i: (0,) * a.ndim)
    spec = lambda s: pl.BlockSpec((tm,) + s.shape[1:], lambda i: (i,) + (0,) * (len(s.shape) - 1))
    gm = jax.ShapeDtypeStruct((n, GM_WIDTH), BF16)
    if stream:
        heads = jax.ShapeDtypeStruct((n, SB_HEADS, SB_HEAD_DIM), F32)
        out_shape = [gm, heads, heads,
                     heads,
                     jax.ShapeDtypeStruct((n, GM_WIDTH), F32)]
        out_specs = [spec(s) for s in out_shape]
    else:
        per_seq = seq // tm
        rows_t = jax.ShapeDtypeStruct((n // seq, SB_WIDTH, seq), F32)
        spec_t = pl.BlockSpec((None, SB_WIDTH, tm), lambda i: (i // per_seq, 0, i % per_seq))
        dense = jax.ShapeDtypeStruct((n, SB_WIDTH), BF16)
        out_shape = [gm, rows_t, rows_t, dense, dense, dense]
        out_specs = [spec(gm), spec_t, spec_t] + [spec(dense)] * 3
    return pl.pallas_call(
        functools.partial(_mixer_in_body, chunk=chunk, stream=stream),
        grid=(n // tm,),
        in_specs=[pl.BlockSpec((tm, D_MODEL), lambda i: (i, 0))] + [full(a) for a in params],
        out_specs=out_specs,
        out_shape=out_shape,
        compiler_params=pltpu.CompilerParams(
            dimension_semantics=("parallel",), vmem_limit_bytes=VMEM_LIMIT),
        name="mixer_in_stream" if stream else "mixer_in",
    )(x2d, *params)


def _sb_step(qst, kj, vj, suffix, c, causal=None):
    z = lax.dot_general(qst, kj, _NT, preferred_element_type=F32)
    sp = _softplus2(z)
    if causal is not None:
        sp = jnp.where(causal, sp, 0.0)
    later = jnp.dot(sp.astype(BF16), suffix, preferred_element_type=F32) + c
    w = jnp.exp2(z - sp - later)
    if causal is not None:
        w = jnp.where(causal, w, 0.0)
    pv = jnp.dot(w.astype(BF16), vj, preferred_element_type=F32)
    return pv, c + jnp.sum(sp, axis=-1, keepdims=True)


def _prompt_attn_body(q_ref, k_ref, v_ref, sfx_ref, sfx_wide_ref, o_ref, qst_ref, c_ref, acc_ref):
    i = pl.program_id(1)
    lane = lax.broadcasted_iota(jnp.int32, (SB_BLOCK, LANES), 1)
    first = lane < SB_HEAD_DIM
    for hp in range(HEAD_PAIRS):
        q = q_ref[:, hp * LANES:(hp + 1) * LANES]
        zero = jnp.zeros_like(q)
        qst_ref[hp, :SB_BLOCK, :] = jnp.where(first, q, zero)
        qst_ref[hp, SB_BLOCK:, :] = jnp.where(first, zero, q)
    m = 2 * SB_BLOCK
    suffix = {SB_BLOCK: sfx_ref, SB_WIDE: sfx_wide_ref}

    def sweep(j, keys, causal, fresh):
        rows = pl.ds(pl.multiple_of(j * SB_BLOCK, SB_BLOCK), keys)
        cols = lambda hp: slice(hp * LANES, (hp + 1) * LANES)
        z = jnp.concatenate(
            [lax.dot_general(qst_ref[hp], k_ref[rows, cols(hp)], _NT, preferred_element_type=F32)
             for hp in range(HEAD_PAIRS)], axis=0)
        sp = _softplus2(z)
        if causal is not None:
            sp = jnp.where(causal, sp, 0.0)
        c0 = jnp.zeros((HEAD_PAIRS * m, 1), F32) if fresh else c_ref[...]
        later = jnp.dot(sp.astype(BF16), suffix[keys][...], preferred_element_type=F32) + c0
        w = jnp.exp2(z - sp - later)
        if causal is not None:
            w = jnp.where(causal, w, 0.0)
        wb = w.astype(BF16)
        pv = jnp.concatenate(
            [jnp.dot(wb[hp * m:(hp + 1) * m, :], v_ref[rows, cols(hp)], preferred_element_type=F32)
             for hp in range(HEAD_PAIRS)], axis=0)
        acc_ref[...] = pv if fresh else acc_ref[...] + pv
        c_ref[...] = c0 + jnp.sum(sp, axis=-1, keepdims=True)

    j0 = i
    t = lax.broadcasted_iota(jnp.int32, (HEAD_PAIRS * m, SB_BLOCK), 0) % SB_BLOCK
    s = lax.broadcasted_iota(jnp.int32, (HEAD_PAIRS * m, SB_BLOCK), 1)
    sweep(i, SB_BLOCK, s < t, True)

    def least_carry():
        return jnp.min(c_ref[...])

    per_wide = SB_WIDE // SB_BLOCK
    n_wide = j0 // per_wide

    def more(state):
        jj, carry = state
        return (jj < n_wide) & (carry < SB_UNDERFLOW_BITS)

    def body(state):
        jj, _ = state
        sweep(j0 - per_wide * (jj + 1), SB_WIDE, None, False)
        return jj + 1, least_carry()

    done, carry = lax.while_loop(more, body, (jnp.int32(0), jnp.float32(0.0)))

    @pl.when((done == n_wide) & (j0 % per_wide == 1) & (carry < SB_UNDERFLOW_BITS))
    def _():
        sweep(0, SB_BLOCK, None, False)

    for hp in range(HEAD_PAIRS):
        lo = hp * m
        o_ref[:, hp * LANES:(hp + 1) * LANES] = jnp.where(
            first, acc_ref[lo:lo + SB_BLOCK, :], acc_ref[lo + SB_BLOCK:lo + m, :]).astype(o_ref.dtype)


def _prompt_attn(qs, kb, vb):
    b, s, _ = qs.shape
    assert s % SB_BLOCK == 0, s
    sfx = (_suffix_matrix(SB_BLOCK), _suffix_matrix(SB_WIDE))
    q_spec = pl.BlockSpec((None, SB_BLOCK, SB_WIDTH), lambda bi, i: (bi, i, 0))
    kv_spec = pl.BlockSpec((None, s, SB_WIDTH), lambda bi, i: (bi, 0, 0))
    m = 2 * SB_BLOCK
    return pl.pallas_call(
        _prompt_attn_body,
        grid=(b, s // SB_BLOCK),
        in_specs=[q_spec, kv_spec, kv_spec] + [pl.BlockSpec(a.shape, lambda bi, i: (0, 0)) for a in sfx],
        out_specs=q_spec,
        out_shape=jax.ShapeDtypeStruct((b, s, SB_WIDTH), BF16),
        scratch_shapes=[pltpu.VMEM((HEAD_PAIRS, m, LANES), BF16),
                        pltpu.VMEM((HEAD_PAIRS * m, 1), F32),
                        pltpu.VMEM((HEAD_PAIRS * m, LANES), F32)],
        compiler_params=pltpu.CompilerParams(
            dimension_semantics=("parallel", "arbitrary"), vmem_limit_bytes=VMEM_LIMIT),
        name="prompt_attn",
    )(qs, kb, vb, *sfx)


def _sample_attn_body(q_ref, kn_ref, vn_ref, ck_hbm, cv_hbm, o_ref, kbuf, vbuf, sems, c_ref, acc_ref,
                      *, layer):
    b = pl.program_id(0)
    n = q_ref.shape[0]
    m = SB_HEADS * n
    nblk = ck_hbm.shape[-1] // CACHE_BLOCK

    def copies(stream, blk, slot):
        cols = pl.ds(blk * CACHE_BLOCK, CACHE_BLOCK)
        return (pltpu.make_async_copy(ck_hbm.at[layer, stream, :, :, cols], kbuf.at[slot], sems.at[slot, 0]),
                pltpu.make_async_copy(cv_hbm.at[layer, stream, :, :, cols], vbuf.at[slot], sems.at[slot, 1]))

    def start(stream, blk, slot):
        for cp in copies(stream, blk, slot):
            cp.start()

    def wait(stream, blk, slot):
        for cp in copies(stream, blk, slot):
            cp.wait()

    def slot_of(jj):
        return jnp.where(jj == 0, b % 2, 2 + jj % 2)

    @pl.when(b == 0)
    def _():
        start(0, nblk - 1, 0)

    @pl.when(b + 1 < pl.num_programs(0))
    def _():
        start(b + 1, nblk - 1, (b + 1) % 2)

    q = [q_ref[:, h, :].astype(BF16) for h in range(SB_HEADS)]

    def block(k_of, v_of, suffix, causal, fresh, dim_major):
        qk = lambda a, k: (jnp.dot(a, k, preferred_element_type=F32) if dim_major
                           else lax.dot_general(a, k, _NT, preferred_element_type=F32))
        wv = lambda a, v: (lax.dot_general(a, v, _NT, preferred_element_type=F32) if dim_major
                           else jnp.dot(a, v, preferred_element_type=F32))
        z = jnp.concatenate([qk(q[h], k_of(h)) for h in range(SB_HEADS)], axis=0)
        sp = _softplus2(z)
        if causal is not None:
            sp = jnp.where(causal, sp, 0.0)
        c0 = jnp.zeros((m, 1), F32) if fresh else c_ref[...]
        later = jnp.dot(sp.astype(BF16), suffix, preferred_element_type=F32) + c0
        w = jnp.exp2(z - sp - later)
        if causal is not None:
            w = jnp.where(causal, w, 0.0)
        wb = w.astype(BF16)
        for h in range(SB_HEADS):
            pv = wv(wb[h * n:(h + 1) * n, :], v_of(h))
            acc_ref[h] = pv if fresh else acc_ref[h] + pv
        c_ref[...] = c0 + jnp.sum(sp, axis=-1, keepdims=True)

    pad = jnp.zeros((LANES - n, SB_HEAD_DIM), BF16)
    t = lax.broadcasted_iota(jnp.int32, (m, LANES), 0) % n
    s = lax.broadcasted_iota(jnp.int32, (m, LANES), 1)
    block(lambda h: jnp.concatenate([kn_ref[:, h, :].astype(BF16), pad], axis=0),
          lambda h: jnp.concatenate([vn_ref[:, h, :].astype(BF16), pad], axis=0),
          _suffix_matrix(LANES), s < t, True, False)

    suffix = _suffix_matrix(CACHE_BLOCK)

    def more(state):
        jj, carry = state
        return (jj < nblk) & (carry < SB_UNDERFLOW_BITS)

    def body(state):
        jj, _ = state
        blk = nblk - 1 - jj
        slot = slot_of(jj)
        wait(b, blk, slot)

        @pl.when(blk > 0)
        def _():
            start(b, blk - 1, slot_of(jj + 1))

        block(lambda h: kbuf[slot, h].astype(BF16), lambda h: vbuf[slot, h].astype(BF16),
              suffix, None, False, True)
        return jj + 1, jnp.min(c_ref[...])

    done, _ = lax.while_loop(more, body, (jnp.int32(0), jnp.float32(0.0)))

    @pl.when(done < nblk)
    def _():
        wait(b, nblk - 1 - done, slot_of(done))

    for h in range(SB_HEADS):
        o_ref[:, h, :] = acc_ref[h]


def _sample_attn(q4, kn4, vn4, cache_k, cache_v, layer):
    db, n = q4.shape[:2]
    m = SB_HEADS * n
    new_spec = pl.BlockSpec((None, n, SB_HEADS, SB_HEAD_DIM), lambda b: (b, 0, 0, 0))
    hbm = pl.BlockSpec(memory_space=pl.ANY)
    buf = pltpu.VMEM((4, SB_HEADS, SB_HEAD_DIM, CACHE_BLOCK), F32)
    return pl.pallas_call(
        functools.partial(_sample_attn_body, layer=layer),
        grid=(db,),
        in_specs=[new_spec, new_spec, new_spec, hbm, hbm],
        out_specs=new_spec,
        out_shape=jax.ShapeDtypeStruct(q4.shape, F32),
        scratch_shapes=[buf, buf, pltpu.SemaphoreType.DMA((4, 2)),
                        pltpu.VMEM((m, 1), F32),
                        pltpu.VMEM((SB_HEADS, n, SB_HEAD_DIM), F32)],
        compiler_params=pltpu.CompilerParams(
            dimension_semantics=("arbitrary",), vmem_limit_bytes=VMEM_LIMIT),
        name="sample_attn",
    )(q4, kn4, vn4, cache_k, cache_v)


def _route(logits):
    lane = lax.broadcasted_iota(jnp.int32, logits.shape, 1)
    neg = jnp.float32(-jnp.inf)
    big = jnp.int32(1 << 20)
    rmax = lambda a: jnp.max(a, axis=-1, keepdims=True)
    rmin = lambda a: jnp.min(a, axis=-1, keepdims=True)
    rsum = lambda a: jnp.sum(a, axis=-1, keepdims=True)

    gmask = lane < N_EXPERT_GROUPS
    gl = jnp.where(gmask, logits, neg)
    gmax = rmax(gl)
    g_sel = rmin(jnp.where(gl == gmax, lane, big))
    p_sel = 1.0 / rsum(jnp.where(gmask, jnp.exp(gl - gmax), 0.0))

    e_lo = N_EXPERT_GROUPS + g_sel * EXPERTS_PER_GROUP
    emask = (lane >= e_lo) & (lane < e_lo + EXPERTS_PER_GROUP)
    el = jnp.where(emask, logits, neg)
    emax = rmax(el)
    ex = jnp.where(emask, jnp.exp(el - emax), 0.0)
    p_e = ex / rsum(ex)
    pm = jnp.where(emask, p_e, -1.0)
    v1 = rmax(pm)
    i1 = rmin(jnp.where(pm == v1, lane, big))
    pm2 = jnp.where(lane == i1, -1.0, pm)
    v2 = rmax(pm2)
    i2 = rmin(jnp.where(pm2 == v2, lane, big))
    tot = v1 + v2
    comb = (jnp.where(lane == i1, p_sel * v1 / tot, 0.0)
            + jnp.where(lane == i2, p_sel * v2 / tot, 0.0))
    gates = jnp.zeros_like(comb)
    for e in range(EXPERTS_PER_GROUP):
        gates = jnp.where(lane == e, rsum(jnp.where(lane == e_lo + e, comb, 0.0)), gates)
    return comb, g_sel, gates


def _rmsnorm_rows(h, g):
    return h * lax.rsqrt(jnp.mean(h * h, axis=-1, keepdims=True) + EPS) * g


def _slot_matrix(slot):
    lane = lax.broadcasted_iota(jnp.int32, (slot.shape[0], MOE_SLOTS), 1)
    return jnp.where(lane == slot.astype(jnp.int32), 1.0, 0.0)


def _post_attn_body(x_ref, gm_ref, sb_ref, wo_ref, g2_ref, wr_ref, br_ref,
                    h_ref, route_ref, bucket_ref, gate_ref, cnt_ref):
    h = (x_ref[...]
         + jnp.dot(gm_ref[...], wo_ref[:GM_WIDTH, :], preferred_element_type=F32)
         + jnp.dot(sb_ref[...], wo_ref[GM_WIDTH:, :], preferred_element_type=F32))
    h_ref[...] = h
    hn = _rmsnorm_rows(h, g2_ref[...])
    hi = hn.astype(BF16)
    lo = (hn - hi.astype(F32)).astype(BF16)
    both = jnp.dot(hi, wr_ref[...], preferred_element_type=F32)
    logits = (both[:, :LANES] + jnp.dot(lo, wr_ref[:, :LANES], preferred_element_type=F32)
              + both[:, LANES:]) + br_ref[...]
    comb, g_sel, gates = _route(logits)

    lane = lax.broadcasted_iota(jnp.int32, (MOE_SUB, LANES), 1)
    r = lax.broadcasted_iota(jnp.int32, (MOE_SUB, MOE_SUB), 0)
    c = lax.broadcasted_iota(jnp.int32, (MOE_SUB, MOE_SUB), 1)
    earlier = jnp.where(r > c, 1.0, 0.0).astype(BF16)
    for st in range(h.shape[0] // MOE_SUB):
        rows = slice(st * MOE_SUB, (st + 1) * MOE_SUB)
        g = g_sel[rows]
        onehot = jnp.where(lane == g, 1.0, 0.0)
        before = jnp.dot(earlier, onehot.astype(BF16), preferred_element_type=F32)
        rank = jnp.sum(onehot * before, axis=-1, keepdims=True)
        slot = jnp.where(rank < MOE_CAP, g.astype(F32) * MOE_CAP + rank, -1.0)
        place = _slot_matrix(slot).T.astype(BF16)
        bucket_ref[st] = jnp.dot(place, hi[rows], preferred_element_type=F32).astype(BF16)
        gt = gates[rows]
        g1 = gt.astype(BF16)
        g2 = (gt - g1.astype(F32)).astype(BF16)
        g3 = (gt - g1.astype(F32) - g2.astype(F32)).astype(BF16)
        gate_ref[st] = (jnp.dot(place, g1, preferred_element_type=F32)
                        + jnp.dot(place, g2, preferred_element_type=F32)
                        + jnp.dot(place, g3, preferred_element_type=F32))
        route_ref[rows, :] = jnp.where(lane == SLOT_LANE, slot, comb[rows])
        cnt_ref[st] = jnp.broadcast_to(jnp.sum(onehot, axis=0, keepdims=True), cnt_ref.shape[1:])


def _moe_ffn_body(x_ref, gate_ref, wg_ref, wu_ref, wd_ref, y_ref, wgb_ref, wub_ref, wdb_ref):
    @pl.when(pl.program_id(1) == 0)
    def _():
        wgb_ref[...] = wg_ref[...].astype(BF16)
        wub_ref[...] = wu_ref[...].astype(BF16)
        wdb_ref[...] = wd_ref[...].astype(BF16)

    x = x_ref[...].reshape(-1, D_MODEL)
    gates = gate_ref[...].reshape(-1, LANES)
    parts = []
    for e in range(EXPERTS_PER_GROUP):
        hg = jnp.dot(x, wgb_ref[e], preferred_element_type=F32)
        hu = jnp.dot(x, wub_ref[e], preferred_element_type=F32)
        parts.append((hg / (1.0 + jnp.exp(-hg)) * hu * gates[:, e:e + 1]).astype(BF16))
    a = jnp.concatenate(parts, axis=-1)
    y = jnp.dot(a, wdb_ref[...].reshape(GROUP_FF, D_MODEL), preferred_element_type=F32)
    y_ref[...] = y.reshape(y_ref.shape)


def _moe_combine_body(h_ref, route_ref, ys_ref, o_ref):
    for st in range(h_ref.shape[0] // MOE_SUB):
        rows = slice(st * MOE_SUB, (st + 1) * MOE_SUB)
        pick = _slot_matrix(route_ref[rows, SLOT_LANE:SLOT_LANE + 1]).astype(BF16)
        ys = ys_ref[st]
        hi = ys.astype(BF16)
        lo = (ys - hi.astype(F32)).astype(BF16)
        o_ref[rows, :] = (h_ref[rows, :] + jnp.dot(pick, hi, preferred_element_type=F32)
                          + jnp.dot(pick, lo, preferred_element_type=F32))


def _moe_dense_body(h_ref, route_ref, g2_ref, wg_ref, wu_ref, wd_ref, o_ref, hn_ref):
    g = pl.program_id(1)

    @pl.when(g == 0)
    def _():
        h = h_ref[...]
        o_ref[...] = h
        hn_ref[...] = _rmsnorm_rows(h, g2_ref[...]).astype(BF16)

    hn = hn_ref[...]
    comb = route_ref[...]
    lane = lax.broadcasted_iota(jnp.int32, comb.shape, 1)
    parts = []
    for e in range(EXPERTS_PER_GROUP):
        gate = jnp.sum(jnp.where(lane == N_EXPERT_GROUPS + g * EXPERTS_PER_GROUP + e, comb, 0.0),
                       axis=-1, keepdims=True)
        hg = jnp.dot(hn, wg_ref[e], preferred_element_type=F32)
        hu = jnp.dot(hn, wu_ref[e], preferred_element_type=F32)
        parts.append((hg / (1.0 + jnp.exp(-hg)) * hu * gate).astype(BF16))
    a = jnp.concatenate(parts, axis=-1)
    o_ref[...] += jnp.dot(a, wd_ref[...].reshape(GROUP_FF, D_MODEL), preferred_element_type=F32)


def _group_spec(a, index):
    return pl.BlockSpec((EXPERTS_PER_GROUP,) + a.shape[1:], index)


def _out_moe(x2d, gm, sb, params, *, tm):
    n = x2d.shape[0]
    wo, g2, wr, br, wg, wu, wd = params
    assert n % tm == 0 and tm % MOE_SUB == 0, (n, tm)
    n_sub = n // MOE_SUB
    sub = tm // MOE_SUB
    sup = min(MOE_SUPER, n_sub)
    params_1d = pltpu.CompilerParams(dimension_semantics=("parallel",), vmem_limit_bytes=VMEM_LIMIT)
    params_2d = pltpu.CompilerParams(
        dimension_semantics=("parallel", "arbitrary"), vmem_limit_bytes=VMEM_LIMIT)

    row = lambda w: pl.BlockSpec((tm, w), lambda i: (i, 0))
    full = lambda a: pl.BlockSpec(a.shape, lambda i: (0,) * a.ndim)
    per_sub = lambda rows, w: pl.BlockSpec((sub, rows, w), lambda i: (i, 0, 0))
    h, route, buckets, gates, cnt = pl.pallas_call(
        _post_attn_body,
        grid=(n // tm,),
        in_specs=[row(D_MODEL), row(GM_WIDTH), row(SB_WIDTH),
                  full(wo), full(g2), full(wr), full(br)],
        out_specs=[row(D_MODEL), row(LANES), per_sub(MOE_SLOTS, D_MODEL),
                   per_sub(MOE_SLOTS, LANES), per_sub(8, LANES)],
        out_shape=[jax.ShapeDtypeStruct((n, D_MODEL), F32),
                   jax.ShapeDtypeStruct((n, LANES), F32),
                   jax.ShapeDtypeStruct((n_sub, MOE_SLOTS, D_MODEL), BF16),
                   jax.ShapeDtypeStruct((n_sub, MOE_SLOTS, LANES), F32),
                   jax.ShapeDtypeStruct((n_sub, 8, LANES), F32)],
        compiler_params=params_1d,
        name="post_attn",
    )(x2d, gm, sb, wo, g2, wr, br)

    def sparse():
        by_group = lambda a: a.reshape(n_sub, N_EXPERT_GROUPS, MOE_CAP, a.shape[-1])
        bucket = lambda w: pl.BlockSpec((sup, 1, MOE_CAP, w), lambda g, s: (s, g, 0, 0))
        ys = pl.pallas_call(
            _moe_ffn_body,
            grid=(N_EXPERT_GROUPS, n_sub // sup),
            in_specs=[bucket(D_MODEL), bucket(LANES)]
                     + [_group_spec(w, lambda g, s: (g, 0, 0)) for w in (wg, wu, wd)],
            out_specs=bucket(D_MODEL),
            out_shape=jax.ShapeDtypeStruct((n_sub, N_EXPERT_GROUPS, MOE_CAP, D_MODEL), F32),
            scratch_shapes=[pltpu.VMEM((EXPERTS_PER_GROUP,) + w.shape[1:], BF16) for w in (wg, wu, wd)],
            compiler_params=params_2d,
            name="moe_ffn",
        )(by_group(buckets), by_group(gates), wg, wu, wd)
        return pl.pallas_call(
            _moe_combine_body,
            grid=(n // tm,),
            in_specs=[row(D_MODEL), row(LANES), per_sub(MOE_SLOTS, D_MODEL)],
            out_specs=row(D_MODEL),
            out_shape=jax.ShapeDtypeStruct((n, D_MODEL), F32),
            compiler_params=params_1d,
            name="moe_combine",
        )(h, route, ys.reshape(n_sub, MOE_SLOTS, D_MODEL))

    def dense():
        row2 = lambda w: pl.BlockSpec((tm, w), lambda i, g: (i, 0))
        wgb, wub, wdb = (w.astype(BF16) for w in (wg, wu, wd))
        return pl.pallas_call(
            _moe_dense_body,
            grid=(n // tm, N_EXPERT_GROUPS),
            in_specs=[row2(D_MODEL), row2(LANES), pl.BlockSpec(g2.shape, lambda i, g: (0, 0))]
                     + [_group_spec(w, lambda i, g: (g, 0, 0)) for w in (wg, wu, wd)],
            out_specs=row2(D_MODEL),
            out_shape=jax.ShapeDtypeStruct((n, D_MODEL), F32),
            scratch_shapes=[pltpu.VMEM((tm, D_MODEL), BF16)],
            compiler_params=params_2d,
            name="moe_dense",
        )(h, route, g2, wgb, wub, wdb)

    return lax.cond(jnp.max(cnt) <= MOE_CAP, sparse, dense)


def _layer(layer, xp, xs, cache_k, cache_v, norm1_g, w_in, gm_v_norm_g, gm_w_s, gm_b_s, q_norm_g,
           k_norm_g, w_out, norm2_g, w_router_group, b_router_group, w_router_expert,
           b_router_expert, w_gate, w_up, w_down):
    b, s, d = xp.shape
    db, n, _ = xs.shape
    head_of = jnp.arange(SB_WIDTH) // SB_HEAD_DIM
    bd = jnp.where(head_of[:, None] == head_of[None, :], 1.0 / SB_HEAD_DIM, 0.0).astype(BF16)
    in_params = (norm1_g[None, :], w_in.astype(BF16), gm_v_norm_g[None, :],
                 jnp.tile(q_norm_g, SB_HEADS)[None, :], jnp.tile(k_norm_g, SB_HEADS)[None, :],
                 gm_w_s, gm_b_s.T, bd)

    wr = jnp.concatenate(
        [w_router_group, jnp.transpose(w_router_expert, (1, 0, 2)).reshape(d, N_EXPERTS)], axis=1)
    wr = jnp.pad(wr, ((0, 0), (0, LANES - wr.shape[1])))
    wr_hi = wr.astype(BF16)
    wr_lo = (wr - wr_hi.astype(F32)).astype(BF16)
    br = jnp.pad(jnp.concatenate([b_router_group, b_router_expert.reshape(-1)]),
                 (0, LANES - N_EXPERT_GROUPS - N_EXPERTS))[None, :]

    moe_params = (w_out.astype(BF16), norm2_g[None, :], jnp.concatenate([wr_hi, wr_lo], axis=1), br,
                  w_gate, w_up, w_down)

    xp2 = xp.reshape(b * s, d)
    gm, kf, vf, qs, kb, vb = _mixer_in(xp2, in_params, tm=512, chunk=GM_CHUNK, stream=False, seq=s)
    shp = (b, s, SB_WIDTH)
    sb = _prompt_attn(qs.reshape(shp), kb.reshape(shp), vb.reshape(shp))
    yp = _out_moe(xp2, gm, sb.reshape(b * s, SB_WIDTH), moe_params, tm=1024).reshape(b, s, d)

    xs2 = xs.reshape(db * n, d)
    gm_s, kf_s, vf_s, q_s, gv_s = _mixer_in(xs2, in_params, tm=db * n, chunk=n, stream=True)
    shs = (db, n, SB_HEADS, SB_HEAD_DIM)
    frame_minor = lambda c: jnp.transpose(c, (0, 1, 3, 4, 2))
    sb_s = _sample_attn(q_s.reshape(shs), kf_s.reshape(shs), vf_s.reshape(shs),
                        frame_minor(cache_k), frame_minor(cache_v), layer)
    sb_s = sb_s.reshape(db * n, SB_WIDTH).astype(BF16)
    ys = _out_moe(xs2, gm_s, sb_s, moe_params, tm=db * n).reshape(db, n, d)

    heads = (SB_HEADS, SB_HEAD_DIM)
    rows = lambda a: jnp.transpose(a.reshape(b, *heads, s), (0, 3, 1, 2))
    return (yp, ys, rows(kf), rows(vf),
            kf_s.reshape(db, n, *heads), vf_s.reshape(db, n, *heads),
            gv_s.reshape(db, n, GM_GROUPS, GM_GROUP_DIM))


def kernel(x_prompt, x_sample, cache_sb_k, cache_sb_v, norm1_g, w_in, gm_v_norm_g, gm_w_s, gm_b_s, q_norm_g, k_norm_g, w_out, norm2_g, w_router_group, b_router_group, w_router_expert, b_router_expert, w_gate, w_up, w_down):
    depth = w_in.shape[0]
    yp, ys = x_prompt, x_sample
    outs = [[] for _ in range(5)]
    for l in range(depth):
        yp, ys, *rest = _layer(
            l, yp, ys, cache_sb_k, cache_sb_v, norm1_g[l], w_in[l], gm_v_norm_g[l], gm_w_s[l],
            gm_b_s[l], q_norm_g[l], k_norm_g[l], w_out[l], norm2_g[l], w_router_group[l],
            b_router_group[l], w_router_expert[l], b_router_expert[l], w_gate[l], w_up[l],
            w_down[l])
        for acc, r in zip(outs, rest):
            acc.append(r)
    return (yp, ys) + tuple(jnp.stack(o, axis=0) for o in outs)
```

```python
import functools

import jax
import jax.numpy as jnp
from jax import lax
from jax.experimental import pallas as pl
from jax.experimental.pallas import tpu as pltpu

D_MODEL = 1024
GM_WIDTH = 512
GM_GROUPS = 4
GM_GROUP_DIM = 128
GM_CHUNK = 128
SB_WIDTH = 512
SB_HEAD_DIM = 64
SB_HEADS = 8
IN_WIDTH = 2 * GM_WIDTH + 3 * SB_WIDTH
N_EXPERT_GROUPS = 4
EXPERTS_PER_GROUP = 4
N_EXPERTS = 16
EXPERT_FF = 256
GROUP_FF = EXPERTS_PER_GROUP * EXPERT_FF
EPS = 1e-6

LANES = 128
HEAD_PAIRS = SB_WIDTH // LANES
SB_BLOCK = 128
SB_WIDE = 2 * SB_BLOCK
CACHE_BLOCK = 256
MOE_SUB = 256
MOE_CAP = 96
MOE_SLOTS = N_EXPERT_GROUPS * MOE_CAP
MOE_SUPER = 8
SLOT_LANE = 0
VMEM_LIMIT = 56 * 1024 * 1024

LOG2_E = 1.4426950408889634
Q_SCALE = SB_HEAD_DIM ** -0.5 * LOG2_E
SB_UNDERFLOW_BITS = 160.0

F32 = jnp.float32
BF16 = jnp.bfloat16
_NT = (((1,), (1,)), ((), ()))


def _gelu_tanh(x):
    return 0.5 * x * (1.0 + jnp.tanh(0.7978845608028654 * (x + 0.044715 * (x * x * x))))


def _softplus2(z):
    return jnp.maximum(z, 0.0) + jnp.log2(1.0 + jnp.exp2(-jnp.abs(z)))


def _suffix_matrix(n):
    r = lax.broadcasted_iota(jnp.int32, (n, n), 0)
    c = lax.broadcasted_iota(jnp.int32, (n, n), 1)
    return jnp.where(r > c, 1.0, 0.0).astype(BF16)


def _mixer_in_body(x_ref, g1_ref, w_in_ref, gvg_ref, qg_ref, kg_ref, ws_ref, bst_ref, bd_ref,
                   gm_ref, kf_ref, vf_ref, *rest, chunk, stream):
    x = x_ref[...]
    tm = x.shape[0]
    xn = x * lax.rsqrt(jnp.mean(x * x, axis=-1, keepdims=True) + EPS) * g1_ref[...]
    proj = jnp.dot(xn.astype(BF16), w_in_ref[...], preferred_element_type=F32)

    gu = _gelu_tanh(proj[:, :GM_WIDTH])
    gvr = _gelu_tanh(proj[:, GM_WIDTH:2 * GM_WIDTH])
    gv = gvr * lax.rsqrt(jnp.mean(gvr * gvr, axis=-1, keepdims=True) + EPS) * gvg_ref[...]

    o = 2 * GM_WIDTH
    q = proj[:, o:o + SB_WIDTH]
    k = proj[:, o + SB_WIDTH:o + 2 * SB_WIDTH]
    v = proj[:, o + 2 * SB_WIDTH:]
    bd = bd_ref[...]
    qms = jnp.dot((q * q).astype(BF16), bd, preferred_element_type=F32)
    kms = jnp.dot((k * k).astype(BF16), bd, preferred_element_type=F32)
    qs = q * lax.rsqrt(qms + EPS) * qg_ref[...] * Q_SCALE
    kn = k * lax.rsqrt(kms + EPS) * kg_ref[...]
    if stream:
        for h in range(SB_HEADS):
            lanes = slice(h * SB_HEAD_DIM, (h + 1) * SB_HEAD_DIM)
            kf_ref[:, h, :] = kn[:, lanes]
            vf_ref[:, h, :] = v[:, lanes]
            rest[0][:, h, :] = qs[:, lanes]
        rest[1][...] = gv
    else:
        kf_ref[...] = kn.T
        vf_ref[...] = v.T
        qs_ref, kb_ref, vb_ref = rest
        qs_ref[...] = qs.astype(BF16)
        kb_ref[...] = kn.astype(BF16)
        vb_ref[...] = v.astype(BF16)

    r = lax.broadcasted_iota(jnp.int32, (chunk, chunk), 0)
    c = lax.broadcasted_iota(jnp.int32, (chunk, chunk), 1)
    gvb = gv.astype(BF16)
    for g in range(GM_GROUPS):
        w = jnp.where(r >= c, ws_ref[g, :chunk, :chunk], 0.0).astype(BF16)
        b = bst_ref[:chunk, g:g + 1]
        lo = g * GM_GROUP_DIM
        for ci in range(tm // chunk):
            rows = slice(ci * chunk, (ci + 1) * chunk)
            mixed = jnp.dot(w, gvb[rows, lo:lo + GM_GROUP_DIM], preferred_element_type=F32) + b
            gm_ref[rows, lo:lo + GM_GROUP_DIM] = (gu[rows, lo:lo + GM_GROUP_DIM] * mixed).astype(BF16)


def _mixer_in(x2d, params, *, tm, chunk, stream, seq=None):
    n = x2d.shape[0]
    full = lambda a: pl.BlockSpec(a.shape, lambda i: (0,) * a.ndim)
    spec = lambda s: pl.BlockSpec((tm,) + s.shape[1:], lambda i: (i,) + (0,) * (len(s.shape) - 1))
    gm = jax.ShapeDtypeStruct((n, GM_WIDTH), BF16)
    if stream:
        heads = jax.ShapeDtypeStruct((n, SB_HEADS, SB_HEAD_DIM), F32)
        out_shape = [gm, heads, heads,
                     heads,
                     jax.ShapeDtypeStruct((n, GM_WIDTH), F32)]
        out_specs = [spec(s) for s in out_shape]
    else:
        per_seq = seq // tm
        rows_t = jax.ShapeDtypeStruct((n // seq, SB_WIDTH, seq), F32)
        spec_t = pl.BlockSpec((None, SB_WIDTH, tm), lambda i: (i // per_seq, 0, i % per_seq))
        dense = jax.ShapeDtypeStruct((n, SB_WIDTH), BF16)
        out_shape = [gm, rows_t, rows_t, dense, dense, dense]
        out_specs = [spec(gm), spec_t, spec_t] + [spec(dense)] * 3
    return pl.pallas_call(
        functools.partial(_mixer_in_body, chunk=chunk, stream=stream),
        grid=(n // tm,),
        in_specs=[pl.BlockSpec((tm, D_MODEL), lambda i: (i, 0))] + [full(a) for a in params],
        out_specs=out_specs,
        out_shape=out_shape,
        compiler_params=pltpu.CompilerParams(
            dimension_semantics=("parallel",), vmem_limit_bytes=VMEM_LIMIT),
        name="mixer_in_stream" if stream else "mixer_in",
    )(x2d, *params)


def _sb_step(qst, kj, vj, suffix, c, causal=None):
    z = lax.dot_general(qst, kj, _NT, preferred_element_type=F32)
    sp = _softplus2(z)
    if causal is not None:
        sp = jnp.where(causal, sp, 0.0)
    later = jnp.dot(sp.astype(BF16), suffix, preferred_element_type=F32) + c
    w = jnp.exp2(z - sp - later)
    if causal is not None:
        w = jnp.where(causal, w, 0.0)
    pv = jnp.dot(w.astype(BF16), vj, preferred_element_type=F32)
    return pv, c + jnp.sum(sp, axis=-1, keepdims=True)


def _prompt_attn_body(q_ref, k_ref, v_ref, sfx_ref, sfx_wide_ref, o_ref, qst_ref, c_ref, acc_ref):
    i = pl.program_id(1)
    lane = lax.broadcasted_iota(jnp.int32, (SB_BLOCK, LANES), 1)
    first = lane < SB_HEAD_DIM
    for hp in range(HEAD_PAIRS):
        q = q_ref[:, hp * LANES:(hp + 1) * LANES]
        zero = jnp.zeros_like(q)
        qst_ref[hp, :SB_BLOCK, :] = jnp.where(first, q, zero)
        qst_ref[hp, SB_BLOCK:, :] = jnp.where(first, zero, q)
    m = 2 * SB_BLOCK
    suffix = {SB_BLOCK: sfx_ref, SB_WIDE: sfx_wide_ref}

    def sweep(j, keys, causal, fresh):
        rows = pl.ds(pl.multiple_of(j * SB_BLOCK, SB_BLOCK), keys)
        cols = lambda hp: slice(hp * LANES, (hp + 1) * LANES)
        z = jnp.concatenate(
            [lax.dot_general(qst_ref[hp], k_ref[rows, cols(hp)], _NT, preferred_element_type=F32)
             for hp in range(HEAD_PAIRS)], axis=0)
        sp = _softplus2(z)
        if causal is not None:
            sp = jnp.where(causal, sp, 0.0)
        c0 = jnp.zeros((HEAD_PAIRS * m, 1), F32) if fresh else c_ref[...]
        later = jnp.dot(sp.astype(BF16), suffix[keys][...], preferred_element_type=F32) + c0
        w = jnp.exp2(z - sp - later)
        if causal is not None:
            w = jnp.where(causal, w, 0.0)
        wb = w.astype(BF16)
        pv = jnp.concatenate(
            [jnp.dot(wb[hp * m:(hp + 1) * m, :], v_ref[rows, cols(hp)], preferred_element_type=F32)
             for hp in range(HEAD_PAIRS)], axis=0)
        acc_ref[...] = pv if fresh else acc_ref[...] + pv
        c_ref[...] = c0 + jnp.sum(sp, axis=-1, keepdims=True)

    j0 = i
    t = lax.broadcasted_iota(jnp.int32, (HEAD_PAIRS * m, SB_BLOCK), 0) % SB_BLOCK
    s = lax.broadcasted_iota(jnp.int32, (HEAD_PAIRS * m, SB_BLOCK), 1)
    sweep(i, SB_BLOCK, s < t, True)

    def least_carry():
        return jnp.min(c_ref[...])

    per_wide = SB_WIDE // SB_BLOCK
    n_wide = j0 // per_wide

    def more(state):
        jj, carry = state
        return (jj < n_wide) & (carry < SB_UNDERFLOW_BITS)

    def body(state):
        jj, _ = state
        sweep(j0 - per_wide * (jj + 1), SB_WIDE, None, False)
        return jj + 1, least_carry()

    done, carry = lax.while_loop(more, body, (jnp.int32(0), jnp.float32(0.0)))

    @pl.when((done == n_wide) & (j0 % per_wide == 1) & (carry < SB_UNDERFLOW_BITS))
    def _():
        sweep(0, SB_BLOCK, None, False)

    for hp in range(HEAD_PAIRS):
        lo = hp * m
        o_ref[:, hp * LANES:(hp + 1) * LANES] = jnp.where(
            first, acc_ref[lo:lo + SB_BLOCK, :], acc_ref[lo + SB_BLOCK:lo + m, :]).astype(o_ref.dtype)


def _prompt_attn(qs, kb, vb):
    b, s, _ = qs.shape
    assert s % SB_BLOCK == 0, s
    sfx = (_suffix_matrix(SB_BLOCK), _suffix_matrix(SB_WIDE))
    q_spec = pl.BlockSpec((None, SB_BLOCK, SB_WIDTH), lambda bi, i: (bi, i, 0))
    kv_spec = pl.BlockSpec((None, s, SB_WIDTH), lambda bi, i: (bi, 0, 0))
    m = 2 * SB_BLOCK
    return pl.pallas_call(
        _prompt_attn_body,
        grid=(b, s // SB_BLOCK),
        in_specs=[q_spec, kv_spec, kv_spec] + [pl.BlockSpec(a.shape, lambda bi, i: (0, 0)) for a in sfx],
        out_specs=q_spec,
        out_shape=jax.ShapeDtypeStruct((b, s, SB_WIDTH), BF16),
        scratch_shapes=[pltpu.VMEM((HEAD_PAIRS, m, LANES), BF16),
                        pltpu.VMEM((HEAD_PAIRS * m, 1), F32),
                        pltpu.VMEM((HEAD_PAIRS * m, LANES), F32)],
        compiler_params=pltpu.CompilerParams(
            dimension_semantics=("parallel", "arbitrary"), vmem_limit_bytes=VMEM_LIMIT),
        name="prompt_attn",
    )(qs, kb, vb, *sfx)


def _sample_attn_body(q_ref, kn_ref, vn_ref, ck_hbm, cv_hbm, o_ref, kbuf, vbuf, sems, c_ref, acc_ref,
                      *, layer):
    b = pl.program_id(0)
    n = q_ref.shape[0]
    m = SB_HEADS * n
    nblk = ck_hbm.shape[-1] // CACHE_BLOCK

    def copies(stream, blk, slot):
        cols = pl.ds(blk * CACHE_BLOCK, CACHE_BLOCK)
        return (pltpu.make_async_copy(ck_hbm.at[layer, stream, :, :, cols], kbuf.at[slot], sems.at[slot, 0]),
                pltpu.make_async_copy(cv_hbm.at[layer, stream, :, :, cols], vbuf.at[slot], sems.at[slot, 1]))

    def start(stream, blk, slot):
        for cp in copies(stream, blk, slot):
            cp.start()

    def wait(stream, blk, slot):
        for cp in copies(stream, blk, slot):
            cp.wait()

    def slot_of(jj):
        return jnp.where(jj == 0, b % 2, 2 + jj % 2)

    @pl.when(b == 0)
    def _():
        start(0, nblk - 1, 0)

    @pl.when(b + 1 < pl.num_programs(0))
    def _():
        start(b + 1, nblk - 1, (b + 1) % 2)

    q = [q_ref[:, h, :].astype(BF16) for h in range(SB_HEADS)]

    def block(k_of, v_of, suffix, causal, fresh, dim_major):
        qk = lambda a, k: (jnp.dot(a, k, preferred_element_type=F32) if dim_major
                           else lax.dot_general(a, k, _NT, preferred_element_type=F32))
        wv = lambda a, v: (lax.dot_general(a, v, _NT, preferred_element_type=F32) if dim_major
                           else jnp.dot(a, v, preferred_element_type=F32))
        z = jnp.concatenate([qk(q[h], k_of(h)) for h in range(SB_HEADS)], axis=0)
        sp = _softplus2(z)
        if causal is not None:
            sp = jnp.where(causal, sp, 0.0)
        c0 = jnp.zeros((m, 1), F32) if fresh else c_ref[...]
        later = jnp.dot(sp.astype(BF16), suffix, preferred_element_type=F32) + c0
        w = jnp.exp2(z - sp - later)
        if causal is not None:
            w = jnp.where(causal, w, 0.0)
        wb = w.astype(BF16)
        for h in range(SB_HEADS):
            pv = wv(wb[h * n:(h + 1) * n, :], v_of(h))
            acc_ref[h] = pv if fresh else acc_ref[h] + pv
        c_ref[...] = c0 + jnp.sum(sp, axis=-1, keepdims=True)

    pad = jnp.zeros((LANES - n, SB_HEAD_DIM), BF16)
    t = lax.broadcasted_iota(jnp.int32, (m, LANES), 0) % n
    s = lax.broadcasted_iota(jnp.int32, (m, LANES), 1)
    block(lambda h: jnp.concatenate([kn_ref[:, h, :].astype(BF16), pad], axis=0),
          lambda h: jnp.concatenate([vn_ref[:, h, :].astype(BF16), pad], axis=0),
          _suffix_matrix(LANES), s < t, True, False)

    suffix = _suffix_matrix(CACHE_BLOCK)

    def more(state):
        jj, carry = state
        return (jj < nblk) & (carry < SB_UNDERFLOW_BITS)

    def body(state):
        jj, _ = state
        blk = nblk - 1 - jj
        slot = slot_of(jj)
        wait(b, blk, slot)

        @pl.when(blk > 0)
        def _():
            start(b, blk - 1, slot_of(jj + 1))

        block(lambda h: kbuf[slot, h].astype(BF16), lambda h: vbuf[slot, h].astype(BF16),
              suffix, None, False, True)
        return jj + 1, jnp.min(c_ref[...])

    done, _ = lax.while_loop(more, body, (jnp.int32(0), jnp.float32(0.0)))

    @pl.when(done < nblk)
    def _():
        wait(b, nblk - 1 - done, slot_of(done))

    for h in range(SB_HEADS):
        o_ref[:, h, :] = acc_ref[h]


def _sample_attn(q4, kn4, vn4, cache_k, cache_v, layer):
    db, n = q4.shape[:2]
    m = SB_HEADS * n
    new_spec = pl.BlockSpec((None, n, SB_HEADS, SB_HEAD_DIM), lambda b: (b, 0, 0, 0))
    hbm = pl.BlockSpec(memory_space=pl.ANY)
    buf = pltpu.VMEM((4, SB_HEADS, SB_HEAD_DIM, CACHE_BLOCK), F32)
    return pl.pallas_call(
        functools.partial(_sample_attn_body, layer=layer),
        grid=(db,),
        in_specs=[new_spec, new_spec, new_spec, hbm, hbm],
        out_specs=new_spec,
        out_shape=jax.ShapeDtypeStruct(q4.shape, F32),
        scratch_shapes=[buf, buf, pltpu.SemaphoreType.DMA((4, 2)),
                        pltpu.VMEM((m, 1), F32),
                        pltpu.VMEM((SB_HEADS, n, SB_HEAD_DIM), F32)],
        compiler_params=pltpu.CompilerParams(
            dimension_semantics=("arbitrary",), vmem_limit_bytes=VMEM_LIMIT),
        name="sample_attn",
    )(q4, kn4, vn4, cache_k, cache_v)


def _route(logits):
    lane = lax.broadcasted_iota(jnp.int32, logits.shape, 1).astype(F32)
    neg = jnp.float32(-jnp.inf)
    big = jnp.float32(1 << 20)
    rmax = lambda a: jnp.max(a, axis=-1, keepdims=True)
    rmin = lambda a: jnp.min(a, axis=-1, keepdims=True)
    rsum = lambda a: jnp.sum(a, axis=-1, keepdims=True)

    gmask = lane < N_EXPERT_GROUPS
    gl = jnp.where(gmask, logits, neg)
    gmax = rmax(gl)
    g_sel = rmin(jnp.where(gl == gmax, lane, big))
    p_sel = 1.0 / rsum(jnp.where(gmask, jnp.exp(gl - gmax), 0.0))

    e_lo = N_EXPERT_GROUPS + g_sel * EXPERTS_PER_GROUP
    emask = (lane >= e_lo) & (lane < e_lo + EXPERTS_PER_GROUP)
    el = jnp.where(emask, logits, neg)
    emax = rmax(el)
    ex = jnp.where(emask, jnp.exp(el - emax), 0.0)
    p_e = ex / rsum(ex)
    pm = jnp.where(emask, p_e, -1.0)
    v1 = rmax(pm)
    i1 = rmin(jnp.where(pm == v1, lane, big))
    pm2 = jnp.where(lane == i1, -1.0, pm)
    v2 = rmax(pm2)
    i2 = rmin(jnp.where(pm2 == v2, lane, big))
    tot = v1 + v2
    gate1 = p_sel * v1 / tot
    gate2 = p_sel * v2 / tot
    comb = jnp.where(lane == i1, gate1, 0.0) + jnp.where(lane == i2, gate2, 0.0)
    gates = jnp.where(lane == i1 - e_lo, gate1, 0.0) + jnp.where(lane == i2 - e_lo, gate2, 0.0)
    return comb, g_sel, gates


def _rmsnorm_rows(h, g):
    return h * lax.rsqrt(jnp.mean(h * h, axis=-1, keepdims=True) + EPS) * g


def _slot_matrix(slot):
    lane = lax.broadcasted_iota(jnp.int32, (slot.shape[0], MOE_SLOTS), 1)
    return jnp.where(lane == slot.astype(jnp.int32), 1.0, 0.0)


def _post_attn_body(x_ref, gm_ref, sb_ref, wo_ref, g2_ref, wr_ref, br_ref,
                    h_ref, route_ref, bucket_ref, gate_ref, cnt_ref):
    h = (x_ref[...]
         + jnp.dot(gm_ref[...], wo_ref[:GM_WIDTH, :], preferred_element_type=F32)
         + jnp.dot(sb_ref[...], wo_ref[GM_WIDTH:, :], preferred_element_type=F32))
    h_ref[...] = h
    hn = _rmsnorm_rows(h, g2_ref[...])
    hi = hn.astype(BF16)
    lo = (hn - hi.astype(F32)).astype(BF16)
    both = jnp.dot(hi, wr_ref[...], preferred_element_type=F32)
    logits = (both[:, :LANES] + jnp.dot(lo, wr_ref[:, :LANES], preferred_element_type=F32)
              + both[:, LANES:]) + br_ref[...]
    comb, g_sel, gates = _route(logits)

    subs = [slice(st * MOE_SUB, (st + 1) * MOE_SUB) for st in range(h.shape[0] // MOE_SUB)]
    lane = lax.broadcasted_iota(jnp.int32, comb.shape, 1).astype(F32)
    r = lax.broadcasted_iota(jnp.int32, (MOE_SUB, MOE_SUB), 0)
    c = lax.broadcasted_iota(jnp.int32, (MOE_SUB, MOE_SUB), 1)
    earlier = jnp.where(r > c, 1.0, 0.0).astype(BF16)
    onehot = jnp.where(lane == g_sel, 1.0, 0.0)
    onehot_b = onehot.astype(BF16)
    before = jnp.concatenate(
        [jnp.dot(earlier, onehot_b[rows], preferred_element_type=F32) for rows in subs], axis=0)
    rank = jnp.sum(onehot * before, axis=-1, keepdims=True)
    slot = jnp.where(rank < MOE_CAP, g_sel * MOE_CAP + rank, -1.0)
    route_ref[...] = jnp.where(lane == SLOT_LANE, slot, comb)
    g1 = gates.astype(BF16)
    g2 = (gates - g1.astype(F32)).astype(BF16)
    g3 = (gates - g1.astype(F32) - g2.astype(F32)).astype(BF16)
    g12 = jnp.concatenate([g1, g2], axis=1)
    for st, rows in enumerate(subs):
        place = _slot_matrix(slot[rows]).T.astype(BF16)
        bucket_ref[st] = jnp.dot(place, hi[rows], preferred_element_type=F32).astype(BF16)
        placed = jnp.dot(place, g12[rows], preferred_element_type=F32)
        gate_ref[st] = (placed[:, :LANES] + placed[:, LANES:]
                        + jnp.dot(place, g3[rows], preferred_element_type=F32))
        cnt_ref[st] = jnp.broadcast_to(jnp.sum(onehot[rows], axis=0, keepdims=True), cnt_ref.shape[1:])


def _moe_ffn_body(x_ref, gate_ref, wg_ref, wu_ref, wd_ref, y_ref, wgb_ref, wub_ref, wdb_ref):
    @pl.when(pl.program_id(1) == 0)
    def _():
        wgb_ref[...] = wg_ref[...].astype(BF16)
        wub_ref[...] = wu_ref[...].astype(BF16)
        wdb_ref[...] = wd_ref[...].astype(BF16)

    x = x_ref[...].reshape(-1, D_MODEL)
    gates = gate_ref[...].reshape(-1, LANES)
    parts = []
    for e in range(EXPERTS_PER_GROUP):
        hg = jnp.dot(x, wgb_ref[e], preferred_element_type=F32)
        hu = jnp.dot(x, wub_ref[e], preferred_element_type=F32)
        parts.append((hg / (1.0 + jnp.exp(-hg)) * hu * gates[:, e:e + 1]).astype(BF16))
    a = jnp.concatenate(parts, axis=-1)
    y = jnp.dot(a, wdb_ref[...].reshape(GROUP_FF, D_MODEL), preferred_element_type=F32)
    y_ref[...] = y.reshape(y_ref.shape)


def _moe_combine_body(h_ref, route_ref, ys_ref, o_ref):
    for st in range(h_ref.shape[0] // MOE_SUB):
        rows = slice(st * MOE_SUB, (st + 1) * MOE_SUB)
        pick = _slot_matrix(route_ref[rows, SLOT_LANE:SLOT_LANE + 1]).astype(BF16)
        ys = ys_ref[st]
        hi = ys.astype(BF16)
        lo = (ys - hi.astype(F32)).astype(BF16)
        o_ref[rows, :] = (h_ref[rows, :] + jnp.dot(pick, hi, preferred_element_type=F32)
                          + jnp.dot(pick, lo, preferred_element_type=F32))


def _moe_dense_body(h_ref, route_ref, g2_ref, wg_ref, wu_ref, wd_ref, o_ref, hn_ref):
    g = pl.program_id(1)

    @pl.when(g == 0)
    def _():
        h = h_ref[...]
        o_ref[...] = h
        hn_ref[...] = _rmsnorm_rows(h, g2_ref[...]).astype(BF16)

    hn = hn_ref[...]
    comb = route_ref[...]
    lane = lax.broadcasted_iota(jnp.int32, comb.shape, 1)
    parts = []
    for e in range(EXPERTS_PER_GROUP):
        gate = jnp.sum(jnp.where(lane == N_EXPERT_GROUPS + g * EXPERTS_PER_GROUP + e, comb, 0.0),
                       axis=-1, keepdims=True)
        hg = jnp.dot(hn, wg_ref[e], preferred_element_type=F32)
        hu = jnp.dot(hn, wu_ref[e], preferred_element_type=F32)
        parts.append((hg / (1.0 + jnp.exp(-hg)) * hu * gate).astype(BF16))
    a = jnp.concatenate(parts, axis=-1)
    o_ref[...] += jnp.dot(a, wd_ref[...].reshape(GROUP_FF, D_MODEL), preferred_element_type=F32)


def _group_spec(a, index):
    return pl.BlockSpec((EXPERTS_PER_GROUP,) + a.shape[1:], index)


def _out_moe(x2d, gm, sb, params, *, tm):
    n = x2d.shape[0]
    wo, g2, wr, br, wg, wu, wd = params
    assert n % tm == 0 and tm % MOE_SUB == 0, (n, tm)
    n_sub = n // MOE_SUB
    sub = tm // MOE_SUB
    sup = min(MOE_SUPER, n_sub)
    params_1d = pltpu.CompilerParams(dimension_semantics=("parallel",), vmem_limit_bytes=VMEM_LIMIT)
    params_2d = pltpu.CompilerParams(
        dimension_semantics=("parallel", "arbitrary"), vmem_limit_bytes=VMEM_LIMIT)

    row = lambda w: pl.BlockSpec((tm, w), lambda i: (i, 0))
    full = lambda a: pl.BlockSpec(a.shape, lambda i: (0,) * a.ndim)
    per_sub = lambda rows, w: pl.BlockSpec((sub, rows, w), lambda i: (i, 0, 0))
    h, route, buckets, gates, cnt = pl.pallas_call(
        _post_attn_body,
        grid=(n // tm,),
        in_specs=[row(D_MODEL), row(GM_WIDTH), row(SB_WIDTH),
                  full(wo), full(g2), full(wr), full(br)],
        out_specs=[row(D_MODEL), row(LANES), per_sub(MOE_SLOTS, D_MODEL),
                   per_sub(MOE_SLOTS, LANES), per_sub(8, LANES)],
        out_shape=[jax.ShapeDtypeStruct((n, D_MODEL), F32),
                   jax.ShapeDtypeStruct((n, LANES), F32),
                   jax.ShapeDtypeStruct((n_sub, MOE_SLOTS, D_MODEL), BF16),
                   jax.ShapeDtypeStruct((n_sub, MOE_SLOTS, LANES), F32),
                   jax.ShapeDtypeStruct((n_sub, 8, LANES), F32)],
        compiler_params=params_1d,
        name="post_attn",
    )(x2d, gm, sb, wo, g2, wr, br)

    def sparse():
        by_group = lambda a: a.reshape(n_sub, N_EXPERT_GROUPS, MOE_CAP, a.shape[-1])
        bucket = lambda w: pl.BlockSpec((sup, 1, MOE_CAP, w), lambda g, s: (s, g, 0, 0))
        ys = pl.pallas_call(
            _moe_ffn_body,
            grid=(N_EXPERT_GROUPS, n_sub // sup),
            in_specs=[bucket(D_MODEL), bucket(LANES)]
                     + [_group_spec(w, lambda g, s: (g, 0, 0)) for w in (wg, wu, wd)],
            out_specs=bucket(D_MODEL),
            out_shape=jax.ShapeDtypeStruct((n_sub, N_EXPERT_GROUPS, MOE_CAP, D_MODEL), F32),
            scratch_shapes=[pltpu.VMEM((EXPERTS_PER_GROUP,) + w.shape[1:], BF16) for w in (wg, wu, wd)],
            compiler_params=params_2d,
            name="moe_ffn",
        )(by_group(buckets), by_group(gates), wg, wu, wd)
        return pl.pallas_call(
            _moe_combine_body,
            grid=(n // tm,),
            in_specs=[row(D_MODEL), row(LANES), per_sub(MOE_SLOTS, D_MODEL)],
            out_specs=row(D_MODEL),
            out_shape=jax.ShapeDtypeStruct((n, D_MODEL), F32),
            compiler_params=params_1d,
            name="moe_combine",
        )(h, route, ys.reshape(n_sub, MOE_SLOTS, D_MODEL))

    def dense():
        row2 = lambda w: pl.BlockSpec((tm, w), lambda i, g: (i, 0))
        wgb, wub, wdb = (w.astype(BF16) for w in (wg, wu, wd))
        return pl.pallas_call(
            _moe_dense_body,
            grid=(n // tm, N_EXPERT_GROUPS),
            in_specs=[row2(D_MODEL), row2(LANES), pl.BlockSpec(g2.shape, lambda i, g: (0, 0))]
                     + [_group_spec(w, lambda i, g: (g, 0, 0)) for w in (wg, wu, wd)],
            out_specs=row2(D_MODEL),
            out_shape=jax.ShapeDtypeStruct((n, D_MODEL), F32),
            scratch_shapes=[pltpu.VMEM((tm, D_MODEL), BF16)],
            compiler_params=params_2d,
            name="moe_dense",
        )(h, route, g2, wgb, wub, wdb)

    return lax.cond(jnp.max(cnt) <= MOE_CAP, sparse, dense)


def _layer(layer, xp, xs, cache_k, cache_v, norm1_g, w_in, gm_v_norm_g, gm_w_s, gm_b_s, q_norm_g,
           k_norm_g, w_out, norm2_g, w_router_group, b_router_group, w_router_expert,
           b_router_expert, w_gate, w_up, w_down):
    b, s, d = xp.shape
    db, n, _ = xs.shape
    head_of = jnp.arange(SB_WIDTH) // SB_HEAD_DIM
    bd = jnp.where(head_of[:, None] == head_of[None, :], 1.0 / SB_HEAD_DIM, 0.0).astype(BF16)
    in_params = (norm1_g[None, :], w_in.astype(BF16), gm_v_norm_g[None, :],
                 jnp.tile(q_norm_g, SB_HEADS)[None, :], jnp.tile(k_norm_g, SB_HEADS)[None, :],
                 gm_w_s, gm_b_s.T, bd)

    wr = jnp.concatenate(
        [w_router_group, jnp.transpose(w_router_expert, (1, 0, 2)).reshape(d, N_EXPERTS)], axis=1)
    wr = jnp.pad(wr, ((0, 0), (0, LANES - wr.shape[1])))
    wr_hi = wr.astype(BF16)
    wr_lo = (wr - wr_hi.astype(F32)).astype(BF16)
    br = jnp.pad(jnp.concatenate([b_router_group, b_router_expert.reshape(-1)]),
                 (0, LANES - N_EXPERT_GROUPS - N_EXPERTS))[None, :]

    moe_params = (w_out.astype(BF16), norm2_g[None, :], jnp.concatenate([wr_hi, wr_lo], axis=1), br,
                  w_gate, w_up, w_down)

    xp2 = xp.reshape(b * s, d)
    gm, kf, vf, qs, kb, vb = _mixer_in(xp2, in_params, tm=512, chunk=GM_CHUNK, stream=False, seq=s)
    shp = (b, s, SB_WIDTH)
    sb = _prompt_attn(qs.reshape(shp), kb.reshape(shp), vb.reshape(shp))
    yp = _out_moe(xp2, gm, sb.reshape(b * s, SB_WIDTH), moe_params, tm=1024).reshape(b, s, d)

    xs2 = xs.reshape(db * n, d)
    gm_s, kf_s, vf_s, q_s, gv_s = _mixer_in(xs2, in_params, tm=db * n, chunk=n, stream=True)
    shs = (db, n, SB_HEADS, SB_HEAD_DIM)
    frame_minor = lambda c: jnp.transpose(c, (0, 1, 3, 4, 2))
    sb_s = _sample_attn(q_s.reshape(shs), kf_s.reshape(shs), vf_s.reshape(shs),
                        frame_minor(cache_k), frame_minor(cache_v), layer)
    sb_s = sb_s.reshape(db * n, SB_WIDTH).astype(BF16)
    ys = _out_moe(xs2, gm_s, sb_s, moe_params, tm=db * n).reshape(db, n, d)

    heads = (SB_HEADS, SB_HEAD_DIM)
    rows = lambda a: jnp.transpose(a.reshape(b, *heads, s), (0, 3, 1, 2))
    return (yp, ys, rows(kf), rows(vf),
            kf_s.reshape(db, n, *heads), vf_s.reshape(db, n, *heads),
            gv_s.reshape(db, n, GM_GROUPS, GM_GROUP_DIM))


def kernel(x_prompt, x_sample, cache_sb_k, cache_sb_v, norm1_g, w_in, gm_v_norm_g, gm_w_s, gm_b_s, q_norm_g, k_norm_g, w_out, norm2_g, w_router_group, b_router_group, w_router_expert, b_router_expert, w_gate, w_up, w_down):
    depth = w_in.shape[0]
    yp, ys = x_prompt, x_sample
    outs = [[] for _ in range(5)]
    for l in range(depth):
        yp, ys, *rest = _layer(
            l, yp, ys, cache_sb_k, cache_sb_v, norm1_g[l], w_in[l], gm_v_norm_g[l], gm_w_s[l],
            gm_b_s[l], q_norm_g[l], k_norm_g[l], w_out[l], norm2_g[l], w_router_group[l],
            b_router_group[l], w_router_expert[l], b_router_expert[l], w_gate[l], w_up[l],
            w_down[l])
        for acc, r in zip(outs, rest):
            acc.append(r)
    return (yp, ys) + tuple(jnp.stack(o, axis=0) for o in outs)
```

```python
import functools

import jax
import jax.numpy as jnp
from jax import lax
from jax.experimental import pallas as pl
from jax.experimental.pallas import tpu as pltpu

D_MODEL = 1024
GM_WIDTH = 512
GM_GROUPS = 4
GM_GROUP_DIM = 128
GM_CHUNK = 128
SB_WIDTH = 512
SB_HEAD_DIM = 64
SB_HEADS = 8
IN_WIDTH = 2 * GM_WIDTH + 3 * SB_WIDTH
N_EXPERT_GROUPS = 4
EXPERTS_PER_GROUP = 4
N_EXPERTS = 16
EXPERT_FF = 256
GROUP_FF = EXPERTS_PER_GROUP * EXPERT_FF
EPS = 1e-6

LANES = 128
HEAD_PAIRS = SB_WIDTH // LANES
SB_BLOCK = 128
SB_WIDE = 2 * SB_BLOCK
CACHE_BLOCK = 256
MOE_SUB = 256
MOE_CAP = 96
MOE_SLOTS = N_EXPERT_GROUPS * MOE_CAP
MOE_SUPER = 8
SLOT_LANE = 0
VMEM_LIMIT = 56 * 1024 * 1024

LOG2_E = 1.4426950408889634
Q_SCALE = SB_HEAD_DIM ** -0.5 * LOG2_E
SB_UNDERFLOW_BITS = 160.0

F32 = jnp.float32
BF16 = jnp.bfloat16
_NT = (((1,), (1,)), ((), ()))


def _gelu_tanh(x):
    return 0.5 * x * (1.0 + jnp.tanh(0.7978845608028654 * (x + 0.044715 * (x * x * x))))


def _softplus2(z):
    return jnp.maximum(z, 0.0) + jnp.log2(1.0 + jnp.exp2(-jnp.abs(z)))


def _suffix_matrix(n):
    r = lax.broadcasted_iota(jnp.int32, (n, n), 0)
    c = lax.broadcasted_iota(jnp.int32, (n, n), 1)
    return jnp.where(r > c, 1.0, 0.0).astype(BF16)


def _mixer_in_body(x_ref, g1_ref, w_in_ref, gvg_ref, qg_ref, kg_ref, ws_ref, bst_ref, bd_ref,
                   gm_ref, kf_ref, vf_ref, *rest, chunk, stream):
    x = x_ref[...]
    tm = x.shape[0]
    xn = x * lax.rsqrt(jnp.mean(x * x, axis=-1, keepdims=True) + EPS) * g1_ref[...]
    proj = jnp.dot(xn.astype(BF16), w_in_ref[...], preferred_element_type=F32)

    gu = _gelu_tanh(proj[:, :GM_WIDTH])
    gvr = _gelu_tanh(proj[:, GM_WIDTH:2 * GM_WIDTH])
    gv = gvr * lax.rsqrt(jnp.mean(gvr * gvr, axis=-1, keepdims=True) + EPS) * gvg_ref[...]

    o = 2 * GM_WIDTH
    q = proj[:, o:o + SB_WIDTH]
    k = proj[:, o + SB_WIDTH:o + 2 * SB_WIDTH]
    v = proj[:, o + 2 * SB_WIDTH:]
    bd = bd_ref[...]
    qms = jnp.dot((q * q).astype(BF16), bd, preferred_element_type=F32)
    kms = jnp.dot((k * k).astype(BF16), bd, preferred_element_type=F32)
    qs = q * lax.rsqrt(qms + EPS) * qg_ref[...] * Q_SCALE
    kn = k * lax.rsqrt(kms + EPS) * kg_ref[...]
    if stream:
        kf_ref[...] = kn
        vf_ref[...] = v
        rest[0][...] = qs
        rest[1][...] = gv
    else:
        kf_ref[...] = kn.T
        vf_ref[...] = v.T
        qs_ref, kb_ref, vb_ref = rest
        qs_ref[...] = qs.astype(BF16)
        kb_ref[...] = kn.astype(BF16)
        vb_ref[...] = v.astype(BF16)

    r = lax.broadcasted_iota(jnp.int32, (chunk, chunk), 0)
    c = lax.broadcasted_iota(jnp.int32, (chunk, chunk), 1)
    gvb = gv.astype(BF16)
    chunks = [slice(ci * chunk, (ci + 1) * chunk) for ci in range(tm // chunk)]
    for g in range(GM_GROUPS):
        w = jnp.where(r >= c, ws_ref[g, :chunk, :chunk], 0.0).astype(BF16)
        b = bst_ref[:chunk, g:g + 1]
        cols = slice(g * GM_GROUP_DIM, (g + 1) * GM_GROUP_DIM)
        mixed = jnp.dot(w, jnp.concatenate([gvb[rows, cols] for rows in chunks], axis=1),
                        preferred_element_type=F32)
        for ci, rows in enumerate(chunks):
            part = mixed[:, ci * GM_GROUP_DIM:(ci + 1) * GM_GROUP_DIM] + b
            gm_ref[rows, cols] = (gu[rows, cols] * part).astype(BF16)


def _mixer_in(x2d, params, *, tm, chunk, stream, seq=None):
    n = x2d.shape[0]
    full = lambda a: pl.BlockSpec(a.shape, lambda i: (0,) * a.ndim)
    spec = lambda s: pl.BlockSpec((tm,) + s.shape[1:], lambda i: (i,) + (0,) * (len(s.shape) - 1))
    gm = jax.ShapeDtypeStruct((n, GM_WIDTH), BF16)
    if stream:
        rows = jax.ShapeDtypeStruct((n, SB_WIDTH), F32)
        out_shape = [gm, rows, rows,
                     rows,
                     jax.ShapeDtypeStruct((n, GM_WIDTH), F32)]
        out_specs = [spec(s) for s in out_shape]
    else:
        per_seq = seq // tm
        rows_t = jax.ShapeDtypeStruct((n // seq, SB_WIDTH, seq), F32)
        spec_t = pl.BlockSpec((None, SB_WIDTH, tm), lambda i: (i // per_seq, 0, i % per_seq))
        dense = jax.ShapeDtypeStruct((n, SB_WIDTH), BF16)
        out_shape = [gm, rows_t, rows_t, dense, dense, dense]
        out_specs = [spec(gm), spec_t, spec_t] + [spec(dense)] * 3
    return pl.pallas_call(
        functools.partial(_mixer_in_body, chunk=chunk, stream=stream),
        grid=(n // tm,),
        in_specs=[pl.BlockSpec((tm, D_MODEL), lambda i: (i, 0))] + [full(a) for a in params],
        out_specs=out_specs,
        out_shape=out_shape,
        compiler_params=pltpu.CompilerParams(
            dimension_semantics=("parallel",), vmem_limit_bytes=VMEM_LIMIT),
        name="mixer_in_stream" if stream else "mixer_in",
    )(x2d, *params)


def _sb_step(qst, kj, vj, suffix, c, causal=None):
    z = lax.dot_general(qst, kj, _NT, preferred_element_type=F32)
    sp = _softplus2(z)
    if causal is not None:
        sp = jnp.where(causal, sp, 0.0)
    later = jnp.dot(sp.astype(BF16), suffix, preferred_element_type=F32) + c
    w = jnp.exp2(z - sp - later)
    if causal is not None:
        w = jnp.where(causal, w, 0.0)
    pv = jnp.dot(w.astype(BF16), vj, preferred_element_type=F32)
    return pv, c + jnp.sum(sp, axis=-1, keepdims=True)


def _prompt_attn_body(q_ref, k_ref, v_ref, sfx_ref, sfx_wide_ref, o_ref, qst_ref, c_ref, acc_ref):
    i = pl.program_id(1)
    lane = lax.broadcasted_iota(jnp.int32, (SB_BLOCK, LANES), 1)
    first = lane < SB_HEAD_DIM
    for hp in range(HEAD_PAIRS):
        q = q_ref[:, hp * LANES:(hp + 1) * LANES]
        zero = jnp.zeros_like(q)
        qst_ref[hp, :SB_BLOCK, :] = jnp.where(first, q, zero)
        qst_ref[hp, SB_BLOCK:, :] = jnp.where(first, zero, q)
    m = 2 * SB_BLOCK
    suffix = {SB_BLOCK: sfx_ref, SB_WIDE: sfx_wide_ref}

    def sweep(j, keys, causal, fresh):
        rows = pl.ds(pl.multiple_of(j * SB_BLOCK, SB_BLOCK), keys)
        cols = lambda hp: slice(hp * LANES, (hp + 1) * LANES)
        z = jnp.concatenate(
            [lax.dot_general(qst_ref[hp], k_ref[rows, cols(hp)], _NT, preferred_element_type=F32)
             for hp in range(HEAD_PAIRS)], axis=0)
        sp = _softplus2(z)
        if causal is not None:
            sp = jnp.where(causal, sp, 0.0)
        c0 = jnp.zeros((HEAD_PAIRS * m, 1), F32) if fresh else c_ref[...]
        later = jnp.dot(sp.astype(BF16), suffix[keys][...], preferred_element_type=F32) + c0
        w = jnp.exp2(z - sp - later)
        if causal is not None:
            w = jnp.where(causal, w, 0.0)
        wb = w.astype(BF16)
        pv = jnp.concatenate(
            [jnp.dot(wb[hp * m:(hp + 1) * m, :], v_ref[rows, cols(hp)], preferred_element_type=F32)
             for hp in range(HEAD_PAIRS)], axis=0)
        acc_ref[...] = pv if fresh else acc_ref[...] + pv
        c_ref[...] = c0 + jnp.sum(sp, axis=-1, keepdims=True)

    j0 = i
    t = lax.broadcasted_iota(jnp.int32, (HEAD_PAIRS * m, SB_BLOCK), 0) % SB_BLOCK
    s = lax.broadcasted_iota(jnp.int32, (HEAD_PAIRS * m, SB_BLOCK), 1)
    sweep(i, SB_BLOCK, s < t, True)

    def least_carry():
        return jnp.min(c_ref[...])

    per_wide = SB_WIDE // SB_BLOCK
    n_wide = j0 // per_wide

    def more(state):
        jj, carry = state
        return (jj < n_wide) & (carry < SB_UNDERFLOW_BITS)

    def body(state):
        jj, _ = state
        sweep(j0 - per_wide * (jj + 1), SB_WIDE, None, False)
        return jj + 1, least_carry()

    done, carry = lax.while_loop(more, body, (jnp.int32(0), jnp.float32(0.0)))

    @pl.when((done == n_wide) & (j0 % per_wide == 1) & (carry < SB_UNDERFLOW_BITS))
    def _():
        sweep(0, SB_BLOCK, None, False)

    for hp in range(HEAD_PAIRS):
        lo = hp * m
        o_ref[:, hp * LANES:(hp + 1) * LANES] = jnp.where(
            first, acc_ref[lo:lo + SB_BLOCK, :], acc_ref[lo + SB_BLOCK:lo + m, :]).astype(o_ref.dtype)


def _prompt_attn(qs, kb, vb):
    b, s, _ = qs.shape
    assert s % SB_BLOCK == 0, s
    sfx = (_suffix_matrix(SB_BLOCK), _suffix_matrix(SB_WIDE))
    q_spec = pl.BlockSpec((None, SB_BLOCK, SB_WIDTH), lambda bi, i: (bi, i, 0))
    kv_spec = pl.BlockSpec((None, s, SB_WIDTH), lambda bi, i: (bi, 0, 0))
    m = 2 * SB_BLOCK
    return pl.pallas_call(
        _prompt_attn_body,
        grid=(b, s // SB_BLOCK),
        in_specs=[q_spec, kv_spec, kv_spec] + [pl.BlockSpec(a.shape, lambda bi, i: (0, 0)) for a in sfx],
        out_specs=q_spec,
        out_shape=jax.ShapeDtypeStruct((b, s, SB_WIDTH), BF16),
        scratch_shapes=[pltpu.VMEM((HEAD_PAIRS, m, LANES), BF16),
                        pltpu.VMEM((HEAD_PAIRS * m, 1), F32),
                        pltpu.VMEM((HEAD_PAIRS * m, LANES), F32)],
        compiler_params=pltpu.CompilerParams(
            dimension_semantics=("parallel", "arbitrary"), vmem_limit_bytes=VMEM_LIMIT),
        name="prompt_attn",
    )(qs, kb, vb, *sfx)


def _sample_attn_body(q_ref, kn_ref, vn_ref, ck_hbm, cv_hbm, o_ref, kbuf, vbuf, sems, c_ref, acc_ref,
                      *, layer):
    b = pl.program_id(0)
    n = q_ref.shape[0]
    m = SB_HEADS * n
    nblk = ck_hbm.shape[-1] // CACHE_BLOCK

    def copies(stream, blk, slot):
        cols = pl.ds(blk * CACHE_BLOCK, CACHE_BLOCK)
        return (pltpu.make_async_copy(ck_hbm.at[layer, stream, :, :, cols], kbuf.at[slot], sems.at[slot, 0]),
                pltpu.make_async_copy(cv_hbm.at[layer, stream, :, :, cols], vbuf.at[slot], sems.at[slot, 1]))

    def start(stream, blk, slot):
        for cp in copies(stream, blk, slot):
            cp.start()

    def wait(stream, blk, slot):
        for cp in copies(stream, blk, slot):
            cp.wait()

    def slot_of(jj):
        return 2 + jj % 2

    @pl.when(b == 0)
    def _():
        start(0, nblk - 1, 0)

    @pl.when(b + 1 < pl.num_programs(0))
    def _():
        start(b + 1, nblk - 1, (b + 1) % 2)

    head = lambda a, h: a[:, h * SB_HEAD_DIM:(h + 1) * SB_HEAD_DIM]
    q_all = q_ref[...].astype(BF16)
    q = [head(q_all, h) for h in range(SB_HEADS)]

    def sweep(slot, newest, suffix, causal, fresh):
        def logits(h):
            z = jnp.dot(q[h], kbuf[slot, h].astype(BF16), preferred_element_type=F32)
            if newest is None:
                return z
            return jnp.concatenate(
                [z, lax.dot_general(q[h], newest[0][h], _NT, preferred_element_type=F32)], axis=1)

        z = jnp.concatenate([logits(h) for h in range(SB_HEADS)], axis=0)
        sp = _softplus2(z)
        if causal is not None:
            sp = jnp.where(causal, sp, 0.0)
        c0 = jnp.zeros((m, 1), F32) if fresh else c_ref[...]
        later = jnp.dot(sp.astype(BF16), suffix, preferred_element_type=F32) + c0
        w = jnp.exp2(z - sp - later)
        if causal is not None:
            w = jnp.where(causal, w, 0.0)
        wb = w.astype(BF16)
        pvs = []
        for h in range(SB_HEADS):
            wh = wb[h * n:(h + 1) * n, :]
            pv = lax.dot_general(wh[:, :CACHE_BLOCK], vbuf[slot, h].astype(BF16), _NT,
                                 preferred_element_type=F32)
            if newest is not None:
                pv = pv + jnp.dot(wh[:, CACHE_BLOCK:], newest[1][h], preferred_element_type=F32)
            pvs.append(pv)
        pv = jnp.stack(pvs)
        acc_ref[...] = pv if fresh else acc_ref[...] + pv
        c_ref[...] = c0 + jnp.sum(sp, axis=-1, keepdims=True)

    pad = jnp.zeros((LANES - n, SB_HEAD_DIM), BF16)
    kn_all, vn_all = kn_ref[...].astype(BF16), vn_ref[...].astype(BF16)
    newest = ([jnp.concatenate([head(kn_all, h), pad], axis=0) for h in range(SB_HEADS)],
              [jnp.concatenate([head(vn_all, h), pad], axis=0) for h in range(SB_HEADS)])
    t = lax.broadcasted_iota(jnp.int32, (m, CACHE_BLOCK + LANES), 0) % n
    s = lax.broadcasted_iota(jnp.int32, (m, CACHE_BLOCK + LANES), 1) - CACHE_BLOCK
    wait(b, nblk - 1, b % 2)
    if nblk > 1:
        start(b, nblk - 2, slot_of(1))
    sweep(b % 2, newest, _suffix_matrix(CACHE_BLOCK + LANES), s < t, True)

    suffix = _suffix_matrix(CACHE_BLOCK)

    def more(state):
        jj, carry = state
        return (jj < nblk) & (carry < SB_UNDERFLOW_BITS)

    def body(state):
        jj, _ = state
        blk = nblk - 1 - jj
        slot = slot_of(jj)
        wait(b, blk, slot)

        @pl.when(blk > 0)
        def _():
            start(b, blk - 1, slot_of(jj + 1))

        sweep(slot, None, suffix, None, False)
        return jj + 1, jnp.min(c_ref[...])

    done, _ = lax.while_loop(more, body, (jnp.int32(1), jnp.min(c_ref[...])))

    @pl.when(done < nblk)
    def _():
        wait(b, nblk - 1 - done, slot_of(done))

    for h in range(SB_HEADS):
        o_ref[:, h, :] = acc_ref[h]


def _sample_attn(qs, kn, vn, cache_k, cache_v, layer):
    db, n = qs.shape[:2]
    m = SB_HEADS * n
    new_spec = pl.BlockSpec((None, n, SB_WIDTH), lambda b: (b, 0, 0))
    hbm = pl.BlockSpec(memory_space=pl.ANY)
    buf = pltpu.VMEM((4, SB_HEADS, SB_HEAD_DIM, CACHE_BLOCK), F32)
    return pl.pallas_call(
        functools.partial(_sample_attn_body, layer=layer),
        grid=(db,),
        in_specs=[new_spec, new_spec, new_spec, hbm, hbm],
        out_specs=pl.BlockSpec((None, n, SB_HEADS, SB_HEAD_DIM), lambda b: (b, 0, 0, 0)),
        out_shape=jax.ShapeDtypeStruct((db, n, SB_HEADS, SB_HEAD_DIM), F32),
        scratch_shapes=[buf, buf, pltpu.SemaphoreType.DMA((4, 2)),
                        pltpu.VMEM((m, 1), F32),
                        pltpu.VMEM((SB_HEADS, n, SB_HEAD_DIM), F32)],
        compiler_params=pltpu.CompilerParams(
            dimension_semantics=("arbitrary",), vmem_limit_bytes=VMEM_LIMIT),
        name="sample_attn",
    )(qs, kn, vn, cache_k, cache_v)


def _route(logits):
    lane = lax.broadcasted_iota(jnp.int32, logits.shape, 1).astype(F32)
    neg = jnp.float32(-jnp.inf)
    big = jnp.float32(1 << 20)
    rmax = lambda a: jnp.max(a, axis=-1, keepdims=True)
    rmin = lambda a: jnp.min(a, axis=-1, keepdims=True)
    rsum = lambda a: jnp.sum(a, axis=-1, keepdims=True)

    gmask = lane < N_EXPERT_GROUPS
    gl = jnp.where(gmask, logits, neg)
    gmax = rmax(gl)
    g_sel = rmin(jnp.where(gl == gmax, lane, big))
    p_sel = 1.0 / rsum(jnp.where(gmask, jnp.exp(gl - gmax), 0.0))

    e_lo = N_EXPERT_GROUPS + g_sel * EXPERTS_PER_GROUP
    emask = (lane >= e_lo) & (lane < e_lo + EXPERTS_PER_GROUP)
    el = jnp.where(emask, logits, neg)
    emax = rmax(el)
    ex = jnp.where(emask, jnp.exp(el - emax), 0.0)
    p_e = ex / rsum(ex)
    pm = jnp.where(emask, p_e, -1.0)
    v1 = rmax(pm)
    i1 = rmin(jnp.where(pm == v1, lane, big))
    pm2 = jnp.where(lane == i1, -1.0, pm)
    v2 = rmax(pm2)
    i2 = rmin(jnp.where(pm2 == v2, lane, big))
    tot = v1 + v2
    gate1 = p_sel * v1 / tot
    gate2 = p_sel * v2 / tot
    comb = jnp.where(lane == i1, gate1, 0.0) + jnp.where(lane == i2, gate2, 0.0)
    gates = jnp.where(lane == i1 - e_lo, gate1, 0.0) + jnp.where(lane == i2 - e_lo, gate2, 0.0)
    return comb, g_sel, gates


def _rmsnorm_rows(h, g):
    return h * lax.rsqrt(jnp.mean(h * h, axis=-1, keepdims=True) + EPS) * g


def _slot_matrix(slot):
    lane = lax.broadcasted_iota(jnp.int32, (slot.shape[0], MOE_SLOTS), 1)
    return jnp.where(lane == slot.astype(jnp.int32), 1.0, 0.0)


def _post_attn_body(x_ref, gm_ref, sb_ref, wo_ref, g2_ref, wr_ref, br_ref,
                    h_ref, route_ref, bucket_ref, gate_ref, cnt_ref):
    h = (x_ref[...]
         + jnp.dot(gm_ref[...], wo_ref[:GM_WIDTH, :], preferred_element_type=F32)
         + jnp.dot(sb_ref[...], wo_ref[GM_WIDTH:, :], preferred_element_type=F32))
    h_ref[...] = h
    hn = _rmsnorm_rows(h, g2_ref[...])
    hi = hn.astype(BF16)
    lo = (hn - hi.astype(F32)).astype(BF16)
    both = jnp.dot(hi, wr_ref[...], preferred_element_type=F32)
    logits = (both[:, :LANES] + jnp.dot(lo, wr_ref[:, :LANES], preferred_element_type=F32)
              + both[:, LANES:]) + br_ref[...]
    comb, g_sel, gates = _route(logits)

    subs = [slice(st * MOE_SUB, (st + 1) * MOE_SUB) for st in range(h.shape[0] // MOE_SUB)]
    lane = lax.broadcasted_iota(jnp.int32, comb.shape, 1).astype(F32)
    r = lax.broadcasted_iota(jnp.int32, (MOE_SUB, MOE_SUB), 0)
    c = lax.broadcasted_iota(jnp.int32, (MOE_SUB, MOE_SUB), 1)
    earlier = jnp.where(r > c, 1.0, 0.0).astype(BF16)
    onehot = jnp.where(lane == g_sel, 1.0, 0.0)
    onehot_b = onehot.astype(BF16)
    before = jnp.concatenate(
        [jnp.dot(earlier, onehot_b[rows], preferred_element_type=F32) for rows in subs], axis=0)
    rank = jnp.sum(onehot * before, axis=-1, keepdims=True)
    slot = jnp.where(rank < MOE_CAP, g_sel * MOE_CAP + rank, -1.0)
    route_ref[...] = jnp.where(lane == SLOT_LANE, slot, comb)
    g1 = gates.astype(BF16)
    g2 = (gates - g1.astype(F32)).astype(BF16)
    g3 = (gates - g1.astype(F32) - g2.astype(F32)).astype(BF16)
    g12 = jnp.concatenate([g1, g2], axis=1)
    for st, rows in enumerate(subs):
        place = _slot_matrix(slot[rows]).T.astype(BF16)
        bucket_ref[st] = jnp.dot(place, hi[rows], preferred_element_type=F32).astype(BF16)
        placed = jnp.dot(place, g12[rows], preferred_element_type=F32)
        gate_ref[st] = (placed[:, :LANES] + placed[:, LANES:]
                        + jnp.dot(place, g3[rows], preferred_element_type=F32))
        cnt_ref[st] = jnp.broadcast_to(jnp.sum(onehot[rows], axis=0, keepdims=True), cnt_ref.shape[1:])


def _moe_ffn_body(x_ref, gate_ref, wg_ref, wu_ref, wd_ref, y_ref, wgb_ref, wub_ref, wdb_ref):
    @pl.when(pl.program_id(1) == 0)
    def _():
        wgb_ref[...] = wg_ref[...].astype(BF16)
        wub_ref[...] = wu_ref[...].astype(BF16)
        wdb_ref[...] = wd_ref[...].astype(BF16)

    x = x_ref[...].reshape(-1, D_MODEL)
    gates = gate_ref[...].reshape(-1, LANES)
    parts = []
    for e in range(EXPERTS_PER_GROUP):
        hg = jnp.dot(x, wgb_ref[e], preferred_element_type=F32)
        hu = jnp.dot(x, wub_ref[e], preferred_element_type=F32)
        parts.append((hg / (1.0 + jnp.exp(-hg)) * hu * gates[:, e:e + 1]).astype(BF16))
    a = jnp.concatenate(parts, axis=-1)
    y = jnp.dot(a, wdb_ref[...].reshape(GROUP_FF, D_MODEL), preferred_element_type=F32)
    y_ref[...] = y.reshape(y_ref.shape)


def _moe_combine_body(h_ref, route_ref, ys_ref, o_ref):
    for st in range(h_ref.shape[0] // MOE_SUB):
        rows = slice(st * MOE_SUB, (st + 1) * MOE_SUB)
        pick = _slot_matrix(route_ref[rows, SLOT_LANE:SLOT_LANE + 1]).astype(BF16)
        ys = ys_ref[st]
        hi = ys.astype(BF16)
        lo = (ys - hi.astype(F32)).astype(BF16)
        o_ref[rows, :] = (h_ref[rows, :] + jnp.dot(pick, hi, preferred_element_type=F32)
                          + jnp.dot(pick, lo, preferred_element_type=F32))


def _moe_dense_body(h_ref, route_ref, g2_ref, wg_ref, wu_ref, wd_ref, o_ref, hn_ref):
    g = pl.program_id(1)

    @pl.when(g == 0)
    def _():
        h = h_ref[...]
        o_ref[...] = h
        hn_ref[...] = _rmsnorm_rows(h, g2_ref[...]).astype(BF16)

    hn = hn_ref[...]
    comb = route_ref[...]
    lane = lax.broadcasted_iota(jnp.int32, comb.shape, 1)
    parts = []
    for e in range(EXPERTS_PER_GROUP):
        gate = jnp.sum(jnp.where(lane == N_EXPERT_GROUPS + g * EXPERTS_PER_GROUP + e, comb, 0.0),
                       axis=-1, keepdims=True)
        hg = jnp.dot(hn, wg_ref[e], preferred_element_type=F32)
        hu = jnp.dot(hn, wu_ref[e], preferred_element_type=F32)
        parts.append((hg / (1.0 + jnp.exp(-hg)) * hu * gate).astype(BF16))
    a = jnp.concatenate(parts, axis=-1)
    o_ref[...] += jnp.dot(a, wd_ref[...].reshape(GROUP_FF, D_MODEL), preferred_element_type=F32)


def _group_spec(a, index):
    return pl.BlockSpec((EXPERTS_PER_GROUP,) + a.shape[1:], index)


def _out_moe(x2d, gm, sb, params, *, tm):
    n = x2d.shape[0]
    wo, g2, wr, br, wg, wu, wd = params
    assert n % tm == 0 and tm % MOE_SUB == 0, (n, tm)
    n_sub = n // MOE_SUB
    sub = tm // MOE_SUB
    sup = min(MOE_SUPER, n_sub)
    params_1d = pltpu.CompilerParams(dimension_semantics=("parallel",), vmem_limit_bytes=VMEM_LIMIT)
    params_2d = pltpu.CompilerParams(
        dimension_semantics=("parallel", "arbitrary"), vmem_limit_bytes=VMEM_LIMIT)

    row = lambda w: pl.BlockSpec((tm, w), lambda i: (i, 0))
    full = lambda a: pl.BlockSpec(a.shape, lambda i: (0,) * a.ndim)
    per_sub = lambda rows, w: pl.BlockSpec((sub, rows, w), lambda i: (i, 0, 0))
    h, route, buckets, gates, cnt = pl.pallas_call(
        _post_attn_body,
        grid=(n // tm,),
        in_specs=[row(D_MODEL), row(GM_WIDTH), row(SB_WIDTH),
                  full(wo), full(g2), full(wr), full(br)],
        out_specs=[row(D_MODEL), row(LANES), per_sub(MOE_SLOTS, D_MODEL),
                   per_sub(MOE_SLOTS, LANES), per_sub(8, LANES)],
        out_shape=[jax.ShapeDtypeStruct((n, D_MODEL), F32),
                   jax.ShapeDtypeStruct((n, LANES), F32),
                   jax.ShapeDtypeStruct((n_sub, MOE_SLOTS, D_MODEL), BF16),
                   jax.ShapeDtypeStruct((n_sub, MOE_SLOTS, LANES), F32),
                   jax.ShapeDtypeStruct((n_sub, 8, LANES), F32)],
        compiler_params=params_1d,
        name="post_attn",
    )(x2d, gm, sb, wo, g2, wr, br)

    def sparse():
        by_group = lambda a: a.reshape(n_sub, N_EXPERT_GROUPS, MOE_CAP, a.shape[-1])
        bucket = lambda w: pl.BlockSpec((sup, 1, MOE_CAP, w), lambda g, s: (s, g, 0, 0))
        ys = pl.pallas_call(
            _moe_ffn_body,
            grid=(N_EXPERT_GROUPS, n_sub // sup),
            in_specs=[bucket(D_MODEL), bucket(LANES)]
                     + [_group_spec(w, lambda g, s: (g, 0, 0)) for w in (wg, wu, wd)],
            out_specs=bucket(D_MODEL),
            out_shape=jax.ShapeDtypeStruct((n_sub, N_EXPERT_GROUPS, MOE_CAP, D_MODEL), F32),
            scratch_shapes=[pltpu.VMEM((EXPERTS_PER_GROUP,) + w.shape[1:], BF16) for w in (wg, wu, wd)],
            compiler_params=params_2d,
            name="moe_ffn",
        )(by_group(buckets), by_group(gates), wg, wu, wd)
        return pl.pallas_call(
            _moe_combine_body,
            grid=(n // tm,),
            in_specs=[row(D_MODEL), row(LANES), per_sub(MOE_SLOTS, D_MODEL)],
            out_specs=row(D_MODEL),
            out_shape=jax.ShapeDtypeStruct((n, D_MODEL), F32),
            compiler_params=params_1d,
            name="moe_combine",
        )(h, route, ys.reshape(n_sub, MOE_SLOTS, D_MODEL))

    def dense():
        row2 = lambda w: pl.BlockSpec((tm, w), lambda i, g: (i, 0))
        wgb, wub, wdb = (w.astype(BF16) for w in (wg, wu, wd))
        return pl.pallas_call(
            _moe_dense_body,
            grid=(n // tm, N_EXPERT_GROUPS),
            in_specs=[row2(D_MODEL), row2(LANES), pl.BlockSpec(g2.shape, lambda i, g: (0, 0))]
                     + [_group_spec(w, lambda i, g: (g, 0, 0)) for w in (wg, wu, wd)],
            out_specs=row2(D_MODEL),
            out_shape=jax.ShapeDtypeStruct((n, D_MODEL), F32),
            scratch_shapes=[pltpu.VMEM((tm, D_MODEL), BF16)],
            compiler_params=params_2d,
            name="moe_dense",
        )(h, route, g2, wgb, wub, wdb)

    return lax.cond(jnp.max(cnt) <= MOE_CAP, sparse, dense)


def _layer(layer, xp, xs, cache_k, cache_v, norm1_g, w_in, gm_v_norm_g, gm_w_s, gm_b_s, q_norm_g,
           k_norm_g, w_out, norm2_g, w_router_group, b_router_group, w_router_expert,
           b_router_expert, w_gate, w_up, w_down):
    b, s, d = xp.shape
    db, n, _ = xs.shape
    head_of = jnp.arange(SB_WIDTH) // SB_HEAD_DIM
    bd = jnp.where(head_of[:, None] == head_of[None, :], 1.0 / SB_HEAD_DIM, 0.0).astype(BF16)
    in_params = (norm1_g[None, :], w_in.astype(BF16), gm_v_norm_g[None, :],
                 jnp.tile(q_norm_g, SB_HEADS)[None, :], jnp.tile(k_norm_g, SB_HEADS)[None, :],
                 gm_w_s, gm_b_s.T, bd)

    wr = jnp.concatenate(
        [w_router_group, jnp.transpose(w_router_expert, (1, 0, 2)).reshape(d, N_EXPERTS)], axis=1)
    wr = jnp.pad(wr, ((0, 0), (0, LANES - wr.shape[1])))
    wr_hi = wr.astype(BF16)
    wr_lo = (wr - wr_hi.astype(F32)).astype(BF16)
    br = jnp.pad(jnp.concatenate([b_router_group, b_router_expert.reshape(-1)]),
                 (0, LANES - N_EXPERT_GROUPS - N_EXPERTS))[None, :]

    moe_params = (w_out.astype(BF16), norm2_g[None, :], jnp.concatenate([wr_hi, wr_lo], axis=1), br,
                  w_gate, w_up, w_down)

    xp2 = xp.reshape(b * s, d)
    gm, kf, vf, qs, kb, vb = _mixer_in(xp2, in_params, tm=512, chunk=GM_CHUNK, stream=False, seq=s)
    shp = (b, s, SB_WIDTH)
    sb = _prompt_attn(qs.reshape(shp), kb.reshape(shp), vb.reshape(shp))
    yp = _out_moe(xp2, gm, sb.reshape(b * s, SB_WIDTH), moe_params, tm=1024).reshape(b, s, d)

    xs2 = xs.reshape(db * n, d)
    gm_s, kf_s, vf_s, q_s, gv_s = _mixer_in(xs2, in_params, tm=db * n, chunk=n, stream=True)
    shs = (db, n, SB_WIDTH)
    frame_minor = lambda c: jnp.transpose(c, (0, 1, 3, 4, 2))
    sb_s = _sample_attn(q_s.reshape(shs), kf_s.reshape(shs), vf_s.reshape(shs),
                        frame_minor(cache_k), frame_minor(cache_v), layer)
    sb_s = sb_s.reshape(db * n, SB_WIDTH).astype(BF16)
    ys = _out_moe(xs2, gm_s, sb_s, moe_params, tm=db * n).reshape(db, n, d)

    heads = (SB_HEADS, SB_HEAD_DIM)
    rows = lambda a: jnp.transpose(a.reshape(b, *heads, s), (0, 3, 1, 2))
    return (yp, ys, rows(kf), rows(vf),
            kf_s.reshape(db, n, *heads), vf_s.reshape(db, n, *heads),
            gv_s.reshape(db, n, GM_GROUPS, GM_GROUP_DIM))


def kernel(x_prompt, x_sample, cache_sb_k, cache_sb_v, norm1_g, w_in, gm_v_norm_g, gm_w_s, gm_b_s, q_norm_g, k_norm_g, w_out, norm2_g, w_router_group, b_router_group, w_router_expert, b_router_expert, w_gate, w_up, w_down):
    depth = w_in.shape[0]
    yp, ys = x_prompt, x_sample
    outs = [[] for _ in range(5)]
    for l in range(depth):
        yp, ys, *rest = _layer(
            l, yp, ys, cache_sb_k, cache_sb_v, norm1_g[l], w_in[l], gm_v_norm_g[l], gm_w_s[l],
            gm_b_s[l], q_norm_g[l], k_norm_g[l], w_out[l], norm2_g[l], w_router_group[l],
            b_router_group[l], w_router_expert[l], b_router_expert[l], w_gate[l], w_up[l],
            w_down[l])
        for acc, r in zip(outs, rest):
            acc.append(r)
    return (yp, ys) + tuple(jnp.stack(o, axis=0) for o in outs)
```

```python
import functools

import jax
import jax.numpy as jnp
from jax import lax
from jax.experimental import pallas as pl
from jax.experimental.pallas import tpu as pltpu

D_MODEL = 1024
GM_WIDTH = 512
GM_GROUPS = 4
GM_GROUP_DIM = 128
GM_CHUNK = 128
SB_WIDTH = 512
SB_HEAD_DIM = 64
SB_HEADS = 8
N_EXPERT_GROUPS = 4
EXPERTS_PER_GROUP = 4
N_EXPERTS = 16
EXPERT_FF = 256
GROUP_FF = EXPERTS_PER_GROUP * EXPERT_FF
EPS = 1e-6

LANES = 128
HEAD_PAIRS = SB_WIDTH // LANES
SB_BLOCK = 128
SB_WIDE = 2 * SB_BLOCK
CACHE_BLOCK = 256
MOE_SUB = 256
MOE_CAP = 96
MOE_SLOTS = N_EXPERT_GROUPS * MOE_CAP
MOE_SUPER = 8
SLOT_LANE = 0
VMEM_LIMIT = 56 * 1024 * 1024

LOG2_E = 1.4426950408889634
Q_SCALE = SB_HEAD_DIM ** -0.5 * LOG2_E
SB_UNDERFLOW_BITS = 160.0

F32 = jnp.float32
BF16 = jnp.bfloat16
_NT = (((1,), (1,)), ((), ()))


def _gelu_tanh(x):
    return 0.5 * x * (1.0 + jnp.tanh(0.7978845608028654 * (x + 0.044715 * (x * x * x))))


def _softplus2(z):
    return jnp.maximum(z, 0.0) + jnp.log2(1.0 + jnp.exp2(-jnp.abs(z)))


def _suffix_matrix(n):
    r = lax.broadcasted_iota(jnp.int32, (n, n), 0)
    c = lax.broadcasted_iota(jnp.int32, (n, n), 1)
    return jnp.where(r > c, 1.0, 0.0).astype(BF16)


def _mixer_in_body(x_ref, g1_ref, w_in_ref, gvg_ref, qg_ref, kg_ref, ws_ref, bst_ref, bd_ref,
                   gm_ref, kf_ref, vf_ref, *rest, chunk, stream):
    x = x_ref[...]
    tm = x.shape[0]
    xn = x * lax.rsqrt(jnp.mean(x * x, axis=-1, keepdims=True) + EPS) * g1_ref[...]
    proj = jnp.dot(xn.astype(BF16), w_in_ref[...], preferred_element_type=F32)

    gu = _gelu_tanh(proj[:, :GM_WIDTH])
    gvr = _gelu_tanh(proj[:, GM_WIDTH:2 * GM_WIDTH])
    gv = gvr * lax.rsqrt(jnp.mean(gvr * gvr, axis=-1, keepdims=True) + EPS) * gvg_ref[...]

    o = 2 * GM_WIDTH
    q = proj[:, o:o + SB_WIDTH]
    k = proj[:, o + SB_WIDTH:o + 2 * SB_WIDTH]
    v = proj[:, o + 2 * SB_WIDTH:]
    bd = bd_ref[...]
    qms = jnp.dot((q * q).astype(BF16), bd, preferred_element_type=F32)
    kms = jnp.dot((k * k).astype(BF16), bd, preferred_element_type=F32)
    qs = q * lax.rsqrt(qms + EPS) * qg_ref[...] * Q_SCALE
    kn = k * lax.rsqrt(kms + EPS) * kg_ref[...]
    if stream:
        kf_ref[...] = kn
        vf_ref[...] = v
        rest[0][...] = qs
        rest[1][...] = gv
    else:
        kf_ref[...] = kn.T
        vf_ref[...] = v.T
        qs_ref, kb_ref, vb_ref = rest
        qs_ref[...] = qs.astype(BF16)
        kb_ref[...] = kn.astype(BF16)
        vb_ref[...] = v.astype(BF16)

    r = lax.broadcasted_iota(jnp.int32, (chunk, chunk), 0)
    c = lax.broadcasted_iota(jnp.int32, (chunk, chunk), 1)
    gvb = gv.astype(BF16)
    chunks = [slice(ci * chunk, (ci + 1) * chunk) for ci in range(tm // chunk)]
    for g in range(GM_GROUPS):
        w = jnp.where(r >= c, ws_ref[g, :chunk, :chunk], 0.0).astype(BF16)
        b = bst_ref[:chunk, g:g + 1]
        cols = slice(g * GM_GROUP_DIM, (g + 1) * GM_GROUP_DIM)
        mixed = jnp.dot(w, jnp.concatenate([gvb[rows, cols] for rows in chunks], axis=1),
                        preferred_element_type=F32)
        for ci, rows in enumerate(chunks):
            part = mixed[:, ci * GM_GROUP_DIM:(ci + 1) * GM_GROUP_DIM] + b
            gm_ref[rows, cols] = (gu[rows, cols] * part).astype(BF16)


def _mixer_in(x2d, params, *, tm, chunk, stream, seq=None):
    n = x2d.shape[0]
    assert n % tm == 0 and tm % chunk == 0 and chunk <= GM_CHUNK, (n, tm, chunk)
    full = lambda a: pl.BlockSpec(a.shape, lambda i: (0,) * a.ndim)
    spec = lambda s: pl.BlockSpec((tm,) + s.shape[1:], lambda i: (i,) + (0,) * (len(s.shape) - 1))
    gm = jax.ShapeDtypeStruct((n, GM_WIDTH), BF16)
    if stream:
        rows = jax.ShapeDtypeStruct((n, SB_WIDTH), F32)
        out_shape = [gm, rows, rows,
                     rows,
                     jax.ShapeDtypeStruct((n, GM_WIDTH), F32)]
        out_specs = [spec(s) for s in out_shape]
    else:
        per_seq = seq // tm
        rows_t = jax.ShapeDtypeStruct((n // seq, SB_WIDTH, seq), F32)
        spec_t = pl.BlockSpec((None, SB_WIDTH, tm), lambda i: (i // per_seq, 0, i % per_seq))
        dense = jax.ShapeDtypeStruct((n, SB_WIDTH), BF16)
        out_shape = [gm, rows_t, rows_t, dense, dense, dense]
        out_specs = [spec(gm), spec_t, spec_t] + [spec(dense)] * 3
    return pl.pallas_call(
        functools.partial(_mixer_in_body, chunk=chunk, stream=stream),
        grid=(n // tm,),
        in_specs=[pl.BlockSpec((tm, D_MODEL), lambda i: (i, 0))] + [full(a) for a in params],
        out_specs=out_specs,
        out_shape=out_shape,
        compiler_params=pltpu.CompilerParams(
            dimension_semantics=("parallel",), vmem_limit_bytes=VMEM_LIMIT),
        name="mixer_in_stream" if stream else "mixer_in",
    )(x2d, *params)


def _prompt_attn_body(q_ref, k_ref, v_ref, sfx_ref, sfx_wide_ref, o_ref, qst_ref, c_ref, acc_ref):
    i = pl.program_id(1)
    lane = lax.broadcasted_iota(jnp.int32, (SB_BLOCK, LANES), 1)
    first = lane < SB_HEAD_DIM
    for hp in range(HEAD_PAIRS):
        q = q_ref[:, hp * LANES:(hp + 1) * LANES]
        zero = jnp.zeros_like(q)
        qst_ref[hp, :SB_BLOCK, :] = jnp.where(first, q, zero)
        qst_ref[hp, SB_BLOCK:, :] = jnp.where(first, zero, q)
    m = 2 * SB_BLOCK
    suffix = {SB_BLOCK: sfx_ref, SB_WIDE: sfx_wide_ref}

    def sweep(j, keys, causal, fresh):
        rows = pl.ds(pl.multiple_of(j * SB_BLOCK, SB_BLOCK), keys)
        cols = lambda hp: slice(hp * LANES, (hp + 1) * LANES)
        z = jnp.concatenate(
            [lax.dot_general(qst_ref[hp], k_ref[rows, cols(hp)], _NT, preferred_element_type=F32)
             for hp in range(HEAD_PAIRS)], axis=0)
        sp = _softplus2(z)
        if causal is not None:
            sp = jnp.where(causal, sp, 0.0)
        c0 = jnp.zeros((HEAD_PAIRS * m, 1), F32) if fresh else c_ref[...]
        later = jnp.dot(sp.astype(BF16), suffix[keys][...], preferred_element_type=F32) + c0
        w = jnp.exp2(z - sp - later)
        if causal is not None:
            w = jnp.where(causal, w, 0.0)
        wb = w.astype(BF16)
        pv = jnp.concatenate(
            [jnp.dot(wb[hp * m:(hp + 1) * m, :], v_ref[rows, cols(hp)], preferred_element_type=F32)
             for hp in range(HEAD_PAIRS)], axis=0)
        acc_ref[...] = pv if fresh else acc_ref[...] + pv
        c_ref[...] = c0 + jnp.sum(sp, axis=-1, keepdims=True)

    j0 = i
    t = lax.broadcasted_iota(jnp.int32, (HEAD_PAIRS * m, SB_BLOCK), 0) % SB_BLOCK
    s = lax.broadcasted_iota(jnp.int32, (HEAD_PAIRS * m, SB_BLOCK), 1)
    sweep(i, SB_BLOCK, s < t, True)

    def least_carry():
        return jnp.min(c_ref[...])

    per_wide = SB_WIDE // SB_BLOCK
    n_wide = j0 // per_wide

    def more(state):
        jj, carry = state
        return (jj < n_wide) & (carry < SB_UNDERFLOW_BITS)

    def body(state):
        jj, _ = state
        sweep(j0 - per_wide * (jj + 1), SB_WIDE, None, False)
        return jj + 1, least_carry()

    done, carry = lax.while_loop(more, body, (jnp.int32(0), jnp.float32(0.0)))

    @pl.when((done == n_wide) & (j0 % per_wide == 1) & (carry < SB_UNDERFLOW_BITS))
    def _():
        sweep(0, SB_BLOCK, None, False)

    for hp in range(HEAD_PAIRS):
        lo = hp * m
        o_ref[:, hp * LANES:(hp + 1) * LANES] = jnp.where(
            first, acc_ref[lo:lo + SB_BLOCK, :], acc_ref[lo + SB_BLOCK:lo + m, :]).astype(o_ref.dtype)


def _prompt_attn(qs, kb, vb):
    b, s, _ = qs.shape
    assert s % SB_BLOCK == 0, s
    sfx = (_suffix_matrix(SB_BLOCK), _suffix_matrix(SB_WIDE))
    q_spec = pl.BlockSpec((None, SB_BLOCK, SB_WIDTH), lambda bi, i: (bi, i, 0))
    kv_spec = pl.BlockSpec((None, s, SB_WIDTH), lambda bi, i: (bi, 0, 0))
    m = 2 * SB_BLOCK
    return pl.pallas_call(
        _prompt_attn_body,
        grid=(b, s // SB_BLOCK),
        in_specs=[q_spec, kv_spec, kv_spec] + [pl.BlockSpec(a.shape, lambda bi, i: (0, 0)) for a in sfx],
        out_specs=q_spec,
        out_shape=jax.ShapeDtypeStruct((b, s, SB_WIDTH), BF16),
        scratch_shapes=[pltpu.VMEM((HEAD_PAIRS, m, LANES), BF16),
                        pltpu.VMEM((HEAD_PAIRS * m, 1), F32),
                        pltpu.VMEM((HEAD_PAIRS * m, LANES), F32)],
        compiler_params=pltpu.CompilerParams(
            dimension_semantics=("parallel", "arbitrary"), vmem_limit_bytes=VMEM_LIMIT),
        name="prompt_attn",
    )(qs, kb, vb, *sfx)


def _sample_attn_body(q_ref, kn_ref, vn_ref, ck_hbm, cv_hbm, o_ref, kbuf, vbuf, sems, c_ref, acc_ref,
                      *, layer):
    b = pl.program_id(0)
    n = q_ref.shape[0]
    m = SB_HEADS * n
    nblk = ck_hbm.shape[-1] // CACHE_BLOCK

    def copies(stream, blk, slot):
        cols = pl.ds(blk * CACHE_BLOCK, CACHE_BLOCK)
        return (pltpu.make_async_copy(ck_hbm.at[layer, stream, :, :, cols], kbuf.at[slot], sems.at[slot, 0]),
                pltpu.make_async_copy(cv_hbm.at[layer, stream, :, :, cols], vbuf.at[slot], sems.at[slot, 1]))

    def start(stream, blk, slot):
        for cp in copies(stream, blk, slot):
            cp.start()

    def wait(stream, blk, slot):
        for cp in copies(stream, blk, slot):
            cp.wait()

    def slot_of(jj):
        return 2 + jj % 2

    @pl.when(b == 0)
    def _():
        start(0, nblk - 1, 0)

    @pl.when(b + 1 < pl.num_programs(0))
    def _():
        start(b + 1, nblk - 1, (b + 1) % 2)

    head = lambda a, h: a[:, h * SB_HEAD_DIM:(h + 1) * SB_HEAD_DIM]
    q_all = q_ref[...].astype(BF16)
    q = [head(q_all, h) for h in range(SB_HEADS)]

    def sweep(slot, newest, suffix, causal, fresh):
        def logits(h):
            z = jnp.dot(q[h], kbuf[slot, h].astype(BF16), preferred_element_type=F32)
            if newest is None:
                return z
            return jnp.concatenate(
                [z, lax.dot_general(q[h], newest[0][h], _NT, preferred_element_type=F32)], axis=1)

        z = jnp.concatenate([logits(h) for h in range(SB_HEADS)], axis=0)
        sp = _softplus2(z)
        if causal is not None:
            sp = jnp.where(causal, sp, 0.0)
        c0 = jnp.zeros((m, 1), F32) if fresh else c_ref[...]
        later = jnp.dot(sp.astype(BF16), suffix, preferred_element_type=F32) + c0
        w = jnp.exp2(z - sp - later)
        if causal is not None:
            w = jnp.where(causal, w, 0.0)
        wb = w.astype(BF16)
        pvs = []
        for h in range(SB_HEADS):
            wh = wb[h * n:(h + 1) * n, :]
            pv = lax.dot_general(wh[:, :CACHE_BLOCK], vbuf[slot, h].astype(BF16), _NT,
                                 preferred_element_type=F32)
            if newest is not None:
                pv = pv + jnp.dot(wh[:, CACHE_BLOCK:], newest[1][h], preferred_element_type=F32)
            pvs.append(pv)
        pv = jnp.stack(pvs)
        acc_ref[...] = pv if fresh else acc_ref[...] + pv
        c_ref[...] = c0 + jnp.sum(sp, axis=-1, keepdims=True)

    pad = jnp.zeros((LANES - n, SB_HEAD_DIM), BF16)
    kn_all, vn_all = kn_ref[...].astype(BF16), vn_ref[...].astype(BF16)
    newest = ([jnp.concatenate([head(kn_all, h), pad], axis=0) for h in range(SB_HEADS)],
              [jnp.concatenate([head(vn_all, h), pad], axis=0) for h in range(SB_HEADS)])
    t = lax.broadcasted_iota(jnp.int32, (m, CACHE_BLOCK + LANES), 0) % n
    s = lax.broadcasted_iota(jnp.int32, (m, CACHE_BLOCK + LANES), 1) - CACHE_BLOCK
    wait(b, nblk - 1, b % 2)
    sweep(b % 2, newest, _suffix_matrix(CACHE_BLOCK + LANES), s < t, True)

    suffix = _suffix_matrix(CACHE_BLOCK)

    def more(state):
        jj, carry = state
        return (jj < nblk) & (carry < SB_UNDERFLOW_BITS)

    def body(state):
        jj, _ = state
        blk = nblk - 1 - jj
        slot = slot_of(jj)
        wait(b, blk, slot)

        @pl.when(blk > 0)
        def _():
            start(b, blk - 1, slot_of(jj + 1))

        sweep(slot, None, suffix, None, False)
        return jj + 1, jnp.min(c_ref[...])

    state = (jnp.int32(1), jnp.min(c_ref[...]))

    if nblk > 1:
        @pl.when(more(state))
        def _():
            start(b, nblk - 2, slot_of(1))

    done, _ = lax.while_loop(more, body, state)

    @pl.when((done > 1) & (done < nblk))
    def _():
        wait(b, nblk - 1 - done, slot_of(done))

    for h in range(SB_HEADS):
        o_ref[:, h, :] = acc_ref[h]


def _sample_attn(qs, kn, vn, cache_k, cache_v, layer):
    db, n = qs.shape[:2]
    past = cache_k.shape[-1]
    assert n <= LANES and past >= CACHE_BLOCK and past % CACHE_BLOCK == 0, (n, past)
    m = SB_HEADS * n
    new_spec = pl.BlockSpec((None, n, SB_WIDTH), lambda b: (b, 0, 0))
    hbm = pl.BlockSpec(memory_space=pl.ANY)
    buf = pltpu.VMEM((4, SB_HEADS, SB_HEAD_DIM, CACHE_BLOCK), F32)
    return pl.pallas_call(
        functools.partial(_sample_attn_body, layer=layer),
        grid=(db,),
        in_specs=[new_spec, new_spec, new_spec, hbm, hbm],
        out_specs=pl.BlockSpec((None, n, SB_HEADS, SB_HEAD_DIM), lambda b: (b, 0, 0, 0)),
        out_shape=jax.ShapeDtypeStruct((db, n, SB_HEADS, SB_HEAD_DIM), F32),
        scratch_shapes=[buf, buf, pltpu.SemaphoreType.DMA((4, 2)),
                        pltpu.VMEM((m, 1), F32),
                        pltpu.VMEM((SB_HEADS, n, SB_HEAD_DIM), F32)],
        compiler_params=pltpu.CompilerParams(
            dimension_semantics=("arbitrary",), vmem_limit_bytes=VMEM_LIMIT),
        name="sample_attn",
    )(qs, kn, vn, cache_k, cache_v)


def _route(logits):
    lane = lax.broadcasted_iota(jnp.int32, logits.shape, 1).astype(F32)
    neg = jnp.float32(-jnp.inf)
    big = jnp.float32(1 << 20)
    rmax = lambda a: jnp.max(a, axis=-1, keepdims=True)
    rmin = lambda a: jnp.min(a, axis=-1, keepdims=True)
    rsum = lambda a: jnp.sum(a, axis=-1, keepdims=True)

    gmask = lane < N_EXPERT_GROUPS
    gl = jnp.where(gmask, logits, neg)
    gmax = rmax(gl)
    g_sel = rmin(jnp.where(gl == gmax, lane, big))
    p_sel = 1.0 / rsum(jnp.where(gmask, jnp.exp(gl - gmax), 0.0))

    e_lo = N_EXPERT_GROUPS + g_sel * EXPERTS_PER_GROUP
    emask = (lane >= e_lo) & (lane < e_lo + EXPERTS_PER_GROUP)
    el = jnp.where(emask, logits, neg)
    emax = rmax(el)
    ex = jnp.where(emask, jnp.exp(el - emax), 0.0)
    p_e = ex / rsum(ex)
    pm = jnp.where(emask, p_e, -1.0)
    v1 = rmax(pm)
    i1 = rmin(jnp.where(pm == v1, lane, big))
    pm2 = jnp.where(lane == i1, -1.0, pm)
    v2 = rmax(pm2)
    i2 = rmin(jnp.where(pm2 == v2, lane, big))
    tot = v1 + v2
    gate1 = p_sel * v1 / tot
    gate2 = p_sel * v2 / tot
    comb = jnp.where(lane == i1, gate1, 0.0) + jnp.where(lane == i2, gate2, 0.0)
    gates = jnp.where(lane == i1 - e_lo, gate1, 0.0) + jnp.where(lane == i2 - e_lo, gate2, 0.0)
    return comb, g_sel, gates


def _rmsnorm_rows(h, g):
    return h * lax.rsqrt(jnp.mean(h * h, axis=-1, keepdims=True) + EPS) * g


def _slot_matrix(slot):
    lane = lax.broadcasted_iota(jnp.int32, (slot.shape[0], MOE_SLOTS), 1)
    return jnp.where(lane == slot.astype(jnp.int32), 1.0, 0.0)


def _post_attn_body(x_ref, gm_ref, sb_ref, wo_ref, g2_ref, wr_ref, br_ref,
                    h_ref, route_ref, bucket_ref, gate_ref, cnt_ref):
    h = (x_ref[...]
         + jnp.dot(gm_ref[...], wo_ref[:GM_WIDTH, :], preferred_element_type=F32)
         + jnp.dot(sb_ref[...], wo_ref[GM_WIDTH:, :], preferred_element_type=F32))
    h_ref[...] = h
    hn = _rmsnorm_rows(h, g2_ref[...])
    hi = hn.astype(BF16)
    lo = (hn - hi.astype(F32)).astype(BF16)
    both = jnp.dot(hi, wr_ref[...], preferred_element_type=F32)
    logits = (both[:, :LANES] + jnp.dot(lo, wr_ref[:, :LANES], preferred_element_type=F32)
              + both[:, LANES:]) + br_ref[...]
    comb, g_sel, gates = _route(logits)

    subs = [slice(st * MOE_SUB, (st + 1) * MOE_SUB) for st in range(h.shape[0] // MOE_SUB)]
    lane = lax.broadcasted_iota(jnp.int32, comb.shape, 1).astype(F32)
    r = lax.broadcasted_iota(jnp.int32, (MOE_SUB, MOE_SUB), 0)
    c = lax.broadcasted_iota(jnp.int32, (MOE_SUB, MOE_SUB), 1)
    earlier = jnp.where(r > c, 1.0, 0.0).astype(BF16)
    onehot = jnp.where(lane == g_sel, 1.0, 0.0)
    onehot_b = onehot.astype(BF16)
    before = jnp.concatenate(
        [jnp.dot(earlier, onehot_b[rows], preferred_element_type=F32) for rows in subs], axis=0)
    rank = jnp.sum(onehot * before, axis=-1, keepdims=True)
    slot = jnp.where(rank < MOE_CAP, g_sel * MOE_CAP + rank, -1.0)
    route_ref[...] = jnp.where(lane == SLOT_LANE, slot, comb)
    g1 = gates.astype(BF16)
    g2 = (gates - g1.astype(F32)).astype(BF16)
    g3 = (gates - g1.astype(F32) - g2.astype(F32)).astype(BF16)
    g12 = jnp.concatenate([g1, g2], axis=1)
    for st, rows in enumerate(subs):
        place = _slot_matrix(slot[rows]).T.astype(BF16)
        bucket_ref[st] = jnp.dot(place, hi[rows], preferred_element_type=F32).astype(BF16)
        placed = jnp.dot(place, g12[rows], preferred_element_type=F32)
        gate_ref[st] = (placed[:, :LANES] + placed[:, LANES:]
                        + jnp.dot(place, g3[rows], preferred_element_type=F32))
        cnt_ref[st] = jnp.broadcast_to(jnp.sum(onehot[rows], axis=0, keepdims=True), cnt_ref.shape[1:])


def _moe_ffn_body(x_ref, gate_ref, wg_ref, wu_ref, wd_ref, y_ref, wgb_ref, wub_ref, wdb_ref):
    @pl.when(pl.program_id(1) == 0)
    def _():
        wgb_ref[...] = wg_ref[...].astype(BF16)
        wub_ref[...] = wu_ref[...].astype(BF16)
        wdb_ref[...] = wd_ref[...].astype(BF16)

    x = x_ref[...].reshape(-1, D_MODEL)
    gates = gate_ref[...].reshape(-1, LANES)
    parts = []
    for e in range(EXPERTS_PER_GROUP):
        hg = jnp.dot(x, wgb_ref[e], preferred_element_type=F32)
        hu = jnp.dot(x, wub_ref[e], preferred_element_type=F32)
        parts.append((hg / (1.0 + jnp.exp(-hg)) * hu * gates[:, e:e + 1]).astype(BF16))
    a = jnp.concatenate(parts, axis=-1)
    y = jnp.dot(a, wdb_ref[...].reshape(GROUP_FF, D_MODEL), preferred_element_type=F32)
    y_ref[...] = y.reshape(y_ref.shape)


def _moe_combine_body(h_ref, route_ref, ys_ref, o_ref):
    for st in range(h_ref.shape[0] // MOE_SUB):
        rows = slice(st * MOE_SUB, (st + 1) * MOE_SUB)
        pick = _slot_matrix(route_ref[rows, SLOT_LANE:SLOT_LANE + 1]).astype(BF16)
        ys = ys_ref[st]
        hi = ys.astype(BF16)
        lo = (ys - hi.astype(F32)).astype(BF16)
        o_ref[rows, :] = (h_ref[rows, :] + jnp.dot(pick, hi, preferred_element_type=F32)
                          + jnp.dot(pick, lo, preferred_element_type=F32))


def _moe_dense_body(h_ref, route_ref, g2_ref, wg_ref, wu_ref, wd_ref, o_ref, hn_ref):
    g = pl.program_id(1)

    @pl.when(g == 0)
    def _():
        h = h_ref[...]
        o_ref[...] = h
        hn_ref[...] = _rmsnorm_rows(h, g2_ref[...]).astype(BF16)

    hn = hn_ref[...]
    comb = route_ref[...]
    lane = lax.broadcasted_iota(jnp.int32, comb.shape, 1)
    parts = []
    for e in range(EXPERTS_PER_GROUP):
        gate = jnp.sum(jnp.where(lane == N_EXPERT_GROUPS + g * EXPERTS_PER_GROUP + e, comb, 0.0),
                       axis=-1, keepdims=True)
        hg = jnp.dot(hn, wg_ref[e], preferred_element_type=F32)
        hu = jnp.dot(hn, wu_ref[e], preferred_element_type=F32)
        parts.append((hg / (1.0 + jnp.exp(-hg)) * hu * gate).astype(BF16))
    a = jnp.concatenate(parts, axis=-1)
    o_ref[...] += jnp.dot(a, wd_ref[...].reshape(GROUP_FF, D_MODEL), preferred_element_type=F32)


def _group_spec(a, index):
    return pl.BlockSpec((EXPERTS_PER_GROUP,) + a.shape[1:], index)


def _out_moe(x2d, gm, sb, params, *, tm):
    n = x2d.shape[0]
    wo, g2, wr, br, wg, wu, wd = params
    assert n % tm == 0 and tm % MOE_SUB == 0, (n, tm)
    n_sub = n // MOE_SUB
    sub = tm // MOE_SUB
    sup = min(MOE_SUPER, n_sub)
    params_1d = pltpu.CompilerParams(dimension_semantics=("parallel",), vmem_limit_bytes=VMEM_LIMIT)
    params_2d = pltpu.CompilerParams(
        dimension_semantics=("parallel", "arbitrary"), vmem_limit_bytes=VMEM_LIMIT)

    row = lambda w: pl.BlockSpec((tm, w), lambda i: (i, 0))
    full = lambda a: pl.BlockSpec(a.shape, lambda i: (0,) * a.ndim)
    per_sub = lambda rows, w: pl.BlockSpec((sub, rows, w), lambda i: (i, 0, 0))
    h, route, buckets, gates, cnt = pl.pallas_call(
        _post_attn_body,
        grid=(n // tm,),
        in_specs=[row(D_MODEL), row(GM_WIDTH), row(SB_WIDTH),
                  full(wo), full(g2), full(wr), full(br)],
        out_specs=[row(D_MODEL), row(LANES), per_sub(MOE_SLOTS, D_MODEL),
                   per_sub(MOE_SLOTS, LANES), per_sub(8, LANES)],
        out_shape=[jax.ShapeDtypeStruct((n, D_MODEL), F32),
                   jax.ShapeDtypeStruct((n, LANES), F32),
                   jax.ShapeDtypeStruct((n_sub, MOE_SLOTS, D_MODEL), BF16),
                   jax.ShapeDtypeStruct((n_sub, MOE_SLOTS, LANES), F32),
                   jax.ShapeDtypeStruct((n_sub, 8, LANES), F32)],
        compiler_params=params_1d,
        name="post_attn",
    )(x2d, gm, sb, wo, g2, wr, br)

    def sparse():
        by_group = lambda a: a.reshape(n_sub, N_EXPERT_GROUPS, MOE_CAP, a.shape[-1])
        bucket = lambda w: pl.BlockSpec((sup, 1, MOE_CAP, w), lambda g, s: (s, g, 0, 0))
        ys = pl.pallas_call(
            _moe_ffn_body,
            grid=(N_EXPERT_GROUPS, n_sub // sup),
            in_specs=[bucket(D_MODEL), bucket(LANES)]
                     + [_group_spec(w, lambda g, s: (g, 0, 0)) for w in (wg, wu, wd)],
            out_specs=bucket(D_MODEL),
            out_shape=jax.ShapeDtypeStruct((n_sub, N_EXPERT_GROUPS, MOE_CAP, D_MODEL), F32),
            scratch_shapes=[pltpu.VMEM((EXPERTS_PER_GROUP,) + w.shape[1:], BF16) for w in (wg, wu, wd)],
            compiler_params=params_2d,
            name="moe_ffn",
        )(by_group(buckets), by_group(gates), wg, wu, wd)
        return pl.pallas_call(
            _moe_combine_body,
            grid=(n // tm,),
            in_specs=[row(D_MODEL), row(LANES), per_sub(MOE_SLOTS, D_MODEL)],
            out_specs=row(D_MODEL),
            out_shape=jax.ShapeDtypeStruct((n, D_MODEL), F32),
            compiler_params=params_1d,
            name="moe_combine",
        )(h, route, ys.reshape(n_sub, MOE_SLOTS, D_MODEL))

    def dense():
        row2 = lambda w: pl.BlockSpec((tm, w), lambda i, g: (i, 0))
        wgb, wub, wdb = (w.astype(BF16) for w in (wg, wu, wd))
        return pl.pallas_call(
            _moe_dense_body,
            grid=(n // tm, N_EXPERT_GROUPS),
            in_specs=[row2(D_MODEL), row2(LANES), pl.BlockSpec(g2.shape, lambda i, g: (0, 0))]
                     + [_group_spec(w, lambda i, g: (g, 0, 0)) for w in (wg, wu, wd)],
            out_specs=row2(D_MODEL),
            out_shape=jax.ShapeDtypeStruct((n, D_MODEL), F32),
            scratch_shapes=[pltpu.VMEM((tm, D_MODEL), BF16)],
            compiler_params=params_2d,
            name="moe_dense",
        )(h, route, g2, wgb, wub, wdb)

    return lax.cond(jnp.max(cnt) <= MOE_CAP, sparse, dense)


def _layer(layer, xp, xs, cache_k, cache_v, norm1_g, w_in, gm_v_norm_g, gm_w_s, gm_b_s, q_norm_g,
           k_norm_g, w_out, norm2_g, w_router_group, b_router_group, w_router_expert,
           b_router_expert, w_gate, w_up, w_down):
    b, s, d = xp.shape
    db, n, _ = xs.shape
    head_of = jnp.arange(SB_WIDTH) // SB_HEAD_DIM
    bd = jnp.where(head_of[:, None] == head_of[None, :], 1.0 / SB_HEAD_DIM, 0.0).astype(BF16)
    in_params = (norm1_g[None, :], w_in.astype(BF16), gm_v_norm_g[None, :],
                 jnp.tile(q_norm_g, SB_HEADS)[None, :], jnp.tile(k_norm_g, SB_HEADS)[None, :],
                 gm_w_s, gm_b_s.T, bd)

    wr = jnp.concatenate(
        [w_router_group, jnp.transpose(w_router_expert, (1, 0, 2)).reshape(d, N_EXPERTS)], axis=1)
    wr = jnp.pad(wr, ((0, 0), (0, LANES - wr.shape[1])))
    wr_hi = wr.astype(BF16)
    wr_lo = (wr - wr_hi.astype(F32)).astype(BF16)
    br = jnp.pad(jnp.concatenate([b_router_group, b_router_expert.reshape(-1)]),
                 (0, LANES - N_EXPERT_GROUPS - N_EXPERTS))[None, :]

    moe_params = (w_out.astype(BF16), norm2_g[None, :], jnp.concatenate([wr_hi, wr_lo], axis=1), br,
                  w_gate, w_up, w_down)

    xp2 = xp.reshape(b * s, d)
    gm, kf, vf, qs, kb, vb = _mixer_in(xp2, in_params, tm=512, chunk=GM_CHUNK, stream=False, seq=s)
    shp = (b, s, SB_WIDTH)
    sb = _prompt_attn(qs.reshape(shp), kb.reshape(shp), vb.reshape(shp))
    yp = _out_moe(xp2, gm, sb.reshape(b * s, SB_WIDTH), moe_params, tm=1024).reshape(b, s, d)

    xs2 = xs.reshape(db * n, d)
    gm_s, kf_s, vf_s, q_s, gv_s = _mixer_in(xs2, in_params, tm=db * n, chunk=n, stream=True)
    shs = (db, n, SB_WIDTH)
    frame_minor = lambda c: jnp.transpose(c, (0, 1, 3, 4, 2))
    sb_s = _sample_attn(q_s.reshape(shs), kf_s.reshape(shs), vf_s.reshape(shs),
                        frame_minor(cache_k), frame_minor(cache_v), layer)
    sb_s = sb_s.reshape(db * n, SB_WIDTH).astype(BF16)
    ys = _out_moe(xs2, gm_s, sb_s, moe_params, tm=db * n).reshape(db, n, d)

    heads = (SB_HEADS, SB_HEAD_DIM)
    rows = lambda a: jnp.transpose(a.reshape(b, *heads, s), (0, 3, 1, 2))
    return (yp, ys, rows(kf), rows(vf),
            kf_s.reshape(db, n, *heads), vf_s.reshape(db, n, *heads),
            gv_s.reshape(db, n, GM_GROUPS, GM_GROUP_DIM))


def kernel(x_prompt, x_sample, cache_sb_k, cache_sb_v, norm1_g, w_in, gm_v_norm_g, gm_w_s, gm_b_s, q_norm_g, k_norm_g, w_out, norm2_g, w_router_group, b_router_group, w_router_expert, b_router_expert, w_gate, w_up, w_down):
    depth = w_in.shape[0]
    yp, ys = x_prompt, x_sample
    outs = [[] for _ in range(5)]
    for l in range(depth):
        yp, ys, *rest = _layer(
            l, yp, ys, cache_sb_k, cache_sb_v, norm1_g[l], w_in[l], gm_v_norm_g[l], gm_w_s[l],
            gm_b_s[l], q_norm_g[l], k_norm_g[l], w_out[l], norm2_g[l], w_router_group[l],
            b_router_group[l], w_router_expert[l], b_router_expert[l], w_gate[l], w_up[l],
            w_down[l])
        for acc, r in zip(outs, rest):
            acc.append(r)
    return (yp, ys) + tuple(jnp.stack(o, axis=0) for o in outs)
```

```python
import functools

import jax
import jax.numpy as jnp
from jax import lax
from jax.experimental import pallas as pl
from jax.experimental.pallas import tpu as pltpu

D_MODEL = 1024
GM_WIDTH = 512
GM_GROUPS = 4
GM_GROUP_DIM = 128
GM_CHUNK = 128
SB_WIDTH = 512
SB_HEAD_DIM = 64
SB_HEADS = 8
N_EXPERT_GROUPS = 4
EXPERTS_PER_GROUP = 4
N_EXPERTS = 16
EXPERT_FF = 256
GROUP_FF = EXPERTS_PER_GROUP * EXPERT_FF
EPS = 1e-6

LANES = 128
HEAD_PAIRS = SB_WIDTH // LANES
SB_BLOCK = 128
SB_WIDE = 2 * SB_BLOCK
CACHE_BLOCK = 256
MOE_SUB = 256
MOE_CAP = 96
MOE_SLOTS = N_EXPERT_GROUPS * MOE_CAP
MOE_SUPER = 8
SLOT_LANE = 0
VMEM_LIMIT = 56 * 1024 * 1024

LOG2_E = 1.4426950408889634
Q_SCALE = SB_HEAD_DIM ** -0.5 * LOG2_E
SB_UNDERFLOW_BITS = 160.0

F32 = jnp.float32
BF16 = jnp.bfloat16
_NT = (((1,), (1,)), ((), ()))


def _gelu_tanh(x):
    return 0.5 * x * (1.0 + jnp.tanh(0.7978845608028654 * (x + 0.044715 * (x * x * x))))


def _softplus2(z):
    return jnp.maximum(z, 0.0) + jnp.log2(1.0 + jnp.exp2(-jnp.abs(z)))


def _suffix_matrix(n):
    r = lax.broadcasted_iota(jnp.int32, (n, n), 0)
    c = lax.broadcasted_iota(jnp.int32, (n, n), 1)
    return jnp.where(r > c, 1.0, 0.0).astype(BF16)


def _mixer_in_body(x_ref, g1_ref, w_in_ref, gvg_ref, qg_ref, kg_ref, ws_ref, bst_ref, bd_ref,
                   gm_ref, kf_ref, vf_ref, *rest, chunk, stream):
    x = x_ref[...]
    tm = x.shape[0]
    xn = x * lax.rsqrt(jnp.mean(x * x, axis=-1, keepdims=True) + EPS) * g1_ref[...]
    proj = jnp.dot(xn.astype(BF16), w_in_ref[...], preferred_element_type=F32)

    gu = _gelu_tanh(proj[:, :GM_WIDTH])
    gvr = _gelu_tanh(proj[:, GM_WIDTH:2 * GM_WIDTH])
    gv = gvr * lax.rsqrt(jnp.mean(gvr * gvr, axis=-1, keepdims=True) + EPS) * gvg_ref[...]

    o = 2 * GM_WIDTH
    q = proj[:, o:o + SB_WIDTH]
    k = proj[:, o + SB_WIDTH:o + 2 * SB_WIDTH]
    v = proj[:, o + 2 * SB_WIDTH:]
    bd = bd_ref[...]
    qms = jnp.dot((q * q).astype(BF16), bd, preferred_element_type=F32)
    kms = jnp.dot((k * k).astype(BF16), bd, preferred_element_type=F32)
    qs = q * lax.rsqrt(qms + EPS) * qg_ref[...] * Q_SCALE
    kn = k * lax.rsqrt(kms + EPS) * kg_ref[...]
    if stream:
        kf_ref[...] = kn
        vf_ref[...] = v
        rest[0][...] = qs
        rest[1][...] = gv
    else:
        kf_ref[...] = kn.T
        vf_ref[...] = v.T
        qs_ref, kb_ref, vb_ref = rest
        qs_ref[...] = qs.astype(BF16)
        kb_ref[...] = kn.astype(BF16)
        vb_ref[...] = v.astype(BF16)

    r = lax.broadcasted_iota(jnp.int32, (chunk, chunk), 0)
    c = lax.broadcasted_iota(jnp.int32, (chunk, chunk), 1)
    gvb = gv.astype(BF16)
    chunks = [slice(ci * chunk, (ci + 1) * chunk) for ci in range(tm // chunk)]
    for g in range(GM_GROUPS):
        w = jnp.where(r >= c, ws_ref[g, :chunk, :chunk], 0.0).astype(BF16)
        b = bst_ref[:chunk, g:g + 1]
        cols = slice(g * GM_GROUP_DIM, (g + 1) * GM_GROUP_DIM)
        mixed = jnp.dot(w, jnp.concatenate([gvb[rows, cols] for rows in chunks], axis=1),
                        preferred_element_type=F32)
        for ci, rows in enumerate(chunks):
            part = mixed[:, ci * GM_GROUP_DIM:(ci + 1) * GM_GROUP_DIM] + b
            gm_ref[rows, cols] = (gu[rows, cols] * part).astype(BF16)


def _mixer_in(x2d, params, *, tm, chunk, stream, seq=None):
    n = x2d.shape[0]
    assert n % tm == 0 and tm % chunk == 0 and chunk <= GM_CHUNK, (n, tm, chunk)
    full = lambda a: pl.BlockSpec(a.shape, lambda i: (0,) * a.ndim)
    spec = lambda s: pl.BlockSpec((tm,) + s.shape[1:], lambda i: (i,) + (0,) * (len(s.shape) - 1))
    gm = jax.ShapeDtypeStruct((n, GM_WIDTH), BF16)
    if stream:
        rows = jax.ShapeDtypeStruct((n, SB_WIDTH), F32)
        out_shape = [gm, rows, rows,
                     rows,
                     jax.ShapeDtypeStruct((n, GM_WIDTH), F32)]
        out_specs = [spec(s) for s in out_shape]
    else:
        per_seq = seq // tm
        rows_t = jax.ShapeDtypeStruct((n // seq, SB_WIDTH, seq), F32)
        spec_t = pl.BlockSpec((None, SB_WIDTH, tm), lambda i: (i // per_seq, 0, i % per_seq))
        dense = jax.ShapeDtypeStruct((n, SB_WIDTH), BF16)
        out_shape = [gm, rows_t, rows_t, dense, dense, dense]
        out_specs = [spec(gm), spec_t, spec_t] + [spec(dense)] * 3
    return pl.pallas_call(
        functools.partial(_mixer_in_body, chunk=chunk, stream=stream),
        grid=(n // tm,),
        in_specs=[pl.BlockSpec((tm, D_MODEL), lambda i: (i, 0))] + [full(a) for a in params],
        out_specs=out_specs,
        out_shape=out_shape,
        compiler_params=pltpu.CompilerParams(
            dimension_semantics=("parallel",), vmem_limit_bytes=VMEM_LIMIT),
        name="mixer_in_stream" if stream else "mixer_in",
    )(x2d, *params)


def _prompt_attn_body(q_ref, k_ref, v_ref, sfx_ref, sfx_wide_ref, o_ref, qst_ref, c_ref, acc_ref):
    i = pl.program_id(1)
    lane = lax.broadcasted_iota(jnp.int32, (SB_BLOCK, LANES), 1)
    first = lane < SB_HEAD_DIM
    for hp in range(HEAD_PAIRS):
        q = q_ref[:, hp * LANES:(hp + 1) * LANES]
        zero = jnp.zeros_like(q)
        qst_ref[hp, :SB_BLOCK, :] = jnp.where(first, q, zero)
        qst_ref[hp, SB_BLOCK:, :] = jnp.where(first, zero, q)
    m = 2 * SB_BLOCK
    suffix = {SB_BLOCK: sfx_ref, SB_WIDE: sfx_wide_ref}

    def sweep(j, keys, causal, fresh):
        rows = pl.ds(pl.multiple_of(j * SB_BLOCK, SB_BLOCK), keys)
        cols = lambda hp: slice(hp * LANES, (hp + 1) * LANES)
        z = jnp.concatenate(
            [lax.dot_general(qst_ref[hp], k_ref[rows, cols(hp)], _NT, preferred_element_type=F32)
             for hp in range(HEAD_PAIRS)], axis=0)
        sp = _softplus2(z)
        if causal is not None:
            sp = jnp.where(causal, sp, 0.0)
        c0 = jnp.zeros((HEAD_PAIRS * m, 1), F32) if fresh else c_ref[...]
        later = jnp.dot(sp.astype(BF16), suffix[keys][...], preferred_element_type=F32) + c0
        w = jnp.exp2(z - sp - later)
        if causal is not None:
            w = jnp.where(causal, w, 0.0)
        wb = w.astype(BF16)
        pv = jnp.concatenate(
            [jnp.dot(wb[hp * m:(hp + 1) * m, :], v_ref[rows, cols(hp)], preferred_element_type=F32)
             for hp in range(HEAD_PAIRS)], axis=0)
        acc_ref[...] = pv if fresh else acc_ref[...] + pv
        c_ref[...] = c0 + jnp.sum(sp, axis=-1, keepdims=True)

    j0 = i
    t = lax.broadcasted_iota(jnp.int32, (HEAD_PAIRS * m, SB_BLOCK), 0) % SB_BLOCK
    s = lax.broadcasted_iota(jnp.int32, (HEAD_PAIRS * m, SB_BLOCK), 1)
    sweep(i, SB_BLOCK, s < t, True)

    def least_carry():
        return jnp.min(c_ref[...])

    per_wide = SB_WIDE // SB_BLOCK
    n_wide = j0 // per_wide

    def more(state):
        jj, carry = state
        return (jj < n_wide) & (carry < SB_UNDERFLOW_BITS)

    def body(state):
        jj, _ = state
        sweep(j0 - per_wide * (jj + 1), SB_WIDE, None, False)
        return jj + 1, least_carry()

    done, carry = lax.while_loop(more, body, (jnp.int32(0), jnp.float32(0.0)))

    @pl.when((done == n_wide) & (j0 % per_wide == 1) & (carry < SB_UNDERFLOW_BITS))
    def _():
        sweep(0, SB_BLOCK, None, False)

    for hp in range(HEAD_PAIRS):
        lo = hp * m
        o_ref[:, hp * LANES:(hp + 1) * LANES] = jnp.where(
            first, acc_ref[lo:lo + SB_BLOCK, :], acc_ref[lo + SB_BLOCK:lo + m, :]).astype(o_ref.dtype)


def _prompt_attn(qs, kb, vb):
    b, s, _ = qs.shape
    assert s % SB_BLOCK == 0, s
    sfx = (_suffix_matrix(SB_BLOCK), _suffix_matrix(SB_WIDE))
    q_spec = pl.BlockSpec((None, SB_BLOCK, SB_WIDTH), lambda bi, i: (bi, i, 0))
    kv_spec = pl.BlockSpec((None, s, SB_WIDTH), lambda bi, i: (bi, 0, 0))
    m = 2 * SB_BLOCK
    return pl.pallas_call(
        _prompt_attn_body,
        grid=(b, s // SB_BLOCK),
        in_specs=[q_spec, kv_spec, kv_spec] + [pl.BlockSpec(a.shape, lambda bi, i: (0, 0)) for a in sfx],
        out_specs=q_spec,
        out_shape=jax.ShapeDtypeStruct((b, s, SB_WIDTH), BF16),
        scratch_shapes=[pltpu.VMEM((HEAD_PAIRS, m, LANES), BF16),
                        pltpu.VMEM((HEAD_PAIRS * m, 1), F32),
                        pltpu.VMEM((HEAD_PAIRS * m, LANES), F32)],
        compiler_params=pltpu.CompilerParams(
            dimension_semantics=("parallel", "arbitrary"), vmem_limit_bytes=VMEM_LIMIT),
        name="prompt_attn",
    )(qs, kb, vb, *sfx)


def _sample_attn_body(q_ref, kn_ref, vn_ref, ck_hbm, cv_hbm, o_ref, kbuf, vbuf, sems, c_ref, acc_ref,
                      *, layer):
    b = pl.program_id(0)
    n = q_ref.shape[0]
    m = SB_HEADS * n
    nblk = ck_hbm.shape[-1] // CACHE_BLOCK

    def copies(stream, blk, slot):
        cols = pl.ds(blk * CACHE_BLOCK, CACHE_BLOCK)
        return (pltpu.make_async_copy(ck_hbm.at[layer, stream, :, :, cols], kbuf.at[slot], sems.at[slot, 0]),
                pltpu.make_async_copy(cv_hbm.at[layer, stream, :, :, cols], vbuf.at[slot], sems.at[slot, 1]))

    def start(stream, blk, slot):
        for cp in copies(stream, blk, slot):
            cp.start()

    def wait(stream, blk, slot):
        for cp in copies(stream, blk, slot):
            cp.wait()

    def slot_of(jj):
        return 2 + jj % 2

    @pl.when(b == 0)
    def _():
        start(0, nblk - 1, 0)

    @pl.when(b + 1 < pl.num_programs(0))
    def _():
        start(b + 1, nblk - 1, (b + 1) % 2)

    head = lambda a, h: a[:, h * SB_HEAD_DIM:(h + 1) * SB_HEAD_DIM]
    q_all = q_ref[...].astype(BF16)
    q = [head(q_all, h) for h in range(SB_HEADS)]

    def sweep(slot, newest, suffix, causal, fresh):
        def logits(h):
            z = jnp.dot(q[h], kbuf[slot, h].astype(BF16), preferred_element_type=F32)
            if newest is None:
                return z
            return jnp.concatenate(
                [z, lax.dot_general(q[h], newest[0][h], _NT, preferred_element_type=F32)], axis=1)

        z = jnp.concatenate([logits(h) for h in range(SB_HEADS)], axis=0)
        sp = _softplus2(z)
        if causal is not None:
            sp = jnp.where(causal, sp, 0.0)
        c0 = jnp.zeros((m, 1), F32) if fresh else c_ref[...]
        later = jnp.dot(sp.astype(BF16), suffix, preferred_element_type=F32) + c0
        w = jnp.exp2(z - sp - later)
        if causal is not None:
            w = jnp.where(causal, w, 0.0)
        wb = w.astype(BF16)
        pvs = []
        for h in range(SB_HEADS):
            wh = wb[h * n:(h + 1) * n, :]
            pv = lax.dot_general(wh[:, :CACHE_BLOCK], vbuf[slot, h].astype(BF16), _NT,
                                 preferred_element_type=F32)
            if newest is not None:
                pv = pv + jnp.dot(wh[:, CACHE_BLOCK:], newest[1][h], preferred_element_type=F32)
            pvs.append(pv)
        pv = jnp.stack(pvs)
        acc_ref[...] = pv if fresh else acc_ref[...] + pv
        c_ref[...] = c0 + jnp.sum(sp, axis=-1, keepdims=True)

    pad = jnp.zeros((LANES - n, SB_HEAD_DIM), BF16)
    kn_all, vn_all = kn_ref[...].astype(BF16), vn_ref[...].astype(BF16)
    newest = ([jnp.concatenate([head(kn_all, h), pad], axis=0) for h in range(SB_HEADS)],
              [jnp.concatenate([head(vn_all, h), pad], axis=0) for h in range(SB_HEADS)])
    t = lax.broadcasted_iota(jnp.int32, (m, CACHE_BLOCK + LANES), 0) % n
    s = lax.broadcasted_iota(jnp.int32, (m, CACHE_BLOCK + LANES), 1) - CACHE_BLOCK
    wait(b, nblk - 1, b % 2)
    sweep(b % 2, newest, _suffix_matrix(CACHE_BLOCK + LANES), s < t, True)

    suffix = _suffix_matrix(CACHE_BLOCK)

    def more(state):
        jj, carry = state
        return (jj < nblk) & (carry < SB_UNDERFLOW_BITS)

    def body(state):
        jj, _ = state
        blk = nblk - 1 - jj
        slot = slot_of(jj)
        wait(b, blk, slot)

        @pl.when(blk > 0)
        def _():
            start(b, blk - 1, slot_of(jj + 1))

        sweep(slot, None, suffix, None, False)
        return jj + 1, jnp.min(c_ref[...])

    state = (jnp.int32(1), jnp.min(c_ref[...]))

    if nblk > 1:
        @pl.when(more(state))
        def _():
            start(b, nblk - 2, slot_of(1))

    done, _ = lax.while_loop(more, body, state)

    @pl.when((done > 1) & (done < nblk))
    def _():
        wait(b, nblk - 1 - done, slot_of(done))

    for h in range(SB_HEADS):
        o_ref[:, h, :] = acc_ref[h]


def _sample_attn(qs, kn, vn, cache_k, cache_v, layer):
    db, n = qs.shape[:2]
    past = cache_k.shape[-1]
    assert n <= LANES and past >= CACHE_BLOCK and past % CACHE_BLOCK == 0, (n, past)
    m = SB_HEADS * n
    new_spec = pl.BlockSpec((None, n, SB_WIDTH), lambda b: (b, 0, 0))
    hbm = pl.BlockSpec(memory_space=pl.ANY)
    buf = pltpu.VMEM((4, SB_HEADS, SB_HEAD_DIM, CACHE_BLOCK), F32)
    return pl.pallas_call(
        functools.partial(_sample_attn_body, layer=layer),
        grid=(db,),
        in_specs=[new_spec, new_spec, new_spec, hbm, hbm],
        out_specs=pl.BlockSpec((None, n, SB_HEADS, SB_HEAD_DIM), lambda b: (b, 0, 0, 0)),
        out_shape=jax.ShapeDtypeStruct((db, n, SB_HEADS, SB_HEAD_DIM), F32),
        scratch_shapes=[buf, buf, pltpu.SemaphoreType.DMA((4, 2)),
                        pltpu.VMEM((m, 1), F32),
                        pltpu.VMEM((SB_HEADS, n, SB_HEAD_DIM), F32)],
        compiler_params=pltpu.CompilerParams(
            dimension_semantics=("arbitrary",), vmem_limit_bytes=VMEM_LIMIT),
        name="sample_attn",
    )(qs, kn, vn, cache_k, cache_v)


def _route(logits):
    lane = lax.broadcasted_iota(jnp.int32, logits.shape, 1).astype(F32)
    neg = jnp.float32(-jnp.inf)
    big = jnp.float32(1 << 20)
    rmax = lambda a: jnp.max(a, axis=-1, keepdims=True)
    rmin = lambda a: jnp.min(a, axis=-1, keepdims=True)
    rsum = lambda a: jnp.sum(a, axis=-1, keepdims=True)

    gmask = lane < N_EXPERT_GROUPS
    gl = jnp.where(gmask, logits, neg)
    gmax = rmax(gl)
    g_sel = rmin(jnp.where(gl == gmax, lane, big))
    p_sel = 1.0 / rsum(jnp.where(gmask, jnp.exp(gl - gmax), 0.0))

    e_lo = N_EXPERT_GROUPS + g_sel * EXPERTS_PER_GROUP
    emask = (lane >= e_lo) & (lane < e_lo + EXPERTS_PER_GROUP)
    el = jnp.where(emask, logits, neg)
    emax = rmax(el)
    ex = jnp.where(emask, jnp.exp(el - emax), 0.0)
    p_e = ex / rsum(ex)
    pm = jnp.where(emask, p_e, -1.0)
    v1 = rmax(pm)
    i1 = rmin(jnp.where(pm == v1, lane, big))
    pm2 = jnp.where(lane == i1, -1.0, pm)
    v2 = rmax(pm2)
    i2 = rmin(jnp.where(pm2 == v2, lane, big))
    tot = v1 + v2
    gate1 = p_sel * v1 / tot
    gate2 = p_sel * v2 / tot
    comb = jnp.where(lane == i1, gate1, 0.0) + jnp.where(lane == i2, gate2, 0.0)
    gates = jnp.where(lane == i1 - e_lo, gate1, 0.0) + jnp.where(lane == i2 - e_lo, gate2, 0.0)
    return comb, g_sel, gates


def _rmsnorm_rows(h, g):
    return h * lax.rsqrt(jnp.mean(h * h, axis=-1, keepdims=True) + EPS) * g


def _slot_matrix(slot):
    lane = lax.broadcasted_iota(jnp.int32, (slot.shape[0], MOE_SLOTS), 1)
    return jnp.where(lane == slot.astype(jnp.int32), 1.0, 0.0)


def _post_attn_body(x_ref, gm_ref, sb_ref, wo_ref, g2_ref, wr_ref, br_ref,
                    h_ref, route_ref, bucket_ref, gate_ref, cnt_ref):
    h = (x_ref[...]
         + jnp.dot(gm_ref[...], wo_ref[:GM_WIDTH, :], preferred_element_type=F32)
         + jnp.dot(sb_ref[...], wo_ref[GM_WIDTH:, :], preferred_element_type=F32))
    h_ref[...] = h
    hn = _rmsnorm_rows(h, g2_ref[...])
    hi = hn.astype(BF16)
    lo = (hn - hi.astype(F32)).astype(BF16)
    both = jnp.dot(hi, wr_ref[...], preferred_element_type=F32)
    logits = (both[:, :LANES] + jnp.dot(lo, wr_ref[:, :LANES], preferred_element_type=F32)
              + both[:, LANES:]) + br_ref[...]
    comb, g_sel, gates = _route(logits)

    subs = [slice(st * MOE_SUB, (st + 1) * MOE_SUB) for st in range(h.shape[0] // MOE_SUB)]
    lane = lax.broadcasted_iota(jnp.int32, comb.shape, 1).astype(F32)
    r = lax.broadcasted_iota(jnp.int32, (MOE_SUB, MOE_SUB), 0)
    c = lax.broadcasted_iota(jnp.int32, (MOE_SUB, MOE_SUB), 1)
    earlier = jnp.where(r > c, 1.0, 0.0).astype(BF16)
    onehot = jnp.where(lane == g_sel, 1.0, 0.0)
    onehot_b = onehot.astype(BF16)
    before = jnp.concatenate(
        [jnp.dot(earlier, onehot_b[rows], preferred_element_type=F32) for rows in subs], axis=0)
    rank = jnp.sum(onehot * before, axis=-1, keepdims=True)
    slot = jnp.where(rank < MOE_CAP, g_sel * MOE_CAP + rank, -1.0)
    route_ref[...] = jnp.where(lane == SLOT_LANE, slot, comb)
    g1 = gates.astype(BF16)
    g2 = (gates - g1.astype(F32)).astype(BF16)
    g3 = (gates - g1.astype(F32) - g2.astype(F32)).astype(BF16)
    g12 = jnp.concatenate([g1, g2], axis=1)
    for st, rows in enumerate(subs):
        place = _slot_matrix(slot[rows]).T.astype(BF16)
        bucket_ref[st] = jnp.dot(place, hi[rows], preferred_element_type=F32).astype(BF16)
        placed = jnp.dot(place, g12[rows], preferred_element_type=F32)
        gate_ref[st] = (placed[:, :LANES] + placed[:, LANES:]
                        + jnp.dot(place, g3[rows], preferred_element_type=F32))
        cnt_ref[st] = jnp.broadcast_to(jnp.sum(onehot[rows], axis=0, keepdims=True), cnt_ref.shape[1:])


def _moe_ffn_body(x_ref, gate_ref, xs_ref, gates_ref, wg_ref, wu_ref, wd_ref, y_ref, ys_ref,
                  wgb_ref, wub_ref, wdb_ref, *, prompt_steps):
    s = pl.program_id(1)

    @pl.when(s == 0)
    def _():
        wgb_ref[...] = wg_ref[...].astype(BF16)
        wub_ref[...] = wu_ref[...].astype(BF16)
        wdb_ref[...] = wd_ref[...].astype(BF16)

    def ffn(x_ref, gate_ref, y_ref):
        x = x_ref[...].reshape(-1, D_MODEL)
        gates = gate_ref[...].reshape(-1, LANES)
        parts = []
        for e in range(EXPERTS_PER_GROUP):
            hg = jnp.dot(x, wgb_ref[e], preferred_element_type=F32)
            hu = jnp.dot(x, wub_ref[e], preferred_element_type=F32)
            parts.append((hg / (1.0 + jnp.exp(-hg)) * hu * gates[:, e:e + 1]).astype(BF16))
        a = jnp.concatenate(parts, axis=-1)
        y = jnp.dot(a, wdb_ref[...].reshape(GROUP_FF, D_MODEL), preferred_element_type=F32)
        y_ref[...] = y.reshape(y_ref.shape)

    @pl.when(s < prompt_steps)
    def _():
        ffn(x_ref, gate_ref, y_ref)

    @pl.when(s == prompt_steps)
    def _():
        ffn(xs_ref, gates_ref, ys_ref)


def _moe_combine_body(h_ref, route_ref, ys_ref, o_ref):
    for st in range(h_ref.shape[0] // MOE_SUB):
        rows = slice(st * MOE_SUB, (st + 1) * MOE_SUB)
        pick = _slot_matrix(route_ref[rows, SLOT_LANE:SLOT_LANE + 1]).astype(BF16)
        ys = ys_ref[st]
        hi = ys.astype(BF16)
        lo = (ys - hi.astype(F32)).astype(BF16)
        o_ref[rows, :] = (h_ref[rows, :] + jnp.dot(pick, hi, preferred_element_type=F32)
                          + jnp.dot(pick, lo, preferred_element_type=F32))


def _moe_dense_body(h_ref, route_ref, g2_ref, wg_ref, wu_ref, wd_ref, o_ref, hn_ref):
    g = pl.program_id(1)

    @pl.when(g == 0)
    def _():
        h = h_ref[...]
        o_ref[...] = h
        hn_ref[...] = _rmsnorm_rows(h, g2_ref[...]).astype(BF16)

    hn = hn_ref[...]
    comb = route_ref[...]
    lane = lax.broadcasted_iota(jnp.int32, comb.shape, 1)
    parts = []
    for e in range(EXPERTS_PER_GROUP):
        gate = jnp.sum(jnp.where(lane == N_EXPERT_GROUPS + g * EXPERTS_PER_GROUP + e, comb, 0.0),
                       axis=-1, keepdims=True)
        hg = jnp.dot(hn, wg_ref[e], preferred_element_type=F32)
        hu = jnp.dot(hn, wu_ref[e], preferred_element_type=F32)
        parts.append((hg / (1.0 + jnp.exp(-hg)) * hu * gate).astype(BF16))
    a = jnp.concatenate(parts, axis=-1)
    o_ref[...] += jnp.dot(a, wd_ref[...].reshape(GROUP_FF, D_MODEL), preferred_element_type=F32)


def _group_spec(a, index):
    return pl.BlockSpec((EXPERTS_PER_GROUP,) + a.shape[1:], index)


def _post_attn(x2d, gm, sb, params, tm):
    n = x2d.shape[0]
    wo, g2, wr, br = params
    assert n % tm == 0 and tm % MOE_SUB == 0, (n, tm)
    n_sub, sub = n // MOE_SUB, tm // MOE_SUB
    row = lambda w: pl.BlockSpec((tm, w), lambda i: (i, 0))
    full = lambda a: pl.BlockSpec(a.shape, lambda i: (0,) * a.ndim)
    per_sub = lambda rows, w: pl.BlockSpec((sub, rows, w), lambda i: (i, 0, 0))
    return pl.pallas_call(
        _post_attn_body,
        grid=(n // tm,),
        in_specs=[row(D_MODEL), row(GM_WIDTH), row(SB_WIDTH),
                  full(wo), full(g2), full(wr), full(br)],
        out_specs=[row(D_MODEL), row(LANES), per_sub(MOE_SLOTS, D_MODEL),
                   per_sub(MOE_SLOTS, LANES), per_sub(8, LANES)],
        out_shape=[jax.ShapeDtypeStruct((n, D_MODEL), F32),
                   jax.ShapeDtypeStruct((n, LANES), F32),
                   jax.ShapeDtypeStruct((n_sub, MOE_SLOTS, D_MODEL), BF16),
                   jax.ShapeDtypeStruct((n_sub, MOE_SLOTS, LANES), F32),
                   jax.ShapeDtypeStruct((n_sub, 8, LANES), F32)],
        compiler_params=pltpu.CompilerParams(
            dimension_semantics=("parallel",), vmem_limit_bytes=VMEM_LIMIT),
        name="post_attn",
    )(x2d, gm, sb, wo, g2, wr, br)


def _out_moe(prompt, sample, params, *, tm_prompt, tm_sample):
    wo, g2, wr, br, wg, wu, wd = params
    routed = [_post_attn(*grp, (wo, g2, wr, br), tm) for grp, tm in
              ((prompt, tm_prompt), (sample, tm_sample))]
    tms = (tm_prompt, tm_sample)
    params_2d = pltpu.CompilerParams(
        dimension_semantics=("parallel", "arbitrary"), vmem_limit_bytes=VMEM_LIMIT)

    def sparse():
        (_, _, bk_p, gt_p, _), (_, _, bk_s, gt_s, _) = routed
        n_sub, n_sub_s = bk_p.shape[0], bk_s.shape[0]
        sup = min(MOE_SUPER, n_sub)
        assert n_sub % sup == 0, (n_sub, sup)
        steps = n_sub // sup
        by_group = lambda a: a.reshape(a.shape[0], N_EXPERT_GROUPS, MOE_CAP, a.shape[-1])
        bucket = lambda w: pl.BlockSpec((sup, 1, MOE_CAP, w),
                                        lambda g, s: (jnp.minimum(s, steps - 1), g, 0, 0))
        bucket_s = lambda w: pl.BlockSpec((n_sub_s, 1, MOE_CAP, w), lambda g, s: (0, g, 0, 0))
        ys = pl.pallas_call(
            functools.partial(_moe_ffn_body, prompt_steps=steps),
            grid=(N_EXPERT_GROUPS, steps + 1),
            in_specs=[bucket(D_MODEL), bucket(LANES), bucket_s(D_MODEL), bucket_s(LANES)]
                     + [_group_spec(w, lambda g, s: (g, 0, 0)) for w in (wg, wu, wd)],
            out_specs=[bucket(D_MODEL), bucket_s(D_MODEL)],
            out_shape=[jax.ShapeDtypeStruct((n, N_EXPERT_GROUPS, MOE_CAP, D_MODEL), F32)
                       for n in (n_sub, n_sub_s)],
            scratch_shapes=[pltpu.VMEM((EXPERTS_PER_GROUP,) + w.shape[1:], BF16) for w in (wg, wu, wd)],
            compiler_params=params_2d,
            name="moe_ffn",
        )(by_group(bk_p), by_group(gt_p), by_group(bk_s), by_group(gt_s), wg, wu, wd)

        def combine(h, route, y, tm):
            n, sub = h.shape[0], tm // MOE_SUB
            row = lambda w: pl.BlockSpec((tm, w), lambda i: (i, 0))
            return pl.pallas_call(
                _moe_combine_body,
                grid=(n // tm,),
                in_specs=[row(D_MODEL), row(LANES),
                          pl.BlockSpec((sub, MOE_SLOTS, D_MODEL), lambda i: (i, 0, 0))],
                out_specs=row(D_MODEL),
                out_shape=jax.ShapeDtypeStruct((n, D_MODEL), F32),
                compiler_params=pltpu.CompilerParams(
                    dimension_semantics=("parallel",), vmem_limit_bytes=VMEM_LIMIT),
                name="moe_combine",
            )(h, route, y.reshape(-1, MOE_SLOTS, D_MODEL))

        return tuple(combine(r[0], r[1], y, tm) for r, y, tm in zip(routed, ys, tms))

    def dense():
        wgb, wub, wdb = (w.astype(BF16) for w in (wg, wu, wd))

        def all_experts(h, route, tm):
            row2 = lambda w: pl.BlockSpec((tm, w), lambda i, g: (i, 0))
            return pl.pallas_call(
                _moe_dense_body,
                grid=(h.shape[0] // tm, N_EXPERT_GROUPS),
                in_specs=[row2(D_MODEL), row2(LANES), pl.BlockSpec(g2.shape, lambda i, g: (0, 0))]
                         + [_group_spec(w, lambda i, g: (g, 0, 0)) for w in (wg, wu, wd)],
                out_specs=row2(D_MODEL),
                out_shape=jax.ShapeDtypeStruct(h.shape, F32),
                scratch_shapes=[pltpu.VMEM((tm, D_MODEL), BF16)],
                compiler_params=params_2d,
                name="moe_dense",
            )(h, route, g2, wgb, wub, wdb)

        return tuple(all_experts(r[0], r[1], tm) for r, tm in zip(routed, tms))

    most = jnp.maximum(jnp.max(routed[0][4]), jnp.max(routed[1][4]))
    return lax.cond(most <= MOE_CAP, sparse, dense)


def _layer(layer, xp, xs, cache_k, cache_v, norm1_g, w_in, gm_v_norm_g, gm_w_s, gm_b_s, q_norm_g,
           k_norm_g, w_out, norm2_g, w_router_group, b_router_group, w_router_expert,
           b_router_expert, w_gate, w_up, w_down):
    b, s, d = xp.shape
    db, n, _ = xs.shape
    head_of = jnp.arange(SB_WIDTH) // SB_HEAD_DIM
    bd = jnp.where(head_of[:, None] == head_of[None, :], 1.0 / SB_HEAD_DIM, 0.0).astype(BF16)
    in_params = (norm1_g[None, :], w_in.astype(BF16), gm_v_norm_g[None, :],
                 jnp.tile(q_norm_g, SB_HEADS)[None, :], jnp.tile(k_norm_g, SB_HEADS)[None, :],
                 gm_w_s, gm_b_s.T, bd)

    wr = jnp.concatenate(
        [w_router_group, jnp.transpose(w_router_expert, (1, 0, 2)).reshape(d, N_EXPERTS)], axis=1)
    wr = jnp.pad(wr, ((0, 0), (0, LANES - wr.shape[1])))
    wr_hi = wr.astype(BF16)
    wr_lo = (wr - wr_hi.astype(F32)).astype(BF16)
    br = jnp.pad(jnp.concatenate([b_router_group, b_router_expert.reshape(-1)]),
                 (0, LANES - N_EXPERT_GROUPS - N_EXPERTS))[None, :]

    moe_params = (w_out.astype(BF16), norm2_g[None, :], jnp.concatenate([wr_hi, wr_lo], axis=1), br,
                  w_gate, w_up, w_down)

    xp2 = xp.reshape(b * s, d)
    gm, kf, vf, qs, kb, vb = _mixer_in(xp2, in_params, tm=512, chunk=GM_CHUNK, stream=False, seq=s)
    shp = (b, s, SB_WIDTH)
    sb = _prompt_attn(qs.reshape(shp), kb.reshape(shp), vb.reshape(shp)).reshape(b * s, SB_WIDTH)

    xs2 = xs.reshape(db * n, d)
    gm_s, kf_s, vf_s, q_s, gv_s = _mixer_in(xs2, in_params, tm=db * n, chunk=n, stream=True)
    shs = (db, n, SB_WIDTH)
    frame_minor = lambda c: jnp.transpose(c, (0, 1, 3, 4, 2))
    sb_s = _sample_attn(q_s.reshape(shs), kf_s.reshape(shs), vf_s.reshape(shs),
                        frame_minor(cache_k), frame_minor(cache_v), layer)
    sb_s = sb_s.reshape(db * n, SB_WIDTH).astype(BF16)

    yp, ys = _out_moe((xp2, gm, sb), (xs2, gm_s, sb_s), moe_params,
                      tm_prompt=min(1024, b * s), tm_sample=db * n)
    yp, ys = yp.reshape(b, s, d), ys.reshape(db, n, d)

    heads = (SB_HEADS, SB_HEAD_DIM)
    rows = lambda a: jnp.transpose(a.reshape(b, *heads, s), (0, 3, 1, 2))
    return (yp, ys, rows(kf), rows(vf),
            kf_s.reshape(db, n, *heads), vf_s.reshape(db, n, *heads),
            gv_s.reshape(db, n, GM_GROUPS, GM_GROUP_DIM))


def kernel(x_prompt, x_sample, cache_sb_k, cache_sb_v, norm1_g, w_in, gm_v_norm_g, gm_w_s, gm_b_s, q_norm_g, k_norm_g, w_out, norm2_g, w_router_group, b_router_group, w_router_expert, b_router_expert, w_gate, w_up, w_down):
    depth = w_in.shape[0]
    yp, ys = x_prompt, x_sample
    outs = [[] for _ in range(5)]
    for l in range(depth):
        yp, ys, *rest = _layer(
            l, yp, ys, cache_sb_k, cache_sb_v, norm1_g[l], w_in[l], gm_v_norm_g[l], gm_w_s[l],
            gm_b_s[l], q_norm_g[l], k_norm_g[l], w_out[l], norm2_g[l], w_router_group[l],
            b_router_group[l], w_router_expert[l], b_router_expert[l], w_gate[l], w_up[l],
            w_down[l])
        for acc, r in zip(outs, rest):
            acc.append(r)
    return (yp, ys) + tuple(jnp.stack(o, axis=0) for o in outs)
```

```python
import functools

import jax
import jax.numpy as jnp
from jax import lax
from jax.experimental import pallas as pl
from jax.experimental.pallas import tpu as pltpu

D_MODEL = 1024
GM_WIDTH = 512
GM_GROUPS = 4
GM_GROUP_DIM = 128
GM_CHUNK = 128
SB_WIDTH = 512
SB_HEAD_DIM = 64
SB_HEADS = 8
N_EXPERT_GROUPS = 4
EXPERTS_PER_GROUP = 4
N_EXPERTS = 16
EXPERT_FF = 256
GROUP_FF = EXPERTS_PER_GROUP * EXPERT_FF
EPS = 1e-6

LANES = 128
HEAD_PAIRS = SB_WIDTH // LANES
SB_BLOCK = 128
SB_WIDE = 2 * SB_BLOCK
CACHE_BLOCK = 256
MOE_SUB = 256
MOE_CAP = 96
MOE_SLOTS = N_EXPERT_GROUPS * MOE_CAP
MOE_SUPER = 8
SLOT_LANE = 0
VMEM_LIMIT = 56 * 1024 * 1024

LOG2_E = 1.4426950408889634
Q_SCALE = SB_HEAD_DIM ** -0.5 * LOG2_E
SB_UNDERFLOW_BITS = 160.0

F32 = jnp.float32
BF16 = jnp.bfloat16
_NT = (((1,), (1,)), ((), ()))


def _gelu_tanh(x):
    return 0.5 * x * (1.0 + jnp.tanh(0.7978845608028654 * (x + 0.044715 * (x * x * x))))


def _softplus2(z):
    return jnp.maximum(z, 0.0) + jnp.log2(1.0 + jnp.exp2(-jnp.abs(z)))


def _suffix_matrix(n):
    r = lax.broadcasted_iota(jnp.int32, (n, n), 0)
    c = lax.broadcasted_iota(jnp.int32, (n, n), 1)
    return jnp.where(r > c, 1.0, 0.0).astype(BF16)


def _mixer_in_body(x_ref, g1_ref, w_in_ref, gvg_ref, qg_ref, kg_ref, ws_ref, bst_ref, bd_ref,
                   gm_ref, kf_ref, vf_ref, *rest, chunk, stream):
    x = x_ref[...]
    tm = x.shape[0]
    xn = x * lax.rsqrt(jnp.mean(x * x, axis=-1, keepdims=True) + EPS) * g1_ref[...]
    proj = jnp.dot(xn.astype(BF16), w_in_ref[...], preferred_element_type=F32)

    gu = _gelu_tanh(proj[:, :GM_WIDTH])
    gvr = _gelu_tanh(proj[:, GM_WIDTH:2 * GM_WIDTH])
    gv = gvr * lax.rsqrt(jnp.mean(gvr * gvr, axis=-1, keepdims=True) + EPS) * gvg_ref[...]

    o = 2 * GM_WIDTH
    q = proj[:, o:o + SB_WIDTH]
    k = proj[:, o + SB_WIDTH:o + 2 * SB_WIDTH]
    v = proj[:, o + 2 * SB_WIDTH:]
    bd = bd_ref[...]
    qms = jnp.dot((q * q).astype(BF16), bd, preferred_element_type=F32)
    kms = jnp.dot((k * k).astype(BF16), bd, preferred_element_type=F32)
    qs = q * lax.rsqrt(qms + EPS) * qg_ref[...] * Q_SCALE
    kn = k * lax.rsqrt(kms + EPS) * kg_ref[...]
    if stream:
        kf_ref[...] = kn
        vf_ref[...] = v
        rest[0][...] = qs
        rest[1][...] = gv
    else:
        kf_ref[...] = kn.T
        vf_ref[...] = v.T
        qs_ref, kb_ref, vb_ref = rest
        qs_ref[...] = qs.astype(BF16)
        kb_ref[...] = kn.astype(BF16)
        vb_ref[...] = v.astype(BF16)

    r = lax.broadcasted_iota(jnp.int32, (chunk, chunk), 0)
    c = lax.broadcasted_iota(jnp.int32, (chunk, chunk), 1)
    gvb = gv.astype(BF16)
    chunks = [slice(ci * chunk, (ci + 1) * chunk) for ci in range(tm // chunk)]
    for g in range(GM_GROUPS):
        w = jnp.where(r >= c, ws_ref[g, :chunk, :chunk], 0.0).astype(BF16)
        b = bst_ref[:chunk, g:g + 1]
        cols = slice(g * GM_GROUP_DIM, (g + 1) * GM_GROUP_DIM)
        mixed = jnp.dot(w, jnp.concatenate([gvb[rows, cols] for rows in chunks], axis=1),
                        preferred_element_type=F32)
        for ci, rows in enumerate(chunks):
            part = mixed[:, ci * GM_GROUP_DIM:(ci + 1) * GM_GROUP_DIM] + b
            gm_ref[rows, cols] = (gu[rows, cols] * part).astype(BF16)


def _mixer_in(x2d, params, *, tm, chunk, stream, seq=None):
    n = x2d.shape[0]
    assert n % tm == 0 and tm % chunk == 0 and chunk <= GM_CHUNK, (n, tm, chunk)
    full = lambda a: pl.BlockSpec(a.shape, lambda i: (0,) * a.ndim)
    spec = lambda s: pl.BlockSpec((tm,) + s.shape[1:], lambda i: (i,) + (0,) * (len(s.shape) - 1))
    gm = jax.ShapeDtypeStruct((n, GM_WIDTH), BF16)
    if stream:
        rows = jax.ShapeDtypeStruct((n, SB_WIDTH), F32)
        out_shape = [gm, rows, rows,
                     rows,
                     jax.ShapeDtypeStruct((n, GM_WIDTH), F32)]
        out_specs = [spec(s) for s in out_shape]
    else:
        per_seq = seq // tm
        rows_t = jax.ShapeDtypeStruct((n // seq, SB_WIDTH, seq), F32)
        spec_t = pl.BlockSpec((None, SB_WIDTH, tm), lambda i: (i // per_seq, 0, i % per_seq))
        dense = jax.ShapeDtypeStruct((n, SB_WIDTH), BF16)
        out_shape = [gm, rows_t, rows_t, dense, dense, dense]
        out_specs = [spec(gm), spec_t, spec_t] + [spec(dense)] * 3
    return pl.pallas_call(
        functools.partial(_mixer_in_body, chunk=chunk, stream=stream),
        grid=(n // tm,),
        in_specs=[pl.BlockSpec((tm, D_MODEL), lambda i: (i, 0))] + [full(a) for a in params],
        out_specs=out_specs,
        out_shape=out_shape,
        compiler_params=pltpu.CompilerParams(
            dimension_semantics=("parallel",), vmem_limit_bytes=VMEM_LIMIT),
        name="mixer_in_stream" if stream else "mixer_in",
    )(x2d, *params)


def _prompt_attn_body(q_ref, k_ref, v_ref, sfx_ref, sfx_wide_ref, o_ref, qst_ref, c_ref, acc_ref):
    i = pl.program_id(1)
    lane = lax.broadcasted_iota(jnp.int32, (SB_BLOCK, LANES), 1)
    first = lane < SB_HEAD_DIM
    for hp in range(HEAD_PAIRS):
        q = q_ref[:, hp * LANES:(hp + 1) * LANES]
        zero = jnp.zeros_like(q)
        qst_ref[hp, :SB_BLOCK, :] = jnp.where(first, q, zero)
        qst_ref[hp, SB_BLOCK:, :] = jnp.where(first, zero, q)
    m = 2 * SB_BLOCK
    suffix = {SB_BLOCK: sfx_ref, SB_WIDE: sfx_wide_ref}

    def sweep(j, keys, causal, fresh):
        rows = pl.ds(pl.multiple_of(j * SB_BLOCK, SB_BLOCK), keys)
        cols = lambda hp: slice(hp * LANES, (hp + 1) * LANES)
        z = jnp.concatenate(
            [lax.dot_general(qst_ref[hp], k_ref[rows, cols(hp)], _NT, preferred_element_type=F32)
             for hp in range(HEAD_PAIRS)], axis=0)
        sp = _softplus2(z)
        if causal is not None:
            sp = jnp.where(causal, sp, 0.0)
        c0 = jnp.zeros((HEAD_PAIRS * m, 1), F32) if fresh else c_ref[...]
        later = jnp.dot(sp.astype(BF16), suffix[keys][...], preferred_element_type=F32) + c0
        w = jnp.exp2(z - sp - later)
        if causal is not None:
            w = jnp.where(causal, w, 0.0)
        wb = w.astype(BF16)
        pv = jnp.concatenate(
            [jnp.dot(wb[hp * m:(hp + 1) * m, :], v_ref[rows, cols(hp)], preferred_element_type=F32)
             for hp in range(HEAD_PAIRS)], axis=0)
        acc_ref[...] = pv if fresh else acc_ref[...] + pv
        c_ref[...] = c0 + jnp.sum(sp, axis=-1, keepdims=True)

    j0 = i
    t = lax.broadcasted_iota(jnp.int32, (HEAD_PAIRS * m, SB_BLOCK), 0) % SB_BLOCK
    s = lax.broadcasted_iota(jnp.int32, (HEAD_PAIRS * m, SB_BLOCK), 1)
    sweep(i, SB_BLOCK, s < t, True)

    def least_carry():
        return jnp.min(c_ref[...])

    per_wide = SB_WIDE // SB_BLOCK
    n_wide = j0 // per_wide

    def more(state):
        jj, carry = state
        return (jj < n_wide) & (carry < SB_UNDERFLOW_BITS)

    def body(state):
        jj, _ = state
        sweep(j0 - per_wide * (jj + 1), SB_WIDE, None, False)
        return jj + 1, least_carry()

    done, carry = lax.while_loop(more, body, (jnp.int32(0), jnp.float32(0.0)))

    @pl.when((done == n_wide) & (j0 % per_wide == 1) & (carry < SB_UNDERFLOW_BITS))
    def _():
        sweep(0, SB_BLOCK, None, False)

    for hp in range(HEAD_PAIRS):
        lo = hp * m
        o_ref[:, hp * LANES:(hp + 1) * LANES] = jnp.where(
            first, acc_ref[lo:lo + SB_BLOCK, :], acc_ref[lo + SB_BLOCK:lo + m, :]).astype(o_ref.dtype)


def _prompt_attn(qs, kb, vb):
    b, s, _ = qs.shape
    assert s % SB_BLOCK == 0, s
    sfx = (_suffix_matrix(SB_BLOCK), _suffix_matrix(SB_WIDE))
    q_spec = pl.BlockSpec((None, SB_BLOCK, SB_WIDTH), lambda bi, i: (bi, i, 0))
    kv_spec = pl.BlockSpec((None, s, SB_WIDTH), lambda bi, i: (bi, 0, 0))
    m = 2 * SB_BLOCK
    return pl.pallas_call(
        _prompt_attn_body,
        grid=(b, s // SB_BLOCK),
        in_specs=[q_spec, kv_spec, kv_spec] + [pl.BlockSpec(a.shape, lambda bi, i: (0, 0)) for a in sfx],
        out_specs=q_spec,
        out_shape=jax.ShapeDtypeStruct((b, s, SB_WIDTH), BF16),
        scratch_shapes=[pltpu.VMEM((HEAD_PAIRS, m, LANES), BF16),
                        pltpu.VMEM((HEAD_PAIRS * m, 1), F32),
                        pltpu.VMEM((HEAD_PAIRS * m, LANES), F32)],
        compiler_params=pltpu.CompilerParams(
            dimension_semantics=("parallel", "arbitrary"), vmem_limit_bytes=VMEM_LIMIT),
        name="prompt_attn",
    )(qs, kb, vb, *sfx)


def _sample_attn_body(q_ref, kn_ref, vn_ref, ck_hbm, cv_hbm, o_ref, kbuf, vbuf, sems, c_ref, acc_ref,
                      *, layer):
    b = pl.program_id(0)
    n = q_ref.shape[0]
    m = SB_HEADS * n
    nblk = ck_hbm.shape[-1] // CACHE_BLOCK

    def copies(stream, blk, slot):
        cols = pl.ds(blk * CACHE_BLOCK, CACHE_BLOCK)
        return (pltpu.make_async_copy(ck_hbm.at[layer, stream, :, :, cols], kbuf.at[slot], sems.at[slot, 0]),
                pltpu.make_async_copy(cv_hbm.at[layer, stream, :, :, cols], vbuf.at[slot], sems.at[slot, 1]))

    def start(stream, blk, slot):
        for cp in copies(stream, blk, slot):
            cp.start()

    def wait(stream, blk, slot):
        for cp in copies(stream, blk, slot):
            cp.wait()

    def slot_of(jj):
        return 2 + jj % 2

    @pl.when(b == 0)
    def _():
        start(0, nblk - 1, 0)

    @pl.when(b + 1 < pl.num_programs(0))
    def _():
        start(b + 1, nblk - 1, (b + 1) % 2)

    head = lambda a, h: a[:, h * SB_HEAD_DIM:(h + 1) * SB_HEAD_DIM]
    q_all = q_ref[...].astype(BF16)
    q = [head(q_all, h) for h in range(SB_HEADS)]

    def sweep(slot, newest, suffix, causal, fresh):
        def logits(h):
            z = jnp.dot(q[h], kbuf[slot, h].astype(BF16), preferred_element_type=F32)
            if newest is None:
                return z
            return jnp.concatenate(
                [z, lax.dot_general(q[h], newest[0][h], _NT, preferred_element_type=F32)], axis=1)

        z = jnp.concatenate([logits(h) for h in range(SB_HEADS)], axis=0)
        sp = _softplus2(z)
        if causal is not None:
            sp = jnp.where(causal, sp, 0.0)
        c0 = jnp.zeros((m, 1), F32) if fresh else c_ref[...]
        later = jnp.dot(sp.astype(BF16), suffix, preferred_element_type=F32) + c0
        w = jnp.exp2(z - sp - later)
        if causal is not None:
            w = jnp.where(causal, w, 0.0)
        wb = w.astype(BF16)
        pvs = []
        for h in range(SB_HEADS):
            wh = wb[h * n:(h + 1) * n, :]
            pv = lax.dot_general(wh[:, :CACHE_BLOCK], vbuf[slot, h].astype(BF16), _NT,
                                 preferred_element_type=F32)
            if newest is not None:
                pv = pv + jnp.dot(wh[:, CACHE_BLOCK:], newest[1][h], preferred_element_type=F32)
            pvs.append(pv)
        pv = jnp.stack(pvs)
        acc_ref[...] = pv if fresh else acc_ref[...] + pv
        c_ref[...] = c0 + jnp.sum(sp, axis=-1, keepdims=True)

    pad = jnp.zeros((LANES - n, SB_HEAD_DIM), BF16)
    kn_all, vn_all = kn_ref[...].astype(BF16), vn_ref[...].astype(BF16)
    newest = ([jnp.concatenate([head(kn_all, h), pad], axis=0) for h in range(SB_HEADS)],
              [jnp.concatenate([head(vn_all, h), pad], axis=0) for h in range(SB_HEADS)])
    t = lax.broadcasted_iota(jnp.int32, (m, CACHE_BLOCK + LANES), 0) % n
    s = lax.broadcasted_iota(jnp.int32, (m, CACHE_BLOCK + LANES), 1) - CACHE_BLOCK
    wait(b, nblk - 1, b % 2)
    sweep(b % 2, newest, _suffix_matrix(CACHE_BLOCK + LANES), s < t, True)

    suffix = _suffix_matrix(CACHE_BLOCK)

    def more(state):
        jj, carry = state
        return (jj < nblk) & (carry < SB_UNDERFLOW_BITS)

    def body(state):
        jj, _ = state
        blk = nblk - 1 - jj
        slot = slot_of(jj)
        wait(b, blk, slot)

        @pl.when(blk > 0)
        def _():
            start(b, blk - 1, slot_of(jj + 1))

        sweep(slot, None, suffix, None, False)
        return jj + 1, jnp.min(c_ref[...])

    state = (jnp.int32(1), jnp.min(c_ref[...]))

    if nblk > 1:
        @pl.when(more(state))
        def _():
            start(b, nblk - 2, slot_of(1))

    done, _ = lax.while_loop(more, body, state)

    @pl.when((done > 1) & (done < nblk))
    def _():
        wait(b, nblk - 1 - done, slot_of(done))

    for h in range(SB_HEADS):
        o_ref[:, h, :] = acc_ref[h]


def _sample_attn(qs, kn, vn, cache_k, cache_v, layer):
    db, n = qs.shape[:2]
    past = cache_k.shape[-1]
    assert n <= LANES and past >= CACHE_BLOCK and past % CACHE_BLOCK == 0, (n, past)
    m = SB_HEADS * n
    new_spec = pl.BlockSpec((None, n, SB_WIDTH), lambda b: (b, 0, 0))
    hbm = pl.BlockSpec(memory_space=pl.ANY)
    buf = pltpu.VMEM((4, SB_HEADS, SB_HEAD_DIM, CACHE_BLOCK), F32)
    return pl.pallas_call(
        functools.partial(_sample_attn_body, layer=layer),
        grid=(db,),
        in_specs=[new_spec, new_spec, new_spec, hbm, hbm],
        out_specs=pl.BlockSpec((None, n, SB_HEADS, SB_HEAD_DIM), lambda b: (b, 0, 0, 0)),
        out_shape=jax.ShapeDtypeStruct((db, n, SB_HEADS, SB_HEAD_DIM), F32),
        scratch_shapes=[buf, buf, pltpu.SemaphoreType.DMA((4, 2)),
                        pltpu.VMEM((m, 1), F32),
                        pltpu.VMEM((SB_HEADS, n, SB_HEAD_DIM), F32)],
        compiler_params=pltpu.CompilerParams(
            dimension_semantics=("arbitrary",), vmem_limit_bytes=VMEM_LIMIT),
        name="sample_attn",
    )(qs, kn, vn, cache_k, cache_v)


def _route(logits):
    lane = lax.broadcasted_iota(jnp.int32, logits.shape, 1).astype(F32)
    neg = jnp.float32(-jnp.inf)
    big = jnp.float32(1 << 20)
    rmax = lambda a: jnp.max(a, axis=-1, keepdims=True)
    rmin = lambda a: jnp.min(a, axis=-1, keepdims=True)
    rsum = lambda a: jnp.sum(a, axis=-1, keepdims=True)

    gmask = lane < N_EXPERT_GROUPS
    gl = jnp.where(gmask, logits, neg)
    gmax = rmax(gl)
    g_sel = rmin(jnp.where(gl == gmax, lane, big))
    p_sel = 1.0 / rsum(jnp.where(gmask, jnp.exp(gl - gmax), 0.0))

    e_lo = N_EXPERT_GROUPS + g_sel * EXPERTS_PER_GROUP
    emask = (lane >= e_lo) & (lane < e_lo + EXPERTS_PER_GROUP)
    el = jnp.where(emask, logits, neg)
    emax = rmax(el)
    ex = jnp.where(emask, jnp.exp(el - emax), 0.0)
    p_e = ex / rsum(ex)
    pm = jnp.where(emask, p_e, -1.0)
    v1 = rmax(pm)
    i1 = rmin(jnp.where(pm == v1, lane, big))
    pm2 = jnp.where(lane == i1, -1.0, pm)
    v2 = rmax(pm2)
    i2 = rmin(jnp.where(pm2 == v2, lane, big))
    tot = v1 + v2
    gate1 = p_sel * v1 / tot
    gate2 = p_sel * v2 / tot
    comb = jnp.where(lane == i1, gate1, 0.0) + jnp.where(lane == i2, gate2, 0.0)
    gates = jnp.where(lane == i1 - e_lo, gate1, 0.0) + jnp.where(lane == i2 - e_lo, gate2, 0.0)
    return comb, g_sel, gates


def _rmsnorm_rows(h, g):
    return h * lax.rsqrt(jnp.mean(h * h, axis=-1, keepdims=True) + EPS) * g


def _slot_matrix(slot):
    lane = lax.broadcasted_iota(jnp.int32, (slot.shape[0], MOE_SLOTS), 1)
    return jnp.where(lane == slot.astype(jnp.int32), 1.0, 0.0)


def _post_attn_body(x_ref, gm_ref, sb_ref, wo_ref, g2_ref, wr_ref, br_ref,
                    h_ref, route_ref, bucket_ref, gate_ref, cnt_ref):
    h = (x_ref[...]
         + jnp.dot(gm_ref[...], wo_ref[:GM_WIDTH, :], preferred_element_type=F32)
         + jnp.dot(sb_ref[...], wo_ref[GM_WIDTH:, :], preferred_element_type=F32))
    h_ref[...] = h
    hn = _rmsnorm_rows(h, g2_ref[...])
    hi = hn.astype(BF16)
    lo = (hn - hi.astype(F32)).astype(BF16)
    both = jnp.dot(hi, wr_ref[...], preferred_element_type=F32)
    logits = (both[:, :LANES] + jnp.dot(lo, wr_ref[:, :LANES], preferred_element_type=F32)
              + both[:, LANES:]) + br_ref[...]
    comb, g_sel, gates = _route(logits)

    subs = [slice(st * MOE_SUB, (st + 1) * MOE_SUB) for st in range(h.shape[0] // MOE_SUB)]
    lane = lax.broadcasted_iota(jnp.int32, comb.shape, 1).astype(F32)
    r = lax.broadcasted_iota(jnp.int32, (MOE_SUB, MOE_SUB), 0)
    c = lax.broadcasted_iota(jnp.int32, (MOE_SUB, MOE_SUB), 1)
    earlier = jnp.where(r > c, 1.0, 0.0).astype(BF16)
    onehot = jnp.where(lane == g_sel, 1.0, 0.0)
    onehot_b = onehot.astype(BF16)
    before = jnp.concatenate(
        [jnp.dot(earlier, onehot_b[rows], preferred_element_type=F32) for rows in subs], axis=0)
    rank = jnp.sum(onehot * before, axis=-1, keepdims=True)
    slot = jnp.where(rank < MOE_CAP, g_sel * MOE_CAP + rank, -1.0)
    route_ref[...] = jnp.where(lane == SLOT_LANE, slot, comb)
    g1 = gates.astype(BF16)
    g2 = (gates - g1.astype(F32)).astype(BF16)
    g3 = (gates - g1.astype(F32) - g2.astype(F32)).astype(BF16)
    g12 = jnp.concatenate([g1, g2], axis=1)
    for st, rows in enumerate(subs):
        place = _slot_matrix(slot[rows]).T.astype(BF16)
        bucket_ref[st] = jnp.dot(place, hi[rows], preferred_element_type=F32).astype(BF16)
        placed = jnp.dot(place, g12[rows], preferred_element_type=F32)
        gate_ref[st] = (placed[:, :LANES] + placed[:, LANES:]
                        + jnp.dot(place, g3[rows], preferred_element_type=F32))
        cnt_ref[st] = jnp.broadcast_to(jnp.sum(onehot[rows], axis=0, keepdims=True), cnt_ref.shape[1:])


def _moe_ffn_body(x_ref, gate_ref, xs_ref, gates_ref, wg_ref, wu_ref, wd_ref, y_ref, ys_ref,
                  wgb_ref, wub_ref, wdb_ref, *, prompt_steps):
    s = pl.program_id(1)

    @pl.when(s == 0)
    def _():
        wgb_ref[...] = wg_ref[...].astype(BF16)
        wub_ref[...] = wu_ref[...].astype(BF16)
        wdb_ref[...] = wd_ref[...].astype(BF16)

    def ffn(ins, outs):
        x = jnp.concatenate([x_ref[...].reshape(-1, D_MODEL) for x_ref, _ in ins], axis=0)
        gates = jnp.concatenate([g_ref[...].reshape(-1, LANES) for _, g_ref in ins], axis=0)
        parts = []
        for e in range(EXPERTS_PER_GROUP):
            hg = jnp.dot(x, wgb_ref[e], preferred_element_type=F32)
            hu = jnp.dot(x, wub_ref[e], preferred_element_type=F32)
            parts.append((hg / (1.0 + jnp.exp(-hg)) * hu * gates[:, e:e + 1]).astype(BF16))
        a = jnp.concatenate(parts, axis=-1)
        y = jnp.dot(a, wdb_ref[...].reshape(GROUP_FF, D_MODEL), preferred_element_type=F32)
        row = 0
        for out_ref in outs:
            rows = out_ref.shape[0] * out_ref.shape[1] * out_ref.shape[2]
            out_ref[...] = y[row:row + rows].reshape(out_ref.shape)
            row += rows

    @pl.when(s < prompt_steps - 1)
    def _():
        ffn([(x_ref, gate_ref)], [y_ref])

    @pl.when(s == prompt_steps - 1)
    def _():
        ffn([(x_ref, gate_ref), (xs_ref, gates_ref)], [y_ref, ys_ref])


def _moe_combine_body(h_ref, route_ref, ys_ref, o_ref):
    for st in range(h_ref.shape[0] // MOE_SUB):
        rows = slice(st * MOE_SUB, (st + 1) * MOE_SUB)
        pick = _slot_matrix(route_ref[rows, SLOT_LANE:SLOT_LANE + 1]).astype(BF16)
        ys = ys_ref[st]
        hi = ys.astype(BF16)
        lo = (ys - hi.astype(F32)).astype(BF16)
        o_ref[rows, :] = (h_ref[rows, :] + jnp.dot(pick, hi, preferred_element_type=F32)
                          + jnp.dot(pick, lo, preferred_element_type=F32))


def _moe_dense_body(h_ref, route_ref, g2_ref, wg_ref, wu_ref, wd_ref, o_ref, hn_ref):
    g = pl.program_id(1)

    @pl.when(g == 0)
    def _():
        h = h_ref[...]
        o_ref[...] = h
        hn_ref[...] = _rmsnorm_rows(h, g2_ref[...]).astype(BF16)

    hn = hn_ref[...]
    comb = route_ref[...]
    lane = lax.broadcasted_iota(jnp.int32, comb.shape, 1)
    parts = []
    for e in range(EXPERTS_PER_GROUP):
        gate = jnp.sum(jnp.where(lane == N_EXPERT_GROUPS + g * EXPERTS_PER_GROUP + e, comb, 0.0),
                       axis=-1, keepdims=True)
        hg = jnp.dot(hn, wg_ref[e], preferred_element_type=F32)
        hu = jnp.dot(hn, wu_ref[e], preferred_element_type=F32)
        parts.append((hg / (1.0 + jnp.exp(-hg)) * hu * gate).astype(BF16))
    a = jnp.concatenate(parts, axis=-1)
    o_ref[...] += jnp.dot(a, wd_ref[...].reshape(GROUP_FF, D_MODEL), preferred_element_type=F32)


def _group_spec(a, index):
    return pl.BlockSpec((EXPERTS_PER_GROUP,) + a.shape[1:], index)


def _post_attn(x2d, gm, sb, params, tm):
    n = x2d.shape[0]
    wo, g2, wr, br = params
    assert n % tm == 0 and tm % MOE_SUB == 0, (n, tm)
    n_sub, sub = n // MOE_SUB, tm // MOE_SUB
    row = lambda w: pl.BlockSpec((tm, w), lambda i: (i, 0))
    full = lambda a: pl.BlockSpec(a.shape, lambda i: (0,) * a.ndim)
    per_sub = lambda rows, w: pl.BlockSpec((sub, rows, w), lambda i: (i, 0, 0))
    return pl.pallas_call(
        _post_attn_body,
        grid=(n // tm,),
        in_specs=[row(D_MODEL), row(GM_WIDTH), row(SB_WIDTH),
                  full(wo), full(g2), full(wr), full(br)],
        out_specs=[row(D_MODEL), row(LANES), per_sub(MOE_SLOTS, D_MODEL),
                   per_sub(MOE_SLOTS, LANES), per_sub(8, LANES)],
        out_shape=[jax.ShapeDtypeStruct((n, D_MODEL), F32),
                   jax.ShapeDtypeStruct((n, LANES), F32),
                   jax.ShapeDtypeStruct((n_sub, MOE_SLOTS, D_MODEL), BF16),
                   jax.ShapeDtypeStruct((n_sub, MOE_SLOTS, LANES), F32),
                   jax.ShapeDtypeStruct((n_sub, 8, LANES), F32)],
        compiler_params=pltpu.CompilerParams(
            dimension_semantics=("parallel",), vmem_limit_bytes=VMEM_LIMIT),
        name="post_attn",
    )(x2d, gm, sb, wo, g2, wr, br)


def _out_moe(prompt, sample, params, *, tm_prompt, tm_sample):
    wo, g2, wr, br, wg, wu, wd = params
    routed = [_post_attn(*grp, (wo, g2, wr, br), tm) for grp, tm in
              ((prompt, tm_prompt), (sample, tm_sample))]
    tms = (tm_prompt, tm_sample)
    params_2d = pltpu.CompilerParams(
        dimension_semantics=("parallel", "arbitrary"), vmem_limit_bytes=VMEM_LIMIT)

    def sparse():
        (_, _, bk_p, gt_p, _), (_, _, bk_s, gt_s, _) = routed
        n_sub, n_sub_s = bk_p.shape[0], bk_s.shape[0]
        sup = min(MOE_SUPER, n_sub)
        assert n_sub % sup == 0, (n_sub, sup)
        steps = n_sub // sup
        by_group = lambda a: a.reshape(a.shape[0], N_EXPERT_GROUPS, MOE_CAP, a.shape[-1])
        bucket = lambda w: pl.BlockSpec((sup, 1, MOE_CAP, w), lambda g, s: (s, g, 0, 0))
        bucket_s = lambda w: pl.BlockSpec((n_sub_s, 1, MOE_CAP, w), lambda g, s: (0, g, 0, 0))
        ys = pl.pallas_call(
            functools.partial(_moe_ffn_body, prompt_steps=steps),
            grid=(N_EXPERT_GROUPS, steps),
            in_specs=[bucket(D_MODEL), bucket(LANES), bucket_s(D_MODEL), bucket_s(LANES)]
                     + [_group_spec(w, lambda g, s: (g, 0, 0)) for w in (wg, wu, wd)],
            out_specs=[bucket(D_MODEL), bucket_s(D_MODEL)],
            out_shape=[jax.ShapeDtypeStruct((n, N_EXPERT_GROUPS, MOE_CAP, D_MODEL), F32)
                       for n in (n_sub, n_sub_s)],
            scratch_shapes=[pltpu.VMEM((EXPERTS_PER_GROUP,) + w.shape[1:], BF16) for w in (wg, wu, wd)],
            compiler_params=params_2d,
            name="moe_ffn",
        )(by_group(bk_p), by_group(gt_p), by_group(bk_s), by_group(gt_s), wg, wu, wd)

        def combine(h, route, y, tm):
            n, sub = h.shape[0], tm // MOE_SUB
            row = lambda w: pl.BlockSpec((tm, w), lambda i: (i, 0))
            return pl.pallas_call(
                _moe_combine_body,
                grid=(n // tm,),
                in_specs=[row(D_MODEL), row(LANES),
                          pl.BlockSpec((sub, MOE_SLOTS, D_MODEL), lambda i: (i, 0, 0))],
                out_specs=row(D_MODEL),
                out_shape=jax.ShapeDtypeStruct((n, D_MODEL), F32),
                compiler_params=pltpu.CompilerParams(
                    dimension_semantics=("parallel",), vmem_limit_bytes=VMEM_LIMIT),
                name="moe_combine",
            )(h, route, y.reshape(-1, MOE_SLOTS, D_MODEL))

        return tuple(combine(r[0], r[1], y, tm) for r, y, tm in zip(routed, ys, tms))

    def dense():
        wgb, wub, wdb = (w.astype(BF16) for w in (wg, wu, wd))

        def all_experts(h, route, tm):
            row2 = lambda w: pl.BlockSpec((tm, w), lambda i, g: (i, 0))
            return pl.pallas_call(
                _moe_dense_body,
                grid=(h.shape[0] // tm, N_EXPERT_GROUPS),
                in_specs=[row2(D_MODEL), row2(LANES), pl.BlockSpec(g2.shape, lambda i, g: (0, 0))]
                         + [_group_spec(w, lambda i, g: (g, 0, 0)) for w in (wg, wu, wd)],
                out_specs=row2(D_MODEL),
                out_shape=jax.ShapeDtypeStruct(h.shape, F32),
                scratch_shapes=[pltpu.VMEM((tm, D_MODEL), BF16)],
                compiler_params=params_2d,
                name="moe_dense",
            )(h, route, g2, wgb, wub, wdb)

        return tuple(all_experts(r[0], r[1], tm) for r, tm in zip(routed, tms))

    most = jnp.maximum(jnp.max(routed[0][4]), jnp.max(routed[1][4]))
    return lax.cond(most <= MOE_CAP, sparse, dense)


def _layer(layer, xp, xs, cache_k, cache_v, norm1_g, w_in, gm_v_norm_g, gm_w_s, gm_b_s, q_norm_g,
           k_norm_g, w_out, norm2_g, w_router_group, b_router_group, w_router_expert,
           b_router_expert, w_gate, w_up, w_down):
    b, s, d = xp.shape
    db, n, _ = xs.shape
    head_of = jnp.arange(SB_WIDTH) // SB_HEAD_DIM
    bd = jnp.where(head_of[:, None] == head_of[None, :], 1.0 / SB_HEAD_DIM, 0.0).astype(BF16)
    in_params = (norm1_g[None, :], w_in.astype(BF16), gm_v_norm_g[None, :],
                 jnp.tile(q_norm_g, SB_HEADS)[None, :], jnp.tile(k_norm_g, SB_HEADS)[None, :],
                 gm_w_s, gm_b_s.T, bd)

    wr = jnp.concatenate(
        [w_router_group, jnp.transpose(w_router_expert, (1, 0, 2)).reshape(d, N_EXPERTS)], axis=1)
    wr = jnp.pad(wr, ((0, 0), (0, LANES - wr.shape[1])))
    wr_hi = wr.astype(BF16)
    wr_lo = (wr - wr_hi.astype(F32)).astype(BF16)
    br = jnp.pad(jnp.concatenate([b_router_group, b_router_expert.reshape(-1)]),
                 (0, LANES - N_EXPERT_GROUPS - N_EXPERTS))[None, :]

    moe_params = (w_out.astype(BF16), norm2_g[None, :], jnp.concatenate([wr_hi, wr_lo], axis=1), br,
                  w_gate, w_up, w_down)

    xp2 = xp.reshape(b * s, d)
    gm, kf, vf, qs, kb, vb = _mixer_in(xp2, in_params, tm=min(1024, s), chunk=GM_CHUNK, stream=False,
                                       seq=s)
    shp = (b, s, SB_WIDTH)
    sb = _prompt_attn(qs.reshape(shp), kb.reshape(shp), vb.reshape(shp)).reshape(b * s, SB_WIDTH)

    xs2 = xs.reshape(db * n, d)
    gm_s, kf_s, vf_s, q_s, gv_s = _mixer_in(xs2, in_params, tm=db * n, chunk=n, stream=True)
    shs = (db, n, SB_WIDTH)
    frame_minor = lambda c: jnp.transpose(c, (0, 1, 3, 4, 2))
    sb_s = _sample_attn(q_s.reshape(shs), kf_s.reshape(shs), vf_s.reshape(shs),
                        frame_minor(cache_k), frame_minor(cache_v), layer)
    sb_s = sb_s.reshape(db * n, SB_WIDTH).astype(BF16)

    yp, ys = _out_moe((xp2, gm, sb), (xs2, gm_s, sb_s), moe_params,
                      tm_prompt=min(1024, b * s), tm_sample=db * n)
    yp, ys = yp.reshape(b, s, d), ys.reshape(db, n, d)

    heads = (SB_HEADS, SB_HEAD_DIM)
    rows = lambda a: jnp.transpose(a.reshape(b, *heads, s), (0, 3, 1, 2))
    return (yp, ys, rows(kf), rows(vf),
            kf_s.reshape(db, n, *heads), vf_s.reshape(db, n, *heads),
            gv_s.reshape(db, n, GM_GROUPS, GM_GROUP_DIM))


def kernel(x_prompt, x_sample, cache_sb_k, cache_sb_v, norm1_g, w_in, gm_v_norm_g, gm_w_s, gm_b_s, q_norm_g, k_norm_g, w_out, norm2_g, w_router_group, b_router_group, w_router_expert, b_router_expert, w_gate, w_up, w_down):
    depth = w_in.shape[0]
    yp, ys = x_prompt, x_sample
    outs = [[] for _ in range(5)]
    for l in range(depth):
        yp, ys, *rest = _layer(
            l, yp, ys, cache_sb_k, cache_sb_v, norm1_g[l], w_in[l], gm_v_norm_g[l], gm_w_s[l],
            gm_b_s[l], q_norm_g[l], k_norm_g[l], w_out[l], norm2_g[l], w_router_group[l],
            b_router_group[l], w_router_expert[l], b_router_expert[l], w_gate[l], w_up[l],
            w_down[l])
        for acc, r in zip(outs, rest):
            acc.append(r)
    return (yp, ys) + tuple(jnp.stack(o, axis=0) for o in outs)
```

```python
import functools

import jax
import jax.numpy as jnp
from jax import lax
from jax.experimental import pallas as pl
from jax.experimental.pallas import tpu as pltpu

D_MODEL = 1024
GM_WIDTH = 512
GM_GROUPS = 4
GM_GROUP_DIM = 128
GM_CHUNK = 128
SB_WIDTH = 512
SB_HEAD_DIM = 64
SB_HEADS = 8
N_EXPERT_GROUPS = 4
EXPERTS_PER_GROUP = 4
N_EXPERTS = 16
EXPERT_FF = 256
GROUP_FF = EXPERTS_PER_GROUP * EXPERT_FF
EPS = 1e-6

LANES = 128
HEAD_PAIRS = SB_WIDTH // LANES
SB_BLOCK = 128
SB_WIDE = 2 * SB_BLOCK
SB_FIRST = 3 * SB_BLOCK
CACHE_BLOCK = 256
MOE_SUB = 256
MOE_CAP = 96
MOE_SLOTS = N_EXPERT_GROUPS * MOE_CAP
MOE_SUPER = 8
SLOT_LANE = 0
VMEM_LIMIT = 56 * 1024 * 1024

LOG2_E = 1.4426950408889634
Q_SCALE = SB_HEAD_DIM ** -0.5 * LOG2_E
SB_UNDERFLOW_BITS = 160.0

F32 = jnp.float32
BF16 = jnp.bfloat16
_NT = (((1,), (1,)), ((), ()))


def _gelu_tanh(x):
    return 0.5 * x * (1.0 + jnp.tanh(0.7978845608028654 * (x + 0.044715 * (x * x * x))))


def _softplus2(z):
    return jnp.maximum(z, 0.0) + jnp.log2(1.0 + jnp.exp2(-jnp.abs(z)))


def _suffix_matrix(n):
    r = lax.broadcasted_iota(jnp.int32, (n, n), 0)
    c = lax.broadcasted_iota(jnp.int32, (n, n), 1)
    return jnp.where(r > c, 1.0, 0.0).astype(BF16)


def _mixer_in_body(x_ref, g1_ref, w_in_ref, gvg_ref, qg_ref, kg_ref, ws_ref, bst_ref, bd_ref,
                   gm_ref, kf_ref, vf_ref, *rest, chunk, stream):
    x = x_ref[...]
    tm = x.shape[0]
    xn = x * lax.rsqrt(jnp.mean(x * x, axis=-1, keepdims=True) + EPS) * g1_ref[...]
    proj = jnp.dot(xn.astype(BF16), w_in_ref[...], preferred_element_type=F32)

    gu = _gelu_tanh(proj[:, :GM_WIDTH])
    gvr = _gelu_tanh(proj[:, GM_WIDTH:2 * GM_WIDTH])
    gv = gvr * lax.rsqrt(jnp.mean(gvr * gvr, axis=-1, keepdims=True) + EPS) * gvg_ref[...]

    o = 2 * GM_WIDTH
    q = proj[:, o:o + SB_WIDTH]
    k = proj[:, o + SB_WIDTH:o + 2 * SB_WIDTH]
    v = proj[:, o + 2 * SB_WIDTH:]
    bd = bd_ref[...]
    qms = jnp.dot((q * q).astype(BF16), bd, preferred_element_type=F32)
    kms = jnp.dot((k * k).astype(BF16), bd, preferred_element_type=F32)
    qs = q * lax.rsqrt(qms + EPS) * qg_ref[...] * Q_SCALE
    kn = k * lax.rsqrt(kms + EPS) * kg_ref[...]
    if stream:
        kf_ref[...] = kn
        vf_ref[...] = v
        rest[0][...] = qs
        rest[1][...] = gv
    else:
        kf_ref[...] = kn.T
        vf_ref[...] = v.T
        qs_ref, kb_ref, vb_ref = rest
        qs_ref[...] = qs.astype(BF16)
        kb_ref[...] = kn.astype(BF16)
        vb_ref[...] = v.astype(BF16)

    r = lax.broadcasted_iota(jnp.int32, (chunk, chunk), 0)
    c = lax.broadcasted_iota(jnp.int32, (chunk, chunk), 1)
    gvb = gv.astype(BF16)
    chunks = [slice(ci * chunk, (ci + 1) * chunk) for ci in range(tm // chunk)]
    for g in range(GM_GROUPS):
        w = jnp.where(r >= c, ws_ref[g, :chunk, :chunk], 0.0).astype(BF16)
        b = bst_ref[:chunk, g:g + 1]
        cols = slice(g * GM_GROUP_DIM, (g + 1) * GM_GROUP_DIM)
        mixed = jnp.dot(w, jnp.concatenate([gvb[rows, cols] for rows in chunks], axis=1),
                        preferred_element_type=F32)
        for ci, rows in enumerate(chunks):
            part = mixed[:, ci * GM_GROUP_DIM:(ci + 1) * GM_GROUP_DIM] + b
            gm_ref[rows, cols] = (gu[rows, cols] * part).astype(BF16)


def _mixer_in(x2d, params, *, tm, chunk, stream, seq=None):
    n = x2d.shape[0]
    assert n % tm == 0 and tm % chunk == 0 and chunk <= GM_CHUNK, (n, tm, chunk)
    full = lambda a: pl.BlockSpec(a.shape, lambda i: (0,) * a.ndim)
    spec = lambda s: pl.BlockSpec((tm,) + s.shape[1:], lambda i: (i,) + (0,) * (len(s.shape) - 1))
    gm = jax.ShapeDtypeStruct((n, GM_WIDTH), BF16)
    if stream:
        rows = jax.ShapeDtypeStruct((n, SB_WIDTH), F32)
        out_shape = [gm, rows, rows,
                     rows,
                     jax.ShapeDtypeStruct((n, GM_WIDTH), F32)]
        out_specs = [spec(s) for s in out_shape]
    else:
        per_seq = seq // tm
        rows_t = jax.ShapeDtypeStruct((n // seq, SB_WIDTH, seq), F32)
        spec_t = pl.BlockSpec((None, SB_WIDTH, tm), lambda i: (i // per_seq, 0, i % per_seq))
        dense = jax.ShapeDtypeStruct((n, SB_WIDTH), BF16)
        out_shape = [gm, rows_t, rows_t, dense, dense, dense]
        out_specs = [spec(gm), spec_t, spec_t] + [spec(dense)] * 3
    return pl.pallas_call(
        functools.partial(_mixer_in_body, chunk=chunk, stream=stream),
        grid=(n // tm,),
        in_specs=[pl.BlockSpec((tm, D_MODEL), lambda i: (i, 0))] + [full(a) for a in params],
        out_specs=out_specs,
        out_shape=out_shape,
        compiler_params=pltpu.CompilerParams(
            dimension_semantics=("parallel",), vmem_limit_bytes=VMEM_LIMIT),
        name="mixer_in_stream" if stream else "mixer_in",
    )(x2d, *params)


def _prompt_attn_body(q_ref, k_ref, v_ref, sfx_ref, sfx_wide_ref, sfx_first_ref, o_ref,
                      qst_ref, c_ref, acc_ref):
    i = pl.program_id(1)
    lane = lax.broadcasted_iota(jnp.int32, (SB_BLOCK, LANES), 1)
    first = lane < SB_HEAD_DIM
    for hp in range(HEAD_PAIRS):
        q = q_ref[:, hp * LANES:(hp + 1) * LANES]
        zero = jnp.zeros_like(q)
        qst_ref[hp, :SB_BLOCK, :] = jnp.where(first, q, zero)
        qst_ref[hp, SB_BLOCK:, :] = jnp.where(first, zero, q)
    m = 2 * SB_BLOCK
    suffix = {SB_BLOCK: sfx_ref, SB_WIDE: sfx_wide_ref, SB_FIRST: sfx_first_ref}

    def sweep(j, keys, causal, fresh):
        rows = pl.ds(pl.multiple_of(j * SB_BLOCK, SB_BLOCK), keys)
        cols = lambda hp: slice(hp * LANES, (hp + 1) * LANES)
        z = jnp.concatenate(
            [lax.dot_general(qst_ref[hp], k_ref[rows, cols(hp)], _NT, preferred_element_type=F32)
             for hp in range(HEAD_PAIRS)], axis=0)
        def masked(a):
            if causal is None:
                return a
            old = keys - SB_BLOCK
            newest = jnp.where(causal, a[:, old:], 0.0)
            return newest if old == 0 else jnp.concatenate([a[:, :old], newest], axis=1)

        sp = masked(_softplus2(z))
        c0 = jnp.zeros((HEAD_PAIRS * m, 1), F32) if fresh else c_ref[...]
        later = jnp.dot(sp.astype(BF16), suffix[keys][...], preferred_element_type=F32) + c0
        w = masked(jnp.exp2(z - sp - later))
        wb = w.astype(BF16)
        pv = jnp.concatenate(
            [jnp.dot(wb[hp * m:(hp + 1) * m, :], v_ref[rows, cols(hp)], preferred_element_type=F32)
             for hp in range(HEAD_PAIRS)], axis=0)
        acc_ref[...] = pv if fresh else acc_ref[...] + pv
        c_ref[...] = c0 + jnp.sum(sp, axis=-1, keepdims=True)

    below = SB_FIRST // SB_BLOCK - 1
    t = lax.broadcasted_iota(jnp.int32, (HEAD_PAIRS * m, SB_BLOCK), 0) % SB_BLOCK
    s = lax.broadcasted_iota(jnp.int32, (HEAD_PAIRS * m, SB_BLOCK), 1)

    @pl.when(i >= below)
    def _():
        sweep(i - below, SB_FIRST, s < t, True)

    @pl.when(i < below)
    def _():
        sweep(i, SB_BLOCK, s < t, True)

    j0 = jnp.where(i >= below, i - below, i)

    def least_carry():
        return jnp.min(c_ref[...])

    per_wide = SB_WIDE // SB_BLOCK
    n_wide = j0 // per_wide

    def more(state):
        jj, carry = state
        return (jj < n_wide) & (carry < SB_UNDERFLOW_BITS)

    def body(state):
        jj, _ = state
        sweep(j0 - per_wide * (jj + 1), SB_WIDE, None, False)
        return jj + 1, least_carry()

    done, carry = lax.while_loop(more, body, (jnp.int32(0), least_carry()))

    @pl.when((done == n_wide) & (j0 % per_wide == 1) & (carry < SB_UNDERFLOW_BITS))
    def _():
        sweep(0, SB_BLOCK, None, False)

    for hp in range(HEAD_PAIRS):
        lo = hp * m
        o_ref[:, hp * LANES:(hp + 1) * LANES] = jnp.where(
            first, acc_ref[lo:lo + SB_BLOCK, :], acc_ref[lo + SB_BLOCK:lo + m, :]).astype(o_ref.dtype)


def _prompt_attn(qs, kb, vb):
    b, s, _ = qs.shape
    assert s % SB_BLOCK == 0, s
    sfx = tuple(_suffix_matrix(keys) for keys in (SB_BLOCK, SB_WIDE, SB_FIRST))
    q_spec = pl.BlockSpec((None, SB_BLOCK, SB_WIDTH), lambda bi, i: (bi, i, 0))
    kv_spec = pl.BlockSpec((None, s, SB_WIDTH), lambda bi, i: (bi, 0, 0))
    m = 2 * SB_BLOCK
    return pl.pallas_call(
        _prompt_attn_body,
        grid=(b, s // SB_BLOCK),
        in_specs=[q_spec, kv_spec, kv_spec] + [pl.BlockSpec(a.shape, lambda bi, i: (0, 0)) for a in sfx],
        out_specs=q_spec,
        out_shape=jax.ShapeDtypeStruct((b, s, SB_WIDTH), BF16),
        scratch_shapes=[pltpu.VMEM((HEAD_PAIRS, m, LANES), BF16),
                        pltpu.VMEM((HEAD_PAIRS * m, 1), F32),
                        pltpu.VMEM((HEAD_PAIRS * m, LANES), F32)],
        compiler_params=pltpu.CompilerParams(
            dimension_semantics=("parallel", "arbitrary"), vmem_limit_bytes=VMEM_LIMIT),
        name="prompt_attn",
    )(qs, kb, vb, *sfx)


def _sample_attn_body(q_ref, kn_ref, vn_ref, ck_hbm, cv_hbm, o_ref, kbuf, vbuf, sems, c_ref, acc_ref,
                      *, layer):
    b = pl.program_id(0)
    n = q_ref.shape[0]
    m = SB_HEADS * n
    nblk = ck_hbm.shape[-1] // CACHE_BLOCK

    def copies(stream, blk, slot):
        cols = pl.ds(blk * CACHE_BLOCK, CACHE_BLOCK)
        return (pltpu.make_async_copy(ck_hbm.at[layer, stream, :, :, cols], kbuf.at[slot], sems.at[slot, 0]),
                pltpu.make_async_copy(cv_hbm.at[layer, stream, :, :, cols], vbuf.at[slot], sems.at[slot, 1]))

    def start(stream, blk, slot):
        for cp in copies(stream, blk, slot):
            cp.start()

    def wait(stream, blk, slot):
        for cp in copies(stream, blk, slot):
            cp.wait()

    def slot_of(jj):
        return 2 + jj % 2

    @pl.when(b == 0)
    def _():
        start(0, nblk - 1, 0)

    @pl.when(b + 1 < pl.num_programs(0))
    def _():
        start(b + 1, nblk - 1, (b + 1) % 2)

    head = lambda a, h: a[:, h * SB_HEAD_DIM:(h + 1) * SB_HEAD_DIM]
    q_all = q_ref[...].astype(BF16)
    q = [head(q_all, h) for h in range(SB_HEADS)]

    def sweep(slot, newest, suffix, causal, fresh):
        def logits(h):
            z = jnp.dot(q[h], kbuf[slot, h].astype(BF16), preferred_element_type=F32)
            if newest is None:
                return z
            return jnp.concatenate(
                [z, lax.dot_general(q[h], newest[0][h], _NT, preferred_element_type=F32)], axis=1)

        z = jnp.concatenate([logits(h) for h in range(SB_HEADS)], axis=0)
        sp = _softplus2(z)
        if causal is not None:
            sp = jnp.where(causal, sp, 0.0)
        c0 = jnp.zeros((m, 1), F32) if fresh else c_ref[...]
        later = jnp.dot(sp.astype(BF16), suffix, preferred_element_type=F32) + c0
        w = jnp.exp2(z - sp - later)
        if causal is not None:
            w = jnp.where(causal, w, 0.0)
        wb = w.astype(BF16)
        pvs = []
        for h in range(SB_HEADS):
            wh = wb[h * n:(h + 1) * n, :]
            pv = lax.dot_general(wh[:, :CACHE_BLOCK], vbuf[slot, h].astype(BF16), _NT,
                                 preferred_element_type=F32)
            if newest is not None:
                pv = pv + jnp.dot(wh[:, CACHE_BLOCK:], newest[1][h], preferred_element_type=F32)
            pvs.append(pv)
        pv = jnp.stack(pvs)
        acc_ref[...] = pv if fresh else acc_ref[...] + pv
        c_ref[...] = c0 + jnp.sum(sp, axis=-1, keepdims=True)

    pad = jnp.zeros((LANES - n, SB_HEAD_DIM), BF16)
    kn_all, vn_all = kn_ref[...].astype(BF16), vn_ref[...].astype(BF16)
    newest = ([jnp.concatenate([head(kn_all, h), pad], axis=0) for h in range(SB_HEADS)],
              [jnp.concatenate([head(vn_all, h), pad], axis=0) for h in range(SB_HEADS)])
    t = lax.broadcasted_iota(jnp.int32, (m, CACHE_BLOCK + LANES), 0) % n
    s = lax.broadcasted_iota(jnp.int32, (m, CACHE_BLOCK + LANES), 1) - CACHE_BLOCK
    wait(b, nblk - 1, b % 2)
    sweep(b % 2, newest, _suffix_matrix(CACHE_BLOCK + LANES), s < t, True)

    suffix = _suffix_matrix(CACHE_BLOCK)

    def more(state):
        jj, carry = state
        return (jj < nblk) & (carry < SB_UNDERFLOW_BITS)

    def body(state):
        jj, _ = state
        blk = nblk - 1 - jj
        slot = slot_of(jj)
        wait(b, blk, slot)

        @pl.when(blk > 0)
        def _():
            start(b, blk - 1, slot_of(jj + 1))

        sweep(slot, None, suffix, None, False)
        return jj + 1, jnp.min(c_ref[...])

    state = (jnp.int32(1), jnp.min(c_ref[...]))

    if nblk > 1:
        @pl.when(more(state))
        def _():
            start(b, nblk - 2, slot_of(1))

    done, _ = lax.while_loop(more, body, state)

    @pl.when((done > 1) & (done < nblk))
    def _():
        wait(b, nblk - 1 - done, slot_of(done))

    for h in range(SB_HEADS):
        o_ref[:, h, :] = acc_ref[h]


def _sample_attn(qs, kn, vn, cache_k, cache_v, layer):
    db, n = qs.shape[:2]
    past = cache_k.shape[-1]
    assert n <= LANES and past >= CACHE_BLOCK and past % CACHE_BLOCK == 0, (n, past)
    m = SB_HEADS * n
    new_spec = pl.BlockSpec((None, n, SB_WIDTH), lambda b: (b, 0, 0))
    hbm = pl.BlockSpec(memory_space=pl.ANY)
    buf = pltpu.VMEM((4, SB_HEADS, SB_HEAD_DIM, CACHE_BLOCK), F32)
    return pl.pallas_call(
        functools.partial(_sample_attn_body, layer=layer),
        grid=(db,),
        in_specs=[new_spec, new_spec, new_spec, hbm, hbm],
        out_specs=pl.BlockSpec((None, n, SB_HEADS, SB_HEAD_DIM), lambda b: (b, 0, 0, 0)),
        out_shape=jax.ShapeDtypeStruct((db, n, SB_HEADS, SB_HEAD_DIM), F32),
        scratch_shapes=[buf, buf, pltpu.SemaphoreType.DMA((4, 2)),
                        pltpu.VMEM((m, 1), F32),
                        pltpu.VMEM((SB_HEADS, n, SB_HEAD_DIM), F32)],
        compiler_params=pltpu.CompilerParams(
            dimension_semantics=("arbitrary",), vmem_limit_bytes=VMEM_LIMIT),
        name="sample_attn",
    )(qs, kn, vn, cache_k, cache_v)


def _route(logits):
    lane = lax.broadcasted_iota(jnp.int32, logits.shape, 1).astype(F32)
    neg = jnp.float32(-jnp.inf)
    big = jnp.float32(1 << 20)
    rmax = lambda a: jnp.max(a, axis=-1, keepdims=True)
    rmin = lambda a: jnp.min(a, axis=-1, keepdims=True)
    rsum = lambda a: jnp.sum(a, axis=-1, keepdims=True)

    gmask = lane < N_EXPERT_GROUPS
    gl = jnp.where(gmask, logits, neg)
    gmax = rmax(gl)
    g_sel = rmin(jnp.where(gl == gmax, lane, big))
    p_sel = 1.0 / rsum(jnp.where(gmask, jnp.exp(gl - gmax), 0.0))

    e_lo = N_EXPERT_GROUPS + g_sel * EXPERTS_PER_GROUP
    emask = (lane >= e_lo) & (lane < e_lo + EXPERTS_PER_GROUP)
    el = jnp.where(emask, logits, neg)
    emax = rmax(el)
    ex = jnp.where(emask, jnp.exp(el - emax), 0.0)
    p_e = ex / rsum(ex)
    pm = jnp.where(emask, p_e, -1.0)
    v1 = rmax(pm)
    i1 = rmin(jnp.where(pm == v1, lane, big))
    pm2 = jnp.where(lane == i1, -1.0, pm)
    v2 = rmax(pm2)
    i2 = rmin(jnp.where(pm2 == v2, lane, big))
    tot = v1 + v2
    gate1 = p_sel * v1 / tot
    gate2 = p_sel * v2 / tot
    comb = jnp.where(lane == i1, gate1, 0.0) + jnp.where(lane == i2, gate2, 0.0)
    gates = jnp.where(lane == i1 - e_lo, gate1, 0.0) + jnp.where(lane == i2 - e_lo, gate2, 0.0)
    return comb, g_sel, gates


def _rmsnorm_rows(h, g):
    return h * lax.rsqrt(jnp.mean(h * h, axis=-1, keepdims=True) + EPS) * g


def _slot_matrix(slot):
    lane = lax.broadcasted_iota(jnp.int32, (slot.shape[0], MOE_SLOTS), 1)
    return jnp.where(lane == slot.astype(jnp.int32), 1.0, 0.0)


def _post_attn_body(x_ref, gm_ref, sb_ref, wo_ref, g2_ref, wr_ref, br_ref,
                    h_ref, route_ref, bucket_ref, gate_ref, cnt_ref):
    h = (x_ref[...]
         + jnp.dot(gm_ref[...], wo_ref[:GM_WIDTH, :], preferred_element_type=F32)
         + jnp.dot(sb_ref[...], wo_ref[GM_WIDTH:, :], preferred_element_type=F32))
    h_ref[...] = h
    hn = _rmsnorm_rows(h, g2_ref[...])
    hi = hn.astype(BF16)
    lo = (hn - hi.astype(F32)).astype(BF16)
    both = jnp.dot(hi, wr_ref[...], preferred_element_type=F32)
    logits = (both[:, :LANES] + jnp.dot(lo, wr_ref[:, :LANES], preferred_element_type=F32)
              + both[:, LANES:]) + br_ref[...]
    comb, g_sel, gates = _route(logits)

    subs = [slice(st * MOE_SUB, (st + 1) * MOE_SUB) for st in range(h.shape[0] // MOE_SUB)]
    lane = lax.broadcasted_iota(jnp.int32, comb.shape, 1).astype(F32)
    r = lax.broadcasted_iota(jnp.int32, (MOE_SUB, MOE_SUB), 0)
    c = lax.broadcasted_iota(jnp.int32, (MOE_SUB, MOE_SUB), 1)
    earlier = jnp.where(r > c, 1.0, 0.0).astype(BF16)
    onehot = jnp.where(lane == g_sel, 1.0, 0.0)
    onehot_b = onehot.astype(BF16)
    before = jnp.concatenate(
        [jnp.dot(earlier, onehot_b[rows], preferred_element_type=F32) for rows in subs], axis=0)
    rank = jnp.sum(onehot * before, axis=-1, keepdims=True)
    slot = jnp.where(rank < MOE_CAP, g_sel * MOE_CAP + rank, -1.0)
    route_ref[...] = jnp.where(lane == SLOT_LANE, slot, comb)
    g1 = gates.astype(BF16)
    g2 = (gates - g1.astype(F32)).astype(BF16)
    g3 = (gates - g1.astype(F32) - g2.astype(F32)).astype(BF16)
    g12 = jnp.concatenate([g1, g2], axis=1)
    for st, rows in enumerate(subs):
        place = _slot_matrix(slot[rows]).T.astype(BF16)
        bucket_ref[st] = jnp.dot(place, hi[rows], preferred_element_type=F32).astype(BF16)
        placed = jnp.dot(place, g12[rows], preferred_element_type=F32)
        gate_ref[st] = (placed[:, :LANES] + placed[:, LANES:]
                        + jnp.dot(place, g3[rows], preferred_element_type=F32))
        cnt_ref[st] = jnp.broadcast_to(jnp.sum(onehot[rows], axis=0, keepdims=True), cnt_ref.shape[1:])


def _moe_ffn_body(x_ref, gate_ref, xs_ref, gates_ref, wg_ref, wu_ref, wd_ref, y_ref, ys_ref,
                  wgb_ref, wub_ref, wdb_ref, *, prompt_steps):
    s = pl.program_id(1)

    @pl.when(s == 0)
    def _():
        wgb_ref[...] = wg_ref[...].astype(BF16)
        wub_ref[...] = wu_ref[...].astype(BF16)
        wdb_ref[...] = wd_ref[...].astype(BF16)

    def ffn(ins, outs):
        x = jnp.concatenate([x_ref[...].reshape(-1, D_MODEL) for x_ref, _ in ins], axis=0)
        gates = jnp.concatenate([g_ref[...].reshape(-1, LANES) for _, g_ref in ins], axis=0)
        parts = []
        for e in range(EXPERTS_PER_GROUP):
            hg = jnp.dot(x, wgb_ref[e], preferred_element_type=F32)
            hu = jnp.dot(x, wub_ref[e], preferred_element_type=F32)
            parts.append((hg / (1.0 + jnp.exp(-hg)) * hu * gates[:, e:e + 1]).astype(BF16))
        a = jnp.concatenate(parts, axis=-1)
        y = jnp.dot(a, wdb_ref[...].reshape(GROUP_FF, D_MODEL), preferred_element_type=F32)
        row = 0
        for out_ref in outs:
            rows = out_ref.shape[0] * out_ref.shape[1] * out_ref.shape[2]
            out_ref[...] = y[row:row + rows].reshape(out_ref.shape)
            row += rows

    @pl.when(s < prompt_steps - 1)
    def _():
        ffn([(x_ref, gate_ref)], [y_ref])

    @pl.when(s == prompt_steps - 1)
    def _():
        ffn([(x_ref, gate_ref), (xs_ref, gates_ref)], [y_ref, ys_ref])


def _moe_combine_body(h_ref, route_ref, ys_ref, o_ref):
    for st in range(h_ref.shape[0] // MOE_SUB):
        rows = slice(st * MOE_SUB, (st + 1) * MOE_SUB)
        pick = _slot_matrix(route_ref[rows, SLOT_LANE:SLOT_LANE + 1]).astype(BF16)
        ys = ys_ref[st]
        hi = ys.astype(BF16)
        lo = (ys - hi.astype(F32)).astype(BF16)
        o_ref[rows, :] = (h_ref[rows, :] + jnp.dot(pick, hi, preferred_element_type=F32)
                          + jnp.dot(pick, lo, preferred_element_type=F32))


def _moe_dense_body(h_ref, route_ref, g2_ref, wg_ref, wu_ref, wd_ref, o_ref, hn_ref):
    g = pl.program_id(1)

    @pl.when(g == 0)
    def _():
        h = h_ref[...]
        o_ref[...] = h
        hn_ref[...] = _rmsnorm_rows(h, g2_ref[...]).astype(BF16)

    hn = hn_ref[...]
    comb = route_ref[...]
    lane = lax.broadcasted_iota(jnp.int32, comb.shape, 1)
    parts = []
    for e in range(EXPERTS_PER_GROUP):
        gate = jnp.sum(jnp.where(lane == N_EXPERT_GROUPS + g * EXPERTS_PER_GROUP + e, comb, 0.0),
                       axis=-1, keepdims=True)
        hg = jnp.dot(hn, wg_ref[e], preferred_element_type=F32)
        hu = jnp.dot(hn, wu_ref[e], preferred_element_type=F32)
        parts.append((hg / (1.0 + jnp.exp(-hg)) * hu * gate).astype(BF16))
    a = jnp.concatenate(parts, axis=-1)
    o_ref[...] += jnp.dot(a, wd_ref[...].reshape(GROUP_FF, D_MODEL), preferred_element_type=F32)


def _group_spec(a, index):
    return pl.BlockSpec((EXPERTS_PER_GROUP,) + a.shape[1:], index)


def _post_attn(x2d, gm, sb, params, tm):
    n = x2d.shape[0]
    wo, g2, wr, br = params
    assert n % tm == 0 and tm % MOE_SUB == 0, (n, tm)
    n_sub, sub = n // MOE_SUB, tm // MOE_SUB
    row = lambda w: pl.BlockSpec((tm, w), lambda i: (i, 0))
    full = lambda a: pl.BlockSpec(a.shape, lambda i: (0,) * a.ndim)
    per_sub = lambda rows, w: pl.BlockSpec((sub, rows, w), lambda i: (i, 0, 0))
    return pl.pallas_call(
        _post_attn_body,
        grid=(n // tm,),
        in_specs=[row(D_MODEL), row(GM_WIDTH), row(SB_WIDTH),
                  full(wo), full(g2), full(wr), full(br)],
        out_specs=[row(D_MODEL), row(LANES), per_sub(MOE_SLOTS, D_MODEL),
                   per_sub(MOE_SLOTS, LANES), per_sub(8, LANES)],
        out_shape=[jax.ShapeDtypeStruct((n, D_MODEL), F32),
                   jax.ShapeDtypeStruct((n, LANES), F32),
                   jax.ShapeDtypeStruct((n_sub, MOE_SLOTS, D_MODEL), BF16),
                   jax.ShapeDtypeStruct((n_sub, MOE_SLOTS, LANES), F32),
                   jax.ShapeDtypeStruct((n_sub, 8, LANES), F32)],
        compiler_params=pltpu.CompilerParams(
            dimension_semantics=("parallel",), vmem_limit_bytes=VMEM_LIMIT),
        name="post_attn",
    )(x2d, gm, sb, wo, g2, wr, br)


def _out_moe(prompt, sample, params, *, tm_prompt, tm_sample):
    wo, g2, wr, br, wg, wu, wd = params
    routed = [_post_attn(*grp, (wo, g2, wr, br), tm) for grp, tm in
              ((prompt, tm_prompt), (sample, tm_sample))]
    tms = (tm_prompt, tm_sample)
    params_2d = pltpu.CompilerParams(
        dimension_semantics=("parallel", "arbitrary"), vmem_limit_bytes=VMEM_LIMIT)

    def sparse():
        (_, _, bk_p, gt_p, _), (_, _, bk_s, gt_s, _) = routed
        n_sub, n_sub_s = bk_p.shape[0], bk_s.shape[0]
        sup = min(MOE_SUPER, n_sub)
        assert n_sub % sup == 0, (n_sub, sup)
        steps = n_sub // sup
        by_group = lambda a: a.reshape(a.shape[0], N_EXPERT_GROUPS, MOE_CAP, a.shape[-1])
        bucket = lambda w: pl.BlockSpec((sup, 1, MOE_CAP, w), lambda g, s: (s, g, 0, 0))
        bucket_s = lambda w: pl.BlockSpec((n_sub_s, 1, MOE_CAP, w), lambda g, s: (0, g, 0, 0))
        ys = pl.pallas_call(
            functools.partial(_moe_ffn_body, prompt_steps=steps),
            grid=(N_EXPERT_GROUPS, steps),
            in_specs=[bucket(D_MODEL), bucket(LANES), bucket_s(D_MODEL), bucket_s(LANES)]
                     + [_group_spec(w, lambda g, s: (g, 0, 0)) for w in (wg, wu, wd)],
            out_specs=[bucket(D_MODEL), bucket_s(D_MODEL)],
            out_shape=[jax.ShapeDtypeStruct((n, N_EXPERT_GROUPS, MOE_CAP, D_MODEL), F32)
                       for n in (n_sub, n_sub_s)],
            scratch_shapes=[pltpu.VMEM((EXPERTS_PER_GROUP,) + w.shape[1:], BF16) for w in (wg, wu, wd)],
            compiler_params=params_2d,
            name="moe_ffn",
        )(by_group(bk_p), by_group(gt_p), by_group(bk_s), by_group(gt_s), wg, wu, wd)

        def combine(h, route, y, tm):
            n, sub = h.shape[0], tm // MOE_SUB
            row = lambda w: pl.BlockSpec((tm, w), lambda i: (i, 0))
            return pl.pallas_call(
                _moe_combine_body,
                grid=(n // tm,),
                in_specs=[row(D_MODEL), row(LANES),
                          pl.BlockSpec((sub, MOE_SLOTS, D_MODEL), lambda i: (i, 0, 0))],
                out_specs=row(D_MODEL),
                out_shape=jax.ShapeDtypeStruct((n, D_MODEL), F32),
                compiler_params=pltpu.CompilerParams(
                    dimension_semantics=("parallel",), vmem_limit_bytes=VMEM_LIMIT),
                name="moe_combine",
            )(h, route, y.reshape(-1, MOE_SLOTS, D_MODEL))

        return tuple(combine(r[0], r[1], y, tm) for r, y, tm in zip(routed, ys, tms))

    def dense():
        wgb, wub, wdb = (w.astype(BF16) for w in (wg, wu, wd))

        def all_experts(h, route, tm):
            row2 = lambda w: pl.BlockSpec((tm, w), lambda i, g: (i, 0))
            return pl.pallas_call(
                _moe_dense_body,
                grid=(h.shape[0] // tm, N_EXPERT_GROUPS),
                in_specs=[row2(D_MODEL), row2(LANES), pl.BlockSpec(g2.shape, lambda i, g: (0, 0))]
                         + [_group_spec(w, lambda i, g: (g, 0, 0)) for w in (wg, wu, wd)],
                out_specs=row2(D_MODEL),
                out_shape=jax.ShapeDtypeStruct(h.shape, F32),
                scratch_shapes=[pltpu.VMEM((tm, D_MODEL), BF16)],
                compiler_params=params_2d,
                name="moe_dense",
            )(h, route, g2, wgb, wub, wdb)

        return tuple(all_experts(r[0], r[1], tm) for r, tm in zip(routed, tms))

    most = jnp.maximum(jnp.max(routed[0][4]), jnp.max(routed[1][4]))
    return lax.cond(most <= MOE_CAP, sparse, dense)


def _layer(layer, xp, xs, cache_k, cache_v, norm1_g, w_in, gm_v_norm_g, gm_w_s, gm_b_s, q_norm_g,
           k_norm_g, w_out, norm2_g, w_router_group, b_router_group, w_router_expert,
           b_router_expert, w_gate, w_up, w_down):
    b, s, d = xp.shape
    db, n, _ = xs.shape
    head_of = jnp.arange(SB_WIDTH) // SB_HEAD_DIM
    bd = jnp.where(head_of[:, None] == head_of[None, :], 1.0 / SB_HEAD_DIM, 0.0).astype(BF16)
    in_params = (norm1_g[None, :], w_in.astype(BF16), gm_v_norm_g[None, :],
                 jnp.tile(q_norm_g, SB_HEADS)[None, :], jnp.tile(k_norm_g, SB_HEADS)[None, :],
                 gm_w_s, gm_b_s.T, bd)

    wr = jnp.concatenate(
        [w_router_group, jnp.transpose(w_router_expert, (1, 0, 2)).reshape(d, N_EXPERTS)], axis=1)
    wr = jnp.pad(wr, ((0, 0), (0, LANES - wr.shape[1])))
    wr_hi = wr.astype(BF16)
    wr_lo = (wr - wr_hi.astype(F32)).astype(BF16)
    br = jnp.pad(jnp.concatenate([b_router_group, b_router_expert.reshape(-1)]),
                 (0, LANES - N_EXPERT_GROUPS - N_EXPERTS))[None, :]

    moe_params = (w_out.astype(BF16), norm2_g[None, :], jnp.concatenate([wr_hi, wr_lo], axis=1), br,
                  w_gate, w_up, w_down)

    xp2 = xp.reshape(b * s, d)
    gm, kf, vf, qs, kb, vb = _mixer_in(xp2, in_params, tm=min(1024, s), chunk=GM_CHUNK, stream=False,
                                       seq=s)
    shp = (b, s, SB_WIDTH)
    sb = _prompt_attn(qs.reshape(shp), kb.reshape(shp), vb.reshape(shp)).reshape(b * s, SB_WIDTH)

    xs2 = xs.reshape(db * n, d)
    gm_s, kf_s, vf_s, q_s, gv_s = _mixer_in(xs2, in_params, tm=db * n, chunk=n, stream=True)
    shs = (db, n, SB_WIDTH)
    frame_minor = lambda c: jnp.transpose(c, (0, 1, 3, 4, 2))
    sb_s = _sample_attn(q_s.reshape(shs), kf_s.reshape(shs), vf_s.reshape(shs),
                        frame_minor(cache_k), frame_minor(cache_v), layer)
    sb_s = sb_s.reshape(db * n, SB_WIDTH).astype(BF16)

    yp, ys = _out_moe((xp2, gm, sb), (xs2, gm_s, sb_s), moe_params,
                      tm_prompt=min(1024, b * s), tm_sample=db * n)
    yp, ys = yp.reshape(b, s, d), ys.reshape(db, n, d)

    heads = (SB_HEADS, SB_HEAD_DIM)
    rows = lambda a: jnp.transpose(a.reshape(b, *heads, s), (0, 3, 1, 2))
    return (yp, ys, rows(kf), rows(vf),
            kf_s.reshape(db, n, *heads), vf_s.reshape(db, n, *heads),
            gv_s.reshape(db, n, GM_GROUPS, GM_GROUP_DIM))


def kernel(x_prompt, x_sample, cache_sb_k, cache_sb_v, norm1_g, w_in, gm_v_norm_g, gm_w_s, gm_b_s, q_norm_g, k_norm_g, w_out, norm2_g, w_router_group, b_router_group, w_router_expert, b_router_expert, w_gate, w_up, w_down):
    depth = w_in.shape[0]
    yp, ys = x_prompt, x_sample
    outs = [[] for _ in range(5)]
    for l in range(depth):
        yp, ys, *rest = _layer(
            l, yp, ys, cache_sb_k, cache_sb_v, norm1_g[l], w_in[l], gm_v_norm_g[l], gm_w_s[l],
            gm_b_s[l], q_norm_g[l], k_norm_g[l], w_out[l], norm2_g[l], w_router_group[l],
            b_router_group[l], w_router_expert[l], b_router_expert[l], w_gate[l], w_up[l],
            w_down[l])
        for acc, r in zip(outs, rest):
            acc.append(r)
    return (yp, ys) + tuple(jnp.stack(o, axis=0) for o in outs)
```

```python
import functools

import jax
import jax.numpy as jnp
from jax import lax
from jax.experimental import pallas as pl
from jax.experimental.pallas import tpu as pltpu

D_MODEL = 1024
GM_WIDTH = 512
GM_GROUPS = 4
GM_GROUP_DIM = 128
GM_CHUNK = 128
SB_WIDTH = 512
SB_HEAD_DIM = 64
SB_HEADS = 8
N_EXPERT_GROUPS = 4
EXPERTS_PER_GROUP = 4
N_EXPERTS = 16
EXPERT_FF = 256
GROUP_FF = EXPERTS_PER_GROUP * EXPERT_FF
EPS = 1e-6

LANES = 128
HEAD_PAIRS = SB_WIDTH // LANES
SB_BLOCK = 128
SB_WIDE = 2 * SB_BLOCK
SB_FIRST = 3 * SB_BLOCK
CACHE_BLOCK = 256
MOE_SUB = 256
MOE_CAP = 96
MOE_SLOTS = N_EXPERT_GROUPS * MOE_CAP
MOE_SUPER = 8
SLOT_LANE = 0
VMEM_LIMIT = 56 * 1024 * 1024

LOG2_E = 1.4426950408889634
Q_SCALE = SB_HEAD_DIM ** -0.5 * LOG2_E
SB_UNDERFLOW_BITS = 160.0

F32 = jnp.float32
BF16 = jnp.bfloat16
_NT = (((1,), (1,)), ((), ()))


def _gelu_tanh(x):
    return 0.5 * x * (1.0 + jnp.tanh(0.7978845608028654 * (x + 0.044715 * (x * x * x))))


def _softplus2(z):
    return jnp.maximum(z, 0.0) + jnp.log2(1.0 + jnp.exp2(-jnp.abs(z)))


def _suffix_matrix(n):
    r = lax.broadcasted_iota(jnp.int32, (n, n), 0)
    c = lax.broadcasted_iota(jnp.int32, (n, n), 1)
    return jnp.where(r > c, 1.0, 0.0).astype(BF16)


def _mixer_in_body(x_ref, g1_ref, w_in_ref, gvg_ref, qg_ref, kg_ref, ws_ref, bst_ref, bd_ref,
                   gm_ref, kf_ref, vf_ref, *rest, chunk, stream):
    x = x_ref[...]
    tm = x.shape[0]
    xn = x * lax.rsqrt(jnp.mean(x * x, axis=-1, keepdims=True) + EPS) * g1_ref[...]
    proj = jnp.dot(xn.astype(BF16), w_in_ref[...], preferred_element_type=F32)

    gu = _gelu_tanh(proj[:, :GM_WIDTH])
    gvr = _gelu_tanh(proj[:, GM_WIDTH:2 * GM_WIDTH])
    gv = gvr * lax.rsqrt(jnp.mean(gvr * gvr, axis=-1, keepdims=True) + EPS) * gvg_ref[...]

    o = 2 * GM_WIDTH
    q = proj[:, o:o + SB_WIDTH]
    k = proj[:, o + SB_WIDTH:o + 2 * SB_WIDTH]
    v = proj[:, o + 2 * SB_WIDTH:]
    bd = bd_ref[...]
    qms = jnp.dot((q * q).astype(BF16), bd, preferred_element_type=F32)
    kms = jnp.dot((k * k).astype(BF16), bd, preferred_element_type=F32)
    qs = q * lax.rsqrt(qms + EPS) * qg_ref[...] * Q_SCALE
    kn = k * lax.rsqrt(kms + EPS) * kg_ref[...]
    if stream:
        kf_ref[...] = kn
        vf_ref[...] = v
        rest[0][...] = qs
        rest[1][...] = gv
    else:
        kt = kn.T
        kf_ref[...] = kt
        vf_ref[...] = v.T
        qs_ref, kb_ref, vb_ref = rest
        qs_ref[...] = qs.astype(BF16)
        kb_ref[...] = kt.astype(BF16)
        vb_ref[...] = v.astype(BF16)

    r = lax.broadcasted_iota(jnp.int32, (chunk, chunk), 0)
    c = lax.broadcasted_iota(jnp.int32, (chunk, chunk), 1)
    gvb = gv.astype(BF16)
    chunks = [slice(ci * chunk, (ci + 1) * chunk) for ci in range(tm // chunk)]
    for g in range(GM_GROUPS):
        w = jnp.where(r >= c, ws_ref[g, :chunk, :chunk], 0.0).astype(BF16)
        b = bst_ref[:chunk, g:g + 1]
        cols = slice(g * GM_GROUP_DIM, (g + 1) * GM_GROUP_DIM)
        mixed = jnp.dot(w, jnp.concatenate([gvb[rows, cols] for rows in chunks], axis=1),
                        preferred_element_type=F32)
        for ci, rows in enumerate(chunks):
            part = mixed[:, ci * GM_GROUP_DIM:(ci + 1) * GM_GROUP_DIM] + b
            gm_ref[rows, cols] = (gu[rows, cols] * part).astype(BF16)


def _mixer_in(x2d, params, *, tm, chunk, stream, seq=None):
    n = x2d.shape[0]
    assert n % tm == 0 and tm % chunk == 0 and chunk <= GM_CHUNK, (n, tm, chunk)
    full = lambda a: pl.BlockSpec(a.shape, lambda i: (0,) * a.ndim)
    spec = lambda s: pl.BlockSpec((tm,) + s.shape[1:], lambda i: (i,) + (0,) * (len(s.shape) - 1))
    gm = jax.ShapeDtypeStruct((n, GM_WIDTH), BF16)
    if stream:
        rows = jax.ShapeDtypeStruct((n, SB_WIDTH), F32)
        out_shape = [gm, rows, rows,
                     rows,
                     jax.ShapeDtypeStruct((n, GM_WIDTH), F32)]
        out_specs = [spec(s) for s in out_shape]
    else:
        per_seq = seq // tm
        rows_t = jax.ShapeDtypeStruct((n // seq, SB_WIDTH, seq), F32)
        spec_t = pl.BlockSpec((None, SB_WIDTH, tm), lambda i: (i // per_seq, 0, i % per_seq))
        dense = jax.ShapeDtypeStruct((n, SB_WIDTH), BF16)
        dense_t = jax.ShapeDtypeStruct(rows_t.shape, BF16)
        out_shape = [gm, rows_t, rows_t, dense, dense_t, dense]
        out_specs = [spec(gm), spec_t, spec_t, spec(dense), spec_t, spec(dense)]
    return pl.pallas_call(
        functools.partial(_mixer_in_body, chunk=chunk, stream=stream),
        grid=(n // tm,),
        in_specs=[pl.BlockSpec((tm, D_MODEL), lambda i: (i, 0))] + [full(a) for a in params],
        out_specs=out_specs,
        out_shape=out_shape,
        compiler_params=pltpu.CompilerParams(
            dimension_semantics=("parallel",), vmem_limit_bytes=VMEM_LIMIT),
        name="mixer_in_stream" if stream else "mixer_in",
    )(x2d, *params)


def _prompt_attn_body(q_ref, k_ref, v_ref, sfx_ref, sfx_wide_ref, sfx_first_ref, o_ref,
                      qst_ref, c_ref, acc_ref):
    i = pl.program_id(1)
    lane = lax.broadcasted_iota(jnp.int32, (SB_BLOCK, LANES), 1)
    first = lane < SB_HEAD_DIM
    for hp in range(HEAD_PAIRS):
        q = q_ref[:, hp * LANES:(hp + 1) * LANES]
        zero = jnp.zeros_like(q)
        qst_ref[hp, :SB_BLOCK, :] = jnp.where(first, q, zero)
        qst_ref[hp, SB_BLOCK:, :] = jnp.where(first, zero, q)
    m = 2 * SB_BLOCK
    suffix = {SB_BLOCK: sfx_ref, SB_WIDE: sfx_wide_ref, SB_FIRST: sfx_first_ref}

    def sweep(j, keys, causal, fresh):
        rows = pl.ds(pl.multiple_of(j * SB_BLOCK, SB_BLOCK), keys)
        cols = lambda hp: slice(hp * LANES, (hp + 1) * LANES)
        z = jnp.concatenate(
            [jnp.dot(qst_ref[hp], k_ref[cols(hp), rows], preferred_element_type=F32)
             for hp in range(HEAD_PAIRS)], axis=0)
        def masked(a):
            if causal is None:
                return a
            old = keys - SB_BLOCK
            newest = jnp.where(causal, a[:, old:], 0.0)
            return newest if old == 0 else jnp.concatenate([a[:, :old], newest], axis=1)

        sp = masked(_softplus2(z))
        c0 = jnp.zeros((HEAD_PAIRS * m, 1), F32) if fresh else c_ref[...]
        later = jnp.dot(sp.astype(BF16), suffix[keys][...], preferred_element_type=F32) + c0
        w = masked(jnp.exp2(z - sp - later))
        wb = w.astype(BF16)
        pv = jnp.concatenate(
            [jnp.dot(wb[hp * m:(hp + 1) * m, :], v_ref[rows, cols(hp)], preferred_element_type=F32)
             for hp in range(HEAD_PAIRS)], axis=0)
        acc_ref[...] = pv if fresh else acc_ref[...] + pv
        c_ref[...] = c0 + jnp.sum(sp, axis=-1, keepdims=True)

    below = SB_FIRST // SB_BLOCK - 1
    t = lax.broadcasted_iota(jnp.int32, (HEAD_PAIRS * m, SB_BLOCK), 0) % SB_BLOCK
    s = lax.broadcasted_iota(jnp.int32, (HEAD_PAIRS * m, SB_BLOCK), 1)

    @pl.when(i >= below)
    def _():
        sweep(i - below, SB_FIRST, s < t, True)

    @pl.when(i < below)
    def _():
        sweep(i, SB_BLOCK, s < t, True)

    j0 = jnp.where(i >= below, i - below, i)

    def least_carry():
        return jnp.min(c_ref[...])

    per_wide = SB_WIDE // SB_BLOCK
    n_wide = j0 // per_wide

    def more(state):
        jj, carry = state
        return (jj < n_wide) & (carry < SB_UNDERFLOW_BITS)

    def body(state):
        jj, _ = state
        sweep(j0 - per_wide * (jj + 1), SB_WIDE, None, False)
        return jj + 1, least_carry()

    done, carry = lax.while_loop(more, body, (jnp.int32(0), least_carry()))

    @pl.when((done == n_wide) & (j0 % per_wide == 1) & (carry < SB_UNDERFLOW_BITS))
    def _():
        sweep(0, SB_BLOCK, None, False)

    for hp in range(HEAD_PAIRS):
        lo = hp * m
        o_ref[:, hp * LANES:(hp + 1) * LANES] = jnp.where(
            first, acc_ref[lo:lo + SB_BLOCK, :], acc_ref[lo + SB_BLOCK:lo + m, :]).astype(o_ref.dtype)


def _prompt_attn(qs, kt, vb):
    b, s, _ = qs.shape
    assert s % SB_BLOCK == 0, s
    sfx = tuple(_suffix_matrix(keys) for keys in (SB_BLOCK, SB_WIDE, SB_FIRST))
    q_spec = pl.BlockSpec((None, SB_BLOCK, SB_WIDTH), lambda bi, i: (bi, i, 0))
    kt_spec = pl.BlockSpec((None, SB_WIDTH, s), lambda bi, i: (bi, 0, 0))
    kv_spec = pl.BlockSpec((None, s, SB_WIDTH), lambda bi, i: (bi, 0, 0))
    m = 2 * SB_BLOCK
    return pl.pallas_call(
        _prompt_attn_body,
        grid=(b, s // SB_BLOCK),
        in_specs=[q_spec, kt_spec, kv_spec] + [pl.BlockSpec(a.shape, lambda bi, i: (0, 0)) for a in sfx],
        out_specs=q_spec,
        out_shape=jax.ShapeDtypeStruct((b, s, SB_WIDTH), BF16),
        scratch_shapes=[pltpu.VMEM((HEAD_PAIRS, m, LANES), BF16),
                        pltpu.VMEM((HEAD_PAIRS * m, 1), F32),
                        pltpu.VMEM((HEAD_PAIRS * m, LANES), F32)],
        compiler_params=pltpu.CompilerParams(
            dimension_semantics=("parallel", "arbitrary"), vmem_limit_bytes=VMEM_LIMIT),
        name="prompt_attn",
    )(qs, kt, vb, *sfx)


def _sample_attn_body(q_ref, kn_ref, vn_ref, ck_hbm, cv_hbm, o_ref, kbuf, vbuf, sems, c_ref, acc_ref,
                      *, layer):
    b = pl.program_id(0)
    n = q_ref.shape[0]
    m = SB_HEADS * n
    nblk = ck_hbm.shape[-1] // CACHE_BLOCK

    def copies(stream, blk, slot):
        cols = pl.ds(blk * CACHE_BLOCK, CACHE_BLOCK)
        return (pltpu.make_async_copy(ck_hbm.at[layer, stream, :, :, cols], kbuf.at[slot], sems.at[slot, 0]),
                pltpu.make_async_copy(cv_hbm.at[layer, stream, :, :, cols], vbuf.at[slot], sems.at[slot, 1]))

    def start(stream, blk, slot):
        for cp in copies(stream, blk, slot):
            cp.start()

    def wait(stream, blk, slot):
        for cp in copies(stream, blk, slot):
            cp.wait()

    def slot_of(jj):
        return 2 + jj % 2

    @pl.when(b == 0)
    def _():
        start(0, nblk - 1, 0)

    @pl.when(b + 1 < pl.num_programs(0))
    def _():
        start(b + 1, nblk - 1, (b + 1) % 2)

    head = lambda a, h: a[:, h * SB_HEAD_DIM:(h + 1) * SB_HEAD_DIM]
    q_all = q_ref[...].astype(BF16)
    q = [head(q_all, h) for h in range(SB_HEADS)]

    def sweep(slot, newest, suffix, causal, fresh):
        def logits(h):
            z = jnp.dot(q[h], kbuf[slot, h].astype(BF16), preferred_element_type=F32)
            if newest is None:
                return z
            return jnp.concatenate(
                [z, lax.dot_general(q[h], newest[0][h], _NT, preferred_element_type=F32)], axis=1)

        z = jnp.concatenate([logits(h) for h in range(SB_HEADS)], axis=0)
        sp = _softplus2(z)
        if causal is not None:
            sp = jnp.where(causal, sp, 0.0)
        c0 = jnp.zeros((m, 1), F32) if fresh else c_ref[...]
        later = jnp.dot(sp.astype(BF16), suffix, preferred_element_type=F32) + c0
        w = jnp.exp2(z - sp - later)
        if causal is not None:
            w = jnp.where(causal, w, 0.0)
        wb = w.astype(BF16)
        pvs = []
        for h in range(SB_HEADS):
            wh = wb[h * n:(h + 1) * n, :]
            pv = lax.dot_general(wh[:, :CACHE_BLOCK], vbuf[slot, h].astype(BF16), _NT,
                                 preferred_element_type=F32)
            if newest is not None:
                pv = pv + jnp.dot(wh[:, CACHE_BLOCK:], newest[1][h], preferred_element_type=F32)
            pvs.append(pv)
        pv = jnp.stack(pvs)
        acc_ref[...] = pv if fresh else acc_ref[...] + pv
        c_ref[...] = c0 + jnp.sum(sp, axis=-1, keepdims=True)

    pad = jnp.zeros((LANES - n, SB_HEAD_DIM), BF16)
    kn_all, vn_all = kn_ref[...].astype(BF16), vn_ref[...].astype(BF16)
    newest = ([jnp.concatenate([head(kn_all, h), pad], axis=0) for h in range(SB_HEADS)],
              [jnp.concatenate([head(vn_all, h), pad], axis=0) for h in range(SB_HEADS)])
    t = lax.broadcasted_iota(jnp.int32, (m, CACHE_BLOCK + LANES), 0) % n
    s = lax.broadcasted_iota(jnp.int32, (m, CACHE_BLOCK + LANES), 1) - CACHE_BLOCK
    wait(b, nblk - 1, b % 2)
    sweep(b % 2, newest, _suffix_matrix(CACHE_BLOCK + LANES), s < t, True)

    suffix = _suffix_matrix(CACHE_BLOCK)

    def more(state):
        jj, carry = state
        return (jj < nblk) & (carry < SB_UNDERFLOW_BITS)

    def body(state):
        jj, _ = state
        blk = nblk - 1 - jj
        slot = slot_of(jj)
        wait(b, blk, slot)

        @pl.when(blk > 0)
        def _():
            start(b, blk - 1, slot_of(jj + 1))

        sweep(slot, None, suffix, None, False)
        return jj + 1, jnp.min(c_ref[...])

    state = (jnp.int32(1), jnp.min(c_ref[...]))

    if nblk > 1:
        @pl.when(more(state))
        def _():
            start(b, nblk - 2, slot_of(1))

    done, _ = lax.while_loop(more, body, state)

    @pl.when((done > 1) & (done < nblk))
    def _():
        wait(b, nblk - 1 - done, slot_of(done))

    for h in range(SB_HEADS):
        o_ref[:, h, :] = acc_ref[h]


def _sample_attn(qs, kn, vn, cache_k, cache_v, layer):
    db, n = qs.shape[:2]
    past = cache_k.shape[-1]
    assert n <= LANES and past >= CACHE_BLOCK and past % CACHE_BLOCK == 0, (n, past)
    m = SB_HEADS * n
    new_spec = pl.BlockSpec((None, n, SB_WIDTH), lambda b: (b, 0, 0))
    hbm = pl.BlockSpec(memory_space=pl.ANY)
    buf = pltpu.VMEM((4, SB_HEADS, SB_HEAD_DIM, CACHE_BLOCK), F32)
    return pl.pallas_call(
        functools.partial(_sample_attn_body, layer=layer),
        grid=(db,),
        in_specs=[new_spec, new_spec, new_spec, hbm, hbm],
        out_specs=pl.BlockSpec((None, n, SB_HEADS, SB_HEAD_DIM), lambda b: (b, 0, 0, 0)),
        out_shape=jax.ShapeDtypeStruct((db, n, SB_HEADS, SB_HEAD_DIM), F32),
        scratch_shapes=[buf, buf, pltpu.SemaphoreType.DMA((4, 2)),
                        pltpu.VMEM((m, 1), F32),
                        pltpu.VMEM((SB_HEADS, n, SB_HEAD_DIM), F32)],
        compiler_params=pltpu.CompilerParams(
            dimension_semantics=("arbitrary",), vmem_limit_bytes=VMEM_LIMIT),
        name="sample_attn",
    )(qs, kn, vn, cache_k, cache_v)


def _route(logits):
    lane = lax.broadcasted_iota(jnp.int32, logits.shape, 1).astype(F32)
    neg = jnp.float32(-jnp.inf)
    big = jnp.float32(1 << 20)
    rmax = lambda a: jnp.max(a, axis=-1, keepdims=True)
    rmin = lambda a: jnp.min(a, axis=-1, keepdims=True)
    rsum = lambda a: jnp.sum(a, axis=-1, keepdims=True)

    gmask = lane < N_EXPERT_GROUPS
    gl = jnp.where(gmask, logits, neg)
    gmax = rmax(gl)
    g_sel = rmin(jnp.where(gl == gmax, lane, big))
    p_sel = 1.0 / rsum(jnp.where(gmask, jnp.exp(gl - gmax), 0.0))

    e_lo = N_EXPERT_GROUPS + g_sel * EXPERTS_PER_GROUP
    emask = (lane >= e_lo) & (lane < e_lo + EXPERTS_PER_GROUP)
    el = jnp.where(emask, logits, neg)
    emax = rmax(el)
    ex = jnp.where(emask, jnp.exp(el - emax), 0.0)
    p_e = ex / rsum(ex)
    pm = jnp.where(emask, p_e, -1.0)
    v1 = rmax(pm)
    i1 = rmin(jnp.where(pm == v1, lane, big))
    pm2 = jnp.where(lane == i1, -1.0, pm)
    v2 = rmax(pm2)
    i2 = rmin(jnp.where(pm2 == v2, lane, big))
    tot = v1 + v2
    gate1 = p_sel * v1 / tot
    gate2 = p_sel * v2 / tot
    comb = jnp.where(lane == i1, gate1, 0.0) + jnp.where(lane == i2, gate2, 0.0)
    gates = jnp.where(lane == i1 - e_lo, gate1, 0.0) + jnp.where(lane == i2 - e_lo, gate2, 0.0)
    return comb, g_sel, gates


def _rmsnorm_rows(h, g):
    return h * lax.rsqrt(jnp.mean(h * h, axis=-1, keepdims=True) + EPS) * g


def _slot_matrix(slot):
    lane = lax.broadcasted_iota(jnp.int32, (slot.shape[0], MOE_SLOTS), 1)
    return jnp.where(lane == slot.astype(jnp.int32), 1.0, 0.0)


def _post_attn_body(x_ref, gm_ref, sb_ref, wo_ref, g2_ref, wr_ref, br_ref,
                    h_ref, route_ref, bucket_ref, gate_ref, cnt_ref):
    h = (x_ref[...]
         + jnp.dot(gm_ref[...], wo_ref[:GM_WIDTH, :], preferred_element_type=F32)
         + jnp.dot(sb_ref[...], wo_ref[GM_WIDTH:, :], preferred_element_type=F32))
    h_ref[...] = h
    hn = _rmsnorm_rows(h, g2_ref[...])
    hi = hn.astype(BF16)
    lo = (hn - hi.astype(F32)).astype(BF16)
    both = jnp.dot(hi, wr_ref[...], preferred_element_type=F32)
    logits = (both[:, :LANES] + jnp.dot(lo, wr_ref[:, :LANES], preferred_element_type=F32)
              + both[:, LANES:]) + br_ref[...]
    comb, g_sel, gates = _route(logits)

    subs = [slice(st * MOE_SUB, (st + 1) * MOE_SUB) for st in range(h.shape[0] // MOE_SUB)]
    lane = lax.broadcasted_iota(jnp.int32, comb.shape, 1).astype(F32)
    r = lax.broadcasted_iota(jnp.int32, (MOE_SUB, MOE_SUB), 0)
    c = lax.broadcasted_iota(jnp.int32, (MOE_SUB, MOE_SUB), 1)
    earlier = jnp.where(r > c, 1.0, 0.0).astype(BF16)
    onehot = jnp.where(lane == g_sel, 1.0, 0.0)
    onehot_b = onehot.astype(BF16)
    before = jnp.concatenate(
        [jnp.dot(earlier, onehot_b[rows], preferred_element_type=F32) for rows in subs], axis=0)
    rank = jnp.sum(onehot * before, axis=-1, keepdims=True)
    slot = jnp.where(rank < MOE_CAP, g_sel * MOE_CAP + rank, -1.0)
    route_ref[...] = jnp.where(lane == SLOT_LANE, slot, comb)
    g1 = gates.astype(BF16)
    g2 = (gates - g1.astype(F32)).astype(BF16)
    g3 = (gates - g1.astype(F32) - g2.astype(F32)).astype(BF16)
    g12 = jnp.concatenate([g1, g2], axis=1)
    for st, rows in enumerate(subs):
        place = _slot_matrix(slot[rows]).T.astype(BF16)
        bucket_ref[st] = jnp.dot(place, hi[rows], preferred_element_type=F32).astype(BF16)
        placed = jnp.dot(place, g12[rows], preferred_element_type=F32)
        gate_ref[st] = (placed[:, :LANES] + placed[:, LANES:]
                        + jnp.dot(place, g3[rows], preferred_element_type=F32))
        cnt_ref[st] = jnp.broadcast_to(jnp.sum(onehot[rows], axis=0, keepdims=True), cnt_ref.shape[1:])


def _moe_ffn_body(x_ref, gate_ref, xs_ref, gates_ref, wg_ref, wu_ref, wd_ref, y_ref, ys_ref,
                  wgb_ref, wub_ref, wdb_ref, *, prompt_steps):
    s = pl.program_id(1)

    @pl.when(s == 0)
    def _():
        wgb_ref[...] = wg_ref[...].astype(BF16)
        wub_ref[...] = wu_ref[...].astype(BF16)
        wdb_ref[...] = wd_ref[...].astype(BF16)

    def ffn(ins, outs):
        x = jnp.concatenate([x_ref[...].reshape(-1, D_MODEL) for x_ref, _ in ins], axis=0)
        gates = jnp.concatenate([g_ref[...].reshape(-1, LANES) for _, g_ref in ins], axis=0)
        parts = []
        for e in range(EXPERTS_PER_GROUP):
            hg = jnp.dot(x, wgb_ref[e], preferred_element_type=F32)
            hu = jnp.dot(x, wub_ref[e], preferred_element_type=F32)
            parts.append((hg / (1.0 + jnp.exp(-hg)) * hu * gates[:, e:e + 1]).astype(BF16))
        a = jnp.concatenate(parts, axis=-1)
        y = jnp.dot(a, wdb_ref[...].reshape(GROUP_FF, D_MODEL), preferred_element_type=F32)
        row = 0
        for out_ref in outs:
            rows = out_ref.shape[0] * out_ref.shape[1] * out_ref.shape[2]
            out_ref[...] = y[row:row + rows].reshape(out_ref.shape)
            row += rows

    @pl.when(s < prompt_steps - 1)
    def _():
        ffn([(x_ref, gate_ref)], [y_ref])

    @pl.when(s == prompt_steps - 1)
    def _():
        ffn([(x_ref, gate_ref), (xs_ref, gates_ref)], [y_ref, ys_ref])


def _moe_combine_body(h_ref, route_ref, ys_ref, o_ref):
    for st in range(h_ref.shape[0] // MOE_SUB):
        rows = slice(st * MOE_SUB, (st + 1) * MOE_SUB)
        pick = _slot_matrix(route_ref[rows, SLOT_LANE:SLOT_LANE + 1]).astype(BF16)
        ys = ys_ref[st]
        hi = ys.astype(BF16)
        lo = (ys - hi.astype(F32)).astype(BF16)
        o_ref[rows, :] = (h_ref[rows, :] + jnp.dot(pick, hi, preferred_element_type=F32)
                          + jnp.dot(pick, lo, preferred_element_type=F32))


def _moe_dense_body(h_ref, route_ref, g2_ref, wg_ref, wu_ref, wd_ref, o_ref, hn_ref):
    g = pl.program_id(1)

    @pl.when(g == 0)
    def _():
        h = h_ref[...]
        o_ref[...] = h
        hn_ref[...] = _rmsnorm_rows(h, g2_ref[...]).astype(BF16)

    hn = hn_ref[...]
    comb = route_ref[...]
    lane = lax.broadcasted_iota(jnp.int32, comb.shape, 1)
    parts = []
    for e in range(EXPERTS_PER_GROUP):
        gate = jnp.sum(jnp.where(lane == N_EXPERT_GROUPS + g * EXPERTS_PER_GROUP + e, comb, 0.0),
                       axis=-1, keepdims=True)
        hg = jnp.dot(hn, wg_ref[e], preferred_element_type=F32)
        hu = jnp.dot(hn, wu_ref[e], preferred_element_type=F32)
        parts.append((hg / (1.0 + jnp.exp(-hg)) * hu * gate).astype(BF16))
    a = jnp.concatenate(parts, axis=-1)
    o_ref[...] += jnp.dot(a, wd_ref[...].reshape(GROUP_FF, D_MODEL), preferred_element_type=F32)


def _group_spec(a, index):
    return pl.BlockSpec((EXPERTS_PER_GROUP,) + a.shape[1:], index)


def _post_attn(x2d, gm, sb, params, tm):
    n = x2d.shape[0]
    wo, g2, wr, br = params
    assert n % tm == 0 and tm % MOE_SUB == 0, (n, tm)
    n_sub, sub = n // MOE_SUB, tm // MOE_SUB
    row = lambda w: pl.BlockSpec((tm, w), lambda i: (i, 0))
    full = lambda a: pl.BlockSpec(a.shape, lambda i: (0,) * a.ndim)
    per_sub = lambda rows, w: pl.BlockSpec((sub, rows, w), lambda i: (i, 0, 0))
    return pl.pallas_call(
        _post_attn_body,
        grid=(n // tm,),
        in_specs=[row(D_MODEL), row(GM_WIDTH), row(SB_WIDTH),
                  full(wo), full(g2), full(wr), full(br)],
        out_specs=[row(D_MODEL), row(LANES), per_sub(MOE_SLOTS, D_MODEL),
                   per_sub(MOE_SLOTS, LANES), per_sub(8, LANES)],
        out_shape=[jax.ShapeDtypeStruct((n, D_MODEL), F32),
                   jax.ShapeDtypeStruct((n, LANES), F32),
                   jax.ShapeDtypeStruct((n_sub, MOE_SLOTS, D_MODEL), BF16),
                   jax.ShapeDtypeStruct((n_sub, MOE_SLOTS, LANES), F32),
                   jax.ShapeDtypeStruct((n_sub, 8, LANES), F32)],
        compiler_params=pltpu.CompilerParams(
            dimension_semantics=("parallel",), vmem_limit_bytes=VMEM_LIMIT),
        name="post_attn",
    )(x2d, gm, sb, wo, g2, wr, br)


def _out_moe(prompt, sample, params, *, tm_prompt, tm_sample):
    wo, g2, wr, br, wg, wu, wd = params
    routed = [_post_attn(*grp, (wo, g2, wr, br), tm) for grp, tm in
              ((prompt, tm_prompt), (sample, tm_sample))]
    tms = (tm_prompt, tm_sample)
    params_2d = pltpu.CompilerParams(
        dimension_semantics=("parallel", "arbitrary"), vmem_limit_bytes=VMEM_LIMIT)

    def sparse():
        (_, _, bk_p, gt_p, _), (_, _, bk_s, gt_s, _) = routed
        n_sub, n_sub_s = bk_p.shape[0], bk_s.shape[0]
        sup = min(MOE_SUPER, n_sub)
        assert n_sub % sup == 0, (n_sub, sup)
        steps = n_sub // sup
        by_group = lambda a: a.reshape(a.shape[0], N_EXPERT_GROUPS, MOE_CAP, a.shape[-1])
        bucket = lambda w: pl.BlockSpec((sup, 1, MOE_CAP, w), lambda g, s: (s, g, 0, 0))
        bucket_s = lambda w: pl.BlockSpec((n_sub_s, 1, MOE_CAP, w), lambda g, s: (0, g, 0, 0))
        ys = pl.pallas_call(
            functools.partial(_moe_ffn_body, prompt_steps=steps),
            grid=(N_EXPERT_GROUPS, steps),
            in_specs=[bucket(D_MODEL), bucket(LANES), bucket_s(D_MODEL), bucket_s(LANES)]
                     + [_group_spec(w, lambda g, s: (g, 0, 0)) for w in (wg, wu, wd)],
            out_specs=[bucket(D_MODEL), bucket_s(D_MODEL)],
            out_shape=[jax.ShapeDtypeStruct((n, N_EXPERT_GROUPS, MOE_CAP, D_MODEL), F32)
                       for n in (n_sub, n_sub_s)],
            scratch_shapes=[pltpu.VMEM((EXPERTS_PER_GROUP,) + w.shape[1:], BF16) for w in (wg, wu, wd)],
            compiler_params=params_2d,
            name="moe_ffn",
        )(by_group(bk_p), by_group(gt_p), by_group(bk_s), by_group(gt_s), wg, wu, wd)

        def combine(h, route, y, tm):
            n, sub = h.shape[0], tm // MOE_SUB
            row = lambda w: pl.BlockSpec((tm, w), lambda i: (i, 0))
            return pl.pallas_call(
                _moe_combine_body,
                grid=(n // tm,),
                in_specs=[row(D_MODEL), row(LANES),
                          pl.BlockSpec((sub, MOE_SLOTS, D_MODEL), lambda i: (i, 0, 0))],
                out_specs=row(D_MODEL),
                out_shape=jax.ShapeDtypeStruct((n, D_MODEL), F32),
                compiler_params=pltpu.CompilerParams(
                    dimension_semantics=("parallel",), vmem_limit_bytes=VMEM_LIMIT),
                name="moe_combine",
            )(h, route, y.reshape(-1, MOE_SLOTS, D_MODEL))

        return tuple(combine(r[0], r[1], y, tm) for r, y, tm in zip(routed, ys, tms))

    def dense():
        wgb, wub, wdb = (w.astype(BF16) for w in (wg, wu, wd))

        def all_experts(h, route, tm):
            row2 = lambda w: pl.BlockSpec((tm, w), lambda i, g: (i, 0))
            return pl.pallas_call(
                _moe_dense_body,
                grid=(h.shape[0] // tm, N_EXPERT_GROUPS),
                in_specs=[row2(D_MODEL), row2(LANES), pl.BlockSpec(g2.shape, lambda i, g: (0, 0))]
                         + [_group_spec(w, lambda i, g: (g, 0, 0)) for w in (wg, wu, wd)],
                out_specs=row2(D_MODEL),
                out_shape=jax.ShapeDtypeStruct(h.shape, F32),
                scratch_shapes=[pltpu.VMEM((tm, D_MODEL), BF16)],
                compiler_params=params_2d,
                name="moe_dense",
            )(h, route, g2, wgb, wub, wdb)

        return tuple(all_experts(r[0], r[1], tm) for r, tm in zip(routed, tms))

    most = jnp.maximum(jnp.max(routed[0][4]), jnp.max(routed[1][4]))
    return lax.cond(most <= MOE_CAP, sparse, dense)


def _layer(layer, xp, xs, cache_k, cache_v, norm1_g, w_in, gm_v_norm_g, gm_w_s, gm_b_s, q_norm_g,
           k_norm_g, w_out, norm2_g, w_router_group, b_router_group, w_router_expert,
           b_router_expert, w_gate, w_up, w_down):
    b, s, d = xp.shape
    db, n, _ = xs.shape
    head_of = jnp.arange(SB_WIDTH) // SB_HEAD_DIM
    bd = jnp.where(head_of[:, None] == head_of[None, :], 1.0 / SB_HEAD_DIM, 0.0).astype(BF16)
    in_params = (norm1_g[None, :], w_in.astype(BF16), gm_v_norm_g[None, :],
                 jnp.tile(q_norm_g, SB_HEADS)[None, :], jnp.tile(k_norm_g, SB_HEADS)[None, :],
                 gm_w_s, gm_b_s.T, bd)

    wr = jnp.concatenate(
        [w_router_group, jnp.transpose(w_router_expert, (1, 0, 2)).reshape(d, N_EXPERTS)], axis=1)
    wr = jnp.pad(wr, ((0, 0), (0, LANES - wr.shape[1])))
    wr_hi = wr.astype(BF16)
    wr_lo = (wr - wr_hi.astype(F32)).astype(BF16)
    br = jnp.pad(jnp.concatenate([b_router_group, b_router_expert.reshape(-1)]),
                 (0, LANES - N_EXPERT_GROUPS - N_EXPERTS))[None, :]

    moe_params = (w_out.astype(BF16), norm2_g[None, :], jnp.concatenate([wr_hi, wr_lo], axis=1), br,
                  w_gate, w_up, w_down)

    xp2 = xp.reshape(b * s, d)
    gm, kf, vf, qs, kb, vb = _mixer_in(xp2, in_params, tm=min(1024, s), chunk=GM_CHUNK, stream=False,
                                       seq=s)
    shp = (b, s, SB_WIDTH)
    sb = _prompt_attn(qs.reshape(shp), kb, vb.reshape(shp)).reshape(b * s, SB_WIDTH)

    xs2 = xs.reshape(db * n, d)
    gm_s, kf_s, vf_s, q_s, gv_s = _mixer_in(xs2, in_params, tm=db * n, chunk=n, stream=True)
    shs = (db, n, SB_WIDTH)
    frame_minor = lambda c: jnp.transpose(c, (0, 1, 3, 4, 2))
    sb_s = _sample_attn(q_s.reshape(shs), kf_s.reshape(shs), vf_s.reshape(shs),
                        frame_minor(cache_k), frame_minor(cache_v), layer)
    sb_s = sb_s.reshape(db * n, SB_WIDTH).astype(BF16)

    yp, ys = _out_moe((xp2, gm, sb), (xs2, gm_s, sb_s), moe_params,
                      tm_prompt=min(1024, b * s), tm_sample=db * n)
    yp, ys = yp.reshape(b, s, d), ys.reshape(db, n, d)

    heads = (SB_HEADS, SB_HEAD_DIM)
    rows = lambda a: jnp.transpose(a.reshape(b, *heads, s), (0, 3, 1, 2))
    return (yp, ys, rows(kf), rows(vf),
            kf_s.reshape(db, n, *heads), vf_s.reshape(db, n, *heads),
            gv_s.reshape(db, n, GM_GROUPS, GM_GROUP_DIM))


def kernel(x_prompt, x_sample, cache_sb_k, cache_sb_v, norm1_g, w_in, gm_v_norm_g, gm_w_s, gm_b_s, q_norm_g, k_norm_g, w_out, norm2_g, w_router_group, b_router_group, w_router_expert, b_router_expert, w_gate, w_up, w_down):
    depth = w_in.shape[0]
    yp, ys = x_prompt, x_sample
    outs = [[] for _ in range(5)]
    for l in range(depth):
        yp, ys, *rest = _layer(
            l, yp, ys, cache_sb_k, cache_sb_v, norm1_g[l], w_in[l], gm_v_norm_g[l], gm_w_s[l],
            gm_b_s[l], q_norm_g[l], k_norm_g[l], w_out[l], norm2_g[l], w_router_group[l],
            b_router_group[l], w_router_expert[l], b_router_expert[l], w_gate[l], w_up[l],
            w_down[l])
        for acc, r in zip(outs, rest):
            acc.append(r)
    return (yp, ys) + tuple(jnp.stack(o, axis=0) for o in outs)
```

```python
import functools

import jax
import jax.numpy as jnp
from jax import lax
from jax.experimental import pallas as pl
from jax.experimental.pallas import tpu as pltpu

D_MODEL = 1024
GM_WIDTH = 512
GM_GROUPS = 4
GM_GROUP_DIM = 128
GM_CHUNK = 128
SB_WIDTH = 512
SB_HEAD_DIM = 64
SB_HEADS = 8
N_EXPERT_GROUPS = 4
EXPERTS_PER_GROUP = 4
N_EXPERTS = 16
EXPERT_FF = 256
GROUP_FF = EXPERTS_PER_GROUP * EXPERT_FF
EPS = 1e-6

LANES = 128
HEAD_PAIRS = SB_WIDTH // LANES
SB_BLOCK = 128
SB_WIDE = 2 * SB_BLOCK
SB_FIRST = 3 * SB_BLOCK
SB_PER_STEP = 4
CACHE_BLOCK = 256
MOE_SUB = 256
MOE_CAP = 96
MOE_SLOTS = N_EXPERT_GROUPS * MOE_CAP
MOE_SUPER = 8
SLOT_LANE = 0
VMEM_LIMIT = 56 * 1024 * 1024

LOG2_E = 1.4426950408889634
Q_SCALE = SB_HEAD_DIM ** -0.5 * LOG2_E
SB_UNDERFLOW_BITS = 160.0

F32 = jnp.float32
BF16 = jnp.bfloat16
_NT = (((1,), (1,)), ((), ()))


def _gelu_tanh(x):
    return 0.5 * x * (1.0 + jnp.tanh(0.7978845608028654 * (x + 0.044715 * (x * x * x))))


def _softplus2(z):
    return jnp.maximum(z, 0.0) + jnp.log2(1.0 + jnp.exp2(-jnp.abs(z)))


def _suffix_matrix(n):
    r = lax.broadcasted_iota(jnp.int32, (n, n), 0)
    c = lax.broadcasted_iota(jnp.int32, (n, n), 1)
    return jnp.where(r > c, 1.0, 0.0).astype(BF16)


def _mixer_in_body(x_ref, g1_ref, w_in_ref, gvg_ref, qg_ref, kg_ref, ws_ref, bst_ref, bd_ref,
                   gm_ref, kf_ref, vf_ref, *rest, chunk, stream):
    x = x_ref[...]
    tm = x.shape[0]
    xn = x * lax.rsqrt(jnp.mean(x * x, axis=-1, keepdims=True) + EPS) * g1_ref[...]
    proj = jnp.dot(xn.astype(BF16), w_in_ref[...], preferred_element_type=F32)

    gu = _gelu_tanh(proj[:, :GM_WIDTH])
    gvr = _gelu_tanh(proj[:, GM_WIDTH:2 * GM_WIDTH])
    gv = gvr * lax.rsqrt(jnp.mean(gvr * gvr, axis=-1, keepdims=True) + EPS) * gvg_ref[...]

    o = 2 * GM_WIDTH
    q = proj[:, o:o + SB_WIDTH]
    k = proj[:, o + SB_WIDTH:o + 2 * SB_WIDTH]
    v = proj[:, o + 2 * SB_WIDTH:]
    bd = bd_ref[...]
    qms = jnp.dot((q * q).astype(BF16), bd, preferred_element_type=F32)
    kms = jnp.dot((k * k).astype(BF16), bd, preferred_element_type=F32)
    qs = q * lax.rsqrt(qms + EPS) * qg_ref[...] * Q_SCALE
    kn = k * lax.rsqrt(kms + EPS) * kg_ref[...]
    if stream:
        kf_ref[...] = kn
        vf_ref[...] = v
        rest[0][...] = qs
        rest[1][...] = gv
    else:
        kt = kn.T
        kf_ref[...] = kt
        vf_ref[...] = v.T
        qs_ref, kb_ref, vb_ref = rest
        qs_ref[...] = qs.astype(BF16)
        kb_ref[...] = kt.astype(BF16)
        vb_ref[...] = v.astype(BF16)

    r = lax.broadcasted_iota(jnp.int32, (chunk, chunk), 0)
    c = lax.broadcasted_iota(jnp.int32, (chunk, chunk), 1)
    gvb = gv.astype(BF16)
    chunks = [slice(ci * chunk, (ci + 1) * chunk) for ci in range(tm // chunk)]
    for g in range(GM_GROUPS):
        w = jnp.where(r >= c, ws_ref[g, :chunk, :chunk], 0.0).astype(BF16)
        b = bst_ref[:chunk, g:g + 1]
        cols = slice(g * GM_GROUP_DIM, (g + 1) * GM_GROUP_DIM)
        mixed = jnp.dot(w, jnp.concatenate([gvb[rows, cols] for rows in chunks], axis=1),
                        preferred_element_type=F32)
        for ci, rows in enumerate(chunks):
            part = mixed[:, ci * GM_GROUP_DIM:(ci + 1) * GM_GROUP_DIM] + b
            gm_ref[rows, cols] = (gu[rows, cols] * part).astype(BF16)


def _mixer_in(x2d, params, *, tm, chunk, stream, seq=None):
    n = x2d.shape[0]
    assert n % tm == 0 and tm % chunk == 0 and chunk <= GM_CHUNK, (n, tm, chunk)
    full = lambda a: pl.BlockSpec(a.shape, lambda i: (0,) * a.ndim)
    spec = lambda s: pl.BlockSpec((tm,) + s.shape[1:], lambda i: (i,) + (0,) * (len(s.shape) - 1))
    gm = jax.ShapeDtypeStruct((n, GM_WIDTH), BF16)
    if stream:
        rows = jax.ShapeDtypeStruct((n, SB_WIDTH), F32)
        out_shape = [gm, rows, rows,
                     rows,
                     jax.ShapeDtypeStruct((n, GM_WIDTH), F32)]
        out_specs = [spec(s) for s in out_shape]
    else:
        per_seq = seq // tm
        rows_t = jax.ShapeDtypeStruct((n // seq, SB_WIDTH, seq), F32)
        spec_t = pl.BlockSpec((None, SB_WIDTH, tm), lambda i: (i // per_seq, 0, i % per_seq))
        dense = jax.ShapeDtypeStruct((n, SB_WIDTH), BF16)
        dense_t = jax.ShapeDtypeStruct(rows_t.shape, BF16)
        out_shape = [gm, rows_t, rows_t, dense, dense_t, dense]
        out_specs = [spec(gm), spec_t, spec_t, spec(dense), spec_t, spec(dense)]
    return pl.pallas_call(
        functools.partial(_mixer_in_body, chunk=chunk, stream=stream),
        grid=(n // tm,),
        in_specs=[pl.BlockSpec((tm, D_MODEL), lambda i: (i, 0))] + [full(a) for a in params],
        out_specs=out_specs,
        out_shape=out_shape,
        compiler_params=pltpu.CompilerParams(
            dimension_semantics=("parallel",), vmem_limit_bytes=VMEM_LIMIT),
        name="mixer_in_stream" if stream else "mixer_in",
    )(x2d, *params)


def _prompt_attn_body(q_ref, k_ref, v_ref, sfx_ref, sfx_wide_ref, sfx_first_ref, o_ref, *scratch):
    def one(sub, carry):
        rows = pl.ds(pl.multiple_of(sub * SB_BLOCK, SB_BLOCK), SB_BLOCK)
        _prompt_attn_block(pl.program_id(1) * SB_PER_STEP + sub, q_ref.at[rows], k_ref, v_ref, sfx_ref,
                           sfx_wide_ref, sfx_first_ref, o_ref.at[rows], *scratch)
        return carry

    lax.fori_loop(0, SB_PER_STEP, one, 0)


def _prompt_attn_block(i, q_ref, k_ref, v_ref, sfx_ref, sfx_wide_ref, sfx_first_ref, o_ref,
                       qst_ref, c_ref, acc_ref):
    lane = lax.broadcasted_iota(jnp.int32, (SB_BLOCK, LANES), 1)
    first = lane < SB_HEAD_DIM
    for hp in range(HEAD_PAIRS):
        q = q_ref[:, hp * LANES:(hp + 1) * LANES]
        zero = jnp.zeros_like(q)
        qst_ref[hp, :SB_BLOCK, :] = jnp.where(first, q, zero)
        qst_ref[hp, SB_BLOCK:, :] = jnp.where(first, zero, q)
    m = 2 * SB_BLOCK
    suffix = {SB_BLOCK: sfx_ref, SB_WIDE: sfx_wide_ref, SB_FIRST: sfx_first_ref}

    def sweep(j, keys, causal, fresh):
        rows = pl.ds(pl.multiple_of(j * SB_BLOCK, SB_BLOCK), keys)
        cols = lambda hp: slice(hp * LANES, (hp + 1) * LANES)
        z = jnp.concatenate(
            [jnp.dot(qst_ref[hp], k_ref[cols(hp), rows], preferred_element_type=F32)
             for hp in range(HEAD_PAIRS)], axis=0)
        def masked(a):
            if causal is None:
                return a
            old = keys - SB_BLOCK
            newest = jnp.where(causal, a[:, old:], 0.0)
            return newest if old == 0 else jnp.concatenate([a[:, :old], newest], axis=1)

        sp = masked(_softplus2(z))
        c0 = jnp.zeros((HEAD_PAIRS * m, 1), F32) if fresh else c_ref[...]
        later = jnp.dot(sp.astype(BF16), suffix[keys][...], preferred_element_type=F32) + c0
        w = masked(jnp.exp2(z - sp - later))
        wb = w.astype(BF16)
        pv = jnp.concatenate(
            [jnp.dot(wb[hp * m:(hp + 1) * m, :], v_ref[rows, cols(hp)], preferred_element_type=F32)
             for hp in range(HEAD_PAIRS)], axis=0)
        acc_ref[...] = pv if fresh else acc_ref[...] + pv
        c_ref[...] = c0 + jnp.sum(sp, axis=-1, keepdims=True)

    below = SB_FIRST // SB_BLOCK - 1
    t = lax.broadcasted_iota(jnp.int32, (HEAD_PAIRS * m, SB_BLOCK), 0) % SB_BLOCK
    s = lax.broadcasted_iota(jnp.int32, (HEAD_PAIRS * m, SB_BLOCK), 1)

    @pl.when(i >= below)
    def _():
        sweep(i - below, SB_FIRST, s < t, True)

    @pl.when(i < below)
    def _():
        sweep(i, SB_BLOCK, s < t, True)

    j0 = jnp.where(i >= below, i - below, i)

    def least_carry():
        return jnp.min(c_ref[...])

    per_wide = SB_WIDE // SB_BLOCK
    n_wide = j0 // per_wide

    def more(state):
        jj, carry = state
        return (jj < n_wide) & (carry < SB_UNDERFLOW_BITS)

    def body(state):
        jj, _ = state
        sweep(j0 - per_wide * (jj + 1), SB_WIDE, None, False)
        return jj + 1, least_carry()

    done, carry = lax.while_loop(more, body, (jnp.int32(0), least_carry()))

    @pl.when((done == n_wide) & (j0 % per_wide == 1) & (carry < SB_UNDERFLOW_BITS))
    def _():
        sweep(0, SB_BLOCK, None, False)

    for hp in range(HEAD_PAIRS):
        lo = hp * m
        o_ref[:, hp * LANES:(hp + 1) * LANES] = jnp.where(
            first, acc_ref[lo:lo + SB_BLOCK, :], acc_ref[lo + SB_BLOCK:lo + m, :]).astype(o_ref.dtype)


def _prompt_attn(qs, kt, vb):
    b, s, _ = qs.shape
    assert s % SB_BLOCK == 0, s
    sfx = tuple(_suffix_matrix(keys) for keys in (SB_BLOCK, SB_WIDE, SB_FIRST))
    per_step = SB_PER_STEP * SB_BLOCK
    assert s % per_step == 0, s
    q_spec = pl.BlockSpec((None, per_step, SB_WIDTH), lambda bi, i: (bi, i, 0))
    kt_spec = pl.BlockSpec((None, SB_WIDTH, s), lambda bi, i: (bi, 0, 0))
    kv_spec = pl.BlockSpec((None, s, SB_WIDTH), lambda bi, i: (bi, 0, 0))
    m = 2 * SB_BLOCK
    return pl.pallas_call(
        _prompt_attn_body,
        grid=(b, s // per_step),
        in_specs=[q_spec, kt_spec, kv_spec] + [pl.BlockSpec(a.shape, lambda bi, i: (0, 0)) for a in sfx],
        out_specs=q_spec,
        out_shape=jax.ShapeDtypeStruct((b, s, SB_WIDTH), BF16),
        scratch_shapes=[pltpu.VMEM((HEAD_PAIRS, m, LANES), BF16),
                        pltpu.VMEM((HEAD_PAIRS * m, 1), F32),
                        pltpu.VMEM((HEAD_PAIRS * m, LANES), F32)],
        compiler_params=pltpu.CompilerParams(
            dimension_semantics=("parallel", "arbitrary"), vmem_limit_bytes=VMEM_LIMIT),
        name="prompt_attn",
    )(qs, kt, vb, *sfx)


def _sample_attn_body(q_ref, kn_ref, vn_ref, ck_hbm, cv_hbm, o_ref, kbuf, vbuf, sems, c_ref, acc_ref,
                      *, layer):
    b = pl.program_id(0)
    n = q_ref.shape[0]
    m = SB_HEADS * n
    nblk = ck_hbm.shape[-1] // CACHE_BLOCK

    def copies(stream, blk, slot):
        cols = pl.ds(blk * CACHE_BLOCK, CACHE_BLOCK)
        return (pltpu.make_async_copy(ck_hbm.at[layer, stream, :, :, cols], kbuf.at[slot], sems.at[slot, 0]),
                pltpu.make_async_copy(cv_hbm.at[layer, stream, :, :, cols], vbuf.at[slot], sems.at[slot, 1]))

    def start(stream, blk, slot):
        for cp in copies(stream, blk, slot):
            cp.start()

    def wait(stream, blk, slot):
        for cp in copies(stream, blk, slot):
            cp.wait()

    def slot_of(jj):
        return 2 + jj % 2

    @pl.when(b == 0)
    def _():
        start(0, nblk - 1, 0)

    @pl.when(b + 1 < pl.num_programs(0))
    def _():
        start(b + 1, nblk - 1, (b + 1) % 2)

    head = lambda a, h: a[:, h * SB_HEAD_DIM:(h + 1) * SB_HEAD_DIM]
    q_all = q_ref[...].astype(BF16)
    q = [head(q_all, h) for h in range(SB_HEADS)]

    def sweep(slot, newest, suffix, causal, fresh):
        def logits(h):
            z = jnp.dot(q[h], kbuf[slot, h].astype(BF16), preferred_element_type=F32)
            if newest is None:
                return z
            return jnp.concatenate(
                [z, lax.dot_general(q[h], newest[0][h], _NT, preferred_element_type=F32)], axis=1)

        z = jnp.concatenate([logits(h) for h in range(SB_HEADS)], axis=0)
        sp = _softplus2(z)
        if causal is not None:
            sp = jnp.where(causal, sp, 0.0)
        c0 = jnp.zeros((m, 1), F32) if fresh else c_ref[...]
        later = jnp.dot(sp.astype(BF16), suffix, preferred_element_type=F32) + c0
        w = jnp.exp2(z - sp - later)
        if causal is not None:
            w = jnp.where(causal, w, 0.0)
        wb = w.astype(BF16)
        pvs = []
        for h in range(SB_HEADS):
            wh = wb[h * n:(h + 1) * n, :]
            pv = lax.dot_general(wh[:, :CACHE_BLOCK], vbuf[slot, h].astype(BF16), _NT,
                                 preferred_element_type=F32)
            if newest is not None:
                pv = pv + jnp.dot(wh[:, CACHE_BLOCK:], newest[1][h], preferred_element_type=F32)
            pvs.append(pv)
        pv = jnp.stack(pvs)
        acc_ref[...] = pv if fresh else acc_ref[...] + pv
        c_ref[...] = c0 + jnp.sum(sp, axis=-1, keepdims=True)

    pad = jnp.zeros((LANES - n, SB_HEAD_DIM), BF16)
    kn_all, vn_all = kn_ref[...].astype(BF16), vn_ref[...].astype(BF16)
    newest = ([jnp.concatenate([head(kn_all, h), pad], axis=0) for h in range(SB_HEADS)],
              [jnp.concatenate([head(vn_all, h), pad], axis=0) for h in range(SB_HEADS)])
    t = lax.broadcasted_iota(jnp.int32, (m, CACHE_BLOCK + LANES), 0) % n
    s = lax.broadcasted_iota(jnp.int32, (m, CACHE_BLOCK + LANES), 1) - CACHE_BLOCK
    wait(b, nblk - 1, b % 2)
    sweep(b % 2, newest, _suffix_matrix(CACHE_BLOCK + LANES), s < t, True)

    suffix = _suffix_matrix(CACHE_BLOCK)

    def more(state):
        jj, carry = state
        return (jj < nblk) & (carry < SB_UNDERFLOW_BITS)

    def body(state):
        jj, _ = state
        blk = nblk - 1 - jj
        slot = slot_of(jj)
        wait(b, blk, slot)

        @pl.when(blk > 0)
        def _():
            start(b, blk - 1, slot_of(jj + 1))

        sweep(slot, None, suffix, None, False)
        return jj + 1, jnp.min(c_ref[...])

    state = (jnp.int32(1), jnp.min(c_ref[...]))

    if nblk > 1:
        @pl.when(more(state))
        def _():
            start(b, nblk - 2, slot_of(1))

    done, _ = lax.while_loop(more, body, state)

    @pl.when((done > 1) & (done < nblk))
    def _():
        wait(b, nblk - 1 - done, slot_of(done))

    for h in range(SB_HEADS):
        o_ref[:, h, :] = acc_ref[h]


def _sample_attn(qs, kn, vn, cache_k, cache_v, layer):
    db, n = qs.shape[:2]
    past = cache_k.shape[-1]
    assert n <= LANES and past >= CACHE_BLOCK and past % CACHE_BLOCK == 0, (n, past)
    m = SB_HEADS * n
    new_spec = pl.BlockSpec((None, n, SB_WIDTH), lambda b: (b, 0, 0))
    hbm = pl.BlockSpec(memory_space=pl.ANY)
    buf = pltpu.VMEM((4, SB_HEADS, SB_HEAD_DIM, CACHE_BLOCK), F32)
    return pl.pallas_call(
        functools.partial(_sample_attn_body, layer=layer),
        grid=(db,),
        in_specs=[new_spec, new_spec, new_spec, hbm, hbm],
        out_specs=pl.BlockSpec((None, n, SB_HEADS, SB_HEAD_DIM), lambda b: (b, 0, 0, 0)),
        out_shape=jax.ShapeDtypeStruct((db, n, SB_HEADS, SB_HEAD_DIM), F32),
        scratch_shapes=[buf, buf, pltpu.SemaphoreType.DMA((4, 2)),
                        pltpu.VMEM((m, 1), F32),
                        pltpu.VMEM((SB_HEADS, n, SB_HEAD_DIM), F32)],
        compiler_params=pltpu.CompilerParams(
            dimension_semantics=("arbitrary",), vmem_limit_bytes=VMEM_LIMIT),
        name="sample_attn",
    )(qs, kn, vn, cache_k, cache_v)


def _route(logits):
    lane = lax.broadcasted_iota(jnp.int32, logits.shape, 1).astype(F32)
    neg = jnp.float32(-jnp.inf)
    big = jnp.float32(1 << 20)
    rmax = lambda a: jnp.max(a, axis=-1, keepdims=True)
    rmin = lambda a: jnp.min(a, axis=-1, keepdims=True)
    rsum = lambda a: jnp.sum(a, axis=-1, keepdims=True)

    gmask = lane < N_EXPERT_GROUPS
    gl = jnp.where(gmask, logits, neg)
    gmax = rmax(gl)
    g_sel = rmin(jnp.where(gl == gmax, lane, big))
    p_sel = 1.0 / rsum(jnp.where(gmask, jnp.exp(gl - gmax), 0.0))

    e_lo = N_EXPERT_GROUPS + g_sel * EXPERTS_PER_GROUP
    emask = (lane >= e_lo) & (lane < e_lo + EXPERTS_PER_GROUP)
    el = jnp.where(emask, logits, neg)
    emax = rmax(el)
    ex = jnp.where(emask, jnp.exp(el - emax), 0.0)
    p_e = ex / rsum(ex)
    pm = jnp.where(emask, p_e, -1.0)
    v1 = rmax(pm)
    i1 = rmin(jnp.where(pm == v1, lane, big))
    pm2 = jnp.where(lane == i1, -1.0, pm)
    v2 = rmax(pm2)
    i2 = rmin(jnp.where(pm2 == v2, lane, big))
    tot = v1 + v2
    gate1 = p_sel * v1 / tot
    gate2 = p_sel * v2 / tot
    comb = jnp.where(lane == i1, gate1, 0.0) + jnp.where(lane == i2, gate2, 0.0)
    gates = jnp.where(lane == i1 - e_lo, gate1, 0.0) + jnp.where(lane == i2 - e_lo, gate2, 0.0)
    return comb, g_sel, gates


def _rmsnorm_rows(h, g):
    return h * lax.rsqrt(jnp.mean(h * h, axis=-1, keepdims=True) + EPS) * g


def _slot_matrix(slot):
    lane = lax.broadcasted_iota(jnp.int32, (slot.shape[0], MOE_SLOTS), 1)
    return jnp.where(lane == slot.astype(jnp.int32), 1.0, 0.0)


def _post_attn_body(x_ref, gm_ref, sb_ref, wo_ref, g2_ref, wr_ref, br_ref,
                    h_ref, route_ref, bucket_ref, gate_ref, cnt_ref):
    h = (x_ref[...]
         + jnp.dot(gm_ref[...], wo_ref[:GM_WIDTH, :], preferred_element_type=F32)
         + jnp.dot(sb_ref[...], wo_ref[GM_WIDTH:, :], preferred_element_type=F32))
    h_ref[...] = h
    hn = _rmsnorm_rows(h, g2_ref[...])
    hi = hn.astype(BF16)
    lo = (hn - hi.astype(F32)).astype(BF16)
    both = jnp.dot(hi, wr_ref[...], preferred_element_type=F32)
    logits = (both[:, :LANES] + jnp.dot(lo, wr_ref[:, :LANES], preferred_element_type=F32)
              + both[:, LANES:]) + br_ref[...]
    comb, g_sel, gates = _route(logits)

    subs = [slice(st * MOE_SUB, (st + 1) * MOE_SUB) for st in range(h.shape[0] // MOE_SUB)]
    lane = lax.broadcasted_iota(jnp.int32, comb.shape, 1).astype(F32)
    r = lax.broadcasted_iota(jnp.int32, (MOE_SUB, MOE_SUB), 0)
    c = lax.broadcasted_iota(jnp.int32, (MOE_SUB, MOE_SUB), 1)
    earlier = jnp.where(r > c, 1.0, 0.0).astype(BF16)
    onehot = jnp.where(lane == g_sel, 1.0, 0.0)
    onehot_b = onehot.astype(BF16)
    before = jnp.concatenate(
        [jnp.dot(earlier, onehot_b[rows], preferred_element_type=F32) for rows in subs], axis=0)
    rank = jnp.sum(onehot * before, axis=-1, keepdims=True)
    slot = jnp.where(rank < MOE_CAP, g_sel * MOE_CAP + rank, -1.0)
    route_ref[...] = jnp.where(lane == SLOT_LANE, slot, comb)
    g1 = gates.astype(BF16)
    g2 = (gates - g1.astype(F32)).astype(BF16)
    g3 = (gates - g1.astype(F32) - g2.astype(F32)).astype(BF16)
    g12 = jnp.concatenate([g1, g2], axis=1)
    for st, rows in enumerate(subs):
        place = _slot_matrix(slot[rows]).T.astype(BF16)
        bucket_ref[st] = jnp.dot(place, hi[rows], preferred_element_type=F32).astype(BF16)
        placed = jnp.dot(place, g12[rows], preferred_element_type=F32)
        gate_ref[st] = (placed[:, :LANES] + placed[:, LANES:]
                        + jnp.dot(place, g3[rows], preferred_element_type=F32))
        cnt_ref[st] = jnp.broadcast_to(jnp.sum(onehot[rows], axis=0, keepdims=True), cnt_ref.shape[1:])


def _moe_ffn_body(x_ref, gate_ref, xs_ref, gates_ref, wg_ref, wu_ref, wd_ref, y_ref, ys_ref,
                  wgb_ref, wub_ref, wdb_ref, *, prompt_steps):
    s = pl.program_id(1)

    @pl.when(s == 0)
    def _():
        wgb_ref[...] = wg_ref[...].astype(BF16)
        wub_ref[...] = wu_ref[...].astype(BF16)
        wdb_ref[...] = wd_ref[...].astype(BF16)

    def ffn(ins, outs):
        x = jnp.concatenate([x_ref[...].reshape(-1, D_MODEL) for x_ref, _ in ins], axis=0)
        gates = jnp.concatenate([g_ref[...].reshape(-1, LANES) for _, g_ref in ins], axis=0)
        parts = []
        for e in range(EXPERTS_PER_GROUP):
            hg = jnp.dot(x, wgb_ref[e], preferred_element_type=F32)
            hu = jnp.dot(x, wub_ref[e], preferred_element_type=F32)
            parts.append((hg / (1.0 + jnp.exp(-hg)) * hu * gates[:, e:e + 1]).astype(BF16))
        a = jnp.concatenate(parts, axis=-1)
        y = jnp.dot(a, wdb_ref[...].reshape(GROUP_FF, D_MODEL), preferred_element_type=F32)
        row = 0
        for out_ref in outs:
            rows = out_ref.shape[0] * out_ref.shape[1] * out_ref.shape[2]
            out_ref[...] = y[row:row + rows].reshape(out_ref.shape)
            row += rows

    @pl.when(s < prompt_steps - 1)
    def _():
        ffn([(x_ref, gate_ref)], [y_ref])

    @pl.when(s == prompt_steps - 1)
    def _():
        ffn([(x_ref, gate_ref), (xs_ref, gates_ref)], [y_ref, ys_ref])


def _moe_combine_body(h_ref, route_ref, ys_ref, o_ref):
    for st in range(h_ref.shape[0] // MOE_SUB):
        rows = slice(st * MOE_SUB, (st + 1) * MOE_SUB)
        pick = _slot_matrix(route_ref[rows, SLOT_LANE:SLOT_LANE + 1]).astype(BF16)
        ys = ys_ref[st]
        hi = ys.astype(BF16)
        lo = (ys - hi.astype(F32)).astype(BF16)
        o_ref[rows, :] = (h_ref[rows, :] + jnp.dot(pick, hi, preferred_element_type=F32)
                          + jnp.dot(pick, lo, preferred_element_type=F32))


def _moe_dense_body(h_ref, route_ref, g2_ref, wg_ref, wu_ref, wd_ref, o_ref, hn_ref):
    g = pl.program_id(1)

    @pl.when(g == 0)
    def _():
        h = h_ref[...]
        o_ref[...] = h
        hn_ref[...] = _rmsnorm_rows(h, g2_ref[...]).astype(BF16)

    hn = hn_ref[...]
    comb = route_ref[...]
    lane = lax.broadcasted_iota(jnp.int32, comb.shape, 1)
    parts = []
    for e in range(EXPERTS_PER_GROUP):
        gate = jnp.sum(jnp.where(lane == N_EXPERT_GROUPS + g * EXPERTS_PER_GROUP + e, comb, 0.0),
                       axis=-1, keepdims=True)
        hg = jnp.dot(hn, wg_ref[e], preferred_element_type=F32)
        hu = jnp.dot(hn, wu_ref[e], preferred_element_type=F32)
        parts.append((hg / (1.0 + jnp.exp(-hg)) * hu * gate).astype(BF16))
    a = jnp.concatenate(parts, axis=-1)
    o_ref[...] += jnp.dot(a, wd_ref[...].reshape(GROUP_FF, D_MODEL), preferred_element_type=F32)


def _group_spec(a, index):
    return pl.BlockSpec((EXPERTS_PER_GROUP,) + a.shape[1:], index)


def _post_attn(x2d, gm, sb, params, tm):
    n = x2d.shape[0]
    wo, g2, wr, br = params
    assert n % tm == 0 and tm % MOE_SUB == 0, (n, tm)
    n_sub, sub = n // MOE_SUB, tm // MOE_SUB
    row = lambda w: pl.BlockSpec((tm, w), lambda i: (i, 0))
    full = lambda a: pl.BlockSpec(a.shape, lambda i: (0,) * a.ndim)
    per_sub = lambda rows, w: pl.BlockSpec((sub, rows, w), lambda i: (i, 0, 0))
    return pl.pallas_call(
        _post_attn_body,
        grid=(n // tm,),
        in_specs=[row(D_MODEL), row(GM_WIDTH), row(SB_WIDTH),
                  full(wo), full(g2), full(wr), full(br)],
        out_specs=[row(D_MODEL), row(LANES), per_sub(MOE_SLOTS, D_MODEL),
                   per_sub(MOE_SLOTS, LANES), per_sub(8, LANES)],
        out_shape=[jax.ShapeDtypeStruct((n, D_MODEL), F32),
                   jax.ShapeDtypeStruct((n, LANES), F32),
                   jax.ShapeDtypeStruct((n_sub, MOE_SLOTS, D_MODEL), BF16),
                   jax.ShapeDtypeStruct((n_sub, MOE_SLOTS, LANES), F32),
                   jax.ShapeDtypeStruct((n_sub, 8, LANES), F32)],
        compiler_params=pltpu.CompilerParams(
            dimension_semantics=("parallel",), vmem_limit_bytes=VMEM_LIMIT),
        name="post_attn",
    )(x2d, gm, sb, wo, g2, wr, br)


def _out_moe(prompt, sample, params, *, tm_prompt, tm_sample):
    wo, g2, wr, br, wg, wu, wd = params
    routed = [_post_attn(*grp, (wo, g2, wr, br), tm) for grp, tm in
              ((prompt, tm_prompt), (sample, tm_sample))]
    tms = (tm_prompt, tm_sample)
    params_2d = pltpu.CompilerParams(
        dimension_semantics=("parallel", "arbitrary"), vmem_limit_bytes=VMEM_LIMIT)

    def sparse():
        (_, _, bk_p, gt_p, _), (_, _, bk_s, gt_s, _) = routed
        n_sub, n_sub_s = bk_p.shape[0], bk_s.shape[0]
        sup = min(MOE_SUPER, n_sub)
        assert n_sub % sup == 0, (n_sub, sup)
        steps = n_sub // sup
        by_group = lambda a: a.reshape(a.shape[0], N_EXPERT_GROUPS, MOE_CAP, a.shape[-1])
        bucket = lambda w: pl.BlockSpec((sup, 1, MOE_CAP, w), lambda g, s: (s, g, 0, 0))
        bucket_s = lambda w: pl.BlockSpec((n_sub_s, 1, MOE_CAP, w), lambda g, s: (0, g, 0, 0))
        ys = pl.pallas_call(
            functools.partial(_moe_ffn_body, prompt_steps=steps),
            grid=(N_EXPERT_GROUPS, steps),
            in_specs=[bucket(D_MODEL), bucket(LANES), bucket_s(D_MODEL), bucket_s(LANES)]
                     + [_group_spec(w, lambda g, s: (g, 0, 0)) for w in (wg, wu, wd)],
            out_specs=[bucket(D_MODEL), bucket_s(D_MODEL)],
            out_shape=[jax.ShapeDtypeStruct((n, N_EXPERT_GROUPS, MOE_CAP, D_MODEL), F32)
                       for n in (n_sub, n_sub_s)],
            scratch_shapes=[pltpu.VMEM((EXPERTS_PER_GROUP,) + w.shape[1:], BF16) for w in (wg, wu, wd)],
            compiler_params=params_2d,
            name="moe_ffn",
        )(by_group(bk_p), by_group(gt_p), by_group(bk_s), by_group(gt_s), wg, wu, wd)

        def combine(h, route, y, tm):
            n, sub = h.shape[0], tm // MOE_SUB
            row = lambda w: pl.BlockSpec((tm, w), lambda i: (i, 0))
            return pl.pallas_call(
                _moe_combine_body,
                grid=(n // tm,),
                in_specs=[row(D_MODEL), row(LANES),
                          pl.BlockSpec((sub, MOE_SLOTS, D_MODEL), lambda i: (i, 0, 0))],
                out_specs=row(D_MODEL),
                out_shape=jax.ShapeDtypeStruct((n, D_MODEL), F32),
                compiler_params=pltpu.CompilerParams(
                    dimension_semantics=("parallel",), vmem_limit_bytes=VMEM_LIMIT),
                name="moe_combine",
            )(h, route, y.reshape(-1, MOE_SLOTS, D_MODEL))

        return tuple(combine(r[0], r[1], y, tm) for r, y, tm in zip(routed, ys, tms))

    def dense():
        wgb, wub, wdb = (w.astype(BF16) for w in (wg, wu, wd))

        def all_experts(h, route, tm):
            row2 = lambda w: pl.BlockSpec((tm, w), lambda i, g: (i, 0))
            return pl.pallas_call(
                _moe_dense_body,
                grid=(h.shape[0] // tm, N_EXPERT_GROUPS),
                in_specs=[row2(D_MODEL), row2(LANES), pl.BlockSpec(g2.shape, lambda i, g: (0, 0))]
                         + [_group_spec(w, lambda i, g: (g, 0, 0)) for w in (wg, wu, wd)],
                out_specs=row2(D_MODEL),
                out_shape=jax.ShapeDtypeStruct(h.shape, F32),
                scratch_shapes=[pltpu.VMEM((tm, D_MODEL), BF16)],
                compiler_params=params_2d,
                name="moe_dense",
            )(h, route, g2, wgb, wub, wdb)

        return tuple(all_experts(r[0], r[1], tm) for r, tm in zip(routed, tms))

    most = jnp.maximum(jnp.max(routed[0][4]), jnp.max(routed[1][4]))
    return lax.cond(most <= MOE_CAP, sparse, dense)


def _layer(layer, xp, xs, cache_k, cache_v, norm1_g, w_in, gm_v_norm_g, gm_w_s, gm_b_s, q_norm_g,
           k_norm_g, w_out, norm2_g, w_router_group, b_router_group, w_router_expert,
           b_router_expert, w_gate, w_up, w_down):
    b, s, d = xp.shape
    db, n, _ = xs.shape
    head_of = jnp.arange(SB_WIDTH) // SB_HEAD_DIM
    bd = jnp.where(head_of[:, None] == head_of[None, :], 1.0 / SB_HEAD_DIM, 0.0).astype(BF16)
    in_params = (norm1_g[None, :], w_in.astype(BF16), gm_v_norm_g[None, :],
                 jnp.tile(q_norm_g, SB_HEADS)[None, :], jnp.tile(k_norm_g, SB_HEADS)[None, :],
                 gm_w_s, gm_b_s.T, bd)

    wr = jnp.concatenate(
        [w_router_group, jnp.transpose(w_router_expert, (1, 0, 2)).reshape(d, N_EXPERTS)], axis=1)
    wr = jnp.pad(wr, ((0, 0), (0, LANES - wr.shape[1])))
    wr_hi = wr.astype(BF16)
    wr_lo = (wr - wr_hi.astype(F32)).astype(BF16)
    br = jnp.pad(jnp.concatenate([b_router_group, b_router_expert.reshape(-1)]),
                 (0, LANES - N_EXPERT_GROUPS - N_EXPERTS))[None, :]

    moe_params = (w_out.astype(BF16), norm2_g[None, :], jnp.concatenate([wr_hi, wr_lo], axis=1), br,
                  w_gate, w_up, w_down)

    xp2 = xp.reshape(b * s, d)
    gm, kf, vf, qs, kb, vb = _mixer_in(xp2, in_params, tm=min(1024, s), chunk=GM_CHUNK, stream=False,
                                       seq=s)
    shp = (b, s, SB_WIDTH)
    sb = _prompt_attn(qs.reshape(shp), kb, vb.reshape(shp)).reshape(b * s, SB_WIDTH)

    xs2 = xs.reshape(db * n, d)
    gm_s, kf_s, vf_s, q_s, gv_s = _mixer_in(xs2, in_params, tm=db * n, chunk=n, stream=True)
    shs = (db, n, SB_WIDTH)
    frame_minor = lambda c: jnp.transpose(c, (0, 1, 3, 4, 2))
    sb_s = _sample_attn(q_s.reshape(shs), kf_s.reshape(shs), vf_s.reshape(shs),
                        frame_minor(cache_k), frame_minor(cache_v), layer)
    sb_s = sb_s.reshape(db * n, SB_WIDTH).astype(BF16)

    yp, ys = _out_moe((xp2, gm, sb), (xs2, gm_s, sb_s), moe_params,
                      tm_prompt=min(1024, b * s), tm_sample=db * n)
    yp, ys = yp.reshape(b, s, d), ys.reshape(db, n, d)

    heads = (SB_HEADS, SB_HEAD_DIM)
    rows = lambda a: jnp.transpose(a.reshape(b, *heads, s), (0, 3, 1, 2))
    return (yp, ys, rows(kf), rows(vf),
            kf_s.reshape(db, n, *heads), vf_s.reshape(db, n, *heads),
            gv_s.reshape(db, n, GM_GROUPS, GM_GROUP_DIM))


def kernel(x_prompt, x_sample, cache_sb_k, cache_sb_v, norm1_g, w_in, gm_v_norm_g, gm_w_s, gm_b_s, q_norm_g, k_norm_g, w_out, norm2_g, w_router_group, b_router_group, w_router_expert, b_router_expert, w_gate, w_up, w_down):
    depth = w_in.shape[0]
    yp, ys = x_prompt, x_sample
    outs = [[] for _ in range(5)]
    for l in range(depth):
        yp, ys, *rest = _layer(
            l, yp, ys, cache_sb_k, cache_sb_v, norm1_g[l], w_in[l], gm_v_norm_g[l], gm_w_s[l],
            gm_b_s[l], q_norm_g[l], k_norm_g[l], w_out[l], norm2_g[l], w_router_group[l],
            b_router_group[l], w_router_expert[l], b_router_expert[l], w_gate[l], w_up[l],
            w_down[l])
        for acc, r in zip(outs, rest):
            acc.append(r)
    return (yp, ys) + tuple(jnp.stack(o, axis=0) for o in outs)
```

```python
import functools

import jax
import jax.numpy as jnp
from jax import lax
from jax.experimental import pallas as pl
from jax.experimental.pallas import tpu as pltpu

D_MODEL = 1024
GM_WIDTH = 512
GM_GROUPS = 4
GM_GROUP_DIM = 128
GM_CHUNK = 128
SB_WIDTH = 512
SB_HEAD_DIM = 64
SB_HEADS = 8
N_EXPERT_GROUPS = 4
EXPERTS_PER_GROUP = 4
N_EXPERTS = 16
EXPERT_FF = 256
GROUP_FF = EXPERTS_PER_GROUP * EXPERT_FF
EPS = 1e-6

LANES = 128
HEAD_PAIRS = SB_WIDTH // LANES
SB_BLOCK = 128
SB_WIDE = 2 * SB_BLOCK
SB_FIRST = 3 * SB_BLOCK
SB_PER_STEP = 4
CACHE_BLOCK = 256
MOE_SUB = 256
MOE_CAP = 96
MOE_SHORT = 80
MOE_SLOTS = N_EXPERT_GROUPS * MOE_CAP
MOE_SUPER = 8
SLOT_LANE = 0
VMEM_LIMIT = 56 * 1024 * 1024

LOG2_E = 1.4426950408889634
Q_SCALE = SB_HEAD_DIM ** -0.5 * LOG2_E
SB_UNDERFLOW_BITS = 160.0

F32 = jnp.float32
BF16 = jnp.bfloat16
_NT = (((1,), (1,)), ((), ()))


def _gelu_tanh(x):
    return 0.5 * x * (1.0 + jnp.tanh(0.7978845608028654 * (x + 0.044715 * (x * x * x))))


def _softplus2(z):
    return jnp.maximum(z, 0.0) + jnp.log2(1.0 + jnp.exp2(-jnp.abs(z)))


def _suffix_matrix(n):
    r = lax.broadcasted_iota(jnp.int32, (n, n), 0)
    c = lax.broadcasted_iota(jnp.int32, (n, n), 1)
    return jnp.where(r > c, 1.0, 0.0).astype(BF16)


def _mixer_in_body(x_ref, g1_ref, w_in_ref, gvg_ref, qg_ref, kg_ref, ws_ref, bst_ref, bd_ref,
                   gm_ref, kf_ref, vf_ref, *rest, chunk, stream):
    x = x_ref[...]
    tm = x.shape[0]
    xn = x * lax.rsqrt(jnp.mean(x * x, axis=-1, keepdims=True) + EPS) * g1_ref[...]
    proj = jnp.dot(xn.astype(BF16), w_in_ref[...], preferred_element_type=F32)

    gu = _gelu_tanh(proj[:, :GM_WIDTH])
    gvr = _gelu_tanh(proj[:, GM_WIDTH:2 * GM_WIDTH])
    gv = gvr * lax.rsqrt(jnp.mean(gvr * gvr, axis=-1, keepdims=True) + EPS) * gvg_ref[...]

    o = 2 * GM_WIDTH
    q = proj[:, o:o + SB_WIDTH]
    k = proj[:, o + SB_WIDTH:o + 2 * SB_WIDTH]
    v = proj[:, o + 2 * SB_WIDTH:]
    bd = bd_ref[...]
    qms = jnp.dot((q * q).astype(BF16), bd, preferred_element_type=F32)
    kms = jnp.dot((k * k).astype(BF16), bd, preferred_element_type=F32)
    qs = q * lax.rsqrt(qms + EPS) * qg_ref[...] * Q_SCALE
    kn = k * lax.rsqrt(kms + EPS) * kg_ref[...]
    if stream:
        kf_ref[...] = kn
        vf_ref[...] = v
        rest[0][...] = qs
        rest[1][...] = gv
    else:
        kt = kn.T
        kf_ref[...] = kt
        vf_ref[...] = v.T
        qs_ref, kb_ref, vb_ref = rest
        qs_ref[...] = qs.astype(BF16)
        kb_ref[...] = kt.astype(BF16)
        vb_ref[...] = v.astype(BF16)

    r = lax.broadcasted_iota(jnp.int32, (chunk, chunk), 0)
    c = lax.broadcasted_iota(jnp.int32, (chunk, chunk), 1)
    gvb = gv.astype(BF16)
    chunks = [slice(ci * chunk, (ci + 1) * chunk) for ci in range(tm // chunk)]
    for g in range(GM_GROUPS):
        w = jnp.where(r >= c, ws_ref[g, :chunk, :chunk], 0.0).astype(BF16)
        b = bst_ref[:chunk, g:g + 1]
        cols = slice(g * GM_GROUP_DIM, (g + 1) * GM_GROUP_DIM)
        mixed = jnp.dot(w, jnp.concatenate([gvb[rows, cols] for rows in chunks], axis=1),
                        preferred_element_type=F32)
        for ci, rows in enumerate(chunks):
            part = mixed[:, ci * GM_GROUP_DIM:(ci + 1) * GM_GROUP_DIM] + b
            gm_ref[rows, cols] = (gu[rows, cols] * part).astype(BF16)


def _mixer_in(x2d, params, *, tm, chunk, stream, seq=None):
    n = x2d.shape[0]
    assert n % tm == 0 and tm % chunk == 0 and chunk <= GM_CHUNK, (n, tm, chunk)
    full = lambda a: pl.BlockSpec(a.shape, lambda i: (0,) * a.ndim)
    spec = lambda s: pl.BlockSpec((tm,) + s.shape[1:], lambda i: (i,) + (0,) * (len(s.shape) - 1))
    gm = jax.ShapeDtypeStruct((n, GM_WIDTH), BF16)
    if stream:
        rows = jax.ShapeDtypeStruct((n, SB_WIDTH), F32)
        out_shape = [gm, rows, rows,
                     rows,
                     jax.ShapeDtypeStruct((n, GM_WIDTH), F32)]
        out_specs = [spec(s) for s in out_shape]
    else:
        per_seq = seq // tm
        rows_t = jax.ShapeDtypeStruct((n // seq, SB_WIDTH, seq), F32)
        spec_t = pl.BlockSpec((None, SB_WIDTH, tm), lambda i: (i // per_seq, 0, i % per_seq))
        dense = jax.ShapeDtypeStruct((n, SB_WIDTH), BF16)
        dense_t = jax.ShapeDtypeStruct(rows_t.shape, BF16)
        out_shape = [gm, rows_t, rows_t, dense, dense_t, dense]
        out_specs = [spec(gm), spec_t, spec_t, spec(dense), spec_t, spec(dense)]
    return pl.pallas_call(
        functools.partial(_mixer_in_body, chunk=chunk, stream=stream),
        grid=(n // tm,),
        in_specs=[pl.BlockSpec((tm, D_MODEL), lambda i: (i, 0))] + [full(a) for a in params],
        out_specs=out_specs,
        out_shape=out_shape,
        compiler_params=pltpu.CompilerParams(
            dimension_semantics=("parallel",), vmem_limit_bytes=VMEM_LIMIT),
        name="mixer_in_stream" if stream else "mixer_in",
    )(x2d, *params)


def _prompt_attn_body(q_ref, k_ref, v_ref, sfx_ref, sfx_wide_ref, sfx_first_ref, o_ref, *scratch):
    def one(sub, carry):
        rows = pl.ds(pl.multiple_of(sub * SB_BLOCK, SB_BLOCK), SB_BLOCK)
        _prompt_attn_block(pl.program_id(1) * SB_PER_STEP + sub, q_ref.at[rows], k_ref, v_ref, sfx_ref,
                           sfx_wide_ref, sfx_first_ref, o_ref.at[rows], *scratch)
        return carry

    lax.fori_loop(0, SB_PER_STEP, one, 0)


def _prompt_attn_block(i, q_ref, k_ref, v_ref, sfx_ref, sfx_wide_ref, sfx_first_ref, o_ref,
                       qst_ref, c_ref, acc_ref):
    lane = lax.broadcasted_iota(jnp.int32, (SB_BLOCK, LANES), 1)
    first = lane < SB_HEAD_DIM
    for hp in range(HEAD_PAIRS):
        q = q_ref[:, hp * LANES:(hp + 1) * LANES]
        zero = jnp.zeros_like(q)
        qst_ref[hp, :SB_BLOCK, :] = jnp.where(first, q, zero)
        qst_ref[hp, SB_BLOCK:, :] = jnp.where(first, zero, q)
    m = 2 * SB_BLOCK
    suffix = {SB_BLOCK: sfx_ref, SB_WIDE: sfx_wide_ref, SB_FIRST: sfx_first_ref}

    def sweep(j, keys, causal, fresh):
        rows = pl.ds(pl.multiple_of(j * SB_BLOCK, SB_BLOCK), keys)
        cols = lambda hp: slice(hp * LANES, (hp + 1) * LANES)
        z = jnp.concatenate(
            [jnp.dot(qst_ref[hp], k_ref[cols(hp), rows], preferred_element_type=F32)
             for hp in range(HEAD_PAIRS)], axis=0)
        def masked(a):
            if causal is None:
                return a
            old = keys - SB_BLOCK
            newest = jnp.where(causal, a[:, old:], 0.0)
            return newest if old == 0 else jnp.concatenate([a[:, :old], newest], axis=1)

        sp = masked(_softplus2(z))
        c0 = jnp.zeros((HEAD_PAIRS * m, 1), F32) if fresh else c_ref[...]
        later = jnp.dot(sp.astype(BF16), suffix[keys][...], preferred_element_type=F32) + c0
        w = masked(jnp.exp2(z - sp - later))
        wb = w.astype(BF16)
        pv = jnp.concatenate(
            [jnp.dot(wb[hp * m:(hp + 1) * m, :], v_ref[rows, cols(hp)], preferred_element_type=F32)
             for hp in range(HEAD_PAIRS)], axis=0)
        acc_ref[...] = pv if fresh else acc_ref[...] + pv
        c_ref[...] = c0 + jnp.sum(sp, axis=-1, keepdims=True)

    below = SB_FIRST // SB_BLOCK - 1
    t = lax.broadcasted_iota(jnp.int32, (HEAD_PAIRS * m, SB_BLOCK), 0) % SB_BLOCK
    s = lax.broadcasted_iota(jnp.int32, (HEAD_PAIRS * m, SB_BLOCK), 1)

    @pl.when(i >= below)
    def _():
        sweep(i - below, SB_FIRST, s < t, True)

    @pl.when(i < below)
    def _():
        sweep(i, SB_BLOCK, s < t, True)

    j0 = jnp.where(i >= below, i - below, i)

    def least_carry():
        return jnp.min(c_ref[...])

    per_wide = SB_WIDE // SB_BLOCK
    n_wide = j0 // per_wide

    def more(state):
        jj, carry = state
        return (jj < n_wide) & (carry < SB_UNDERFLOW_BITS)

    def body(state):
        jj, _ = state
        sweep(j0 - per_wide * (jj + 1), SB_WIDE, None, False)
        return jj + 1, least_carry()

    done, carry = lax.while_loop(more, body, (jnp.int32(0), least_carry()))

    @pl.when((done == n_wide) & (j0 % per_wide == 1) & (carry < SB_UNDERFLOW_BITS))
    def _():
        sweep(0, SB_BLOCK, None, False)

    for hp in range(HEAD_PAIRS):
        lo = hp * m
        o_ref[:, hp * LANES:(hp + 1) * LANES] = jnp.where(
            first, acc_ref[lo:lo + SB_BLOCK, :], acc_ref[lo + SB_BLOCK:lo + m, :]).astype(o_ref.dtype)


def _prompt_attn(qs, kt, vb):
    b, s, _ = qs.shape
    assert s % SB_BLOCK == 0, s
    sfx = tuple(_suffix_matrix(keys) for keys in (SB_BLOCK, SB_WIDE, SB_FIRST))
    per_step = SB_PER_STEP * SB_BLOCK
    assert s % per_step == 0, s
    q_spec = pl.BlockSpec((None, per_step, SB_WIDTH), lambda bi, i: (bi, i, 0))
    kt_spec = pl.BlockSpec((None, SB_WIDTH, s), lambda bi, i: (bi, 0, 0))
    kv_spec = pl.BlockSpec((None, s, SB_WIDTH), lambda bi, i: (bi, 0, 0))
    m = 2 * SB_BLOCK
    return pl.pallas_call(
        _prompt_attn_body,
        grid=(b, s // per_step),
        in_specs=[q_spec, kt_spec, kv_spec] + [pl.BlockSpec(a.shape, lambda bi, i: (0, 0)) for a in sfx],
        out_specs=q_spec,
        out_shape=jax.ShapeDtypeStruct((b, s, SB_WIDTH), BF16),
        scratch_shapes=[pltpu.VMEM((HEAD_PAIRS, m, LANES), BF16),
                        pltpu.VMEM((HEAD_PAIRS * m, 1), F32),
                        pltpu.VMEM((HEAD_PAIRS * m, LANES), F32)],
        compiler_params=pltpu.CompilerParams(
            dimension_semantics=("parallel", "arbitrary"), vmem_limit_bytes=VMEM_LIMIT),
        name="prompt_attn",
    )(qs, kt, vb, *sfx)


def _sample_attn_body(q_ref, kn_ref, vn_ref, ck_hbm, cv_hbm, o_ref, kbuf, vbuf, sems, c_ref, acc_ref,
                      *, layer):
    b = pl.program_id(0)
    n = q_ref.shape[0]
    m = SB_HEADS * n
    nblk = ck_hbm.shape[-1] // CACHE_BLOCK

    def copies(stream, blk, slot):
        cols = pl.ds(blk * CACHE_BLOCK, CACHE_BLOCK)
        return (pltpu.make_async_copy(ck_hbm.at[layer, stream, :, :, cols], kbuf.at[slot], sems.at[slot, 0]),
                pltpu.make_async_copy(cv_hbm.at[layer, stream, :, :, cols], vbuf.at[slot], sems.at[slot, 1]))

    def start(stream, blk, slot):
        for cp in copies(stream, blk, slot):
            cp.start()

    def wait(stream, blk, slot):
        for cp in copies(stream, blk, slot):
            cp.wait()

    def slot_of(jj):
        return 2 + jj % 2

    @pl.when(b == 0)
    def _():
        start(0, nblk - 1, 0)

    @pl.when(b + 1 < pl.num_programs(0))
    def _():
        start(b + 1, nblk - 1, (b + 1) % 2)

    head = lambda a, h: a[:, h * SB_HEAD_DIM:(h + 1) * SB_HEAD_DIM]
    q_all = q_ref[...].astype(BF16)
    q = [head(q_all, h) for h in range(SB_HEADS)]

    def sweep(slot, newest, suffix, causal, fresh):
        def logits(h):
            z = jnp.dot(q[h], kbuf[slot, h].astype(BF16), preferred_element_type=F32)
            if newest is None:
                return z
            return jnp.concatenate(
                [z, lax.dot_general(q[h], newest[0][h], _NT, preferred_element_type=F32)], axis=1)

        z = jnp.concatenate([logits(h) for h in range(SB_HEADS)], axis=0)
        sp = _softplus2(z)
        if causal is not None:
            sp = jnp.where(causal, sp, 0.0)
        c0 = jnp.zeros((m, 1), F32) if fresh else c_ref[...]
        later = jnp.dot(sp.astype(BF16), suffix, preferred_element_type=F32) + c0
        w = jnp.exp2(z - sp - later)
        if causal is not None:
            w = jnp.where(causal, w, 0.0)
        wb = w.astype(BF16)
        pvs = []
        for h in range(SB_HEADS):
            wh = wb[h * n:(h + 1) * n, :]
            pv = lax.dot_general(wh[:, :CACHE_BLOCK], vbuf[slot, h].astype(BF16), _NT,
                                 preferred_element_type=F32)
            if newest is not None:
                pv = pv + jnp.dot(wh[:, CACHE_BLOCK:], newest[1][h], preferred_element_type=F32)
            pvs.append(pv)
        pv = jnp.stack(pvs)
        acc_ref[...] = pv if fresh else acc_ref[...] + pv
        c_ref[...] = c0 + jnp.sum(sp, axis=-1, keepdims=True)

    pad = jnp.zeros((LANES - n, SB_HEAD_DIM), BF16)
    kn_all, vn_all = kn_ref[...].astype(BF16), vn_ref[...].astype(BF16)
    newest = ([jnp.concatenate([head(kn_all, h), pad], axis=0) for h in range(SB_HEADS)],
              [jnp.concatenate([head(vn_all, h), pad], axis=0) for h in range(SB_HEADS)])
    t = lax.broadcasted_iota(jnp.int32, (m, CACHE_BLOCK + LANES), 0) % n
    s = lax.broadcasted_iota(jnp.int32, (m, CACHE_BLOCK + LANES), 1) - CACHE_BLOCK
    wait(b, nblk - 1, b % 2)
    sweep(b % 2, newest, _suffix_matrix(CACHE_BLOCK + LANES), s < t, True)

    suffix = _suffix_matrix(CACHE_BLOCK)

    def more(state):
        jj, carry = state
        return (jj < nblk) & (carry < SB_UNDERFLOW_BITS)

    def body(state):
        jj, _ = state
        blk = nblk - 1 - jj
        slot = slot_of(jj)
        wait(b, blk, slot)

        @pl.when(blk > 0)
        def _():
            start(b, blk - 1, slot_of(jj + 1))

        sweep(slot, None, suffix, None, False)
        return jj + 1, jnp.min(c_ref[...])

    state = (jnp.int32(1), jnp.min(c_ref[...]))

    if nblk > 1:
        @pl.when(more(state))
        def _():
            start(b, nblk - 2, slot_of(1))

    done, _ = lax.while_loop(more, body, state)

    @pl.when((done > 1) & (done < nblk))
    def _():
        wait(b, nblk - 1 - done, slot_of(done))

    for h in range(SB_HEADS):
        o_ref[:, h, :] = acc_ref[h]


def _sample_attn(qs, kn, vn, cache_k, cache_v, layer):
    db, n = qs.shape[:2]
    past = cache_k.shape[-1]
    assert n <= LANES and past >= CACHE_BLOCK and past % CACHE_BLOCK == 0, (n, past)
    m = SB_HEADS * n
    new_spec = pl.BlockSpec((None, n, SB_WIDTH), lambda b: (b, 0, 0))
    hbm = pl.BlockSpec(memory_space=pl.ANY)
    buf = pltpu.VMEM((4, SB_HEADS, SB_HEAD_DIM, CACHE_BLOCK), F32)
    return pl.pallas_call(
        functools.partial(_sample_attn_body, layer=layer),
        grid=(db,),
        in_specs=[new_spec, new_spec, new_spec, hbm, hbm],
        out_specs=pl.BlockSpec((None, n, SB_HEADS, SB_HEAD_DIM), lambda b: (b, 0, 0, 0)),
        out_shape=jax.ShapeDtypeStruct((db, n, SB_HEADS, SB_HEAD_DIM), F32),
        scratch_shapes=[buf, buf, pltpu.SemaphoreType.DMA((4, 2)),
                        pltpu.VMEM((m, 1), F32),
                        pltpu.VMEM((SB_HEADS, n, SB_HEAD_DIM), F32)],
        compiler_params=pltpu.CompilerParams(
            dimension_semantics=("arbitrary",), vmem_limit_bytes=VMEM_LIMIT),
        name="sample_attn",
    )(qs, kn, vn, cache_k, cache_v)


def _route(logits):
    lane = lax.broadcasted_iota(jnp.int32, logits.shape, 1).astype(F32)
    neg = jnp.float32(-jnp.inf)
    big = jnp.float32(1 << 20)
    rmax = lambda a: jnp.max(a, axis=-1, keepdims=True)
    rmin = lambda a: jnp.min(a, axis=-1, keepdims=True)
    rsum = lambda a: jnp.sum(a, axis=-1, keepdims=True)

    gmask = lane < N_EXPERT_GROUPS
    gl = jnp.where(gmask, logits, neg)
    gmax = rmax(gl)
    g_sel = rmin(jnp.where(gl == gmax, lane, big))
    p_sel = 1.0 / rsum(jnp.where(gmask, jnp.exp(gl - gmax), 0.0))

    e_lo = N_EXPERT_GROUPS + g_sel * EXPERTS_PER_GROUP
    emask = (lane >= e_lo) & (lane < e_lo + EXPERTS_PER_GROUP)
    el = jnp.where(emask, logits, neg)
    emax = rmax(el)
    ex = jnp.where(emask, jnp.exp(el - emax), 0.0)
    p_e = ex / rsum(ex)
    pm = jnp.where(emask, p_e, -1.0)
    v1 = rmax(pm)
    i1 = rmin(jnp.where(pm == v1, lane, big))
    pm2 = jnp.where(lane == i1, -1.0, pm)
    v2 = rmax(pm2)
    i2 = rmin(jnp.where(pm2 == v2, lane, big))
    tot = v1 + v2
    gate1 = p_sel * v1 / tot
    gate2 = p_sel * v2 / tot
    comb = jnp.where(lane == i1, gate1, 0.0) + jnp.where(lane == i2, gate2, 0.0)
    gates = jnp.where(lane == i1 - e_lo, gate1, 0.0) + jnp.where(lane == i2 - e_lo, gate2, 0.0)
    return comb, g_sel, gates


def _rmsnorm_rows(h, g):
    return h * lax.rsqrt(jnp.mean(h * h, axis=-1, keepdims=True) + EPS) * g


def _slot_matrix(slot):
    lane = lax.broadcasted_iota(jnp.int32, (slot.shape[0], MOE_SLOTS), 1)
    return jnp.where(lane == slot.astype(jnp.int32), 1.0, 0.0)


def _post_attn_body(x_ref, gm_ref, sb_ref, wo_ref, g2_ref, wr_ref, br_ref,
                    h_ref, route_ref, bucket_ref, gate_ref, cnt_ref):
    h = (x_ref[...]
         + jnp.dot(gm_ref[...], wo_ref[:GM_WIDTH, :], preferred_element_type=F32)
         + jnp.dot(sb_ref[...], wo_ref[GM_WIDTH:, :], preferred_element_type=F32))
    h_ref[...] = h
    hn = _rmsnorm_rows(h, g2_ref[...])
    hi = hn.astype(BF16)
    lo = (hn - hi.astype(F32)).astype(BF16)
    both = jnp.dot(hi, wr_ref[...], preferred_element_type=F32)
    logits = (both[:, :LANES] + jnp.dot(lo, wr_ref[:, :LANES], preferred_element_type=F32)
              + both[:, LANES:]) + br_ref[...]
    comb, g_sel, gates = _route(logits)

    subs = [slice(st * MOE_SUB, (st + 1) * MOE_SUB) for st in range(h.shape[0] // MOE_SUB)]
    lane = lax.broadcasted_iota(jnp.int32, comb.shape, 1).astype(F32)
    r = lax.broadcasted_iota(jnp.int32, (MOE_SUB, MOE_SUB), 0)
    c = lax.broadcasted_iota(jnp.int32, (MOE_SUB, MOE_SUB), 1)
    earlier = jnp.where(r > c, 1.0, 0.0).astype(BF16)
    onehot = jnp.where(lane == g_sel, 1.0, 0.0)
    onehot_b = onehot.astype(BF16)
    before = jnp.concatenate(
        [jnp.dot(earlier, onehot_b[rows], preferred_element_type=F32) for rows in subs], axis=0)
    rank = jnp.sum(onehot * before, axis=-1, keepdims=True)
    slot = jnp.where(rank < MOE_CAP, g_sel * MOE_CAP + rank, -1.0)
    route_ref[...] = jnp.where(lane == SLOT_LANE, slot, comb)
    g1 = gates.astype(BF16)
    g2 = (gates - g1.astype(F32)).astype(BF16)
    g3 = (gates - g1.astype(F32) - g2.astype(F32)).astype(BF16)
    g12 = jnp.concatenate([g1, g2], axis=1)
    for st, rows in enumerate(subs):
        place = _slot_matrix(slot[rows]).T.astype(BF16)
        bucket_ref[st] = jnp.dot(place, hi[rows], preferred_element_type=F32).astype(BF16)
        placed = jnp.dot(place, g12[rows], preferred_element_type=F32)
        gate_ref[st] = (placed[:, :LANES] + placed[:, LANES:]
                        + jnp.dot(place, g3[rows], preferred_element_type=F32))
        cnt_ref[st] = jnp.broadcast_to(jnp.sum(onehot[rows], axis=0, keepdims=True), cnt_ref.shape[1:])


def _moe_ffn_body(long_ref, x_ref, gate_ref, xs_ref, gates_ref, wg_ref, wu_ref, wd_ref, y_ref, ys_ref,
                  wgb_ref, wub_ref, wdb_ref, *, prompt_steps):
    g, s = pl.program_id(0), pl.program_id(1)

    @pl.when(s == 0)
    def _():
        wgb_ref[...] = wg_ref[...].astype(BF16)
        wub_ref[...] = wu_ref[...].astype(BF16)
        wdb_ref[...] = wd_ref[...].astype(BF16)

    def ffn(ins, outs, used):
        x = jnp.concatenate([r[:, 0, :used, :].reshape(-1, D_MODEL) for r, _ in ins], axis=0)
        gates = jnp.concatenate([r[:, 0, :used, :].reshape(-1, LANES) for _, r in ins], axis=0)
        parts = []
        for e in range(EXPERTS_PER_GROUP):
            hg = jnp.dot(x, wgb_ref[e], preferred_element_type=F32)
            hu = jnp.dot(x, wub_ref[e], preferred_element_type=F32)
            parts.append((hg / (1.0 + jnp.exp(-hg)) * hu * gates[:, e:e + 1]).astype(BF16))
        a = jnp.concatenate(parts, axis=-1)
        y = jnp.dot(a, wdb_ref[...].reshape(GROUP_FF, D_MODEL), preferred_element_type=F32)
        row = 0
        for out_ref in outs:
            rows = out_ref.shape[0] * used
            out_ref[:, 0, :used, :] = y[row:row + rows].reshape(out_ref.shape[0], used, D_MODEL)
            if used < MOE_CAP:
                out_ref[:, 0, used:, :] = jnp.zeros((out_ref.shape[0], MOE_CAP - used, D_MODEL), F32)
            row += rows

    def step(used):
        @pl.when(s < prompt_steps - 1)
        def _():
            ffn([(x_ref, gate_ref)], [y_ref], used)

        @pl.when(s == prompt_steps - 1)
        def _():
            ffn([(x_ref, gate_ref), (xs_ref, gates_ref)], [y_ref, ys_ref], used)

    @pl.when(long_ref[g, s] == 0)
    def _():
        step(MOE_SHORT)

    @pl.when(long_ref[g, s] != 0)
    def _():
        step(MOE_CAP)


def _moe_combine_body(h_ref, route_ref, ys_ref, o_ref):
    for st in range(h_ref.shape[0] // MOE_SUB):
        rows = slice(st * MOE_SUB, (st + 1) * MOE_SUB)
        pick = _slot_matrix(route_ref[rows, SLOT_LANE:SLOT_LANE + 1]).astype(BF16)
        ys = ys_ref[st]
        hi = ys.astype(BF16)
        lo = (ys - hi.astype(F32)).astype(BF16)
        o_ref[rows, :] = (h_ref[rows, :] + jnp.dot(pick, hi, preferred_element_type=F32)
                          + jnp.dot(pick, lo, preferred_element_type=F32))


def _moe_dense_body(h_ref, route_ref, g2_ref, wg_ref, wu_ref, wd_ref, o_ref, hn_ref):
    g = pl.program_id(1)

    @pl.when(g == 0)
    def _():
        h = h_ref[...]
        o_ref[...] = h
        hn_ref[...] = _rmsnorm_rows(h, g2_ref[...]).astype(BF16)

    hn = hn_ref[...]
    comb = route_ref[...]
    lane = lax.broadcasted_iota(jnp.int32, comb.shape, 1)
    parts = []
    for e in range(EXPERTS_PER_GROUP):
        gate = jnp.sum(jnp.where(lane == N_EXPERT_GROUPS + g * EXPERTS_PER_GROUP + e, comb, 0.0),
                       axis=-1, keepdims=True)
        hg = jnp.dot(hn, wg_ref[e], preferred_element_type=F32)
        hu = jnp.dot(hn, wu_ref[e], preferred_element_type=F32)
        parts.append((hg / (1.0 + jnp.exp(-hg)) * hu * gate).astype(BF16))
    a = jnp.concatenate(parts, axis=-1)
    o_ref[...] += jnp.dot(a, wd_ref[...].reshape(GROUP_FF, D_MODEL), preferred_element_type=F32)


def _group_spec(a, index):
    return pl.BlockSpec((EXPERTS_PER_GROUP,) + a.shape[1:], index)


def _post_attn(x2d, gm, sb, params, tm):
    n = x2d.shape[0]
    wo, g2, wr, br = params
    assert n % tm == 0 and tm % MOE_SUB == 0, (n, tm)
    n_sub, sub = n // MOE_SUB, tm // MOE_SUB
    row = lambda w: pl.BlockSpec((tm, w), lambda i: (i, 0))
    full = lambda a: pl.BlockSpec(a.shape, lambda i: (0,) * a.ndim)
    per_sub = lambda rows, w: pl.BlockSpec((sub, rows, w), lambda i: (i, 0, 0))
    return pl.pallas_call(
        _post_attn_body,
        grid=(n // tm,),
        in_specs=[row(D_MODEL), row(GM_WIDTH), row(SB_WIDTH),
                  full(wo), full(g2), full(wr), full(br)],
        out_specs=[row(D_MODEL), row(LANES), per_sub(MOE_SLOTS, D_MODEL),
                   per_sub(MOE_SLOTS, LANES), per_sub(8, LANES)],
        out_shape=[jax.ShapeDtypeStruct((n, D_MODEL), F32),
                   jax.ShapeDtypeStruct((n, LANES), F32),
                   jax.ShapeDtypeStruct((n_sub, MOE_SLOTS, D_MODEL), BF16),
                   jax.ShapeDtypeStruct((n_sub, MOE_SLOTS, LANES), F32),
                   jax.ShapeDtypeStruct((n_sub, 8, LANES), F32)],
        compiler_params=pltpu.CompilerParams(
            dimension_semantics=("parallel",), vmem_limit_bytes=VMEM_LIMIT),
        name="post_attn",
    )(x2d, gm, sb, wo, g2, wr, br)


def _out_moe(prompt, sample, params, *, tm_prompt, tm_sample):
    wo, g2, wr, br, wg, wu, wd = params
    routed = [_post_attn(*grp, (wo, g2, wr, br), tm) for grp, tm in
              ((prompt, tm_prompt), (sample, tm_sample))]
    tms = (tm_prompt, tm_sample)
    params_2d = pltpu.CompilerParams(
        dimension_semantics=("parallel", "arbitrary"), vmem_limit_bytes=VMEM_LIMIT)

    def sparse():
        (_, _, bk_p, gt_p, cnt_p), (_, _, bk_s, gt_s, cnt_s) = routed
        n_sub, n_sub_s = bk_p.shape[0], bk_s.shape[0]
        sup = min(MOE_SUPER, n_sub)
        assert n_sub % sup == 0, (n_sub, sup)
        steps = n_sub // sup
        fullest = jnp.max(cnt_p[:, 0, :N_EXPERT_GROUPS].reshape(steps, sup, N_EXPERT_GROUPS), axis=1)
        fullest = fullest.at[steps - 1].max(jnp.max(cnt_s[:, 0, :N_EXPERT_GROUPS], axis=0))
        long_steps = (fullest.T > MOE_SHORT).astype(jnp.int32)
        by_group = lambda a: a.reshape(a.shape[0], N_EXPERT_GROUPS, MOE_CAP, a.shape[-1])
        bucket = lambda w: pl.BlockSpec((sup, 1, MOE_CAP, w), lambda g, s, flags: (s, g, 0, 0))
        bucket_s = lambda w: pl.BlockSpec((n_sub_s, 1, MOE_CAP, w), lambda g, s, flags: (0, g, 0, 0))
        ys = pl.pallas_call(
            functools.partial(_moe_ffn_body, prompt_steps=steps),
            grid_spec=pltpu.PrefetchScalarGridSpec(
                num_scalar_prefetch=1,
                grid=(N_EXPERT_GROUPS, steps),
                in_specs=[bucket(D_MODEL), bucket(LANES), bucket_s(D_MODEL), bucket_s(LANES)]
                         + [_group_spec(w, lambda g, s, flags: (g, 0, 0)) for w in (wg, wu, wd)],
                out_specs=[bucket(D_MODEL), bucket_s(D_MODEL)],
                scratch_shapes=[pltpu.VMEM((EXPERTS_PER_GROUP,) + w.shape[1:], BF16)
                                for w in (wg, wu, wd)]),
            out_shape=[jax.ShapeDtypeStruct((n, N_EXPERT_GROUPS, MOE_CAP, D_MODEL), F32)
                       for n in (n_sub, n_sub_s)],
            compiler_params=params_2d,
            name="moe_ffn",
        )(long_steps, by_group(bk_p), by_group(gt_p), by_group(bk_s), by_group(gt_s), wg, wu, wd)

        def combine(h, route, y, tm):
            n, sub = h.shape[0], tm // MOE_SUB
            row = lambda w: pl.BlockSpec((tm, w), lambda i: (i, 0))
            return pl.pallas_call(
                _moe_combine_body,
                grid=(n // tm,),
                in_specs=[row(D_MODEL), row(LANES),
                          pl.BlockSpec((sub, MOE_SLOTS, D_MODEL), lambda i: (i, 0, 0))],
                out_specs=row(D_MODEL),
                out_shape=jax.ShapeDtypeStruct((n, D_MODEL), F32),
                compiler_params=pltpu.CompilerParams(
                    dimension_semantics=("parallel",), vmem_limit_bytes=VMEM_LIMIT),
                name="moe_combine",
            )(h, route, y.reshape(-1, MOE_SLOTS, D_MODEL))

        return tuple(combine(r[0], r[1], y, tm) for r, y, tm in zip(routed, ys, tms))

    def dense():
        wgb, wub, wdb = (w.astype(BF16) for w in (wg, wu, wd))

        def all_experts(h, route, tm):
            row2 = lambda w: pl.BlockSpec((tm, w), lambda i, g: (i, 0))
            return pl.pallas_call(
                _moe_dense_body,
                grid=(h.shape[0] // tm, N_EXPERT_GROUPS),
                in_specs=[row2(D_MODEL), row2(LANES), pl.BlockSpec(g2.shape, lambda i, g: (0, 0))]
                         + [_group_spec(w, lambda i, g: (g, 0, 0)) for w in (wg, wu, wd)],
                out_specs=row2(D_MODEL),
                out_shape=jax.ShapeDtypeStruct(h.shape, F32),
                scratch_shapes=[pltpu.VMEM((tm, D_MODEL), BF16)],
                compiler_params=params_2d,
                name="moe_dense",
            )(h, route, g2, wgb, wub, wdb)

        return tuple(all_experts(r[0], r[1], tm) for r, tm in zip(routed, tms))

    most = jnp.maximum(jnp.max(routed[0][4]), jnp.max(routed[1][4]))
    return lax.cond(most <= MOE_CAP, sparse, dense)


def _layer(layer, xp, xs, cache_k, cache_v, norm1_g, w_in, gm_v_norm_g, gm_w_s, gm_b_s, q_norm_g,
           k_norm_g, w_out, norm2_g, w_router_group, b_router_group, w_router_expert,
           b_router_expert, w_gate, w_up, w_down):
    b, s, d = xp.shape
    db, n, _ = xs.shape
    head_of = jnp.arange(SB_WIDTH) // SB_HEAD_DIM
    bd = jnp.where(head_of[:, None] == head_of[None, :], 1.0 / SB_HEAD_DIM, 0.0).astype(BF16)
    in_params = (norm1_g[None, :], w_in.astype(BF16), gm_v_norm_g[None, :],
                 jnp.tile(q_norm_g, SB_HEADS)[None, :], jnp.tile(k_norm_g, SB_HEADS)[None, :],
                 gm_w_s, gm_b_s.T, bd)

    wr = jnp.concatenate(
        [w_router_group, jnp.transpose(w_router_expert, (1, 0, 2)).reshape(d, N_EXPERTS)], axis=1)
    wr = jnp.pad(wr, ((0, 0), (0, LANES - wr.shape[1])))
    wr_hi = wr.astype(BF16)
    wr_lo = (wr - wr_hi.astype(F32)).astype(BF16)
    br = jnp.pad(jnp.concatenate([b_router_group, b_router_expert.reshape(-1)]),
                 (0, LANES - N_EXPERT_GROUPS - N_EXPERTS))[None, :]

    moe_params = (w_out.astype(BF16), norm2_g[None, :], jnp.concatenate([wr_hi, wr_lo], axis=1), br,
                  w_gate, w_up, w_down)

    xp2 = xp.reshape(b * s, d)
    gm, kf, vf, qs, kb, vb = _mixer_in(xp2, in_params, tm=min(1024, s), chunk=GM_CHUNK, stream=False,
                                       seq=s)
    shp = (b, s, SB_WIDTH)
    sb = _prompt_attn(qs.reshape(shp), kb, vb.reshape(shp)).reshape(b * s, SB_WIDTH)

    xs2 = xs.reshape(db * n, d)
    gm_s, kf_s, vf_s, q_s, gv_s = _mixer_in(xs2, in_params, tm=db * n, chunk=n, stream=True)
    shs = (db, n, SB_WIDTH)
    frame_minor = lambda c: jnp.transpose(c, (0, 1, 3, 4, 2))
    sb_s = _sample_attn(q_s.reshape(shs), kf_s.reshape(shs), vf_s.reshape(shs),
                        frame_minor(cache_k), frame_minor(cache_v), layer)
    sb_s = sb_s.reshape(db * n, SB_WIDTH).astype(BF16)

    yp, ys = _out_moe((xp2, gm, sb), (xs2, gm_s, sb_s), moe_params,
                      tm_prompt=min(1024, b * s), tm_sample=db * n)
    yp, ys = yp.reshape(b, s, d), ys.reshape(db, n, d)

    heads = (SB_HEADS, SB_HEAD_DIM)
    rows = lambda a: jnp.transpose(a.reshape(b, *heads, s), (0, 3, 1, 2))
    return (yp, ys, rows(kf), rows(vf),
            kf_s.reshape(db, n, *heads), vf_s.reshape(db, n, *heads),
            gv_s.reshape(db, n, GM_GROUPS, GM_GROUP_DIM))


def kernel(x_prompt, x_sample, cache_sb_k, cache_sb_v, norm1_g, w_in, gm_v_norm_g, gm_w_s, gm_b_s, q_norm_g, k_norm_g, w_out, norm2_g, w_router_group, b_router_group, w_router_expert, b_router_expert, w_gate, w_up, w_down):
    depth = w_in.shape[0]
    yp, ys = x_prompt, x_sample
    outs = [[] for _ in range(5)]
    for l in range(depth):
        yp, ys, *rest = _layer(
            l, yp, ys, cache_sb_k, cache_sb_v, norm1_g[l], w_in[l], gm_v_norm_g[l], gm_w_s[l],
            gm_b_s[l], q_norm_g[l], k_norm_g[l], w_out[l], norm2_g[l], w_router_group[l],
            b_router_group[l], w_router_expert[l], b_router_expert[l], w_gate[l], w_up[l],
            w_down[l])
        for acc, r in zip(outs, rest):
            acc.append(r)
    return (yp, ys) + tuple(jnp.stack(o, axis=0) for o in outs)
```

```python
import functools

import jax
import jax.numpy as jnp
from jax import lax
from jax.experimental import pallas as pl
from jax.experimental.pallas import tpu as pltpu

D_MODEL = 1024
GM_WIDTH = 512
GM_GROUPS = 4
GM_GROUP_DIM = 128
GM_CHUNK = 128
SB_WIDTH = 512
SB_HEAD_DIM = 64
SB_HEADS = 8
N_EXPERT_GROUPS = 4
EXPERTS_PER_GROUP = 4
N_EXPERTS = 16
EXPERT_FF = 256
GROUP_FF = EXPERTS_PER_GROUP * EXPERT_FF
EPS = 1e-6

LANES = 128
SUBLANES = 8
HEAD_PAIRS = SB_WIDTH // LANES
SB_BLOCK = 128
SB_WIDE = 2 * SB_BLOCK
SB_FIRST = 3 * SB_BLOCK
SB_PER_STEP = 8
CACHE_BLOCK = 256
MOE_SUB = 256
MOE_CAP = 96
MOE_SHORT = 80
MOE_SLOTS = N_EXPERT_GROUPS * MOE_CAP
MOE_SUPER = 8
SLOT_LANE = 0
VMEM_LIMIT = 56 * 1024 * 1024

LOG2_E = 1.4426950408889634
Q_SCALE = SB_HEAD_DIM ** -0.5 * LOG2_E
SB_UNDERFLOW_BITS = 160.0

F32 = jnp.float32
BF16 = jnp.bfloat16
_NT = (((1,), (1,)), ((), ()))


def _gelu_tanh(x):
    return 0.5 * x * (1.0 + jnp.tanh(0.7978845608028654 * (x + 0.044715 * (x * x * x))))


def _softplus2(z):
    return jnp.maximum(z, 0.0) + jnp.log2(1.0 + jnp.exp2(-jnp.abs(z)))


def _suffix_matrix(n):
    r = lax.broadcasted_iota(jnp.int32, (n, n), 0)
    c = lax.broadcasted_iota(jnp.int32, (n, n), 1)
    return jnp.where(r > c, 1.0, 0.0).astype(BF16)


def _mixer_in_body(x_ref, g1_ref, w_in_ref, gvg_ref, qg_ref, kg_ref, ws_ref, bst_ref, bd_ref,
                   gm_ref, kf_ref, vf_ref, *rest, chunk, stream):
    x = x_ref[...]
    tm = x.shape[0]
    xn = x * lax.rsqrt(jnp.mean(x * x, axis=-1, keepdims=True) + EPS) * g1_ref[...]
    proj = jnp.dot(xn.astype(BF16), w_in_ref[...], preferred_element_type=F32)

    gu = _gelu_tanh(proj[:, :GM_WIDTH])
    gvr = _gelu_tanh(proj[:, GM_WIDTH:2 * GM_WIDTH])
    gv = gvr * lax.rsqrt(jnp.mean(gvr * gvr, axis=-1, keepdims=True) + EPS) * gvg_ref[...]

    o = 2 * GM_WIDTH
    q = proj[:, o:o + SB_WIDTH]
    k = proj[:, o + SB_WIDTH:o + 2 * SB_WIDTH]
    v = proj[:, o + 2 * SB_WIDTH:]
    bd = bd_ref[...]
    qms = jnp.dot((q * q).astype(BF16), bd, preferred_element_type=F32)
    kms = jnp.dot((k * k).astype(BF16), bd, preferred_element_type=F32)
    qs = q * lax.rsqrt(qms + EPS) * qg_ref[...] * Q_SCALE
    kn = k * lax.rsqrt(kms + EPS) * kg_ref[...]
    if stream:
        kf_ref[...] = kn
        vf_ref[...] = v
        rest[0][...] = qs
        rest[1][...] = gv
    else:
        kt = kn.T
        kf_ref[...] = kt
        vf_ref[...] = v.T
        qs_ref, kb_ref, vb_ref = rest
        qs_ref[...] = qs.astype(BF16)
        kb_ref[...] = kt.astype(BF16)
        vb_ref[...] = v.astype(BF16)

    r = lax.broadcasted_iota(jnp.int32, (chunk, chunk), 0)
    c = lax.broadcasted_iota(jnp.int32, (chunk, chunk), 1)
    gvb = gv.astype(BF16)
    chunks = [slice(ci * chunk, (ci + 1) * chunk) for ci in range(tm // chunk)]
    for g in range(GM_GROUPS):
        w = jnp.where(r >= c, ws_ref[g, :chunk, :chunk], 0.0).astype(BF16)
        b = bst_ref[:chunk, g:g + 1]
        cols = slice(g * GM_GROUP_DIM, (g + 1) * GM_GROUP_DIM)
        mixed = jnp.dot(w, jnp.concatenate([gvb[rows, cols] for rows in chunks], axis=1),
                        preferred_element_type=F32)
        for ci, rows in enumerate(chunks):
            part = mixed[:, ci * GM_GROUP_DIM:(ci + 1) * GM_GROUP_DIM] + b
            gm_ref[rows, cols] = (gu[rows, cols] * part).astype(BF16)


def _mixer_in(x2d, params, *, tm, chunk, stream, seq=None):
    n = x2d.shape[0]
    assert n % tm == 0 and tm % chunk == 0 and chunk <= GM_CHUNK, (n, tm, chunk)
    full = lambda a: pl.BlockSpec(a.shape, lambda i: (0,) * a.ndim)
    spec = lambda s: pl.BlockSpec((tm,) + s.shape[1:], lambda i: (i,) + (0,) * (len(s.shape) - 1))
    gm = jax.ShapeDtypeStruct((n, GM_WIDTH), BF16)
    if stream:
        rows = jax.ShapeDtypeStruct((n, SB_WIDTH), F32)
        out_shape = [gm, rows, rows,
                     rows,
                     jax.ShapeDtypeStruct((n, GM_WIDTH), F32)]
        out_specs = [spec(s) for s in out_shape]
    else:
        per_seq = seq // tm
        rows_t = jax.ShapeDtypeStruct((n // seq, SB_WIDTH, seq), F32)
        spec_t = pl.BlockSpec((None, SB_WIDTH, tm), lambda i: (i // per_seq, 0, i % per_seq))
        dense = jax.ShapeDtypeStruct((n, SB_WIDTH), BF16)
        dense_t = jax.ShapeDtypeStruct(rows_t.shape, BF16)
        out_shape = [gm, rows_t, rows_t, dense, dense_t, dense]
        out_specs = [spec(gm), spec_t, spec_t, spec(dense), spec_t, spec(dense)]
    return pl.pallas_call(
        functools.partial(_mixer_in_body, chunk=chunk, stream=stream),
        grid=(n // tm,),
        in_specs=[pl.BlockSpec((tm, D_MODEL), lambda i: (i, 0))] + [full(a) for a in params],
        out_specs=out_specs,
        out_shape=out_shape,
        compiler_params=pltpu.CompilerParams(
            dimension_semantics=("parallel",), vmem_limit_bytes=VMEM_LIMIT),
        name="mixer_in_stream" if stream else "mixer_in",
    )(x2d, *params)


def _prompt_attn_body(q_ref, k_ref, v_ref, sfx_ref, sfx_wide_ref, sfx_first_ref, o_ref, *scratch):
    blocks = q_ref.shape[0] // SB_BLOCK

    def one(sub, carry):
        rows = pl.ds(pl.multiple_of(sub * SB_BLOCK, SB_BLOCK), SB_BLOCK)
        _prompt_attn_block(pl.program_id(1) * blocks + sub, q_ref.at[rows], k_ref, v_ref, sfx_ref,
                           sfx_wide_ref, sfx_first_ref, o_ref.at[rows], *scratch)
        return carry

    lax.fori_loop(0, blocks, one, 0)


def _prompt_attn_block(i, q_ref, k_ref, v_ref, sfx_ref, sfx_wide_ref, sfx_first_ref, o_ref,
                       qst_ref, c_ref, acc_ref):
    lane = lax.broadcasted_iota(jnp.int32, (SB_BLOCK, LANES), 1)
    first = lane < SB_HEAD_DIM
    for hp in range(HEAD_PAIRS):
        q = q_ref[:, hp * LANES:(hp + 1) * LANES]
        zero = jnp.zeros_like(q)
        qst_ref[hp, :SB_BLOCK, :] = jnp.where(first, q, zero)
        qst_ref[hp, SB_BLOCK:, :] = jnp.where(first, zero, q)
    m = 2 * SB_BLOCK
    suffix = {SB_BLOCK: sfx_ref, SB_WIDE: sfx_wide_ref, SB_FIRST: sfx_first_ref}

    def sweep(j, keys, causal, fresh):
        rows = pl.ds(pl.multiple_of(j * SB_BLOCK, SB_BLOCK), keys)
        cols = lambda hp: slice(hp * LANES, (hp + 1) * LANES)
        z = jnp.concatenate(
            [jnp.dot(qst_ref[hp], k_ref[cols(hp), rows], preferred_element_type=F32)
             for hp in range(HEAD_PAIRS)], axis=0)
        def masked(a):
            if causal is None:
                return a
            old = keys - SB_BLOCK
            newest = jnp.where(causal, a[:, old:], 0.0)
            return newest if old == 0 else jnp.concatenate([a[:, :old], newest], axis=1)

        sp = masked(_softplus2(z))
        c0 = jnp.zeros((HEAD_PAIRS * m, 1), F32) if fresh else c_ref[...]
        later = jnp.dot(sp.astype(BF16), suffix[keys][...], preferred_element_type=F32) + c0
        w = masked(jnp.exp2(z - sp - later))
        wb = w.astype(BF16)
        pv = jnp.concatenate(
            [jnp.dot(wb[hp * m:(hp + 1) * m, :], v_ref[rows, cols(hp)], preferred_element_type=F32)
             for hp in range(HEAD_PAIRS)], axis=0)
        acc_ref[...] = pv if fresh else acc_ref[...] + pv
        c_ref[...] = c0 + jnp.sum(sp, axis=-1, keepdims=True)

    below = SB_FIRST // SB_BLOCK - 1
    t = lax.broadcasted_iota(jnp.int32, (HEAD_PAIRS * m, SB_BLOCK), 0) % SB_BLOCK
    s = lax.broadcasted_iota(jnp.int32, (HEAD_PAIRS * m, SB_BLOCK), 1)

    @pl.when(i >= below)
    def _():
        sweep(i - below, SB_FIRST, s < t, True)

    @pl.when(i < below)
    def _():
        sweep(i, SB_BLOCK, s < t, True)

    j0 = jnp.where(i >= below, i - below, i)

    def least_carry():
        return jnp.min(c_ref[...])

    per_wide = SB_WIDE // SB_BLOCK
    n_wide = j0 // per_wide

    def more(state):
        jj, carry = state
        return (jj < n_wide) & (carry < SB_UNDERFLOW_BITS)

    def body(state):
        jj, _ = state
        sweep(j0 - per_wide * (jj + 1), SB_WIDE, None, False)
        return jj + 1, least_carry()

    done, carry = lax.while_loop(more, body, (jnp.int32(0), least_carry()))

    @pl.when((done == n_wide) & (j0 % per_wide == 1) & (carry < SB_UNDERFLOW_BITS))
    def _():
        sweep(0, SB_BLOCK, None, False)

    for hp in range(HEAD_PAIRS):
        lo = hp * m
        o_ref[:, hp * LANES:(hp + 1) * LANES] = jnp.where(
            first, acc_ref[lo:lo + SB_BLOCK, :], acc_ref[lo + SB_BLOCK:lo + m, :]).astype(o_ref.dtype)


def _prompt_attn(qs, kt, vb):
    b, s, _ = qs.shape
    assert s % SB_BLOCK == 0, s
    sfx = tuple(_suffix_matrix(keys) for keys in (SB_BLOCK, SB_WIDE, SB_FIRST))
    per_step = min(SB_PER_STEP * SB_BLOCK, s)
    assert s % per_step == 0, s
    q_spec = pl.BlockSpec((None, per_step, SB_WIDTH), lambda bi, i: (bi, i, 0))
    kt_spec = pl.BlockSpec((None, SB_WIDTH, s), lambda bi, i: (bi, 0, 0))
    kv_spec = pl.BlockSpec((None, s, SB_WIDTH), lambda bi, i: (bi, 0, 0))
    m = 2 * SB_BLOCK
    return pl.pallas_call(
        _prompt_attn_body,
        grid=(b, s // per_step),
        in_specs=[q_spec, kt_spec, kv_spec] + [pl.BlockSpec(a.shape, lambda bi, i: (0, 0)) for a in sfx],
        out_specs=q_spec,
        out_shape=jax.ShapeDtypeStruct((b, s, SB_WIDTH), BF16),
        scratch_shapes=[pltpu.VMEM((HEAD_PAIRS, m, LANES), BF16),
                        pltpu.VMEM((HEAD_PAIRS * m, 1), F32),
                        pltpu.VMEM((HEAD_PAIRS * m, LANES), F32)],
        compiler_params=pltpu.CompilerParams(
            dimension_semantics=("parallel", "arbitrary"), vmem_limit_bytes=VMEM_LIMIT),
        name="prompt_attn",
    )(qs, kt, vb, *sfx)


def _sample_attn_body(q_ref, kn_ref, vn_ref, ck_hbm, cv_hbm, o_ref, kbuf, vbuf, sems, c_ref, acc_ref,
                      *, layer):
    b = pl.program_id(0)
    n = q_ref.shape[0]
    m = SB_HEADS * n
    nblk = ck_hbm.shape[-1] // CACHE_BLOCK

    def copies(stream, blk, slot):
        cols = pl.ds(blk * CACHE_BLOCK, CACHE_BLOCK)
        return (pltpu.make_async_copy(ck_hbm.at[layer, stream, :, :, cols], kbuf.at[slot], sems.at[slot, 0]),
                pltpu.make_async_copy(cv_hbm.at[layer, stream, :, :, cols], vbuf.at[slot], sems.at[slot, 1]))

    def start(stream, blk, slot):
        for cp in copies(stream, blk, slot):
            cp.start()

    def wait(stream, blk, slot):
        for cp in copies(stream, blk, slot):
            cp.wait()

    def slot_of(jj):
        return 2 + jj % 2

    @pl.when(b == 0)
    def _():
        start(0, nblk - 1, 0)

    @pl.when(b + 1 < pl.num_programs(0))
    def _():
        start(b + 1, nblk - 1, (b + 1) % 2)

    head = lambda a, h: a[:, h * SB_HEAD_DIM:(h + 1) * SB_HEAD_DIM]
    q_all = q_ref[...].astype(BF16)
    q = [head(q_all, h) for h in range(SB_HEADS)]

    def sweep(slot, newest, suffix, causal, fresh):
        def logits(h):
            z = jnp.dot(q[h], kbuf[slot, h].astype(BF16), preferred_element_type=F32)
            if newest is None:
                return z
            return jnp.concatenate(
                [z, lax.dot_general(q[h], newest[0][h], _NT, preferred_element_type=F32)], axis=1)

        z = jnp.concatenate([logits(h) for h in range(SB_HEADS)], axis=0)
        sp = _softplus2(z)
        if causal is not None:
            sp = jnp.where(causal, sp, 0.0)
        c0 = jnp.zeros((m, 1), F32) if fresh else c_ref[...]
        later = jnp.dot(sp.astype(BF16), suffix, preferred_element_type=F32) + c0
        w = jnp.exp2(z - sp - later)
        if causal is not None:
            w = jnp.where(causal, w, 0.0)
        wb = w.astype(BF16)
        pvs = []
        for h in range(SB_HEADS):
            wh = wb[h * n:(h + 1) * n, :]
            pv = lax.dot_general(wh[:, :CACHE_BLOCK], vbuf[slot, h].astype(BF16), _NT,
                                 preferred_element_type=F32)
            if newest is not None:
                pv = pv + jnp.dot(wh[:, CACHE_BLOCK:], newest[1][h], preferred_element_type=F32)
            pvs.append(pv)
        pv = jnp.stack(pvs)
        acc_ref[...] = pv if fresh else acc_ref[...] + pv
        c_ref[...] = c0 + jnp.sum(sp, axis=-1, keepdims=True)

    pad = jnp.zeros((LANES - n, SB_HEAD_DIM), BF16)
    kn_all, vn_all = kn_ref[...].astype(BF16), vn_ref[...].astype(BF16)
    newest = ([jnp.concatenate([head(kn_all, h), pad], axis=0) for h in range(SB_HEADS)],
              [jnp.concatenate([head(vn_all, h), pad], axis=0) for h in range(SB_HEADS)])
    t = lax.broadcasted_iota(jnp.int32, (m, CACHE_BLOCK + LANES), 0) % n
    s = lax.broadcasted_iota(jnp.int32, (m, CACHE_BLOCK + LANES), 1) - CACHE_BLOCK
    wait(b, nblk - 1, b % 2)
    sweep(b % 2, newest, _suffix_matrix(CACHE_BLOCK + LANES), s < t, True)

    suffix = _suffix_matrix(CACHE_BLOCK)

    def more(state):
        jj, carry = state
        return (jj < nblk) & (carry < SB_UNDERFLOW_BITS)

    def body(state):
        jj, _ = state
        blk = nblk - 1 - jj
        slot = slot_of(jj)
        wait(b, blk, slot)

        @pl.when(blk > 0)
        def _():
            start(b, blk - 1, slot_of(jj + 1))

        sweep(slot, None, suffix, None, False)
        return jj + 1, jnp.min(c_ref[...])

    state = (jnp.int32(1), jnp.min(c_ref[...]))

    if nblk > 1:
        @pl.when(more(state))
        def _():
            start(b, nblk - 2, slot_of(1))

    done, _ = lax.while_loop(more, body, state)

    @pl.when((done > 1) & (done < nblk))
    def _():
        wait(b, nblk - 1 - done, slot_of(done))

    for h in range(SB_HEADS):
        o_ref[:, h, :] = acc_ref[h]


def _sample_attn(qs, kn, vn, cache_k, cache_v, layer):
    db, n = qs.shape[:2]
    past = cache_k.shape[-1]
    assert n <= LANES and past >= CACHE_BLOCK and past % CACHE_BLOCK == 0, (n, past)
    m = SB_HEADS * n
    new_spec = pl.BlockSpec((None, n, SB_WIDTH), lambda b: (b, 0, 0))
    hbm = pl.BlockSpec(memory_space=pl.ANY)
    buf = pltpu.VMEM((4, SB_HEADS, SB_HEAD_DIM, CACHE_BLOCK), F32)
    return pl.pallas_call(
        functools.partial(_sample_attn_body, layer=layer),
        grid=(db,),
        in_specs=[new_spec, new_spec, new_spec, hbm, hbm],
        out_specs=pl.BlockSpec((None, n, SB_HEADS, SB_HEAD_DIM), lambda b: (b, 0, 0, 0)),
        out_shape=jax.ShapeDtypeStruct((db, n, SB_HEADS, SB_HEAD_DIM), F32),
        scratch_shapes=[buf, buf, pltpu.SemaphoreType.DMA((4, 2)),
                        pltpu.VMEM((m, 1), F32),
                        pltpu.VMEM((SB_HEADS, n, SB_HEAD_DIM), F32)],
        compiler_params=pltpu.CompilerParams(
            dimension_semantics=("arbitrary",), vmem_limit_bytes=VMEM_LIMIT),
        name="sample_attn",
    )(qs, kn, vn, cache_k, cache_v)


def _route(logits):
    lane = lax.broadcasted_iota(jnp.int32, logits.shape, 1).astype(F32)
    neg = jnp.float32(-jnp.inf)
    big = jnp.float32(1 << 20)
    rmax = lambda a: jnp.max(a, axis=-1, keepdims=True)
    rmin = lambda a: jnp.min(a, axis=-1, keepdims=True)
    rsum = lambda a: jnp.sum(a, axis=-1, keepdims=True)

    gmask = lane < N_EXPERT_GROUPS
    gl = jnp.where(gmask, logits, neg)
    gmax = rmax(gl)
    g_sel = rmin(jnp.where(gl == gmax, lane, big))
    p_sel = 1.0 / rsum(jnp.where(gmask, jnp.exp(gl - gmax), 0.0))

    e_lo = N_EXPERT_GROUPS + g_sel * EXPERTS_PER_GROUP
    emask = (lane >= e_lo) & (lane < e_lo + EXPERTS_PER_GROUP)
    el = jnp.where(emask, logits, neg)
    emax = rmax(el)
    ex = jnp.where(emask, jnp.exp(el - emax), 0.0)
    p_e = ex / rsum(ex)
    pm = jnp.where(emask, p_e, -1.0)
    v1 = rmax(pm)
    i1 = rmin(jnp.where(pm == v1, lane, big))
    pm2 = jnp.where(lane == i1, -1.0, pm)
    v2 = rmax(pm2)
    i2 = rmin(jnp.where(pm2 == v2, lane, big))
    tot = v1 + v2
    gate1 = p_sel * v1 / tot
    gate2 = p_sel * v2 / tot
    comb = jnp.where(lane == i1, gate1, 0.0) + jnp.where(lane == i2, gate2, 0.0)
    gates = jnp.where(lane == i1 - e_lo, gate1, 0.0) + jnp.where(lane == i2 - e_lo, gate2, 0.0)
    return comb, g_sel, gates


def _rmsnorm_rows(h, g):
    return h * lax.rsqrt(jnp.mean(h * h, axis=-1, keepdims=True) + EPS) * g


def _slot_matrix(slot):
    lane = lax.broadcasted_iota(jnp.int32, (slot.shape[0], MOE_SLOTS), 1)
    return jnp.where(lane == slot.astype(jnp.int32), 1.0, 0.0)


def _post_attn_body(x_ref, gm_ref, sb_ref, wo_ref, g2_ref, wr_ref, br_ref,
                    h_ref, route_ref, bucket_ref, gate_ref, cnt_ref):
    h = (x_ref[...]
         + jnp.dot(gm_ref[...], wo_ref[:GM_WIDTH, :], preferred_element_type=F32)
         + jnp.dot(sb_ref[...], wo_ref[GM_WIDTH:, :], preferred_element_type=F32))
    h_ref[...] = h
    hn = _rmsnorm_rows(h, g2_ref[...])
    hi = hn.astype(BF16)
    lo = (hn - hi.astype(F32)).astype(BF16)
    both = jnp.dot(hi, wr_ref[...], preferred_element_type=F32)
    logits = (both[:, :LANES] + jnp.dot(lo, wr_ref[:, :LANES], preferred_element_type=F32)
              + both[:, LANES:]) + br_ref[...]
    comb, g_sel, gates = _route(logits)

    subs = [slice(st * MOE_SUB, (st + 1) * MOE_SUB) for st in range(h.shape[0] // MOE_SUB)]
    lane = lax.broadcasted_iota(jnp.int32, comb.shape, 1).astype(F32)
    r = lax.broadcasted_iota(jnp.int32, (MOE_SUB, MOE_SUB), 0)
    c = lax.broadcasted_iota(jnp.int32, (MOE_SUB, MOE_SUB), 1)
    earlier = jnp.where(r > c, 1.0, 0.0).astype(BF16)
    onehot = jnp.where(lane == g_sel, 1.0, 0.0)
    onehot_b = onehot.astype(BF16)
    before = jnp.concatenate(
        [jnp.dot(earlier, onehot_b[rows], preferred_element_type=F32) for rows in subs], axis=0)
    rank = jnp.sum(onehot * before, axis=-1, keepdims=True)
    slot = jnp.where(rank < MOE_CAP, g_sel * MOE_CAP + rank, -1.0)
    route_ref[...] = jnp.where(lane == SLOT_LANE, slot, comb)
    g1 = gates.astype(BF16)
    g2 = (gates - g1.astype(F32)).astype(BF16)
    g3 = (gates - g1.astype(F32) - g2.astype(F32)).astype(BF16)
    g12 = jnp.concatenate([g1, g2], axis=1)
    for st, rows in enumerate(subs):
        place = _slot_matrix(slot[rows]).T.astype(BF16)
        bucket_ref[st] = jnp.dot(place, hi[rows], preferred_element_type=F32).astype(BF16)
        placed = jnp.dot(place, g12[rows], preferred_element_type=F32)
        gate_ref[st] = (placed[:, :LANES] + placed[:, LANES:]
                        + jnp.dot(place, g3[rows], preferred_element_type=F32))
        cnt_ref[st] = jnp.broadcast_to(jnp.sum(onehot[rows], axis=0, keepdims=True), cnt_ref.shape[1:])


def _moe_ffn_body(long_ref, x_ref, gate_ref, xs_ref, gates_ref, wg_ref, wu_ref, wd_ref, y_ref, ys_ref,
                  wgb_ref, wub_ref, wdb_ref, *, prompt_steps):
    g, s = pl.program_id(0), pl.program_id(1)

    @pl.when(s == 0)
    def _():
        wgb_ref[...] = wg_ref[...].astype(BF16)
        wub_ref[...] = wu_ref[...].astype(BF16)
        wdb_ref[...] = wd_ref[...].astype(BF16)

    def ffn(ins, outs, used):
        x = jnp.concatenate([r[:, 0, :used, :].reshape(-1, D_MODEL) for r, _ in ins], axis=0)
        gates = jnp.concatenate([r[:, 0, :used, :].reshape(-1, LANES) for _, r in ins], axis=0)
        parts = []
        for e in range(EXPERTS_PER_GROUP):
            hg = jnp.dot(x, wgb_ref[e], preferred_element_type=F32)
            hu = jnp.dot(x, wub_ref[e], preferred_element_type=F32)
            parts.append((hg / (1.0 + jnp.exp(-hg)) * hu * gates[:, e:e + 1]).astype(BF16))
        a = jnp.concatenate(parts, axis=-1)
        y = jnp.dot(a, wdb_ref[...].reshape(GROUP_FF, D_MODEL), preferred_element_type=F32)
        row = 0
        for out_ref in outs:
            rows = out_ref.shape[0] * used
            out_ref[:, 0, :used, :] = y[row:row + rows].reshape(out_ref.shape[0], used, D_MODEL)
            if used < MOE_CAP:
                out_ref[:, 0, used:, :] = jnp.zeros((out_ref.shape[0], MOE_CAP - used, D_MODEL), F32)
            row += rows

    def step(used):
        @pl.when(s < prompt_steps - 1)
        def _():
            ffn([(x_ref, gate_ref)], [y_ref], used)

        @pl.when(s == prompt_steps - 1)
        def _():
            ffn([(x_ref, gate_ref), (xs_ref, gates_ref)], [y_ref, ys_ref], used)

    @pl.when(long_ref[g, s] == 0)
    def _():
        step(MOE_SHORT)

    @pl.when(long_ref[g, s] != 0)
    def _():
        step(MOE_CAP)


def _moe_combine_body(h_ref, route_ref, ys_ref, o_ref):
    for st in range(h_ref.shape[0] // MOE_SUB):
        rows = slice(st * MOE_SUB, (st + 1) * MOE_SUB)
        pick = _slot_matrix(route_ref[rows, SLOT_LANE:SLOT_LANE + 1]).astype(BF16)
        ys = ys_ref[st]
        hi = ys.astype(BF16)
        lo = (ys - hi.astype(F32)).astype(BF16)
        o_ref[rows, :] = (h_ref[rows, :] + jnp.dot(pick, hi, preferred_element_type=F32)
                          + jnp.dot(pick, lo, preferred_element_type=F32))


def _moe_dense_body(h_ref, route_ref, g2_ref, wg_ref, wu_ref, wd_ref, o_ref, hn_ref):
    g = pl.program_id(1)

    @pl.when(g == 0)
    def _():
        h = h_ref[...]
        o_ref[...] = h
        hn_ref[...] = _rmsnorm_rows(h, g2_ref[...]).astype(BF16)

    hn = hn_ref[...]
    comb = route_ref[...]
    lane = lax.broadcasted_iota(jnp.int32, comb.shape, 1)
    parts = []
    for e in range(EXPERTS_PER_GROUP):
        gate = jnp.sum(jnp.where(lane == N_EXPERT_GROUPS + g * EXPERTS_PER_GROUP + e, comb, 0.0),
                       axis=-1, keepdims=True)
        hg = jnp.dot(hn, wg_ref[e], preferred_element_type=F32)
        hu = jnp.dot(hn, wu_ref[e], preferred_element_type=F32)
        parts.append((hg / (1.0 + jnp.exp(-hg)) * hu * gate).astype(BF16))
    a = jnp.concatenate(parts, axis=-1)
    o_ref[...] += jnp.dot(a, wd_ref[...].reshape(GROUP_FF, D_MODEL), preferred_element_type=F32)


def _group_spec(a, index):
    return pl.BlockSpec((EXPERTS_PER_GROUP,) + a.shape[1:], index)


def _post_attn(x2d, gm, sb, params, tm):
    n = x2d.shape[0]
    wo, g2, wr, br = params
    assert n % tm == 0 and tm % MOE_SUB == 0, (n, tm)
    n_sub, sub = n // MOE_SUB, tm // MOE_SUB
    row = lambda w: pl.BlockSpec((tm, w), lambda i: (i, 0))
    full = lambda a: pl.BlockSpec(a.shape, lambda i: (0,) * a.ndim)
    per_sub = lambda rows, w: pl.BlockSpec((sub, rows, w), lambda i: (i, 0, 0))
    return pl.pallas_call(
        _post_attn_body,
        grid=(n // tm,),
        in_specs=[row(D_MODEL), row(GM_WIDTH), row(SB_WIDTH),
                  full(wo), full(g2), full(wr), full(br)],
        out_specs=[row(D_MODEL), row(LANES), per_sub(MOE_SLOTS, D_MODEL),
                   per_sub(MOE_SLOTS, LANES), per_sub(SUBLANES, LANES)],
        out_shape=[jax.ShapeDtypeStruct((n, D_MODEL), F32),
                   jax.ShapeDtypeStruct((n, LANES), F32),
                   jax.ShapeDtypeStruct((n_sub, MOE_SLOTS, D_MODEL), BF16),
                   jax.ShapeDtypeStruct((n_sub, MOE_SLOTS, LANES), F32),
                   jax.ShapeDtypeStruct((n_sub, SUBLANES, LANES), F32)],
        compiler_params=pltpu.CompilerParams(
            dimension_semantics=("parallel",), vmem_limit_bytes=VMEM_LIMIT),
        name="post_attn",
    )(x2d, gm, sb, wo, g2, wr, br)


def _out_moe(prompt, sample, params, *, tm_prompt, tm_sample):
    wo, g2, wr, br, wg, wu, wd = params
    routed = [_post_attn(*grp, (wo, g2, wr, br), tm) for grp, tm in
              ((prompt, tm_prompt), (sample, tm_sample))]
    tms = (tm_prompt, tm_sample)
    params_2d = pltpu.CompilerParams(
        dimension_semantics=("parallel", "arbitrary"), vmem_limit_bytes=VMEM_LIMIT)

    def sparse():
        (_, _, bk_p, gt_p, cnt_p), (_, _, bk_s, gt_s, cnt_s) = routed
        n_sub, n_sub_s = bk_p.shape[0], bk_s.shape[0]
        sup = min(MOE_SUPER, n_sub)
        assert n_sub % sup == 0, (n_sub, sup)
        steps = n_sub // sup
        fullest = jnp.max(cnt_p[:, 0, :N_EXPERT_GROUPS].reshape(steps, sup, N_EXPERT_GROUPS), axis=1)
        fullest = fullest.at[steps - 1].max(jnp.max(cnt_s[:, 0, :N_EXPERT_GROUPS], axis=0))
        long_steps = (fullest.T > MOE_SHORT).astype(jnp.int32)
        by_group = lambda a: a.reshape(a.shape[0], N_EXPERT_GROUPS, MOE_CAP, a.shape[-1])
        bucket = lambda w: pl.BlockSpec((sup, 1, MOE_CAP, w), lambda g, s, flags: (s, g, 0, 0))
        bucket_s = lambda w: pl.BlockSpec((n_sub_s, 1, MOE_CAP, w), lambda g, s, flags: (0, g, 0, 0))
        ys = pl.pallas_call(
            functools.partial(_moe_ffn_body, prompt_steps=steps),
            grid_spec=pltpu.PrefetchScalarGridSpec(
                num_scalar_prefetch=1,
                grid=(N_EXPERT_GROUPS, steps),
                in_specs=[bucket(D_MODEL), bucket(LANES), bucket_s(D_MODEL), bucket_s(LANES)]
                         + [_group_spec(w, lambda g, s, flags: (g, 0, 0)) for w in (wg, wu, wd)],
                out_specs=[bucket(D_MODEL), bucket_s(D_MODEL)],
                scratch_shapes=[pltpu.VMEM((EXPERTS_PER_GROUP,) + w.shape[1:], BF16)
                                for w in (wg, wu, wd)]),
            out_shape=[jax.ShapeDtypeStruct((n, N_EXPERT_GROUPS, MOE_CAP, D_MODEL), F32)
                       for n in (n_sub, n_sub_s)],
            compiler_params=params_2d,
            name="moe_ffn",
        )(long_steps, by_group(bk_p), by_group(gt_p), by_group(bk_s), by_group(gt_s), wg, wu, wd)

        def combine(h, route, y, tm):
            n, sub = h.shape[0], tm // MOE_SUB
            row = lambda w: pl.BlockSpec((tm, w), lambda i: (i, 0))
            return pl.pallas_call(
                _moe_combine_body,
                grid=(n // tm,),
                in_specs=[row(D_MODEL), row(LANES),
                          pl.BlockSpec((sub, MOE_SLOTS, D_MODEL), lambda i: (i, 0, 0))],
                out_specs=row(D_MODEL),
                out_shape=jax.ShapeDtypeStruct((n, D_MODEL), F32),
                compiler_params=pltpu.CompilerParams(
                    dimension_semantics=("parallel",), vmem_limit_bytes=VMEM_LIMIT),
                name="moe_combine",
            )(h, route, y.reshape(-1, MOE_SLOTS, D_MODEL))

        return tuple(combine(r[0], r[1], y, tm) for r, y, tm in zip(routed, ys, tms))

    def dense():
        wgb, wub, wdb = (w.astype(BF16) for w in (wg, wu, wd))

        def all_experts(h, route, tm):
            row2 = lambda w: pl.BlockSpec((tm, w), lambda i, g: (i, 0))
            return pl.pallas_call(
                _moe_dense_body,
                grid=(h.shape[0] // tm, N_EXPERT_GROUPS),
                in_specs=[row2(D_MODEL), row2(LANES), pl.BlockSpec(g2.shape, lambda i, g: (0, 0))]
                         + [_group_spec(w, lambda i, g: (g, 0, 0)) for w in (wg, wu, wd)],
                out_specs=row2(D_MODEL),
                out_shape=jax.ShapeDtypeStruct(h.shape, F32),
                scratch_shapes=[pltpu.VMEM((tm, D_MODEL), BF16)],
                compiler_params=params_2d,
                name="moe_dense",
            )(h, route, g2, wgb, wub, wdb)

        return tuple(all_experts(r[0], r[1], tm) for r, tm in zip(routed, tms))

    most = jnp.maximum(jnp.max(routed[0][4]), jnp.max(routed[1][4]))
    return lax.cond(most <= MOE_CAP, sparse, dense)


def _layer(layer, xp, xs, cache_k, cache_v, norm1_g, w_in, gm_v_norm_g, gm_w_s, gm_b_s, q_norm_g,
           k_norm_g, w_out, norm2_g, w_router_group, b_router_group, w_router_expert,
           b_router_expert, w_gate, w_up, w_down):
    b, s, d = xp.shape
    db, n, _ = xs.shape
    head_of = jnp.arange(SB_WIDTH) // SB_HEAD_DIM
    bd = jnp.where(head_of[:, None] == head_of[None, :], 1.0 / SB_HEAD_DIM, 0.0).astype(BF16)
    in_params = (norm1_g[None, :], w_in.astype(BF16), gm_v_norm_g[None, :],
                 jnp.tile(q_norm_g, SB_HEADS)[None, :], jnp.tile(k_norm_g, SB_HEADS)[None, :],
                 gm_w_s, gm_b_s.T, bd)

    wr = jnp.concatenate(
        [w_router_group, jnp.transpose(w_router_expert, (1, 0, 2)).reshape(d, N_EXPERTS)], axis=1)
    wr = jnp.pad(wr, ((0, 0), (0, LANES - wr.shape[1])))
    wr_hi = wr.astype(BF16)
    wr_lo = (wr - wr_hi.astype(F32)).astype(BF16)
    br = jnp.pad(jnp.concatenate([b_router_group, b_router_expert.reshape(-1)]),
                 (0, LANES - N_EXPERT_GROUPS - N_EXPERTS))[None, :]

    moe_params = (w_out.astype(BF16), norm2_g[None, :], jnp.concatenate([wr_hi, wr_lo], axis=1), br,
                  w_gate, w_up, w_down)

    xp2 = xp.reshape(b * s, d)
    gm, kf, vf, qs, kb, vb = _mixer_in(xp2, in_params, tm=min(1024, s), chunk=GM_CHUNK, stream=False,
                                       seq=s)
    shp = (b, s, SB_WIDTH)
    sb = _prompt_attn(qs.reshape(shp), kb, vb.reshape(shp)).reshape(b * s, SB_WIDTH)

    xs2 = xs.reshape(db * n, d)
    gm_s, kf_s, vf_s, q_s, gv_s = _mixer_in(xs2, in_params, tm=db * n, chunk=n, stream=True)
    shs = (db, n, SB_WIDTH)
    frame_minor = lambda c: jnp.transpose(c, (0, 1, 3, 4, 2))
    sb_s = _sample_attn(q_s.reshape(shs), kf_s.reshape(shs), vf_s.reshape(shs),
                        frame_minor(cache_k), frame_minor(cache_v), layer)
    sb_s = sb_s.reshape(db * n, SB_WIDTH).astype(BF16)

    yp, ys = _out_moe((xp2, gm, sb), (xs2, gm_s, sb_s), moe_params,
                      tm_prompt=min(1024, b * s), tm_sample=db * n)
    yp, ys = yp.reshape(b, s, d), ys.reshape(db, n, d)

    heads = (SB_HEADS, SB_HEAD_DIM)
    rows = lambda a: jnp.transpose(a.reshape(b, *heads, s), (0, 3, 1, 2))
    return (yp, ys, rows(kf), rows(vf),
            kf_s.reshape(db, n, *heads), vf_s.reshape(db, n, *heads),
            gv_s.reshape(db, n, GM_GROUPS, GM_GROUP_DIM))


def kernel(x_prompt, x_sample, cache_sb_k, cache_sb_v, norm1_g, w_in, gm_v_norm_g, gm_w_s, gm_b_s, q_norm_g, k_norm_g, w_out, norm2_g, w_router_group, b_router_group, w_router_expert, b_router_expert, w_gate, w_up, w_down):
    depth = w_in.shape[0]
    yp, ys = x_prompt, x_sample
    outs = [[] for _ in range(5)]
    for l in range(depth):
        yp, ys, *rest = _layer(
            l, yp, ys, cache_sb_k, cache_sb_v, norm1_g[l], w_in[l], gm_v_norm_g[l], gm_w_s[l],
            gm_b_s[l], q_norm_g[l], k_norm_g[l], w_out[l], norm2_g[l], w_router_group[l],
            b_router_group[l], w_router_expert[l], b_router_expert[l], w_gate[l], w_up[l],
            w_down[l])
        for acc, r in zip(outs, rest):
            acc.append(r)
    return (yp, ys) + tuple(jnp.stack(o, axis=0) for o in outs)
```

```python
import functools

import jax
import jax.numpy as jnp
from jax import lax
from jax.experimental import pallas as pl
from jax.experimental.pallas import tpu as pltpu

D_MODEL = 1024
GM_WIDTH = 512
GM_GROUPS = 4
GM_GROUP_DIM = 128
GM_CHUNK = 128
SB_WIDTH = 512
SB_HEAD_DIM = 64
SB_HEADS = 8
N_EXPERT_GROUPS = 4
EXPERTS_PER_GROUP = 4
N_EXPERTS = 16
EXPERT_FF = 256
GROUP_FF = EXPERTS_PER_GROUP * EXPERT_FF
EPS = 1e-6

LANES = 128
SUBLANES = 8
HEAD_PAIRS = SB_WIDTH // LANES
SB_BLOCK = 128
SB_WIDE = 2 * SB_BLOCK
SB_FIRST = 3 * SB_BLOCK
SB_PER_STEP = 8
CACHE_BLOCK = 256
MOE_SUB = 256
MOE_CAP = 96
MOE_SHORT = 80
MOE_SLOTS = N_EXPERT_GROUPS * MOE_CAP
MOE_SUPER = 8
SLOT_LANE = 0
VMEM_LIMIT = 56 * 1024 * 1024

LOG2_E = 1.4426950408889634
Q_SCALE = SB_HEAD_DIM ** -0.5 * LOG2_E
SB_UNDERFLOW_BITS = 160.0
MASKED = -1e30

F32 = jnp.float32
BF16 = jnp.bfloat16
_NT = (((1,), (1,)), ((), ()))


def _gelu_tanh(x):
    return 0.5 * x * (1.0 + jnp.tanh(0.7978845608028654 * (x + 0.044715 * (x * x * x))))


def _softplus2(z):
    return jnp.maximum(z, 0.0) + jnp.log2(1.0 + jnp.exp2(-jnp.abs(z)))


def _suffix_matrix(n):
    r = lax.broadcasted_iota(jnp.int32, (n, n), 0)
    c = lax.broadcasted_iota(jnp.int32, (n, n), 1)
    return jnp.where(r > c, 1.0, 0.0).astype(BF16)


def _mixer_in_body(x_ref, g1_ref, w_in_ref, gvg_ref, qg_ref, kg_ref, ws_ref, bst_ref, bd_ref,
                   gm_ref, kf_ref, vf_ref, *rest, chunk, stream):
    x = x_ref[...]
    tm = x.shape[0]
    xn = x * lax.rsqrt(jnp.mean(x * x, axis=-1, keepdims=True) + EPS) * g1_ref[...]
    proj = jnp.dot(xn.astype(BF16), w_in_ref[...], preferred_element_type=F32)

    gu = _gelu_tanh(proj[:, :GM_WIDTH])
    gvr = _gelu_tanh(proj[:, GM_WIDTH:2 * GM_WIDTH])
    gv = gvr * lax.rsqrt(jnp.mean(gvr * gvr, axis=-1, keepdims=True) + EPS) * gvg_ref[...]

    o = 2 * GM_WIDTH
    q = proj[:, o:o + SB_WIDTH]
    k = proj[:, o + SB_WIDTH:o + 2 * SB_WIDTH]
    v = proj[:, o + 2 * SB_WIDTH:]
    bd = bd_ref[...]
    qms = jnp.dot((q * q).astype(BF16), bd, preferred_element_type=F32)
    kms = jnp.dot((k * k).astype(BF16), bd, preferred_element_type=F32)
    qs = q * lax.rsqrt(qms + EPS) * qg_ref[...] * Q_SCALE
    kn = k * lax.rsqrt(kms + EPS) * kg_ref[...]
    if stream:
        kf_ref[...] = kn
        vf_ref[...] = v
        rest[0][...] = qs
        rest[1][...] = gv
    else:
        kt = kn.T
        kf_ref[...] = kt
        vf_ref[...] = v.T
        qs_ref, kb_ref, vb_ref = rest
        qs_ref[...] = qs.astype(BF16)
        kb_ref[...] = kt.astype(BF16)
        vb_ref[...] = v.astype(BF16)

    r = lax.broadcasted_iota(jnp.int32, (chunk, chunk), 0)
    c = lax.broadcasted_iota(jnp.int32, (chunk, chunk), 1)
    gvb = gv.astype(BF16)
    chunks = [slice(ci * chunk, (ci + 1) * chunk) for ci in range(tm // chunk)]
    for g in range(GM_GROUPS):
        w = jnp.where(r >= c, ws_ref[g, :chunk, :chunk], 0.0).astype(BF16)
        b = bst_ref[:chunk, g:g + 1]
        cols = slice(g * GM_GROUP_DIM, (g + 1) * GM_GROUP_DIM)
        mixed = jnp.dot(w, jnp.concatenate([gvb[rows, cols] for rows in chunks], axis=1),
                        preferred_element_type=F32)
        for ci, rows in enumerate(chunks):
            part = mixed[:, ci * GM_GROUP_DIM:(ci + 1) * GM_GROUP_DIM] + b
            gm_ref[rows, cols] = (gu[rows, cols] * part).astype(BF16)


def _mixer_in(x2d, params, *, tm, chunk, stream, seq=None):
    n = x2d.shape[0]
    assert n % tm == 0 and tm % chunk == 0 and chunk <= GM_CHUNK, (n, tm, chunk)
    full = lambda a: pl.BlockSpec(a.shape, lambda i: (0,) * a.ndim)
    spec = lambda s: pl.BlockSpec((tm,) + s.shape[1:], lambda i: (i,) + (0,) * (len(s.shape) - 1))
    gm = jax.ShapeDtypeStruct((n, GM_WIDTH), BF16)
    if stream:
        rows = jax.ShapeDtypeStruct((n, SB_WIDTH), F32)
        out_shape = [gm, rows, rows,
                     rows,
                     jax.ShapeDtypeStruct((n, GM_WIDTH), F32)]
        out_specs = [spec(s) for s in out_shape]
    else:
        per_seq = seq // tm
        rows_t = jax.ShapeDtypeStruct((n // seq, SB_WIDTH, seq), F32)
        spec_t = pl.BlockSpec((None, SB_WIDTH, tm), lambda i: (i // per_seq, 0, i % per_seq))
        dense = jax.ShapeDtypeStruct((n, SB_WIDTH), BF16)
        dense_t = jax.ShapeDtypeStruct(rows_t.shape, BF16)
        out_shape = [gm, rows_t, rows_t, dense, dense_t, dense]
        out_specs = [spec(gm), spec_t, spec_t, spec(dense), spec_t, spec(dense)]
    return pl.pallas_call(
        functools.partial(_mixer_in_body, chunk=chunk, stream=stream),
        grid=(n // tm,),
        in_specs=[pl.BlockSpec((tm, D_MODEL), lambda i: (i, 0))] + [full(a) for a in params],
        out_specs=out_specs,
        out_shape=out_shape,
        compiler_params=pltpu.CompilerParams(
            dimension_semantics=("parallel",), vmem_limit_bytes=VMEM_LIMIT),
        name="mixer_in_stream" if stream else "mixer_in",
    )(x2d, *params)


def _prompt_attn_body(q_ref, k_ref, v_ref, sfx_ref, sfx_wide_ref, sfx_first_ref, o_ref, *scratch):
    blocks = q_ref.shape[0] // SB_BLOCK

    def one(sub, carry):
        rows = pl.ds(pl.multiple_of(sub * SB_BLOCK, SB_BLOCK), SB_BLOCK)
        _prompt_attn_block(pl.program_id(1) * blocks + sub, q_ref.at[rows], k_ref, v_ref, sfx_ref,
                           sfx_wide_ref, sfx_first_ref, o_ref.at[rows], *scratch)
        return carry

    lax.fori_loop(0, blocks, one, 0)


def _prompt_attn_block(i, q_ref, k_ref, v_ref, sfx_ref, sfx_wide_ref, sfx_first_ref, o_ref,
                       qst_ref, c_ref, acc_ref):
    lane = lax.broadcasted_iota(jnp.int32, (SB_BLOCK, LANES), 1)
    first = lane < SB_HEAD_DIM
    for hp in range(HEAD_PAIRS):
        q = q_ref[:, hp * LANES:(hp + 1) * LANES]
        zero = jnp.zeros_like(q)
        qst_ref[hp, :SB_BLOCK, :] = jnp.where(first, q, zero)
        qst_ref[hp, SB_BLOCK:, :] = jnp.where(first, zero, q)
    m = 2 * SB_BLOCK
    suffix = {SB_BLOCK: sfx_ref, SB_WIDE: sfx_wide_ref, SB_FIRST: sfx_first_ref}

    def sweep(j, keys, causal, fresh):
        rows = pl.ds(pl.multiple_of(j * SB_BLOCK, SB_BLOCK), keys)
        cols = lambda hp: slice(hp * LANES, (hp + 1) * LANES)
        z = jnp.concatenate(
            [jnp.dot(qst_ref[hp], k_ref[cols(hp), rows], preferred_element_type=F32)
             for hp in range(HEAD_PAIRS)], axis=0)
        if causal is not None:
            old = keys - SB_BLOCK
            newest = jnp.where(causal, z[:, old:], MASKED)
            z = newest if old == 0 else jnp.concatenate([z[:, :old], newest], axis=1)
        sp = _softplus2(z)
        c0 = jnp.zeros((HEAD_PAIRS * m, 1), F32) if fresh else c_ref[...]
        later = jnp.dot(sp.astype(BF16), suffix[keys][...], preferred_element_type=F32) + c0
        w = jnp.exp2(z - sp - later)
        wb = w.astype(BF16)
        pv = jnp.concatenate(
            [jnp.dot(wb[hp * m:(hp + 1) * m, :], v_ref[rows, cols(hp)], preferred_element_type=F32)
             for hp in range(HEAD_PAIRS)], axis=0)
        acc_ref[...] = pv if fresh else acc_ref[...] + pv
        c_ref[...] = c0 + jnp.sum(sp, axis=-1, keepdims=True)

    below = SB_FIRST // SB_BLOCK - 1
    t = lax.broadcasted_iota(jnp.int32, (HEAD_PAIRS * m, SB_BLOCK), 0) % SB_BLOCK
    s = lax.broadcasted_iota(jnp.int32, (HEAD_PAIRS * m, SB_BLOCK), 1)

    @pl.when(i >= below)
    def _():
        sweep(i - below, SB_FIRST, s < t, True)

    @pl.when(i < below)
    def _():
        sweep(i, SB_BLOCK, s < t, True)

    j0 = jnp.where(i >= below, i - below, i)

    def least_carry():
        return jnp.min(c_ref[...])

    per_wide = SB_WIDE // SB_BLOCK
    n_wide = j0 // per_wide

    def more(state):
        jj, carry = state
        return (jj < n_wide) & (carry < SB_UNDERFLOW_BITS)

    def body(state):
        jj, _ = state
        sweep(j0 - per_wide * (jj + 1), SB_WIDE, None, False)
        return jj + 1, least_carry()

    done, carry = lax.while_loop(more, body, (jnp.int32(0), least_carry()))

    @pl.when((done == n_wide) & (j0 % per_wide == 1) & (carry < SB_UNDERFLOW_BITS))
    def _():
        sweep(0, SB_BLOCK, None, False)

    for hp in range(HEAD_PAIRS):
        lo = hp * m
        o_ref[:, hp * LANES:(hp + 1) * LANES] = jnp.where(
            first, acc_ref[lo:lo + SB_BLOCK, :], acc_ref[lo + SB_BLOCK:lo + m, :]).astype(o_ref.dtype)


def _prompt_attn(qs, kt, vb):
    b, s, _ = qs.shape
    assert s % SB_BLOCK == 0, s
    sfx = tuple(_suffix_matrix(keys) for keys in (SB_BLOCK, SB_WIDE, SB_FIRST))
    per_step = min(SB_PER_STEP * SB_BLOCK, s)
    assert s % per_step == 0, s
    q_spec = pl.BlockSpec((None, per_step, SB_WIDTH), lambda bi, i: (bi, i, 0))
    kt_spec = pl.BlockSpec((None, SB_WIDTH, s), lambda bi, i: (bi, 0, 0))
    kv_spec = pl.BlockSpec((None, s, SB_WIDTH), lambda bi, i: (bi, 0, 0))
    m = 2 * SB_BLOCK
    return pl.pallas_call(
        _prompt_attn_body,
        grid=(b, s // per_step),
        in_specs=[q_spec, kt_spec, kv_spec] + [pl.BlockSpec(a.shape, lambda bi, i: (0, 0)) for a in sfx],
        out_specs=q_spec,
        out_shape=jax.ShapeDtypeStruct((b, s, SB_WIDTH), BF16),
        scratch_shapes=[pltpu.VMEM((HEAD_PAIRS, m, LANES), BF16),
                        pltpu.VMEM((HEAD_PAIRS * m, 1), F32),
                        pltpu.VMEM((HEAD_PAIRS * m, LANES), F32)],
        compiler_params=pltpu.CompilerParams(
            dimension_semantics=("parallel", "arbitrary"), vmem_limit_bytes=VMEM_LIMIT),
        name="prompt_attn",
    )(qs, kt, vb, *sfx)


def _sample_attn_body(q_ref, kn_ref, vn_ref, ck_hbm, cv_hbm, o_ref, kbuf, vbuf, sems, c_ref, acc_ref,
                      *, layer):
    b = pl.program_id(0)
    n = q_ref.shape[0]
    m = SB_HEADS * n
    nblk = ck_hbm.shape[-1] // CACHE_BLOCK

    def copies(stream, blk, slot):
        cols = pl.ds(blk * CACHE_BLOCK, CACHE_BLOCK)
        return (pltpu.make_async_copy(ck_hbm.at[layer, stream, :, :, cols], kbuf.at[slot], sems.at[slot, 0]),
                pltpu.make_async_copy(cv_hbm.at[layer, stream, :, :, cols], vbuf.at[slot], sems.at[slot, 1]))

    def start(stream, blk, slot):
        for cp in copies(stream, blk, slot):
            cp.start()

    def wait(stream, blk, slot):
        for cp in copies(stream, blk, slot):
            cp.wait()

    def slot_of(jj):
        return 2 + jj % 2

    @pl.when(b == 0)
    def _():
        start(0, nblk - 1, 0)

    @pl.when(b + 1 < pl.num_programs(0))
    def _():
        start(b + 1, nblk - 1, (b + 1) % 2)

    head = lambda a, h: a[:, h * SB_HEAD_DIM:(h + 1) * SB_HEAD_DIM]
    q_all = q_ref[...].astype(BF16)
    q = [head(q_all, h) for h in range(SB_HEADS)]

    def sweep(slot, newest, suffix, causal, fresh):
        def logits(h):
            z = jnp.dot(q[h], kbuf[slot, h].astype(BF16), preferred_element_type=F32)
            if newest is None:
                return z
            return jnp.concatenate(
                [z, lax.dot_general(q[h], newest[0][h], _NT, preferred_element_type=F32)], axis=1)

        z = jnp.concatenate([logits(h) for h in range(SB_HEADS)], axis=0)
        if causal is not None:
            z = jnp.where(causal, z, MASKED)
        sp = _softplus2(z)
        c0 = jnp.zeros((m, 1), F32) if fresh else c_ref[...]
        later = jnp.dot(sp.astype(BF16), suffix, preferred_element_type=F32) + c0
        w = jnp.exp2(z - sp - later)
        wb = w.astype(BF16)
        pvs = []
        for h in range(SB_HEADS):
            wh = wb[h * n:(h + 1) * n, :]
            pv = lax.dot_general(wh[:, :CACHE_BLOCK], vbuf[slot, h].astype(BF16), _NT,
                                 preferred_element_type=F32)
            if newest is not None:
                pv = pv + jnp.dot(wh[:, CACHE_BLOCK:], newest[1][h], preferred_element_type=F32)
            pvs.append(pv)
        pv = jnp.stack(pvs)
        acc_ref[...] = pv if fresh else acc_ref[...] + pv
        c_ref[...] = c0 + jnp.sum(sp, axis=-1, keepdims=True)

    pad = jnp.zeros((LANES - n, SB_HEAD_DIM), BF16)
    kn_all, vn_all = kn_ref[...].astype(BF16), vn_ref[...].astype(BF16)
    newest = ([jnp.concatenate([head(kn_all, h), pad], axis=0) for h in range(SB_HEADS)],
              [jnp.concatenate([head(vn_all, h), pad], axis=0) for h in range(SB_HEADS)])
    t = lax.broadcasted_iota(jnp.int32, (m, CACHE_BLOCK + LANES), 0) % n
    s = lax.broadcasted_iota(jnp.int32, (m, CACHE_BLOCK + LANES), 1) - CACHE_BLOCK
    wait(b, nblk - 1, b % 2)
    sweep(b % 2, newest, _suffix_matrix(CACHE_BLOCK + LANES), s < t, True)

    suffix = _suffix_matrix(CACHE_BLOCK)

    def more(state):
        jj, carry = state
        return (jj < nblk) & (carry < SB_UNDERFLOW_BITS)

    def body(state):
        jj, _ = state
        blk = nblk - 1 - jj
        slot = slot_of(jj)
        wait(b, blk, slot)

        @pl.when(blk > 0)
        def _():
            start(b, blk - 1, slot_of(jj + 1))

        sweep(slot, None, suffix, None, False)
        return jj + 1, jnp.min(c_ref[...])

    state = (jnp.int32(1), jnp.min(c_ref[...]))

    if nblk > 1:
        @pl.when(more(state))
        def _():
            start(b, nblk - 2, slot_of(1))

    done, _ = lax.while_loop(more, body, state)

    @pl.when((done > 1) & (done < nblk))
    def _():
        wait(b, nblk - 1 - done, slot_of(done))

    for h in range(SB_HEADS):
        o_ref[:, h, :] = acc_ref[h]


def _sample_attn(qs, kn, vn, cache_k, cache_v, layer):
    db, n = qs.shape[:2]
    past = cache_k.shape[-1]
    assert n <= LANES and past >= CACHE_BLOCK and past % CACHE_BLOCK == 0, (n, past)
    m = SB_HEADS * n
    new_spec = pl.BlockSpec((None, n, SB_WIDTH), lambda b: (b, 0, 0))
    hbm = pl.BlockSpec(memory_space=pl.ANY)
    buf = pltpu.VMEM((4, SB_HEADS, SB_HEAD_DIM, CACHE_BLOCK), F32)
    return pl.pallas_call(
        functools.partial(_sample_attn_body, layer=layer),
        grid=(db,),
        in_specs=[new_spec, new_spec, new_spec, hbm, hbm],
        out_specs=pl.BlockSpec((None, n, SB_HEADS, SB_HEAD_DIM), lambda b: (b, 0, 0, 0)),
        out_shape=jax.ShapeDtypeStruct((db, n, SB_HEADS, SB_HEAD_DIM), F32),
        scratch_shapes=[buf, buf, pltpu.SemaphoreType.DMA((4, 2)),
                        pltpu.VMEM((m, 1), F32),
                        pltpu.VMEM((SB_HEADS, n, SB_HEAD_DIM), F32)],
        compiler_params=pltpu.CompilerParams(
            dimension_semantics=("arbitrary",), vmem_limit_bytes=VMEM_LIMIT),
        name="sample_attn",
    )(qs, kn, vn, cache_k, cache_v)


def _route(logits):
    lane = lax.broadcasted_iota(jnp.int32, logits.shape, 1).astype(F32)
    neg = jnp.float32(-jnp.inf)
    big = jnp.float32(1 << 20)
    rmax = lambda a: jnp.max(a, axis=-1, keepdims=True)
    rmin = lambda a: jnp.min(a, axis=-1, keepdims=True)
    rsum = lambda a: jnp.sum(a, axis=-1, keepdims=True)

    gmask = lane < N_EXPERT_GROUPS
    gl = jnp.where(gmask, logits, neg)
    gmax = rmax(gl)
    g_sel = rmin(jnp.where(gl == gmax, lane, big))
    p_sel = 1.0 / rsum(jnp.where(gmask, jnp.exp(gl - gmax), 0.0))

    e_lo = N_EXPERT_GROUPS + g_sel * EXPERTS_PER_GROUP
    emask = (lane >= e_lo) & (lane < e_lo + EXPERTS_PER_GROUP)
    el = jnp.where(emask, logits, neg)
    emax = rmax(el)
    ex = jnp.where(emask, jnp.exp(el - emax), 0.0)
    p_e = ex / rsum(ex)
    pm = jnp.where(emask, p_e, -1.0)
    v1 = rmax(pm)
    i1 = rmin(jnp.where(pm == v1, lane, big))
    pm2 = jnp.where(lane == i1, -1.0, pm)
    v2 = rmax(pm2)
    i2 = rmin(jnp.where(pm2 == v2, lane, big))
    tot = v1 + v2
    gate1 = p_sel * v1 / tot
    gate2 = p_sel * v2 / tot
    comb = jnp.where(lane == i1, gate1, 0.0) + jnp.where(lane == i2, gate2, 0.0)
    gates = jnp.where(lane == i1 - e_lo, gate1, 0.0) + jnp.where(lane == i2 - e_lo, gate2, 0.0)
    return comb, g_sel, gates


def _rmsnorm_rows(h, g):
    return h * lax.rsqrt(jnp.mean(h * h, axis=-1, keepdims=True) + EPS) * g


def _slot_matrix(slot):
    lane = lax.broadcasted_iota(jnp.int32, (slot.shape[0], MOE_SLOTS), 1)
    return jnp.where(lane == slot.astype(jnp.int32), 1.0, 0.0)


def _post_attn_body(x_ref, gm_ref, sb_ref, wo_ref, g2_ref, wr_ref, br_ref,
                    h_ref, route_ref, bucket_ref, gate_ref, cnt_ref):
    h = (x_ref[...]
         + jnp.dot(gm_ref[...], wo_ref[:GM_WIDTH, :], preferred_element_type=F32)
         + jnp.dot(sb_ref[...], wo_ref[GM_WIDTH:, :], preferred_element_type=F32))
    h_ref[...] = h
    hn = _rmsnorm_rows(h, g2_ref[...])
    hi = hn.astype(BF16)
    lo = (hn - hi.astype(F32)).astype(BF16)
    both = jnp.dot(hi, wr_ref[...], preferred_element_type=F32)
    logits = (both[:, :LANES] + jnp.dot(lo, wr_ref[:, :LANES], preferred_element_type=F32)
              + both[:, LANES:]) + br_ref[...]
    comb, g_sel, gates = _route(logits)

    subs = [slice(st * MOE_SUB, (st + 1) * MOE_SUB) for st in range(h.shape[0] // MOE_SUB)]
    lane = lax.broadcasted_iota(jnp.int32, comb.shape, 1).astype(F32)
    r = lax.broadcasted_iota(jnp.int32, (MOE_SUB, MOE_SUB), 0)
    c = lax.broadcasted_iota(jnp.int32, (MOE_SUB, MOE_SUB), 1)
    earlier = jnp.where(r > c, 1.0, 0.0).astype(BF16)
    onehot = jnp.where(lane == g_sel, 1.0, 0.0)
    onehot_b = onehot.astype(BF16)
    before = jnp.concatenate(
        [jnp.dot(earlier, onehot_b[rows], preferred_element_type=F32) for rows in subs], axis=0)
    rank = jnp.sum(onehot * before, axis=-1, keepdims=True)
    slot = jnp.where(rank < MOE_CAP, g_sel * MOE_CAP + rank, -1.0)
    route_ref[...] = jnp.where(lane == SLOT_LANE, slot, comb)
    g1 = gates.astype(BF16)
    g2 = (gates - g1.astype(F32)).astype(BF16)
    g3 = (gates - g1.astype(F32) - g2.astype(F32)).astype(BF16)
    g12 = jnp.concatenate([g1, g2], axis=1)
    for st, rows in enumerate(subs):
        place = _slot_matrix(slot[rows]).T.astype(BF16)
        bucket_ref[st] = jnp.dot(place, hi[rows], preferred_element_type=F32).astype(BF16)
        placed = jnp.dot(place, g12[rows], preferred_element_type=F32)
        gate_ref[st] = (placed[:, :LANES] + placed[:, LANES:]
                        + jnp.dot(place, g3[rows], preferred_element_type=F32))
        cnt_ref[st] = jnp.broadcast_to(jnp.sum(onehot[rows], axis=0, keepdims=True), cnt_ref.shape[1:])


def _moe_ffn_body(long_ref, x_ref, gate_ref, xs_ref, gates_ref, wg_ref, wu_ref, wd_ref, y_ref, ys_ref,
                  wgb_ref, wub_ref, wdb_ref, *, prompt_steps):
    g, s = pl.program_id(0), pl.program_id(1)

    @pl.when(s == 0)
    def _():
        wgb_ref[...] = wg_ref[...].astype(BF16)
        wub_ref[...] = wu_ref[...].astype(BF16)
        wdb_ref[...] = wd_ref[...].astype(BF16)

    def ffn(ins, outs, used):
        x = jnp.concatenate([r[:, 0, :used, :].reshape(-1, D_MODEL) for r, _ in ins], axis=0)
        gates = jnp.concatenate([r[:, 0, :used, :].reshape(-1, LANES) for _, r in ins], axis=0)
        parts = []
        for e in range(EXPERTS_PER_GROUP):
            hg = jnp.dot(x, wgb_ref[e], preferred_element_type=F32)
            hu = jnp.dot(x, wub_ref[e], preferred_element_type=F32)
            parts.append((hg / (1.0 + jnp.exp(-hg)) * hu * gates[:, e:e + 1]).astype(BF16))
        a = jnp.concatenate(parts, axis=-1)
        y = jnp.dot(a, wdb_ref[...].reshape(GROUP_FF, D_MODEL), preferred_element_type=F32)
        row = 0
        for out_ref in outs:
            rows = out_ref.shape[0] * used
            out_ref[:, 0, :used, :] = y[row:row + rows].reshape(out_ref.shape[0], used, D_MODEL)
            if used < MOE_CAP:
                out_ref[:, 0, used:, :] = jnp.zeros((out_ref.shape[0], MOE_CAP - used, D_MODEL), F32)
            row += rows

    def step(used):
        @pl.when(s < prompt_steps - 1)
        def _():
            ffn([(x_ref, gate_ref)], [y_ref], used)

        @pl.when(s == prompt_steps - 1)
        def _():
            ffn([(x_ref, gate_ref), (xs_ref, gates_ref)], [y_ref, ys_ref], used)

    @pl.when(long_ref[g, s] == 0)
    def _():
        step(MOE_SHORT)

    @pl.when(long_ref[g, s] != 0)
    def _():
        step(MOE_CAP)


def _moe_combine_body(h_ref, route_ref, ys_ref, o_ref):
    for st in range(h_ref.shape[0] // MOE_SUB):
        rows = slice(st * MOE_SUB, (st + 1) * MOE_SUB)
        pick = _slot_matrix(route_ref[rows, SLOT_LANE:SLOT_LANE + 1]).astype(BF16)
        ys = ys_ref[st]
        hi = ys.astype(BF16)
        lo = (ys - hi.astype(F32)).astype(BF16)
        o_ref[rows, :] = (h_ref[rows, :] + jnp.dot(pick, hi, preferred_element_type=F32)
                          + jnp.dot(pick, lo, preferred_element_type=F32))


def _moe_dense_body(h_ref, route_ref, g2_ref, wg_ref, wu_ref, wd_ref, o_ref, hn_ref):
    g = pl.program_id(1)

    @pl.when(g == 0)
    def _():
        h = h_ref[...]
        o_ref[...] = h
        hn_ref[...] = _rmsnorm_rows(h, g2_ref[...]).astype(BF16)

    hn = hn_ref[...]
    comb = route_ref[...]
    lane = lax.broadcasted_iota(jnp.int32, comb.shape, 1)
    parts = []
    for e in range(EXPERTS_PER_GROUP):
        gate = jnp.sum(jnp.where(lane == N_EXPERT_GROUPS + g * EXPERTS_PER_GROUP + e, comb, 0.0),
                       axis=-1, keepdims=True)
        hg = jnp.dot(hn, wg_ref[e], preferred_element_type=F32)
        hu = jnp.dot(hn, wu_ref[e], preferred_element_type=F32)
        parts.append((hg / (1.0 + jnp.exp(-hg)) * hu * gate).astype(BF16))
    a = jnp.concatenate(parts, axis=-1)
    o_ref[...] += jnp.dot(a, wd_ref[...].reshape(GROUP_FF, D_MODEL), preferred_element_type=F32)


def _group_spec(a, index):
    return pl.BlockSpec((EXPERTS_PER_GROUP,) + a.shape[1:], index)


def _post_attn(x2d, gm, sb, params, tm):
    n = x2d.shape[0]
    wo, g2, wr, br = params
    assert n % tm == 0 and tm % MOE_SUB == 0, (n, tm)
    n_sub, sub = n // MOE_SUB, tm // MOE_SUB
    row = lambda w: pl.BlockSpec((tm, w), lambda i: (i, 0))
    full = lambda a: pl.BlockSpec(a.shape, lambda i: (0,) * a.ndim)
    per_sub = lambda rows, w: pl.BlockSpec((sub, rows, w), lambda i: (i, 0, 0))
    return pl.pallas_call(
        _post_attn_body,
        grid=(n // tm,),
        in_specs=[row(D_MODEL), row(GM_WIDTH), row(SB_WIDTH),
                  full(wo), full(g2), full(wr), full(br)],
        out_specs=[row(D_MODEL), row(LANES), per_sub(MOE_SLOTS, D_MODEL),
                   per_sub(MOE_SLOTS, LANES), per_sub(SUBLANES, LANES)],
        out_shape=[jax.ShapeDtypeStruct((n, D_MODEL), F32),
                   jax.ShapeDtypeStruct((n, LANES), F32),
                   jax.ShapeDtypeStruct((n_sub, MOE_SLOTS, D_MODEL), BF16),
                   jax.ShapeDtypeStruct((n_sub, MOE_SLOTS, LANES), F32),
                   jax.ShapeDtypeStruct((n_sub, SUBLANES, LANES), F32)],
        compiler_params=pltpu.CompilerParams(
            dimension_semantics=("parallel",), vmem_limit_bytes=VMEM_LIMIT),
        name="post_attn",
    )(x2d, gm, sb, wo, g2, wr, br)


def _out_moe(prompt, sample, params, *, tm_prompt, tm_sample):
    wo, g2, wr, br, wg, wu, wd = params
    routed = [_post_attn(*grp, (wo, g2, wr, br), tm) for grp, tm in
              ((prompt, tm_prompt), (sample, tm_sample))]
    tms = (tm_prompt, tm_sample)
    params_2d = pltpu.CompilerParams(
        dimension_semantics=("parallel", "arbitrary"), vmem_limit_bytes=VMEM_LIMIT)

    def sparse():
        (_, _, bk_p, gt_p, cnt_p), (_, _, bk_s, gt_s, cnt_s) = routed
        n_sub, n_sub_s = bk_p.shape[0], bk_s.shape[0]
        sup = min(MOE_SUPER, n_sub)
        assert n_sub % sup == 0, (n_sub, sup)
        steps = n_sub // sup
        fullest = jnp.max(cnt_p[:, 0, :N_EXPERT_GROUPS].reshape(steps, sup, N_EXPERT_GROUPS), axis=1)
        fullest = fullest.at[steps - 1].max(jnp.max(cnt_s[:, 0, :N_EXPERT_GROUPS], axis=0))
        long_steps = (fullest.T > MOE_SHORT).astype(jnp.int32)
        by_group = lambda a: a.reshape(a.shape[0], N_EXPERT_GROUPS, MOE_CAP, a.shape[-1])
        bucket = lambda w: pl.BlockSpec((sup, 1, MOE_CAP, w), lambda g, s, flags: (s, g, 0, 0))
        bucket_s = lambda w: pl.BlockSpec((n_sub_s, 1, MOE_CAP, w), lambda g, s, flags: (0, g, 0, 0))
        ys = pl.pallas_call(
            functools.partial(_moe_ffn_body, prompt_steps=steps),
            grid_spec=pltpu.PrefetchScalarGridSpec(
                num_scalar_prefetch=1,
                grid=(N_EXPERT_GROUPS, steps),
                in_specs=[bucket(D_MODEL), bucket(LANES), bucket_s(D_MODEL), bucket_s(LANES)]
                         + [_group_spec(w, lambda g, s, flags: (g, 0, 0)) for w in (wg, wu, wd)],
                out_specs=[bucket(D_MODEL), bucket_s(D_MODEL)],
                scratch_shapes=[pltpu.VMEM((EXPERTS_PER_GROUP,) + w.shape[1:], BF16)
                                for w in (wg, wu, wd)]),
            out_shape=[jax.ShapeDtypeStruct((n, N_EXPERT_GROUPS, MOE_CAP, D_MODEL), F32)
                       for n in (n_sub, n_sub_s)],
            compiler_params=params_2d,
            name="moe_ffn",
        )(long_steps, by_group(bk_p), by_group(gt_p), by_group(bk_s), by_group(gt_s), wg, wu, wd)

        def combine(h, route, y, tm):
            n, sub = h.shape[0], tm // MOE_SUB
            row = lambda w: pl.BlockSpec((tm, w), lambda i: (i, 0))
            return pl.pallas_call(
                _moe_combine_body,
                grid=(n // tm,),
                in_specs=[row(D_MODEL), row(LANES),
                          pl.BlockSpec((sub, MOE_SLOTS, D_MODEL), lambda i: (i, 0, 0))],
                out_specs=row(D_MODEL),
                out_shape=jax.ShapeDtypeStruct((n, D_MODEL), F32),
                compiler_params=pltpu.CompilerParams(
                    dimension_semantics=("parallel",), vmem_limit_bytes=VMEM_LIMIT),
                name="moe_combine",
            )(h, route, y.reshape(-1, MOE_SLOTS, D_MODEL))

        return tuple(combine(r[0], r[1], y, tm) for r, y, tm in zip(routed, ys, tms))

    def dense():
        wgb, wub, wdb = (w.astype(BF16) for w in (wg, wu, wd))

        def all_experts(h, route, tm):
            row2 = lambda w: pl.BlockSpec((tm, w), lambda i, g: (i, 0))
            return pl.pallas_call(
                _moe_dense_body,
                grid=(h.shape[0] // tm, N_EXPERT_GROUPS),
                in_specs=[row2(D_MODEL), row2(LANES), pl.BlockSpec(g2.shape, lambda i, g: (0, 0))]
                         + [_group_spec(w, lambda i, g: (g, 0, 0)) for w in (wg, wu, wd)],
                out_specs=row2(D_MODEL),
                out_shape=jax.ShapeDtypeStruct(h.shape, F32),
                scratch_shapes=[pltpu.VMEM((tm, D_MODEL), BF16)],
                compiler_params=params_2d,
                name="moe_dense",
            )(h, route, g2, wgb, wub, wdb)

        return tuple(all_experts(r[0], r[1], tm) for r, tm in zip(routed, tms))

    most = jnp.maximum(jnp.max(routed[0][4]), jnp.max(routed[1][4]))
    return lax.cond(most <= MOE_CAP, sparse, dense)


def _layer(layer, xp, xs, cache_k, cache_v, norm1_g, w_in, gm_v_norm_g, gm_w_s, gm_b_s, q_norm_g,
           k_norm_g, w_out, norm2_g, w_router_group, b_router_group, w_router_expert,
           b_router_expert, w_gate, w_up, w_down):
    b, s, d = xp.shape
    db, n, _ = xs.shape
    head_of = jnp.arange(SB_WIDTH) // SB_HEAD_DIM
    bd = jnp.where(head_of[:, None] == head_of[None, :], 1.0 / SB_HEAD_DIM, 0.0).astype(BF16)
    in_params = (norm1_g[None, :], w_in.astype(BF16), gm_v_norm_g[None, :],
                 jnp.tile(q_norm_g, SB_HEADS)[None, :], jnp.tile(k_norm_g, SB_HEADS)[None, :],
                 gm_w_s, gm_b_s.T, bd)

    wr = jnp.concatenate(
        [w_router_group, jnp.transpose(w_router_expert, (1, 0, 2)).reshape(d, N_EXPERTS)], axis=1)
    wr = jnp.pad(wr, ((0, 0), (0, LANES - wr.shape[1])))
    wr_hi = wr.astype(BF16)
    wr_lo = (wr - wr_hi.astype(F32)).astype(BF16)
    br = jnp.pad(jnp.concatenate([b_router_group, b_router_expert.reshape(-1)]),
                 (0, LANES - N_EXPERT_GROUPS - N_EXPERTS))[None, :]

    moe_params = (w_out.astype(BF16), norm2_g[None, :], jnp.concatenate([wr_hi, wr_lo], axis=1), br,
                  w_gate, w_up, w_down)

    xp2 = xp.reshape(b * s, d)
    gm, kf, vf, qs, kb, vb = _mixer_in(xp2, in_params, tm=min(1024, s), chunk=GM_CHUNK, stream=False,
                                       seq=s)
    shp = (b, s, SB_WIDTH)
    sb = _prompt_attn(qs.reshape(shp), kb, vb.reshape(shp)).reshape(b * s, SB_WIDTH)

    xs2 = xs.reshape(db * n, d)
    gm_s, kf_s, vf_s, q_s, gv_s = _mixer_in(xs2, in_params, tm=db * n, chunk=n, stream=True)
    shs = (db, n, SB_WIDTH)
    frame_minor = lambda c: jnp.transpose(c, (0, 1, 3, 4, 2))
    sb_s = _sample_attn(q_s.reshape(shs), kf_s.reshape(shs), vf_s.reshape(shs),
                        frame_minor(cache_k), frame_minor(cache_v), layer)
    sb_s = sb_s.reshape(db * n, SB_WIDTH).astype(BF16)

    yp, ys = _out_moe((xp2, gm, sb), (xs2, gm_s, sb_s), moe_params,
                      tm_prompt=min(1024, b * s), tm_sample=db * n)
    yp, ys = yp.reshape(b, s, d), ys.reshape(db, n, d)

    heads = (SB_HEADS, SB_HEAD_DIM)
    rows = lambda a: jnp.transpose(a.reshape(b, *heads, s), (0, 3, 1, 2))
    return (yp, ys, rows(kf), rows(vf),
            kf_s.reshape(db, n, *heads), vf_s.reshape(db, n, *heads),
            gv_s.reshape(db, n, GM_GROUPS, GM_GROUP_DIM))


def kernel(x_prompt, x_sample, cache_sb_k, cache_sb_v, norm1_g, w_in, gm_v_norm_g, gm_w_s, gm_b_s, q_norm_g, k_norm_g, w_out, norm2_g, w_router_group, b_router_group, w_router_expert, b_router_expert, w_gate, w_up, w_down):
    depth = w_in.shape[0]
    yp, ys = x_prompt, x_sample
    outs = [[] for _ in range(5)]
    for l in range(depth):
        yp, ys, *rest = _layer(
            l, yp, ys, cache_sb_k, cache_sb_v, norm1_g[l], w_in[l], gm_v_norm_g[l], gm_w_s[l],
            gm_b_s[l], q_norm_g[l], k_norm_g[l], w_out[l], norm2_g[l], w_router_group[l],
            b_router_group[l], w_router_expert[l], b_router_expert[l], w_gate[l], w_up[l],
            w_down[l])
        for acc, r in zip(outs, rest):
            acc.append(r)
    return (yp, ys) + tuple(jnp.stack(o, axis=0) for o in outs)
```

```python
import functools

import jax
import jax.numpy as jnp
from jax import lax
from jax.experimental import pallas as pl
from jax.experimental.pallas import tpu as pltpu

D_MODEL = 1024
GM_WIDTH = 512
GM_GROUPS = 4
GM_GROUP_DIM = 128
GM_CHUNK = 128
SB_WIDTH = 512
SB_HEAD_DIM = 64
SB_HEADS = 8
N_EXPERT_GROUPS = 4
EXPERTS_PER_GROUP = 4
N_EXPERTS = 16
EXPERT_FF = 256
GROUP_FF = EXPERTS_PER_GROUP * EXPERT_FF
EPS = 1e-6

LANES = 128
SUBLANES = 8
HEAD_PAIRS = SB_WIDTH // LANES
SB_BLOCK = 128
SB_WIDE = 2 * SB_BLOCK
SB_FIRST = 3 * SB_BLOCK
SB_PER_STEP = 8
CACHE_BLOCK = 256
MOE_SUB = 256
MOE_CAP = 96
MOE_SHORT = 80
MOE_SLOTS = N_EXPERT_GROUPS * MOE_CAP
MOE_SUPER = 8
SLOT_LANE = 0
VMEM_LIMIT = 56 * 1024 * 1024

LOG2_E = 1.4426950408889634
Q_SCALE = SB_HEAD_DIM ** -0.5 * LOG2_E
SB_UNDERFLOW_BITS = 160.0
MASKED = -1e30

F32 = jnp.float32
BF16 = jnp.bfloat16
_NT = (((1,), (1,)), ((), ()))


def _gelu_tanh(x):
    return 0.5 * x * (1.0 + jnp.tanh(0.7978845608028654 * (x + 0.044715 * (x * x * x))))


def _softplus2(z):
    return jnp.maximum(z, 0.0) + jnp.log2(1.0 + jnp.exp2(-jnp.abs(z)))


def _suffix_matrix(n):
    r = lax.broadcasted_iota(jnp.int32, (n, n), 0)
    c = lax.broadcasted_iota(jnp.int32, (n, n), 1)
    return jnp.where(r > c, 1.0, 0.0).astype(BF16)


def _mixer_in_body(x_ref, g1_ref, w_in_ref, gvg_ref, qg_ref, kg_ref, ws_ref, bst_ref, bd_ref,
                   gm_ref, kf_ref, vf_ref, *rest, chunk, stream):
    x = x_ref[...]
    tm = x.shape[0]
    xn = x * lax.rsqrt(jnp.mean(x * x, axis=-1, keepdims=True) + EPS) * g1_ref[...]
    proj = jnp.dot(xn.astype(BF16), w_in_ref[...], preferred_element_type=F32)

    gu = _gelu_tanh(proj[:, :GM_WIDTH])
    gvr = _gelu_tanh(proj[:, GM_WIDTH:2 * GM_WIDTH])
    gv = gvr * lax.rsqrt(jnp.mean(gvr * gvr, axis=-1, keepdims=True) + EPS) * gvg_ref[...]

    o = 2 * GM_WIDTH
    q = proj[:, o:o + SB_WIDTH]
    k = proj[:, o + SB_WIDTH:o + 2 * SB_WIDTH]
    v = proj[:, o + 2 * SB_WIDTH:]
    bd = bd_ref[...]
    qms = jnp.dot((q * q).astype(BF16), bd, preferred_element_type=F32)
    kms = jnp.dot((k * k).astype(BF16), bd, preferred_element_type=F32)
    qs = q * lax.rsqrt(qms + EPS) * qg_ref[...] * Q_SCALE
    kn = k * lax.rsqrt(kms + EPS) * kg_ref[...]
    if stream:
        kf_ref[...] = kn
        vf_ref[...] = v
        rest[0][...] = qs
        rest[1][...] = gv
    else:
        kt = kn.T
        kf_ref[...] = kt
        vf_ref[...] = v.T
        qs_ref, kb_ref, vb_ref = rest
        qs_ref[...] = qs.astype(BF16)
        kb_ref[...] = kt.astype(BF16)
        vb_ref[...] = v.astype(BF16)

    r = lax.broadcasted_iota(jnp.int32, (chunk, chunk), 0)
    c = lax.broadcasted_iota(jnp.int32, (chunk, chunk), 1)
    gvb = gv.astype(BF16)
    chunks = [slice(ci * chunk, (ci + 1) * chunk) for ci in range(tm // chunk)]
    for g in range(GM_GROUPS):
        w = jnp.where(r >= c, ws_ref[g, :chunk, :chunk], 0.0).astype(BF16)
        b = bst_ref[:chunk, g:g + 1]
        cols = slice(g * GM_GROUP_DIM, (g + 1) * GM_GROUP_DIM)
        mixed = jnp.dot(w, jnp.concatenate([gvb[rows, cols] for rows in chunks], axis=1),
                        preferred_element_type=F32)
        for ci, rows in enumerate(chunks):
            part = mixed[:, ci * GM_GROUP_DIM:(ci + 1) * GM_GROUP_DIM] + b
            gm_ref[rows, cols] = (gu[rows, cols] * part).astype(BF16)


def _mixer_in(x2d, params, *, tm, chunk, stream, seq=None):
    n = x2d.shape[0]
    assert n % tm == 0 and tm % chunk == 0 and chunk <= GM_CHUNK, (n, tm, chunk)
    full = lambda a: pl.BlockSpec(a.shape, lambda i: (0,) * a.ndim)
    spec = lambda s: pl.BlockSpec((tm,) + s.shape[1:], lambda i: (i,) + (0,) * (len(s.shape) - 1))
    gm = jax.ShapeDtypeStruct((n, GM_WIDTH), BF16)
    if stream:
        rows = jax.ShapeDtypeStruct((n, SB_WIDTH), F32)
        out_shape = [gm, rows, rows,
                     rows,
                     jax.ShapeDtypeStruct((n, GM_WIDTH), F32)]
        out_specs = [spec(s) for s in out_shape]
    else:
        per_seq = seq // tm
        rows_t = jax.ShapeDtypeStruct((n // seq, SB_WIDTH, seq), F32)
        spec_t = pl.BlockSpec((None, SB_WIDTH, tm), lambda i: (i // per_seq, 0, i % per_seq))
        dense = jax.ShapeDtypeStruct((n, SB_WIDTH), BF16)
        dense_t = jax.ShapeDtypeStruct(rows_t.shape, BF16)
        out_shape = [gm, rows_t, rows_t, dense, dense_t, dense]
        out_specs = [spec(gm), spec_t, spec_t, spec(dense), spec_t, spec(dense)]
    return pl.pallas_call(
        functools.partial(_mixer_in_body, chunk=chunk, stream=stream),
        grid=(n // tm,),
        in_specs=[pl.BlockSpec((tm, D_MODEL), lambda i: (i, 0))] + [full(a) for a in params],
        out_specs=out_specs,
        out_shape=out_shape,
        compiler_params=pltpu.CompilerParams(
            dimension_semantics=("parallel",), vmem_limit_bytes=VMEM_LIMIT),
        name="mixer_in_stream" if stream else "mixer_in",
    )(x2d, *params)


def _prompt_attn_body(q_ref, k_ref, v_ref, sfx_ref, sfx_wide_ref, o_ref, *scratch):
    blocks = q_ref.shape[0] // SB_BLOCK

    def one(sub, carry):
        rows = pl.ds(pl.multiple_of(sub * SB_BLOCK, SB_BLOCK), SB_BLOCK)
        _prompt_attn_block(pl.program_id(1) * blocks + sub, q_ref.at[rows], k_ref, v_ref, sfx_ref,
                           sfx_wide_ref, o_ref.at[rows], *scratch)
        return carry

    lax.fori_loop(0, blocks, one, 0)


def _prompt_attn_block(i, q_ref, k_ref, v_ref, sfx_ref, sfx_wide_ref, o_ref, qst_ref, c_ref, acc_ref):
    lane = lax.broadcasted_iota(jnp.int32, (SB_BLOCK, LANES), 1)
    first = lane < SB_HEAD_DIM
    for hp in range(HEAD_PAIRS):
        q = q_ref[:, hp * LANES:(hp + 1) * LANES]
        zero = jnp.zeros_like(q)
        qst_ref[hp, :SB_BLOCK, :] = jnp.where(first, q, zero)
        qst_ref[hp, SB_BLOCK:, :] = jnp.where(first, zero, q)
    m = 2 * SB_BLOCK
    suffix = {SB_BLOCK: sfx_ref, SB_WIDE: sfx_wide_ref}

    def sweep(j, keys, causal, fresh):
        rows = pl.ds(pl.multiple_of(j * SB_BLOCK, SB_BLOCK), keys)
        cols = lambda hp: slice(hp * LANES, (hp + 1) * LANES)
        z = jnp.concatenate(
            [jnp.dot(qst_ref[hp], k_ref[cols(hp), rows], preferred_element_type=F32)
             for hp in range(HEAD_PAIRS)], axis=0)
        if causal is not None:
            old = keys - SB_BLOCK
            newest = jnp.where(causal, z[:, old:], MASKED)
            z = newest if old == 0 else jnp.concatenate([z[:, :old], newest], axis=1)
        sp = _softplus2(z)
        c0 = jnp.zeros((HEAD_PAIRS * m, 1), F32) if fresh else c_ref[...]
        spb = sp.astype(BF16)
        if keys == SB_FIRST:
            newest = sp[:, SB_WIDE:]
            later = jnp.concatenate(
                [jnp.dot(spb[:, :SB_WIDE], suffix[SB_WIDE][...], preferred_element_type=F32)
                 + jnp.sum(newest, axis=-1, keepdims=True),
                 jnp.dot(spb[:, SB_WIDE:], suffix[SB_BLOCK][...], preferred_element_type=F32)],
                axis=1) + c0
        else:
            later = jnp.dot(spb, suffix[keys][...], preferred_element_type=F32) + c0
        w = jnp.exp2(z - sp - later)
        wb = w.astype(BF16)
        pv = jnp.concatenate(
            [jnp.dot(wb[hp * m:(hp + 1) * m, :], v_ref[rows, cols(hp)], preferred_element_type=F32)
             for hp in range(HEAD_PAIRS)], axis=0)
        acc_ref[...] = pv if fresh else acc_ref[...] + pv
        c_ref[...] = c0 + jnp.sum(sp, axis=-1, keepdims=True)

    below = SB_FIRST // SB_BLOCK - 1
    t = lax.broadcasted_iota(jnp.int32, (HEAD_PAIRS * m, SB_BLOCK), 0) % SB_BLOCK
    s = lax.broadcasted_iota(jnp.int32, (HEAD_PAIRS * m, SB_BLOCK), 1)

    @pl.when(i >= below)
    def _():
        sweep(i - below, SB_FIRST, s < t, True)

    @pl.when(i < below)
    def _():
        sweep(i, SB_BLOCK, s < t, True)

    j0 = jnp.where(i >= below, i - below, i)

    def least_carry():
        return jnp.min(c_ref[...])

    per_wide = SB_WIDE // SB_BLOCK
    n_wide = j0 // per_wide

    def more(state):
        jj, carry = state
        return (jj < n_wide) & (carry < SB_UNDERFLOW_BITS)

    def body(state):
        jj, _ = state
        sweep(j0 - per_wide * (jj + 1), SB_WIDE, None, False)
        return jj + 1, least_carry()

    done, carry = lax.while_loop(more, body, (jnp.int32(0), least_carry()))

    @pl.when((done == n_wide) & (j0 % per_wide == 1) & (carry < SB_UNDERFLOW_BITS))
    def _():
        sweep(0, SB_BLOCK, None, False)

    for hp in range(HEAD_PAIRS):
        lo = hp * m
        o_ref[:, hp * LANES:(hp + 1) * LANES] = jnp.where(
            first, acc_ref[lo:lo + SB_BLOCK, :], acc_ref[lo + SB_BLOCK:lo + m, :]).astype(o_ref.dtype)


def _prompt_attn(qs, kt, vb):
    b, s, _ = qs.shape
    assert s % SB_BLOCK == 0, s
    sfx = tuple(_suffix_matrix(keys) for keys in (SB_BLOCK, SB_WIDE))
    per_step = min(SB_PER_STEP * SB_BLOCK, s)
    assert s % per_step == 0, s
    q_spec = pl.BlockSpec((None, per_step, SB_WIDTH), lambda bi, i: (bi, i, 0))
    kt_spec = pl.BlockSpec((None, SB_WIDTH, s), lambda bi, i: (bi, 0, 0))
    kv_spec = pl.BlockSpec((None, s, SB_WIDTH), lambda bi, i: (bi, 0, 0))
    m = 2 * SB_BLOCK
    return pl.pallas_call(
        _prompt_attn_body,
        grid=(b, s // per_step),
        in_specs=[q_spec, kt_spec, kv_spec] + [pl.BlockSpec(a.shape, lambda bi, i: (0, 0)) for a in sfx],
        out_specs=q_spec,
        out_shape=jax.ShapeDtypeStruct((b, s, SB_WIDTH), BF16),
        scratch_shapes=[pltpu.VMEM((HEAD_PAIRS, m, LANES), BF16),
                        pltpu.VMEM((HEAD_PAIRS * m, 1), F32),
                        pltpu.VMEM((HEAD_PAIRS * m, LANES), F32)],
        compiler_params=pltpu.CompilerParams(
            dimension_semantics=("parallel", "arbitrary"), vmem_limit_bytes=VMEM_LIMIT),
        name="prompt_attn",
    )(qs, kt, vb, *sfx)


def _sample_attn_body(q_ref, kn_ref, vn_ref, ck_hbm, cv_hbm, o_ref, kbuf, vbuf, sems, c_ref, acc_ref,
                      *, layer):
    b = pl.program_id(0)
    n = q_ref.shape[0]
    m = SB_HEADS * n
    nblk = ck_hbm.shape[-1] // CACHE_BLOCK

    def copies(stream, blk, slot):
        cols = pl.ds(blk * CACHE_BLOCK, CACHE_BLOCK)
        return (pltpu.make_async_copy(ck_hbm.at[layer, stream, :, :, cols], kbuf.at[slot], sems.at[slot, 0]),
                pltpu.make_async_copy(cv_hbm.at[layer, stream, :, :, cols], vbuf.at[slot], sems.at[slot, 1]))

    def start(stream, blk, slot):
        for cp in copies(stream, blk, slot):
            cp.start()

    def wait(stream, blk, slot):
        for cp in copies(stream, blk, slot):
            cp.wait()

    def slot_of(jj):
        return 2 + jj % 2

    @pl.when(b == 0)
    def _():
        start(0, nblk - 1, 0)

    @pl.when(b + 1 < pl.num_programs(0))
    def _():
        start(b + 1, nblk - 1, (b + 1) % 2)

    head = lambda a, h: a[:, h * SB_HEAD_DIM:(h + 1) * SB_HEAD_DIM]
    q_all = q_ref[...].astype(BF16)
    q = [head(q_all, h) for h in range(SB_HEADS)]

    def sweep(slot, newest, suffix, causal, fresh):
        def logits(h):
            z = jnp.dot(q[h], kbuf[slot, h].astype(BF16), preferred_element_type=F32)
            if newest is None:
                return z
            return jnp.concatenate(
                [z, lax.dot_general(q[h], newest[0][h], _NT, preferred_element_type=F32)], axis=1)

        z = jnp.concatenate([logits(h) for h in range(SB_HEADS)], axis=0)
        if causal is not None:
            z = jnp.where(causal, z, MASKED)
        sp = _softplus2(z)
        c0 = jnp.zeros((m, 1), F32) if fresh else c_ref[...]
        later = jnp.dot(sp.astype(BF16), suffix, preferred_element_type=F32) + c0
        w = jnp.exp2(z - sp - later)
        wb = w.astype(BF16)
        pvs = []
        for h in range(SB_HEADS):
            wh = wb[h * n:(h + 1) * n, :]
            pv = lax.dot_general(wh[:, :CACHE_BLOCK], vbuf[slot, h].astype(BF16), _NT,
                                 preferred_element_type=F32)
            if newest is not None:
                pv = pv + jnp.dot(wh[:, CACHE_BLOCK:], newest[1][h], preferred_element_type=F32)
            pvs.append(pv)
        pv = jnp.stack(pvs)
        acc_ref[...] = pv if fresh else acc_ref[...] + pv
        c_ref[...] = c0 + jnp.sum(sp, axis=-1, keepdims=True)

    pad = jnp.zeros((LANES - n, SB_HEAD_DIM), BF16)
    kn_all, vn_all = kn_ref[...].astype(BF16), vn_ref[...].astype(BF16)
    newest = ([jnp.concatenate([head(kn_all, h), pad], axis=0) for h in range(SB_HEADS)],
              [jnp.concatenate([head(vn_all, h), pad], axis=0) for h in range(SB_HEADS)])
    t = lax.broadcasted_iota(jnp.int32, (m, CACHE_BLOCK + LANES), 0) % n
    s = lax.broadcasted_iota(jnp.int32, (m, CACHE_BLOCK + LANES), 1) - CACHE_BLOCK
    wait(b, nblk - 1, b % 2)
    sweep(b % 2, newest, _suffix_matrix(CACHE_BLOCK + LANES), s < t, True)

    suffix = _suffix_matrix(CACHE_BLOCK)

    def more(state):
        jj, carry = state
        return (jj < nblk) & (carry < SB_UNDERFLOW_BITS)

    def body(state):
        jj, _ = state
        blk = nblk - 1 - jj
        slot = slot_of(jj)
        wait(b, blk, slot)

        @pl.when(blk > 0)
        def _():
            start(b, blk - 1, slot_of(jj + 1))

        sweep(slot, None, suffix, None, False)
        return jj + 1, jnp.min(c_ref[...])

    state = (jnp.int32(1), jnp.min(c_ref[...]))

    if nblk > 1:
        @pl.when(more(state))
        def _():
            start(b, nblk - 2, slot_of(1))

    done, _ = lax.while_loop(more, body, state)

    @pl.when((done > 1) & (done < nblk))
    def _():
        wait(b, nblk - 1 - done, slot_of(done))

    for h in range(SB_HEADS):
        o_ref[:, h, :] = acc_ref[h]


def _sample_attn(qs, kn, vn, cache_k, cache_v, layer):
    db, n = qs.shape[:2]
    past = cache_k.shape[-1]
    assert n <= LANES and past >= CACHE_BLOCK and past % CACHE_BLOCK == 0, (n, past)
    m = SB_HEADS * n
    new_spec = pl.BlockSpec((None, n, SB_WIDTH), lambda b: (b, 0, 0))
    hbm = pl.BlockSpec(memory_space=pl.ANY)
    buf = pltpu.VMEM((4, SB_HEADS, SB_HEAD_DIM, CACHE_BLOCK), F32)
    return pl.pallas_call(
        functools.partial(_sample_attn_body, layer=layer),
        grid=(db,),
        in_specs=[new_spec, new_spec, new_spec, hbm, hbm],
        out_specs=pl.BlockSpec((None, n, SB_HEADS, SB_HEAD_DIM), lambda b: (b, 0, 0, 0)),
        out_shape=jax.ShapeDtypeStruct((db, n, SB_HEADS, SB_HEAD_DIM), F32),
        scratch_shapes=[buf, buf, pltpu.SemaphoreType.DMA((4, 2)),
                        pltpu.VMEM((m, 1), F32),
                        pltpu.VMEM((SB_HEADS, n, SB_HEAD_DIM), F32)],
        compiler_params=pltpu.CompilerParams(
            dimension_semantics=("arbitrary",), vmem_limit_bytes=VMEM_LIMIT),
        name="sample_attn",
    )(qs, kn, vn, cache_k, cache_v)


def _route(logits):
    lane = lax.broadcasted_iota(jnp.int32, logits.shape, 1).astype(F32)
    neg = jnp.float32(-jnp.inf)
    big = jnp.float32(1 << 20)
    rmax = lambda a: jnp.max(a, axis=-1, keepdims=True)
    rmin = lambda a: jnp.min(a, axis=-1, keepdims=True)
    rsum = lambda a: jnp.sum(a, axis=-1, keepdims=True)

    gmask = lane < N_EXPERT_GROUPS
    gl = jnp.where(gmask, logits, neg)
    gmax = rmax(gl)
    g_sel = rmin(jnp.where(gl == gmax, lane, big))
    p_sel = 1.0 / rsum(jnp.where(gmask, jnp.exp(gl - gmax), 0.0))

    e_lo = N_EXPERT_GROUPS + g_sel * EXPERTS_PER_GROUP
    emask = (lane >= e_lo) & (lane < e_lo + EXPERTS_PER_GROUP)
    el = jnp.where(emask, logits, neg)
    emax = rmax(el)
    ex = jnp.where(emask, jnp.exp(el - emax), 0.0)
    p_e = ex / rsum(ex)
    pm = jnp.where(emask, p_e, -1.0)
    v1 = rmax(pm)
    i1 = rmin(jnp.where(pm == v1, lane, big))
    pm2 = jnp.where(lane == i1, -1.0, pm)
    v2 = rmax(pm2)
    i2 = rmin(jnp.where(pm2 == v2, lane, big))
    tot = v1 + v2
    gate1 = p_sel * v1 / tot
    gate2 = p_sel * v2 / tot
    comb = jnp.where(lane == i1, gate1, 0.0) + jnp.where(lane == i2, gate2, 0.0)
    gates = jnp.where(lane == i1 - e_lo, gate1, 0.0) + jnp.where(lane == i2 - e_lo, gate2, 0.0)
    return comb, g_sel, gates


def _rmsnorm_rows(h, g):
    return h * lax.rsqrt(jnp.mean(h * h, axis=-1, keepdims=True) + EPS) * g


def _slot_matrix(slot):
    lane = lax.broadcasted_iota(jnp.int32, (slot.shape[0], MOE_SLOTS), 1)
    return jnp.where(lane == slot.astype(jnp.int32), 1.0, 0.0)


def _post_attn_body(x_ref, gm_ref, sb_ref, wo_ref, g2_ref, wr_ref, br_ref,
                    h_ref, route_ref, bucket_ref, gate_ref, cnt_ref):
    h = (x_ref[...]
         + jnp.dot(gm_ref[...], wo_ref[:GM_WIDTH, :], preferred_element_type=F32)
         + jnp.dot(sb_ref[...], wo_ref[GM_WIDTH:, :], preferred_element_type=F32))
    h_ref[...] = h
    hn = _rmsnorm_rows(h, g2_ref[...])
    hi = hn.astype(BF16)
    lo = (hn - hi.astype(F32)).astype(BF16)
    both = jnp.dot(hi, wr_ref[...], preferred_element_type=F32)
    logits = (both[:, :LANES] + jnp.dot(lo, wr_ref[:, :LANES], preferred_element_type=F32)
              + both[:, LANES:]) + br_ref[...]
    comb, g_sel, gates = _route(logits)

    subs = [slice(st * MOE_SUB, (st + 1) * MOE_SUB) for st in range(h.shape[0] // MOE_SUB)]
    lane = lax.broadcasted_iota(jnp.int32, comb.shape, 1).astype(F32)
    r = lax.broadcasted_iota(jnp.int32, (MOE_SUB, MOE_SUB), 0)
    c = lax.broadcasted_iota(jnp.int32, (MOE_SUB, MOE_SUB), 1)
    earlier = jnp.where(r > c, 1.0, 0.0).astype(BF16)
    onehot = jnp.where(lane == g_sel, 1.0, 0.0)
    onehot_b = onehot.astype(BF16)
    before = jnp.concatenate(
        [jnp.dot(earlier, onehot_b[rows], preferred_element_type=F32) for rows in subs], axis=0)
    rank = jnp.sum(onehot * before, axis=-1, keepdims=True)
    slot = jnp.where(rank < MOE_CAP, g_sel * MOE_CAP + rank, -1.0)
    route_ref[...] = jnp.where(lane == SLOT_LANE, slot, comb)
    g1 = gates.astype(BF16)
    g2 = (gates - g1.astype(F32)).astype(BF16)
    g3 = (gates - g1.astype(F32) - g2.astype(F32)).astype(BF16)
    g12 = jnp.concatenate([g1, g2], axis=1)
    for st, rows in enumerate(subs):
        place = _slot_matrix(slot[rows]).T.astype(BF16)
        bucket_ref[st] = jnp.dot(place, hi[rows], preferred_element_type=F32).astype(BF16)
        placed = jnp.dot(place, g12[rows], preferred_element_type=F32)
        gate_ref[st] = (placed[:, :LANES] + placed[:, LANES:]
                        + jnp.dot(place, g3[rows], preferred_element_type=F32))
        cnt_ref[st] = jnp.broadcast_to(jnp.sum(onehot[rows], axis=0, keepdims=True), cnt_ref.shape[1:])


def _moe_ffn_body(long_ref, x_ref, gate_ref, xs_ref, gates_ref, wg_ref, wu_ref, wd_ref, y_ref, ys_ref,
                  wgb_ref, wub_ref, wdb_ref, *, prompt_steps):
    g, s = pl.program_id(0), pl.program_id(1)

    @pl.when(s == 0)
    def _():
        wgb_ref[...] = wg_ref[...].astype(BF16)
        wub_ref[...] = wu_ref[...].astype(BF16)
        wdb_ref[...] = wd_ref[...].astype(BF16)

    def ffn(ins, outs, used):
        x = jnp.concatenate([r[:, 0, :used, :].reshape(-1, D_MODEL) for r, _ in ins], axis=0)
        gates = jnp.concatenate([r[:, 0, :used, :].reshape(-1, LANES) for _, r in ins], axis=0)
        parts = []
        for e in range(EXPERTS_PER_GROUP):
            hg = jnp.dot(x, wgb_ref[e], preferred_element_type=F32)
            hu = jnp.dot(x, wub_ref[e], preferred_element_type=F32)
            parts.append((hg / (1.0 + jnp.exp(-hg)) * hu * gates[:, e:e + 1]).astype(BF16))
        a = jnp.concatenate(parts, axis=-1)
        y = jnp.dot(a, wdb_ref[...].reshape(GROUP_FF, D_MODEL), preferred_element_type=F32)
        row = 0
        for out_ref in outs:
            rows = out_ref.shape[0] * used
            out_ref[:, 0, :used, :] = y[row:row + rows].reshape(out_ref.shape[0], used, D_MODEL)
            if used < MOE_CAP:
                out_ref[:, 0, used:, :] = jnp.zeros((out_ref.shape[0], MOE_CAP - used, D_MODEL), F32)
            row += rows

    def step(used):
        @pl.when(s < prompt_steps - 1)
        def _():
            ffn([(x_ref, gate_ref)], [y_ref], used)

        @pl.when(s == prompt_steps - 1)
        def _():
            ffn([(x_ref, gate_ref), (xs_ref, gates_ref)], [y_ref, ys_ref], used)

    @pl.when(long_ref[g, s] == 0)
    def _():
        step(MOE_SHORT)

    @pl.when(long_ref[g, s] != 0)
    def _():
        step(MOE_CAP)


def _moe_combine_body(h_ref, route_ref, ys_ref, o_ref):
    for st in range(h_ref.shape[0] // MOE_SUB):
        rows = slice(st * MOE_SUB, (st + 1) * MOE_SUB)
        pick = _slot_matrix(route_ref[rows, SLOT_LANE:SLOT_LANE + 1]).astype(BF16)
        ys = ys_ref[st]
        hi = ys.astype(BF16)
        lo = (ys - hi.astype(F32)).astype(BF16)
        o_ref[rows, :] = (h_ref[rows, :] + jnp.dot(pick, hi, preferred_element_type=F32)
                          + jnp.dot(pick, lo, preferred_element_type=F32))


def _moe_dense_body(h_ref, route_ref, g2_ref, wg_ref, wu_ref, wd_ref, o_ref, hn_ref):
    g = pl.program_id(1)

    @pl.when(g == 0)
    def _():
        h = h_ref[...]
        o_ref[...] = h
        hn_ref[...] = _rmsnorm_rows(h, g2_ref[...]).astype(BF16)

    hn = hn_ref[...]
    comb = route_ref[...]
    lane = lax.broadcasted_iota(jnp.int32, comb.shape, 1)
    parts = []
    for e in range(EXPERTS_PER_GROUP):
        gate = jnp.sum(jnp.where(lane == N_EXPERT_GROUPS + g * EXPERTS_PER_GROUP + e, comb, 0.0),
                       axis=-1, keepdims=True)
        hg = jnp.dot(hn, wg_ref[e], preferred_element_type=F32)
        hu = jnp.dot(hn, wu_ref[e], preferred_element_type=F32)
        parts.append((hg / (1.0 + jnp.exp(-hg)) * hu * gate).astype(BF16))
    a = jnp.concatenate(parts, axis=-1)
    o_ref[...] += jnp.dot(a, wd_ref[...].reshape(GROUP_FF, D_MODEL), preferred_element_type=F32)


def _group_spec(a, index):
    return pl.BlockSpec((EXPERTS_PER_GROUP,) + a.shape[1:], index)


def _post_attn(x2d, gm, sb, params, tm):
    n = x2d.shape[0]
    wo, g2, wr, br = params
    assert n % tm == 0 and tm % MOE_SUB == 0, (n, tm)
    n_sub, sub = n // MOE_SUB, tm // MOE_SUB
    row = lambda w: pl.BlockSpec((tm, w), lambda i: (i, 0))
    full = lambda a: pl.BlockSpec(a.shape, lambda i: (0,) * a.ndim)
    per_sub = lambda rows, w: pl.BlockSpec((sub, rows, w), lambda i: (i, 0, 0))
    return pl.pallas_call(
        _post_attn_body,
        grid=(n // tm,),
        in_specs=[row(D_MODEL), row(GM_WIDTH), row(SB_WIDTH),
                  full(wo), full(g2), full(wr), full(br)],
        out_specs=[row(D_MODEL), row(LANES), per_sub(MOE_SLOTS, D_MODEL),
                   per_sub(MOE_SLOTS, LANES), per_sub(SUBLANES, LANES)],
        out_shape=[jax.ShapeDtypeStruct((n, D_MODEL), F32),
                   jax.ShapeDtypeStruct((n, LANES), F32),
                   jax.ShapeDtypeStruct((n_sub, MOE_SLOTS, D_MODEL), BF16),
                   jax.ShapeDtypeStruct((n_sub, MOE_SLOTS, LANES), F32),
                   jax.ShapeDtypeStruct((n_sub, SUBLANES, LANES), F32)],
        compiler_params=pltpu.CompilerParams(
            dimension_semantics=("parallel",), vmem_limit_bytes=VMEM_LIMIT),
        name="post_attn",
    )(x2d, gm, sb, wo, g2, wr, br)


def _out_moe(prompt, sample, params, *, tm_prompt, tm_sample):
    wo, g2, wr, br, wg, wu, wd = params
    routed = [_post_attn(*grp, (wo, g2, wr, br), tm) for grp, tm in
              ((prompt, tm_prompt), (sample, tm_sample))]
    tms = (tm_prompt, tm_sample)
    params_2d = pltpu.CompilerParams(
        dimension_semantics=("parallel", "arbitrary"), vmem_limit_bytes=VMEM_LIMIT)

    def sparse():
        (_, _, bk_p, gt_p, cnt_p), (_, _, bk_s, gt_s, cnt_s) = routed
        n_sub, n_sub_s = bk_p.shape[0], bk_s.shape[0]
        sup = min(MOE_SUPER, n_sub)
        assert n_sub % sup == 0, (n_sub, sup)
        steps = n_sub // sup
        fullest = jnp.max(cnt_p[:, 0, :N_EXPERT_GROUPS].reshape(steps, sup, N_EXPERT_GROUPS), axis=1)
        fullest = fullest.at[steps - 1].max(jnp.max(cnt_s[:, 0, :N_EXPERT_GROUPS], axis=0))
        long_steps = (fullest.T > MOE_SHORT).astype(jnp.int32)
        by_group = lambda a: a.reshape(a.shape[0], N_EXPERT_GROUPS, MOE_CAP, a.shape[-1])
        bucket = lambda w: pl.BlockSpec((sup, 1, MOE_CAP, w), lambda g, s, flags: (s, g, 0, 0))
        bucket_s = lambda w: pl.BlockSpec((n_sub_s, 1, MOE_CAP, w), lambda g, s, flags: (0, g, 0, 0))
        ys = pl.pallas_call(
            functools.partial(_moe_ffn_body, prompt_steps=steps),
            grid_spec=pltpu.PrefetchScalarGridSpec(
                num_scalar_prefetch=1,
                grid=(N_EXPERT_GROUPS, steps),
                in_specs=[bucket(D_MODEL), bucket(LANES), bucket_s(D_MODEL), bucket_s(LANES)]
                         + [_group_spec(w, lambda g, s, flags: (g, 0, 0)) for w in (wg, wu, wd)],
                out_specs=[bucket(D_MODEL), bucket_s(D_MODEL)],
                scratch_shapes=[pltpu.VMEM((EXPERTS_PER_GROUP,) + w.shape[1:], BF16)
                                for w in (wg, wu, wd)]),
            out_shape=[jax.ShapeDtypeStruct((n, N_EXPERT_GROUPS, MOE_CAP, D_MODEL), F32)
                       for n in (n_sub, n_sub_s)],
            compiler_params=params_2d,
            name="moe_ffn",
        )(long_steps, by_group(bk_p), by_group(gt_p), by_group(bk_s), by_group(gt_s), wg, wu, wd)

        def combine(h, route, y, tm):
            n, sub = h.shape[0], tm // MOE_SUB
            row = lambda w: pl.BlockSpec((tm, w), lambda i: (i, 0))
            return pl.pallas_call(
                _moe_combine_body,
                grid=(n // tm,),
                in_specs=[row(D_MODEL), row(LANES),
                          pl.BlockSpec((sub, MOE_SLOTS, D_MODEL), lambda i: (i, 0, 0))],
                out_specs=row(D_MODEL),
                out_shape=jax.ShapeDtypeStruct((n, D_MODEL), F32),
                compiler_params=pltpu.CompilerParams(
                    dimension_semantics=("parallel",), vmem_limit_bytes=VMEM_LIMIT),
                name="moe_combine",
            )(h, route, y.reshape(-1, MOE_SLOTS, D_MODEL))

        return tuple(combine(r[0], r[1], y, tm) for r, y, tm in zip(routed, ys, tms))

    def dense():
        wgb, wub, wdb = (w.astype(BF16) for w in (wg, wu, wd))

        def all_experts(h, route, tm):
            row2 = lambda w: pl.BlockSpec((tm, w), lambda i, g: (i, 0))
            return pl.pallas_call(
                _moe_dense_body,
                grid=(h.shape[0] // tm, N_EXPERT_GROUPS),
                in_specs=[row2(D_MODEL), row2(LANES), pl.BlockSpec(g2.shape, lambda i, g: (0, 0))]
                         + [_group_spec(w, lambda i, g: (g, 0, 0)) for w in (wg, wu, wd)],
                out_specs=row2(D_MODEL),
                out_shape=jax.ShapeDtypeStruct(h.shape, F32),
                scratch_shapes=[pltpu.VMEM((tm, D_MODEL), BF16)],
                compiler_params=params_2d,
                name="moe_dense",
            )(h, route, g2, wgb, wub, wdb)

        return tuple(all_experts(r[0], r[1], tm) for r, tm in zip(routed, tms))

    most = jnp.maximum(jnp.max(routed[0][4]), jnp.max(routed[1][4]))
    return lax.cond(most <= MOE_CAP, sparse, dense)


def _layer(layer, xp, xs, cache_k, cache_v, norm1_g, w_in, gm_v_norm_g, gm_w_s, gm_b_s, q_norm_g,
           k_norm_g, w_out, norm2_g, w_router_group, b_router_group, w_router_expert,
           b_router_expert, w_gate, w_up, w_down):
    b, s, d = xp.shape
    db, n, _ = xs.shape
    head_of = jnp.arange(SB_WIDTH) // SB_HEAD_DIM
    bd = jnp.where(head_of[:, None] == head_of[None, :], 1.0 / SB_HEAD_DIM, 0.0).astype(BF16)
    in_params = (norm1_g[None, :], w_in.astype(BF16), gm_v_norm_g[None, :],
                 jnp.tile(q_norm_g, SB_HEADS)[None, :], jnp.tile(k_norm_g, SB_HEADS)[None, :],
                 gm_w_s, gm_b_s.T, bd)

    wr = jnp.concatenate(
        [w_router_group, jnp.transpose(w_router_expert, (1, 0, 2)).reshape(d, N_EXPERTS)], axis=1)
    wr = jnp.pad(wr, ((0, 0), (0, LANES - wr.shape[1])))
    wr_hi = wr.astype(BF16)
    wr_lo = (wr - wr_hi.astype(F32)).astype(BF16)
    br = jnp.pad(jnp.concatenate([b_router_group, b_router_expert.reshape(-1)]),
                 (0, LANES - N_EXPERT_GROUPS - N_EXPERTS))[None, :]

    moe_params = (w_out.astype(BF16), norm2_g[None, :], jnp.concatenate([wr_hi, wr_lo], axis=1), br,
                  w_gate, w_up, w_down)

    xp2 = xp.reshape(b * s, d)
    gm, kf, vf, qs, kb, vb = _mixer_in(xp2, in_params, tm=min(1024, s), chunk=GM_CHUNK, stream=False,
                                       seq=s)
    shp = (b, s, SB_WIDTH)
    sb = _prompt_attn(qs.reshape(shp), kb, vb.reshape(shp)).reshape(b * s, SB_WIDTH)

    xs2 = xs.reshape(db * n, d)
    gm_s, kf_s, vf_s, q_s, gv_s = _mixer_in(xs2, in_params, tm=db * n, chunk=n, stream=True)
    shs = (db, n, SB_WIDTH)
    frame_minor = lambda c: jnp.transpose(c, (0, 1, 3, 4, 2))
    sb_s = _sample_attn(q_s.reshape(shs), kf_s.reshape(shs), vf_s.reshape(shs),
                        frame_minor(cache_k), frame_minor(cache_v), layer)
    sb_s = sb_s.reshape(db * n, SB_WIDTH).astype(BF16)

    yp, ys = _out_moe((xp2, gm, sb), (xs2, gm_s, sb_s), moe_params,
                      tm_prompt=min(1024, b * s), tm_sample=db * n)
    yp, ys = yp.reshape(b, s, d), ys.reshape(db, n, d)

    heads = (SB_HEADS, SB_HEAD_DIM)
    rows = lambda a: jnp.transpose(a.reshape(b, *heads, s), (0, 3, 1, 2))
    return (yp, ys, rows(kf), rows(vf),
            kf_s.reshape(db, n, *heads), vf_s.reshape(db, n, *heads),
            gv_s.reshape(db, n, GM_GROUPS, GM_GROUP_DIM))


def kernel(x_prompt, x_sample, cache_sb_k, cache_sb_v, norm1_g, w_in, gm_v_norm_g, gm_w_s, gm_b_s, q_norm_g, k_norm_g, w_out, norm2_g, w_router_group, b_router_group, w_router_expert, b_router_expert, w_gate, w_up, w_down):
    depth = w_in.shape[0]
    yp, ys = x_prompt, x_sample
    outs = [[] for _ in range(5)]
    for l in range(depth):
        yp, ys, *rest = _layer(
            l, yp, ys, cache_sb_k, cache_sb_v, norm1_g[l], w_in[l], gm_v_norm_g[l], gm_w_s[l],
            gm_b_s[l], q_norm_g[l], k_norm_g[l], w_out[l], norm2_g[l], w_router_group[l],
            b_router_group[l], w_router_expert[l], b_router_expert[l], w_gate[l], w_up[l],
            w_down[l])
        for acc, r in zip(outs, rest):
            acc.append(r)
    return (yp, ys) + tuple(jnp.stack(o, axis=0) for o in outs)
```

```python
import functools

import jax
import jax.numpy as jnp
from jax import lax
from jax.experimental import pallas as pl
from jax.experimental.pallas import tpu as pltpu

D_MODEL = 1024
GM_WIDTH = 512
GM_GROUPS = 4
GM_GROUP_DIM = 128
GM_CHUNK = 128
SB_WIDTH = 512
SB_HEAD_DIM = 64
SB_HEADS = 8
N_EXPERT_GROUPS = 4
EXPERTS_PER_GROUP = 4
N_EXPERTS = 16
EXPERT_FF = 256
GROUP_FF = EXPERTS_PER_GROUP * EXPERT_FF
EPS = 1e-6

LANES = 128
SUBLANES = 8
HEAD_PAIRS = SB_WIDTH // LANES
HEAD_TILE = 256
SB_BLOCK = 128
SB_WIDE = 2 * SB_BLOCK
SB_FIRST = 3 * SB_BLOCK
SB_PER_STEP = 8
CACHE_BLOCK = 256
MOE_SUB = 256
MOE_CAP = 96
MOE_SHORT = 80
MOE_SLOTS = N_EXPERT_GROUPS * MOE_CAP
MOE_SUPER = 8
SLOT_LANE = 0
VMEM_LIMIT = 56 * 1024 * 1024

LOG2_E = 1.4426950408889634
Q_SCALE = SB_HEAD_DIM ** -0.5 * LOG2_E
SB_UNDERFLOW_BITS = 160.0
MASKED = -1e30

F32 = jnp.float32
BF16 = jnp.bfloat16
_NT = (((1,), (1,)), ((), ()))


def _gelu_tanh(x):
    return 0.5 * x * (1.0 + jnp.tanh(0.7978845608028654 * (x + 0.044715 * (x * x * x))))


def _softplus2(z):
    return jnp.maximum(z, 0.0) + jnp.log2(1.0 + jnp.exp2(-jnp.abs(z)))


def _suffix_matrix(n):
    r = lax.broadcasted_iota(jnp.int32, (n, n), 0)
    c = lax.broadcasted_iota(jnp.int32, (n, n), 1)
    return jnp.where(r > c, 1.0, 0.0).astype(BF16)


def _mixer_in_body(x_ref, g1_ref, w_in_ref, gvg_ref, qg_ref, kg_ref, ws_ref, bst_ref, bd_ref,
                   gm_ref, kf_ref, vf_ref, *rest, chunk, stream):
    x = x_ref[...]
    tm = x.shape[0]
    xn = x * lax.rsqrt(jnp.mean(x * x, axis=-1, keepdims=True) + EPS) * g1_ref[...]
    proj = jnp.dot(xn.astype(BF16), w_in_ref[...], preferred_element_type=F32)

    gu = _gelu_tanh(proj[:, :GM_WIDTH])
    gvr = _gelu_tanh(proj[:, GM_WIDTH:2 * GM_WIDTH])
    gv = gvr * lax.rsqrt(jnp.mean(gvr * gvr, axis=-1, keepdims=True) + EPS) * gvg_ref[...]

    o = 2 * GM_WIDTH
    q = proj[:, o:o + SB_WIDTH]
    k = proj[:, o + SB_WIDTH:o + 2 * SB_WIDTH]
    v = proj[:, o + 2 * SB_WIDTH:]
    bd = bd_ref[...]

    def head_mean_square(a):
        sq = (a * a).astype(BF16)
        return jnp.concatenate(
            [jnp.dot(sq[:, lo:lo + HEAD_TILE], bd, preferred_element_type=F32)
             for lo in range(0, SB_WIDTH, HEAD_TILE)], axis=1)

    qms = head_mean_square(q)
    kms = head_mean_square(k)
    qs = q * lax.rsqrt(qms + EPS) * qg_ref[...] * Q_SCALE
    kn = k * lax.rsqrt(kms + EPS) * kg_ref[...]
    if stream:
        kf_ref[...] = kn
        vf_ref[...] = v
        rest[0][...] = qs
        rest[1][...] = gv
    else:
        kt = kn.T
        kf_ref[...] = kt
        vf_ref[...] = v.T
        qs_ref, kb_ref, vb_ref = rest
        qs_ref[...] = qs.astype(BF16)
        kb_ref[...] = kt.astype(BF16)
        vb_ref[...] = v.astype(BF16)

    r = lax.broadcasted_iota(jnp.int32, (chunk, chunk), 0)
    c = lax.broadcasted_iota(jnp.int32, (chunk, chunk), 1)
    gvb = gv.astype(BF16)
    chunks = [slice(ci * chunk, (ci + 1) * chunk) for ci in range(tm // chunk)]
    for g in range(GM_GROUPS):
        w = jnp.where(r >= c, ws_ref[g, :chunk, :chunk], 0.0).astype(BF16)
        b = bst_ref[:chunk, g:g + 1]
        cols = slice(g * GM_GROUP_DIM, (g + 1) * GM_GROUP_DIM)
        mixed = jnp.dot(w, jnp.concatenate([gvb[rows, cols] for rows in chunks], axis=1),
                        preferred_element_type=F32)
        for ci, rows in enumerate(chunks):
            part = mixed[:, ci * GM_GROUP_DIM:(ci + 1) * GM_GROUP_DIM] + b
            gm_ref[rows, cols] = (gu[rows, cols] * part).astype(BF16)


def _mixer_in(x2d, params, *, tm, chunk, stream, seq=None):
    n = x2d.shape[0]
    assert n % tm == 0 and tm % chunk == 0 and chunk <= GM_CHUNK, (n, tm, chunk)
    full = lambda a: pl.BlockSpec(a.shape, lambda i: (0,) * a.ndim)
    spec = lambda s: pl.BlockSpec((tm,) + s.shape[1:], lambda i: (i,) + (0,) * (len(s.shape) - 1))
    gm = jax.ShapeDtypeStruct((n, GM_WIDTH), BF16)
    if stream:
        rows = jax.ShapeDtypeStruct((n, SB_WIDTH), F32)
        out_shape = [gm, rows, rows,
                     rows,
                     jax.ShapeDtypeStruct((n, GM_WIDTH), F32)]
        out_specs = [spec(s) for s in out_shape]
    else:
        per_seq = seq // tm
        rows_t = jax.ShapeDtypeStruct((n // seq, SB_WIDTH, seq), F32)
        spec_t = pl.BlockSpec((None, SB_WIDTH, tm), lambda i: (i // per_seq, 0, i % per_seq))
        dense = jax.ShapeDtypeStruct((n, SB_WIDTH), BF16)
        dense_t = jax.ShapeDtypeStruct(rows_t.shape, BF16)
        out_shape = [gm, rows_t, rows_t, dense, dense_t, dense]
        out_specs = [spec(gm), spec_t, spec_t, spec(dense), spec_t, spec(dense)]
    return pl.pallas_call(
        functools.partial(_mixer_in_body, chunk=chunk, stream=stream),
        grid=(n // tm,),
        in_specs=[pl.BlockSpec((tm, D_MODEL), lambda i: (i, 0))] + [full(a) for a in params],
        out_specs=out_specs,
        out_shape=out_shape,
        compiler_params=pltpu.CompilerParams(
            dimension_semantics=("parallel",), vmem_limit_bytes=VMEM_LIMIT),
        name="mixer_in_stream" if stream else "mixer_in",
    )(x2d, *params)


def _prompt_attn_body(q_ref, k_ref, v_ref, sfx_ref, sfx_wide_ref, o_ref, *scratch):
    blocks = q_ref.shape[0] // SB_BLOCK

    def one(sub, carry):
        rows = pl.ds(pl.multiple_of(sub * SB_BLOCK, SB_BLOCK), SB_BLOCK)
        _prompt_attn_block(pl.program_id(1) * blocks + sub, q_ref.at[rows], k_ref, v_ref, sfx_ref,
                           sfx_wide_ref, o_ref.at[rows], *scratch)
        return carry

    lax.fori_loop(0, blocks, one, 0)


def _prompt_attn_block(i, q_ref, k_ref, v_ref, sfx_ref, sfx_wide_ref, o_ref, qst_ref, c_ref, acc_ref):
    lane = lax.broadcasted_iota(jnp.int32, (SB_BLOCK, LANES), 1)
    first = lane < SB_HEAD_DIM
    for hp in range(HEAD_PAIRS):
        q = q_ref[:, hp * LANES:(hp + 1) * LANES]
        zero = jnp.zeros_like(q)
        qst_ref[hp, :SB_BLOCK, :] = jnp.where(first, q, zero)
        qst_ref[hp, SB_BLOCK:, :] = jnp.where(first, zero, q)
    m = 2 * SB_BLOCK
    suffix = {SB_BLOCK: sfx_ref, SB_WIDE: sfx_wide_ref}

    def sweep(j, keys, causal, fresh):
        rows = pl.ds(pl.multiple_of(j * SB_BLOCK, SB_BLOCK), keys)
        cols = lambda hp: slice(hp * LANES, (hp + 1) * LANES)
        z = jnp.concatenate(
            [jnp.dot(qst_ref[hp], k_ref[cols(hp), rows], preferred_element_type=F32)
             for hp in range(HEAD_PAIRS)], axis=0)
        if causal is not None:
            old = keys - SB_BLOCK
            newest = jnp.where(causal, z[:, old:], MASKED)
            z = newest if old == 0 else jnp.concatenate([z[:, :old], newest], axis=1)
        sp = _softplus2(z)
        c0 = jnp.zeros((HEAD_PAIRS * m, 1), F32) if fresh else c_ref[...]
        spb = sp.astype(BF16)
        if keys == SB_FIRST:
            newest = sp[:, SB_WIDE:]
            later = jnp.concatenate(
                [jnp.dot(spb[:, :SB_WIDE], suffix[SB_WIDE][...], preferred_element_type=F32)
                 + jnp.sum(newest, axis=-1, keepdims=True),
                 jnp.dot(spb[:, SB_WIDE:], suffix[SB_BLOCK][...], preferred_element_type=F32)],
                axis=1) + c0
        else:
            later = jnp.dot(spb, suffix[keys][...], preferred_element_type=F32) + c0
        w = jnp.exp2(z - sp - later)
        wb = w.astype(BF16)
        pv = jnp.concatenate(
            [jnp.dot(wb[hp * m:(hp + 1) * m, :], v_ref[rows, cols(hp)], preferred_element_type=F32)
             for hp in range(HEAD_PAIRS)], axis=0)
        acc_ref[...] = pv if fresh else acc_ref[...] + pv
        c_ref[...] = c0 + jnp.sum(sp, axis=-1, keepdims=True)

    below = SB_FIRST // SB_BLOCK - 1
    t = lax.broadcasted_iota(jnp.int32, (HEAD_PAIRS * m, SB_BLOCK), 0) % SB_BLOCK
    s = lax.broadcasted_iota(jnp.int32, (HEAD_PAIRS * m, SB_BLOCK), 1)

    @pl.when(i >= below)
    def _():
        sweep(i - below, SB_FIRST, s < t, True)

    @pl.when(i < below)
    def _():
        sweep(i, SB_BLOCK, s < t, True)

    j0 = jnp.where(i >= below, i - below, i)

    def least_carry():
        return jnp.min(c_ref[...])

    per_wide = SB_WIDE // SB_BLOCK
    n_wide = j0 // per_wide

    def more(state):
        jj, carry = state
        return (jj < n_wide) & (carry < SB_UNDERFLOW_BITS)

    def body(state):
        jj, _ = state
        sweep(j0 - per_wide * (jj + 1), SB_WIDE, None, False)
        return jj + 1, least_carry()

    done, carry = lax.while_loop(more, body, (jnp.int32(0), least_carry()))

    @pl.when((done == n_wide) & (j0 % per_wide == 1) & (carry < SB_UNDERFLOW_BITS))
    def _():
        sweep(0, SB_BLOCK, None, False)

    for hp in range(HEAD_PAIRS):
        lo = hp * m
        o_ref[:, hp * LANES:(hp + 1) * LANES] = jnp.where(
            first, acc_ref[lo:lo + SB_BLOCK, :], acc_ref[lo + SB_BLOCK:lo + m, :]).astype(o_ref.dtype)


def _prompt_attn(qs, kt, vb):
    b, s, _ = qs.shape
    assert s % SB_BLOCK == 0, s
    sfx = tuple(_suffix_matrix(keys) for keys in (SB_BLOCK, SB_WIDE))
    per_step = min(SB_PER_STEP * SB_BLOCK, s)
    assert s % per_step == 0, s
    q_spec = pl.BlockSpec((None, per_step, SB_WIDTH), lambda bi, i: (bi, i, 0))
    kt_spec = pl.BlockSpec((None, SB_WIDTH, s), lambda bi, i: (bi, 0, 0))
    kv_spec = pl.BlockSpec((None, s, SB_WIDTH), lambda bi, i: (bi, 0, 0))
    m = 2 * SB_BLOCK
    return pl.pallas_call(
        _prompt_attn_body,
        grid=(b, s // per_step),
        in_specs=[q_spec, kt_spec, kv_spec] + [pl.BlockSpec(a.shape, lambda bi, i: (0, 0)) for a in sfx],
        out_specs=q_spec,
        out_shape=jax.ShapeDtypeStruct((b, s, SB_WIDTH), BF16),
        scratch_shapes=[pltpu.VMEM((HEAD_PAIRS, m, LANES), BF16),
                        pltpu.VMEM((HEAD_PAIRS * m, 1), F32),
                        pltpu.VMEM((HEAD_PAIRS * m, LANES), F32)],
        compiler_params=pltpu.CompilerParams(
            dimension_semantics=("parallel", "arbitrary"), vmem_limit_bytes=VMEM_LIMIT),
        name="prompt_attn",
    )(qs, kt, vb, *sfx)


def _sample_attn_body(q_ref, kn_ref, vn_ref, ck_hbm, cv_hbm, o_ref, kbuf, vbuf, sems, c_ref, acc_ref,
                      *, layer):
    b = pl.program_id(0)
    n = q_ref.shape[0]
    m = SB_HEADS * n
    nblk = ck_hbm.shape[-1] // CACHE_BLOCK

    def copies(stream, blk, slot):
        cols = pl.ds(blk * CACHE_BLOCK, CACHE_BLOCK)
        return (pltpu.make_async_copy(ck_hbm.at[layer, stream, :, :, cols], kbuf.at[slot], sems.at[slot, 0]),
                pltpu.make_async_copy(cv_hbm.at[layer, stream, :, :, cols], vbuf.at[slot], sems.at[slot, 1]))

    def start(stream, blk, slot):
        for cp in copies(stream, blk, slot):
            cp.start()

    def wait(stream, blk, slot):
        for cp in copies(stream, blk, slot):
            cp.wait()

    def slot_of(jj):
        return 2 + jj % 2

    @pl.when(b == 0)
    def _():
        start(0, nblk - 1, 0)

    @pl.when(b + 1 < pl.num_programs(0))
    def _():
        start(b + 1, nblk - 1, (b + 1) % 2)

    head = lambda a, h: a[:, h * SB_HEAD_DIM:(h + 1) * SB_HEAD_DIM]
    q_all = q_ref[...].astype(BF16)
    q = [head(q_all, h) for h in range(SB_HEADS)]

    def sweep(slot, newest, suffix, causal, fresh):
        def logits(h):
            z = jnp.dot(q[h], kbuf[slot, h].astype(BF16), preferred_element_type=F32)
            if newest is None:
                return z
            return jnp.concatenate(
                [z, lax.dot_general(q[h], newest[0][h], _NT, preferred_element_type=F32)], axis=1)

        z = jnp.concatenate([logits(h) for h in range(SB_HEADS)], axis=0)
        if causal is not None:
            z = jnp.where(causal, z, MASKED)
        sp = _softplus2(z)
        c0 = jnp.zeros((m, 1), F32) if fresh else c_ref[...]
        later = jnp.dot(sp.astype(BF16), suffix, preferred_element_type=F32) + c0
        w = jnp.exp2(z - sp - later)
        wb = w.astype(BF16)
        pvs = []
        for h in range(SB_HEADS):
            wh = wb[h * n:(h + 1) * n, :]
            pv = lax.dot_general(wh[:, :CACHE_BLOCK], vbuf[slot, h].astype(BF16), _NT,
                                 preferred_element_type=F32)
            if newest is not None:
                pv = pv + jnp.dot(wh[:, CACHE_BLOCK:], newest[1][h], preferred_element_type=F32)
            pvs.append(pv)
        pv = jnp.stack(pvs)
        acc_ref[...] = pv if fresh else acc_ref[...] + pv
        c_ref[...] = c0 + jnp.sum(sp, axis=-1, keepdims=True)

    pad = jnp.zeros((LANES - n, SB_HEAD_DIM), BF16)
    kn_all, vn_all = kn_ref[...].astype(BF16), vn_ref[...].astype(BF16)
    newest = ([jnp.concatenate([head(kn_all, h), pad], axis=0) for h in range(SB_HEADS)],
              [jnp.concatenate([head(vn_all, h), pad], axis=0) for h in range(SB_HEADS)])
    t = lax.broadcasted_iota(jnp.int32, (m, CACHE_BLOCK + LANES), 0) % n
    s = lax.broadcasted_iota(jnp.int32, (m, CACHE_BLOCK + LANES), 1) - CACHE_BLOCK
    wait(b, nblk - 1, b % 2)
    sweep(b % 2, newest, _suffix_matrix(CACHE_BLOCK + LANES), s < t, True)

    suffix = _suffix_matrix(CACHE_BLOCK)

    def more(state):
        jj, carry = state
        return (jj < nblk) & (carry < SB_UNDERFLOW_BITS)

    def body(state):
        jj, _ = state
        blk = nblk - 1 - jj
        slot = slot_of(jj)
        wait(b, blk, slot)

        @pl.when(blk > 0)
        def _():
            start(b, blk - 1, slot_of(jj + 1))

        sweep(slot, None, suffix, None, False)
        return jj + 1, jnp.min(c_ref[...])

    state = (jnp.int32(1), jnp.min(c_ref[...]))

    if nblk > 1:
        @pl.when(more(state))
        def _():
            start(b, nblk - 2, slot_of(1))

    done, _ = lax.while_loop(more, body, state)

    @pl.when((done > 1) & (done < nblk))
    def _():
        wait(b, nblk - 1 - done, slot_of(done))

    for h in range(SB_HEADS):
        o_ref[:, h, :] = acc_ref[h]


def _sample_attn(qs, kn, vn, cache_k, cache_v, layer):
    db, n = qs.shape[:2]
    past = cache_k.shape[-1]
    assert n <= LANES and past >= CACHE_BLOCK and past % CACHE_BLOCK == 0, (n, past)
    m = SB_HEADS * n
    new_spec = pl.BlockSpec((None, n, SB_WIDTH), lambda b: (b, 0, 0))
    hbm = pl.BlockSpec(memory_space=pl.ANY)
    buf = pltpu.VMEM((4, SB_HEADS, SB_HEAD_DIM, CACHE_BLOCK), F32)
    return pl.pallas_call(
        functools.partial(_sample_attn_body, layer=layer),
        grid=(db,),
        in_specs=[new_spec, new_spec, new_spec, hbm, hbm],
        out_specs=pl.BlockSpec((None, n, SB_HEADS, SB_HEAD_DIM), lambda b: (b, 0, 0, 0)),
        out_shape=jax.ShapeDtypeStruct((db, n, SB_HEADS, SB_HEAD_DIM), F32),
        scratch_shapes=[buf, buf, pltpu.SemaphoreType.DMA((4, 2)),
                        pltpu.VMEM((m, 1), F32),
                        pltpu.VMEM((SB_HEADS, n, SB_HEAD_DIM), F32)],
        compiler_params=pltpu.CompilerParams(
            dimension_semantics=("arbitrary",), vmem_limit_bytes=VMEM_LIMIT),
        name="sample_attn",
    )(qs, kn, vn, cache_k, cache_v)


def _route(logits):
    lane = lax.broadcasted_iota(jnp.int32, logits.shape, 1).astype(F32)
    neg = jnp.float32(-jnp.inf)
    big = jnp.float32(1 << 20)
    rmax = lambda a: jnp.max(a, axis=-1, keepdims=True)
    rmin = lambda a: jnp.min(a, axis=-1, keepdims=True)
    rsum = lambda a: jnp.sum(a, axis=-1, keepdims=True)

    gmask = lane < N_EXPERT_GROUPS
    gl = jnp.where(gmask, logits, neg)
    gmax = rmax(gl)
    g_sel = rmin(jnp.where(gl == gmax, lane, big))
    p_sel = 1.0 / rsum(jnp.where(gmask, jnp.exp(gl - gmax), 0.0))

    e_lo = N_EXPERT_GROUPS + g_sel * EXPERTS_PER_GROUP
    emask = (lane >= e_lo) & (lane < e_lo + EXPERTS_PER_GROUP)
    el = jnp.where(emask, logits, neg)
    emax = rmax(el)
    ex = jnp.where(emask, jnp.exp(el - emax), 0.0)
    p_e = ex / rsum(ex)
    pm = jnp.where(emask, p_e, -1.0)
    v1 = rmax(pm)
    i1 = rmin(jnp.where(pm == v1, lane, big))
    pm2 = jnp.where(lane == i1, -1.0, pm)
    v2 = rmax(pm2)
    i2 = rmin(jnp.where(pm2 == v2, lane, big))
    tot = v1 + v2
    gate1 = p_sel * v1 / tot
    gate2 = p_sel * v2 / tot
    comb = jnp.where(lane == i1, gate1, 0.0) + jnp.where(lane == i2, gate2, 0.0)
    gates = jnp.where(lane == i1 - e_lo, gate1, 0.0) + jnp.where(lane == i2 - e_lo, gate2, 0.0)
    return comb, g_sel, gates


def _rmsnorm_rows(h, g):
    return h * lax.rsqrt(jnp.mean(h * h, axis=-1, keepdims=True) + EPS) * g


def _slot_matrix(slot):
    lane = lax.broadcasted_iota(jnp.int32, (slot.shape[0], MOE_SLOTS), 1)
    return jnp.where(lane == slot.astype(jnp.int32), 1.0, 0.0)


def _post_attn_body(x_ref, gm_ref, sb_ref, wo_ref, g2_ref, wr_ref, br_ref,
                    h_ref, route_ref, bucket_ref, gate_ref, cnt_ref):
    h = (x_ref[...]
         + jnp.dot(gm_ref[...], wo_ref[:GM_WIDTH, :], preferred_element_type=F32)
         + jnp.dot(sb_ref[...], wo_ref[GM_WIDTH:, :], preferred_element_type=F32))
    h_ref[...] = h
    hn = _rmsnorm_rows(h, g2_ref[...])
    hi = hn.astype(BF16)
    lo = (hn - hi.astype(F32)).astype(BF16)
    both = jnp.dot(hi, wr_ref[...], preferred_element_type=F32)
    logits = (both[:, :LANES] + jnp.dot(lo, wr_ref[:, :LANES], preferred_element_type=F32)
              + both[:, LANES:]) + br_ref[...]
    comb, g_sel, gates = _route(logits)

    subs = [slice(st * MOE_SUB, (st + 1) * MOE_SUB) for st in range(h.shape[0] // MOE_SUB)]
    lane = lax.broadcasted_iota(jnp.int32, comb.shape, 1).astype(F32)
    r = lax.broadcasted_iota(jnp.int32, (MOE_SUB, MOE_SUB), 0)
    c = lax.broadcasted_iota(jnp.int32, (MOE_SUB, MOE_SUB), 1)
    earlier = jnp.where(r > c, 1.0, 0.0).astype(BF16)
    onehot = jnp.where(lane == g_sel, 1.0, 0.0)
    onehot_b = onehot.astype(BF16)
    before = jnp.concatenate(
        [jnp.dot(earlier, onehot_b[rows], preferred_element_type=F32) for rows in subs], axis=0)
    rank = jnp.sum(onehot * before, axis=-1, keepdims=True)
    slot = jnp.where(rank < MOE_CAP, g_sel * MOE_CAP + rank, -1.0)
    route_ref[...] = jnp.where(lane == SLOT_LANE, slot, comb)
    g1 = gates.astype(BF16)
    g2 = (gates - g1.astype(F32)).astype(BF16)
    g3 = (gates - g1.astype(F32) - g2.astype(F32)).astype(BF16)
    g12 = jnp.concatenate([g1, g2], axis=1)
    for st, rows in enumerate(subs):
        place = _slot_matrix(slot[rows]).T.astype(BF16)
        bucket_ref[st] = jnp.dot(place, hi[rows], preferred_element_type=F32).astype(BF16)
        placed = jnp.dot(place, g12[rows], preferred_element_type=F32)
        gate_ref[st] = (placed[:, :LANES] + placed[:, LANES:]
                        + jnp.dot(place, g3[rows], preferred_element_type=F32))
        cnt_ref[st] = jnp.broadcast_to(jnp.sum(onehot[rows], axis=0, keepdims=True), cnt_ref.shape[1:])


def _moe_ffn_body(long_ref, x_ref, gate_ref, xs_ref, gates_ref, wg_ref, wu_ref, wd_ref, y_ref, ys_ref,
                  wgb_ref, wub_ref, wdb_ref, *, prompt_steps):
    g, s = pl.program_id(0), pl.program_id(1)

    @pl.when(s == 0)
    def _():
        wgb_ref[...] = wg_ref[...].astype(BF16)
        wub_ref[...] = wu_ref[...].astype(BF16)
        wdb_ref[...] = wd_ref[...].astype(BF16)

    def ffn(ins, outs, used):
        x = jnp.concatenate([r[:, 0, :used, :].reshape(-1, D_MODEL) for r, _ in ins], axis=0)
        gates = jnp.concatenate([r[:, 0, :used, :].reshape(-1, LANES) for _, r in ins], axis=0)
        parts = []
        for e in range(EXPERTS_PER_GROUP):
            hg = jnp.dot(x, wgb_ref[e], preferred_element_type=F32)
            hu = jnp.dot(x, wub_ref[e], preferred_element_type=F32)
            parts.append((hg / (1.0 + jnp.exp(-hg)) * hu * gates[:, e:e + 1]).astype(BF16))
        a = jnp.concatenate(parts, axis=-1)
        y = jnp.dot(a, wdb_ref[...].reshape(GROUP_FF, D_MODEL), preferred_element_type=F32)
        row = 0
        for out_ref in outs:
            rows = out_ref.shape[0] * used
            out_ref[:, 0, :used, :] = y[row:row + rows].reshape(out_ref.shape[0], used, D_MODEL)
            if used < MOE_CAP:
                out_ref[:, 0, used:, :] = jnp.zeros((out_ref.shape[0], MOE_CAP - used, D_MODEL), F32)
            row += rows

    def step(used):
        @pl.when(s < prompt_steps - 1)
        def _():
            ffn([(x_ref, gate_ref)], [y_ref], used)

        @pl.when(s == prompt_steps - 1)
        def _():
            ffn([(x_ref, gate_ref), (xs_ref, gates_ref)], [y_ref, ys_ref], used)

    @pl.when(long_ref[g, s] == 0)
    def _():
        step(MOE_SHORT)

    @pl.when(long_ref[g, s] != 0)
    def _():
        step(MOE_CAP)


def _moe_combine_body(h_ref, route_ref, ys_ref, o_ref):
    for st in range(h_ref.shape[0] // MOE_SUB):
        rows = slice(st * MOE_SUB, (st + 1) * MOE_SUB)
        pick = _slot_matrix(route_ref[rows, SLOT_LANE:SLOT_LANE + 1]).astype(BF16)
        ys = ys_ref[st]
        hi = ys.astype(BF16)
        lo = (ys - hi.astype(F32)).astype(BF16)
        o_ref[rows, :] = (h_ref[rows, :] + jnp.dot(pick, hi, preferred_element_type=F32)
                          + jnp.dot(pick, lo, preferred_element_type=F32))


def _moe_dense_body(h_ref, route_ref, g2_ref, wg_ref, wu_ref, wd_ref, o_ref, hn_ref):
    g = pl.program_id(1)

    @pl.when(g == 0)
    def _():
        h = h_ref[...]
        o_ref[...] = h
        hn_ref[...] = _rmsnorm_rows(h, g2_ref[...]).astype(BF16)

    hn = hn_ref[...]
    comb = route_ref[...]
    lane = lax.broadcasted_iota(jnp.int32, comb.shape, 1)
    parts = []
    for e in range(EXPERTS_PER_GROUP):
        gate = jnp.sum(jnp.where(lane == N_EXPERT_GROUPS + g * EXPERTS_PER_GROUP + e, comb, 0.0),
                       axis=-1, keepdims=True)
        hg = jnp.dot(hn, wg_ref[e], preferred_element_type=F32)
        hu = jnp.dot(hn, wu_ref[e], preferred_element_type=F32)
        parts.append((hg / (1.0 + jnp.exp(-hg)) * hu * gate).astype(BF16))
    a = jnp.concatenate(parts, axis=-1)
    o_ref[...] += jnp.dot(a, wd_ref[...].reshape(GROUP_FF, D_MODEL), preferred_element_type=F32)


def _group_spec(a, index):
    return pl.BlockSpec((EXPERTS_PER_GROUP,) + a.shape[1:], index)


def _post_attn(x2d, gm, sb, params, tm):
    n = x2d.shape[0]
    wo, g2, wr, br = params
    assert n % tm == 0 and tm % MOE_SUB == 0, (n, tm)
    n_sub, sub = n // MOE_SUB, tm // MOE_SUB
    row = lambda w: pl.BlockSpec((tm, w), lambda i: (i, 0))
    full = lambda a: pl.BlockSpec(a.shape, lambda i: (0,) * a.ndim)
    per_sub = lambda rows, w: pl.BlockSpec((sub, rows, w), lambda i: (i, 0, 0))
    return pl.pallas_call(
        _post_attn_body,
        grid=(n // tm,),
        in_specs=[row(D_MODEL), row(GM_WIDTH), row(SB_WIDTH),
                  full(wo), full(g2), full(wr), full(br)],
        out_specs=[row(D_MODEL), row(LANES), per_sub(MOE_SLOTS, D_MODEL),
                   per_sub(MOE_SLOTS, LANES), per_sub(SUBLANES, LANES)],
        out_shape=[jax.ShapeDtypeStruct((n, D_MODEL), F32),
                   jax.ShapeDtypeStruct((n, LANES), F32),
                   jax.ShapeDtypeStruct((n_sub, MOE_SLOTS, D_MODEL), BF16),
                   jax.ShapeDtypeStruct((n_sub, MOE_SLOTS, LANES), F32),
                   jax.ShapeDtypeStruct((n_sub, SUBLANES, LANES), F32)],
        compiler_params=pltpu.CompilerParams(
            dimension_semantics=("parallel",), vmem_limit_bytes=VMEM_LIMIT),
        name="post_attn",
    )(x2d, gm, sb, wo, g2, wr, br)


def _out_moe(prompt, sample, params, *, tm_prompt, tm_sample):
    wo, g2, wr, br, wg, wu, wd = params
    routed = [_post_attn(*grp, (wo, g2, wr, br), tm) for grp, tm in
              ((prompt, tm_prompt), (sample, tm_sample))]
    tms = (tm_prompt, tm_sample)
    params_2d = pltpu.CompilerParams(
        dimension_semantics=("parallel", "arbitrary"), vmem_limit_bytes=VMEM_LIMIT)

    def sparse():
        (_, _, bk_p, gt_p, cnt_p), (_, _, bk_s, gt_s, cnt_s) = routed
        n_sub, n_sub_s = bk_p.shape[0], bk_s.shape[0]
        sup = min(MOE_SUPER, n_sub)
        assert n_sub % sup == 0, (n_sub, sup)
        steps = n_sub // sup
        fullest = jnp.max(cnt_p[:, 0, :N_EXPERT_GROUPS].reshape(steps, sup, N_EXPERT_GROUPS), axis=1)
        fullest = fullest.at[steps - 1].max(jnp.max(cnt_s[:, 0, :N_EXPERT_GROUPS], axis=0))
        long_steps = (fullest.T > MOE_SHORT).astype(jnp.int32)
        by_group = lambda a: a.reshape(a.shape[0], N_EXPERT_GROUPS, MOE_CAP, a.shape[-1])
        bucket = lambda w: pl.BlockSpec((sup, 1, MOE_CAP, w), lambda g, s, flags: (s, g, 0, 0))
        bucket_s = lambda w: pl.BlockSpec((n_sub_s, 1, MOE_CAP, w), lambda g, s, flags: (0, g, 0, 0))
        ys = pl.pallas_call(
            functools.partial(_moe_ffn_body, prompt_steps=steps),
            grid_spec=pltpu.PrefetchScalarGridSpec(
                num_scalar_prefetch=1,
                grid=(N_EXPERT_GROUPS, steps),
                in_specs=[bucket(D_MODEL), bucket(LANES), bucket_s(D_MODEL), bucket_s(LANES)]
                         + [_group_spec(w, lambda g, s, flags: (g, 0, 0)) for w in (wg, wu, wd)],
                out_specs=[bucket(D_MODEL), bucket_s(D_MODEL)],
                scratch_shapes=[pltpu.VMEM((EXPERTS_PER_GROUP,) + w.shape[1:], BF16)
                                for w in (wg, wu, wd)]),
            out_shape=[jax.ShapeDtypeStruct((n, N_EXPERT_GROUPS, MOE_CAP, D_MODEL), F32)
                       for n in (n_sub, n_sub_s)],
            compiler_params=params_2d,
            name="moe_ffn",
        )(long_steps, by_group(bk_p), by_group(gt_p), by_group(bk_s), by_group(gt_s), wg, wu, wd)

        def combine(h, route, y, tm):
            n, sub = h.shape[0], tm // MOE_SUB
            row = lambda w: pl.BlockSpec((tm, w), lambda i: (i, 0))
            return pl.pallas_call(
                _moe_combine_body,
                grid=(n // tm,),
                in_specs=[row(D_MODEL), row(LANES),
                          pl.BlockSpec((sub, MOE_SLOTS, D_MODEL), lambda i: (i, 0, 0))],
                out_specs=row(D_MODEL),
                out_shape=jax.ShapeDtypeStruct((n, D_MODEL), F32),
                compiler_params=pltpu.CompilerParams(
                    dimension_semantics=("parallel",), vmem_limit_bytes=VMEM_LIMIT),
                name="moe_combine",
            )(h, route, y.reshape(-1, MOE_SLOTS, D_MODEL))

        return tuple(combine(r[0], r[1], y, tm) for r, y, tm in zip(routed, ys, tms))

    def dense():
        wgb, wub, wdb = (w.astype(BF16) for w in (wg, wu, wd))

        def all_experts(h, route, tm):
            row2 = lambda w: pl.BlockSpec((tm, w), lambda i, g: (i, 0))
            return pl.pallas_call(
                _moe_dense_body,
                grid=(h.shape[0] // tm, N_EXPERT_GROUPS),
                in_specs=[row2(D_MODEL), row2(LANES), pl.BlockSpec(g2.shape, lambda i, g: (0, 0))]
                         + [_group_spec(w, lambda i, g: (g, 0, 0)) for w in (wg, wu, wd)],
                out_specs=row2(D_MODEL),
                out_shape=jax.ShapeDtypeStruct(h.shape, F32),
                scratch_shapes=[pltpu.VMEM((tm, D_MODEL), BF16)],
                compiler_params=params_2d,
                name="moe_dense",
            )(h, route, g2, wgb, wub, wdb)

        return tuple(all_experts(r[0], r[1], tm) for r, tm in zip(routed, tms))

    most = jnp.maximum(jnp.max(routed[0][4]), jnp.max(routed[1][4]))
    return lax.cond(most <= MOE_CAP, sparse, dense)


def _layer(layer, xp, xs, cache_k, cache_v, norm1_g, w_in, gm_v_norm_g, gm_w_s, gm_b_s, q_norm_g,
           k_norm_g, w_out, norm2_g, w_router_group, b_router_group, w_router_expert,
           b_router_expert, w_gate, w_up, w_down):
    b, s, d = xp.shape
    db, n, _ = xs.shape
    head_of = jnp.arange(HEAD_TILE) // SB_HEAD_DIM
    bd = jnp.where(head_of[:, None] == head_of[None, :], 1.0 / SB_HEAD_DIM, 0.0).astype(BF16)
    in_params = (norm1_g[None, :], w_in.astype(BF16), gm_v_norm_g[None, :],
                 jnp.tile(q_norm_g, SB_HEADS)[None, :], jnp.tile(k_norm_g, SB_HEADS)[None, :],
                 gm_w_s, gm_b_s.T, bd)

    wr = jnp.concatenate(
        [w_router_group, jnp.transpose(w_router_expert, (1, 0, 2)).reshape(d, N_EXPERTS)], axis=1)
    wr = jnp.pad(wr, ((0, 0), (0, LANES - wr.shape[1])))
    wr_hi = wr.astype(BF16)
    wr_lo = (wr - wr_hi.astype(F32)).astype(BF16)
    br = jnp.pad(jnp.concatenate([b_router_group, b_router_expert.reshape(-1)]),
                 (0, LANES - N_EXPERT_GROUPS - N_EXPERTS))[None, :]

    moe_params = (w_out.astype(BF16), norm2_g[None, :], jnp.concatenate([wr_hi, wr_lo], axis=1), br,
                  w_gate, w_up, w_down)

    xp2 = xp.reshape(b * s, d)
    gm, kf, vf, qs, kb, vb = _mixer_in(xp2, in_params, tm=min(1024, s), chunk=GM_CHUNK, stream=False,
                                       seq=s)
    shp = (b, s, SB_WIDTH)
    sb = _prompt_attn(qs.reshape(shp), kb, vb.reshape(shp)).reshape(b * s, SB_WIDTH)

    xs2 = xs.reshape(db * n, d)
    gm_s, kf_s, vf_s, q_s, gv_s = _mixer_in(xs2, in_params, tm=db * n, chunk=n, stream=True)
    shs = (db, n, SB_WIDTH)
    frame_minor = lambda c: jnp.transpose(c, (0, 1, 3, 4, 2))
    sb_s = _sample_attn(q_s.reshape(shs), kf_s.reshape(shs), vf_s.reshape(shs),
                        frame_minor(cache_k), frame_minor(cache_v), layer)
    sb_s = sb_s.reshape(db * n, SB_WIDTH).astype(BF16)

    yp, ys = _out_moe((xp2, gm, sb), (xs2, gm_s, sb_s), moe_params,
                      tm_prompt=min(1024, b * s), tm_sample=db * n)
    yp, ys = yp.reshape(b, s, d), ys.reshape(db, n, d)

    heads = (SB_HEADS, SB_HEAD_DIM)
    rows = lambda a: jnp.transpose(a.reshape(b, *heads, s), (0, 3, 1, 2))
    return (yp, ys, rows(kf), rows(vf),
            kf_s.reshape(db, n, *heads), vf_s.reshape(db, n, *heads),
            gv_s.reshape(db, n, GM_GROUPS, GM_GROUP_DIM))


def kernel(x_prompt, x_sample, cache_sb_k, cache_sb_v, norm1_g, w_in, gm_v_norm_g, gm_w_s, gm_b_s, q_norm_g, k_norm_g, w_out, norm2_g, w_router_group, b_router_group, w_router_expert, b_router_expert, w_gate, w_up, w_down):
    depth = w_in.shape[0]
    yp, ys = x_prompt, x_sample
    outs = [[] for _ in range(5)]
    for l in range(depth):
        yp, ys, *rest = _layer(
            l, yp, ys, cache_sb_k, cache_sb_v, norm1_g[l], w_in[l], gm_v_norm_g[l], gm_w_s[l],
            gm_b_s[l], q_norm_g[l], k_norm_g[l], w_out[l], norm2_g[l], w_router_group[l],
            b_router_group[l], w_router_expert[l], b_router_expert[l], w_gate[l], w_up[l],
            w_down[l])
        for acc, r in zip(outs, rest):
            acc.append(r)
    return (yp, ys) + tuple(jnp.stack(o, axis=0) for o in outs)
```

```python
import functools

import jax
import jax.numpy as jnp
from jax import lax
from jax.experimental import pallas as pl
from jax.experimental.pallas import tpu as pltpu

D_MODEL = 1024
GM_WIDTH = 512
GM_GROUPS = 4
GM_GROUP_DIM = 128
GM_CHUNK = 128
SB_WIDTH = 512
SB_HEAD_DIM = 64
SB_HEADS = 8
N_EXPERT_GROUPS = 4
EXPERTS_PER_GROUP = 4
N_EXPERTS = 16
EXPERT_FF = 256
GROUP_FF = EXPERTS_PER_GROUP * EXPERT_FF
EPS = 1e-6

LANES = 128
SUBLANES = 8
HEAD_PAIRS = SB_WIDTH // LANES
HEAD_TILE = 256
SB_BLOCK = 128
SB_WIDE = 2 * SB_BLOCK
SB_FIRST = 3 * SB_BLOCK
SB_PER_STEP = 8
CACHE_BLOCK = 256
MOE_SUB = 256
MOE_CAP = 96
MOE_SHORT = 80
MOE_SLOTS = N_EXPERT_GROUPS * MOE_CAP
MOE_SUPER = 8
SLOT_LANE = 0
VMEM_LIMIT = 56 * 1024 * 1024

LOG2_E = 1.4426950408889634
Q_SCALE = SB_HEAD_DIM ** -0.5 * LOG2_E
SB_UNDERFLOW_BITS = 160.0
MASKED = -1e30

F32 = jnp.float32
BF16 = jnp.bfloat16
_NT = (((1,), (1,)), ((), ()))


def _gelu_tanh(x):
    return 0.5 * x * (1.0 + jnp.tanh(0.7978845608028654 * (x + 0.044715 * (x * x * x))))


def _softplus2(z):
    return jnp.maximum(z, 0.0) + jnp.log2(1.0 + jnp.exp2(-jnp.abs(z)))


def _suffix_matrix(n):
    r = lax.broadcasted_iota(jnp.int32, (n, n), 0)
    c = lax.broadcasted_iota(jnp.int32, (n, n), 1)
    return jnp.where(r > c, 1.0, 0.0).astype(BF16)


def _mixer_in_body(x_ref, g1_ref, w_in_ref, gvg_ref, qg_ref, kg_ref, ws_ref, bst_ref, bd_ref,
                   gm_ref, kf_ref, vf_ref, *rest, chunk, stream):
    x = x_ref[...]
    tm = x.shape[0]
    xn = x * lax.rsqrt(jnp.mean(x * x, axis=-1, keepdims=True) + EPS) * g1_ref[...]
    proj = jnp.dot(xn.astype(BF16), w_in_ref[...], preferred_element_type=F32)

    gu = _gelu_tanh(proj[:, :GM_WIDTH])
    gvr = _gelu_tanh(proj[:, GM_WIDTH:2 * GM_WIDTH])
    gv = gvr * lax.rsqrt(jnp.mean(gvr * gvr, axis=-1, keepdims=True) + EPS) * gvg_ref[...]

    o = 2 * GM_WIDTH
    q = proj[:, o:o + SB_WIDTH]
    k = proj[:, o + SB_WIDTH:o + 2 * SB_WIDTH]
    v = proj[:, o + 2 * SB_WIDTH:]
    bd = bd_ref[...]

    def head_mean_square(a):
        sq = (a * a).astype(BF16)
        return jnp.concatenate(
            [jnp.dot(sq[:, lo:lo + HEAD_TILE], bd, preferred_element_type=F32)
             for lo in range(0, SB_WIDTH, HEAD_TILE)], axis=1)

    qms = head_mean_square(q)
    kms = head_mean_square(k)
    qs = q * lax.rsqrt(qms + EPS) * qg_ref[...] * Q_SCALE
    kn = k * lax.rsqrt(kms + EPS) * kg_ref[...]
    if stream:
        kf_ref[...] = kn
        vf_ref[...] = v
        rest[0][...] = qs
        rest[1][...] = gv
    else:
        kt = kn.T
        kf_ref[...] = kt
        vf_ref[...] = v.T
        qs_ref, kb_ref, vb_ref = rest
        qs_ref[...] = qs.astype(BF16)
        kb_ref[...] = kt.astype(BF16)
        vb_ref[...] = v.astype(BF16)

    r = lax.broadcasted_iota(jnp.int32, (chunk, chunk), 0)
    c = lax.broadcasted_iota(jnp.int32, (chunk, chunk), 1)
    gvb = gv.astype(BF16)
    chunks = [slice(ci * chunk, (ci + 1) * chunk) for ci in range(tm // chunk)]
    for g in range(GM_GROUPS):
        w = jnp.where(r >= c, ws_ref[g, :chunk, :chunk], 0.0).astype(BF16)
        b = bst_ref[:chunk, g:g + 1]
        cols = slice(g * GM_GROUP_DIM, (g + 1) * GM_GROUP_DIM)
        mixed = jnp.dot(w, jnp.concatenate([gvb[rows, cols] for rows in chunks], axis=1),
                        preferred_element_type=F32)
        for ci, rows in enumerate(chunks):
            part = mixed[:, ci * GM_GROUP_DIM:(ci + 1) * GM_GROUP_DIM] + b
            gm_ref[rows, cols] = (gu[rows, cols] * part).astype(BF16)


def _mixer_in(x2d, params, *, tm, chunk, stream, seq=None):
    n = x2d.shape[0]
    assert n % tm == 0 and tm % chunk == 0 and chunk <= GM_CHUNK, (n, tm, chunk)
    full = lambda a: pl.BlockSpec(a.shape, lambda i: (0,) * a.ndim)
    spec = lambda s: pl.BlockSpec((tm,) + s.shape[1:], lambda i: (i,) + (0,) * (len(s.shape) - 1))
    gm = jax.ShapeDtypeStruct((n, GM_WIDTH), BF16)
    if stream:
        rows = jax.ShapeDtypeStruct((n, SB_WIDTH), F32)
        out_shape = [gm, rows, rows,
                     rows,
                     jax.ShapeDtypeStruct((n, GM_WIDTH), F32)]
        out_specs = [spec(s) for s in out_shape]
    else:
        per_seq = seq // tm
        rows_t = jax.ShapeDtypeStruct((n // seq, SB_WIDTH, seq), F32)
        spec_t = pl.BlockSpec((None, SB_WIDTH, tm), lambda i: (i // per_seq, 0, i % per_seq))
        dense = jax.ShapeDtypeStruct((n, SB_WIDTH), BF16)
        dense_t = jax.ShapeDtypeStruct(rows_t.shape, BF16)
        out_shape = [gm, rows_t, rows_t, dense, dense_t, dense]
        out_specs = [spec(gm), spec_t, spec_t, spec(dense), spec_t, spec(dense)]
    return pl.pallas_call(
        functools.partial(_mixer_in_body, chunk=chunk, stream=stream),
        grid=(n // tm,),
        in_specs=[pl.BlockSpec((tm, D_MODEL), lambda i: (i, 0))] + [full(a) for a in params],
        out_specs=out_specs,
        out_shape=out_shape,
        compiler_params=pltpu.CompilerParams(
            dimension_semantics=("parallel",), vmem_limit_bytes=VMEM_LIMIT),
        name="mixer_in_stream" if stream else "mixer_in",
    )(x2d, *params)


def _prompt_attn_body(q_ref, k_ref, v_ref, sfx_ref, sfx_wide_ref, o_ref, *scratch):
    blocks = q_ref.shape[0] // SB_BLOCK

    def one(sub, carry):
        rows = pl.ds(pl.multiple_of(sub * SB_BLOCK, SB_BLOCK), SB_BLOCK)
        _prompt_attn_block(pl.program_id(1) * blocks + sub, q_ref.at[rows], k_ref, v_ref, sfx_ref,
                           sfx_wide_ref, o_ref.at[rows], *scratch)
        return carry

    lax.fori_loop(0, blocks, one, 0)


def _prompt_attn_block(i, q_ref, k_ref, v_ref, sfx_ref, sfx_wide_ref, o_ref, qst_ref, c_ref, acc_ref,
                       least_ref):
    lane = lax.broadcasted_iota(jnp.int32, (SB_BLOCK, LANES), 1)
    first = lane < SB_HEAD_DIM
    for hp in range(HEAD_PAIRS):
        q = q_ref[:, hp * LANES:(hp + 1) * LANES]
        zero = jnp.zeros_like(q)
        qst_ref[hp, :SB_BLOCK, :] = jnp.where(first, q, zero)
        qst_ref[hp, SB_BLOCK:, :] = jnp.where(first, zero, q)
    m = 2 * SB_BLOCK
    suffix = {SB_BLOCK: sfx_ref, SB_WIDE: sfx_wide_ref}

    def sweep(j, keys, causal, fresh):
        rows = pl.ds(pl.multiple_of(j * SB_BLOCK, SB_BLOCK), keys)
        cols = lambda hp: slice(hp * LANES, (hp + 1) * LANES)
        z = jnp.concatenate(
            [jnp.dot(qst_ref[hp], k_ref[cols(hp), rows], preferred_element_type=F32)
             for hp in range(HEAD_PAIRS)], axis=0)
        if causal is not None:
            old = keys - SB_BLOCK
            newest = jnp.where(causal, z[:, old:], MASKED)
            z = newest if old == 0 else jnp.concatenate([z[:, :old], newest], axis=1)
        sp = _softplus2(z)
        c0 = jnp.zeros((HEAD_PAIRS * m, 1), F32) if fresh else c_ref[...]
        spb = sp.astype(BF16)
        if keys == SB_FIRST:
            newest = sp[:, SB_WIDE:]
            later = jnp.concatenate(
                [jnp.dot(spb[:, :SB_WIDE], suffix[SB_WIDE][...], preferred_element_type=F32)
                 + jnp.sum(newest, axis=-1, keepdims=True),
                 jnp.dot(spb[:, SB_WIDE:], suffix[SB_BLOCK][...], preferred_element_type=F32)],
                axis=1) + c0
        else:
            later = jnp.dot(spb, suffix[keys][...], preferred_element_type=F32) + c0
        w = jnp.exp2(z - sp - later)
        wb = w.astype(BF16)
        pv = jnp.concatenate(
            [jnp.dot(wb[hp * m:(hp + 1) * m, :], v_ref[rows, cols(hp)], preferred_element_type=F32)
             for hp in range(HEAD_PAIRS)], axis=0)
        acc_ref[...] = pv if fresh else acc_ref[...] + pv
        c = c0 + jnp.sum(sp, axis=-1, keepdims=True)
        c_ref[...] = c
        least_ref[0] = jnp.min(c)

    below = SB_FIRST // SB_BLOCK - 1
    t = lax.broadcasted_iota(jnp.int32, (HEAD_PAIRS * m, SB_BLOCK), 0) % SB_BLOCK
    s = lax.broadcasted_iota(jnp.int32, (HEAD_PAIRS * m, SB_BLOCK), 1)

    @pl.when(i >= below)
    def _():
        sweep(i - below, SB_FIRST, s < t, True)

    @pl.when(i < below)
    def _():
        sweep(i, SB_BLOCK, s < t, True)

    j0 = jnp.where(i >= below, i - below, i)

    def least_carry():
        return least_ref[0]

    per_wide = SB_WIDE // SB_BLOCK
    n_wide = j0 // per_wide

    def more(state):
        jj, carry = state
        return (jj < n_wide) & (carry < SB_UNDERFLOW_BITS)

    def body(state):
        jj, _ = state
        sweep(j0 - per_wide * (jj + 1), SB_WIDE, None, False)
        return jj + 1, least_carry()

    done, carry = lax.while_loop(more, body, (jnp.int32(0), least_carry()))

    @pl.when((done == n_wide) & (j0 % per_wide == 1) & (carry < SB_UNDERFLOW_BITS))
    def _():
        sweep(0, SB_BLOCK, None, False)

    for hp in range(HEAD_PAIRS):
        lo = hp * m
        o_ref[:, hp * LANES:(hp + 1) * LANES] = jnp.where(
            first, acc_ref[lo:lo + SB_BLOCK, :], acc_ref[lo + SB_BLOCK:lo + m, :]).astype(o_ref.dtype)


def _prompt_attn(qs, kt, vb):
    b, s, _ = qs.shape
    assert s % SB_BLOCK == 0, s
    sfx = tuple(_suffix_matrix(keys) for keys in (SB_BLOCK, SB_WIDE))
    per_step = min(SB_PER_STEP * SB_BLOCK, s)
    assert s % per_step == 0, s
    q_spec = pl.BlockSpec((None, per_step, SB_WIDTH), lambda bi, i: (bi, i, 0))
    kt_spec = pl.BlockSpec((None, SB_WIDTH, s), lambda bi, i: (bi, 0, 0))
    kv_spec = pl.BlockSpec((None, s, SB_WIDTH), lambda bi, i: (bi, 0, 0))
    m = 2 * SB_BLOCK
    return pl.pallas_call(
        _prompt_attn_body,
        grid=(b, s // per_step),
        in_specs=[q_spec, kt_spec, kv_spec] + [pl.BlockSpec(a.shape, lambda bi, i: (0, 0)) for a in sfx],
        out_specs=q_spec,
        out_shape=jax.ShapeDtypeStruct((b, s, SB_WIDTH), BF16),
        scratch_shapes=[pltpu.VMEM((HEAD_PAIRS, m, LANES), BF16),
                        pltpu.VMEM((HEAD_PAIRS * m, 1), F32),
                        pltpu.VMEM((HEAD_PAIRS * m, LANES), F32),
                        pltpu.SMEM((1,), F32)],
        compiler_params=pltpu.CompilerParams(
            dimension_semantics=("parallel", "arbitrary"), vmem_limit_bytes=VMEM_LIMIT),
        name="prompt_attn",
    )(qs, kt, vb, *sfx)


def _sample_attn_body(q_ref, kn_ref, vn_ref, ck_hbm, cv_hbm, o_ref, kbuf, vbuf, sems, c_ref, acc_ref,
                      *, layer):
    b = pl.program_id(0)
    n = q_ref.shape[0]
    m = SB_HEADS * n
    nblk = ck_hbm.shape[-1] // CACHE_BLOCK

    def copies(stream, blk, slot):
        cols = pl.ds(blk * CACHE_BLOCK, CACHE_BLOCK)
        return (pltpu.make_async_copy(ck_hbm.at[layer, stream, :, :, cols], kbuf.at[slot], sems.at[slot, 0]),
                pltpu.make_async_copy(cv_hbm.at[layer, stream, :, :, cols], vbuf.at[slot], sems.at[slot, 1]))

    def start(stream, blk, slot):
        for cp in copies(stream, blk, slot):
            cp.start()

    def wait(stream, blk, slot):
        for cp in copies(stream, blk, slot):
            cp.wait()

    def slot_of(jj):
        return 2 + jj % 2

    @pl.when(b == 0)
    def _():
        start(0, nblk - 1, 0)

    @pl.when(b + 1 < pl.num_programs(0))
    def _():
        start(b + 1, nblk - 1, (b + 1) % 2)

    head = lambda a, h: a[:, h * SB_HEAD_DIM:(h + 1) * SB_HEAD_DIM]
    q_all = q_ref[...].astype(BF16)
    q = [head(q_all, h) for h in range(SB_HEADS)]

    def sweep(slot, newest, suffix, causal, fresh):
        def logits(h):
            z = jnp.dot(q[h], kbuf[slot, h].astype(BF16), preferred_element_type=F32)
            if newest is None:
                return z
            return jnp.concatenate(
                [z, lax.dot_general(q[h], newest[0][h], _NT, preferred_element_type=F32)], axis=1)

        z = jnp.concatenate([logits(h) for h in range(SB_HEADS)], axis=0)
        if causal is not None:
            z = jnp.where(causal, z, MASKED)
        sp = _softplus2(z)
        c0 = jnp.zeros((m, 1), F32) if fresh else c_ref[...]
        later = jnp.dot(sp.astype(BF16), suffix, preferred_element_type=F32) + c0
        w = jnp.exp2(z - sp - later)
        wb = w.astype(BF16)
        pvs = []
        for h in range(SB_HEADS):
            wh = wb[h * n:(h + 1) * n, :]
            pv = lax.dot_general(wh[:, :CACHE_BLOCK], vbuf[slot, h].astype(BF16), _NT,
                                 preferred_element_type=F32)
            if newest is not None:
                pv = pv + jnp.dot(wh[:, CACHE_BLOCK:], newest[1][h], preferred_element_type=F32)
            pvs.append(pv)
        pv = jnp.stack(pvs)
        acc_ref[...] = pv if fresh else acc_ref[...] + pv
        c_ref[...] = c0 + jnp.sum(sp, axis=-1, keepdims=True)

    pad = jnp.zeros((LANES - n, SB_HEAD_DIM), BF16)
    kn_all, vn_all = kn_ref[...].astype(BF16), vn_ref[...].astype(BF16)
    newest = ([jnp.concatenate([head(kn_all, h), pad], axis=0) for h in range(SB_HEADS)],
              [jnp.concatenate([head(vn_all, h), pad], axis=0) for h in range(SB_HEADS)])
    t = lax.broadcasted_iota(jnp.int32, (m, CACHE_BLOCK + LANES), 0) % n
    s = lax.broadcasted_iota(jnp.int32, (m, CACHE_BLOCK + LANES), 1) - CACHE_BLOCK
    wait(b, nblk - 1, b % 2)
    sweep(b % 2, newest, _suffix_matrix(CACHE_BLOCK + LANES), s < t, True)

    suffix = _suffix_matrix(CACHE_BLOCK)

    def more(state):
        jj, carry = state
        return (jj < nblk) & (carry < SB_UNDERFLOW_BITS)

    def body(state):
        jj, _ = state
        blk = nblk - 1 - jj
        slot = slot_of(jj)
        wait(b, blk, slot)

        @pl.when(blk > 0)
        def _():
            start(b, blk - 1, slot_of(jj + 1))

        sweep(slot, None, suffix, None, False)
        return jj + 1, jnp.min(c_ref[...])

    state = (jnp.int32(1), jnp.min(c_ref[...]))

    if nblk > 1:
        @pl.when(more(state))
        def _():
            start(b, nblk - 2, slot_of(1))

    done, _ = lax.while_loop(more, body, state)

    @pl.when((done > 1) & (done < nblk))
    def _():
        wait(b, nblk - 1 - done, slot_of(done))

    for h in range(SB_HEADS):
        o_ref[:, h, :] = acc_ref[h]


def _sample_attn(qs, kn, vn, cache_k, cache_v, layer):
    db, n = qs.shape[:2]
    past = cache_k.shape[-1]
    assert n <= LANES and past >= CACHE_BLOCK and past % CACHE_BLOCK == 0, (n, past)
    m = SB_HEADS * n
    new_spec = pl.BlockSpec((None, n, SB_WIDTH), lambda b: (b, 0, 0))
    hbm = pl.BlockSpec(memory_space=pl.ANY)
    buf = pltpu.VMEM((4, SB_HEADS, SB_HEAD_DIM, CACHE_BLOCK), F32)
    return pl.pallas_call(
        functools.partial(_sample_attn_body, layer=layer),
        grid=(db,),
        in_specs=[new_spec, new_spec, new_spec, hbm, hbm],
        out_specs=pl.BlockSpec((None, n, SB_HEADS, SB_HEAD_DIM), lambda b: (b, 0, 0, 0)),
        out_shape=jax.ShapeDtypeStruct((db, n, SB_HEADS, SB_HEAD_DIM), F32),
        scratch_shapes=[buf, buf, pltpu.SemaphoreType.DMA((4, 2)),
                        pltpu.VMEM((m, 1), F32),
                        pltpu.VMEM((SB_HEADS, n, SB_HEAD_DIM), F32)],
        compiler_params=pltpu.CompilerParams(
            dimension_semantics=("arbitrary",), vmem_limit_bytes=VMEM_LIMIT),
        name="sample_attn",
    )(qs, kn, vn, cache_k, cache_v)


def _route(logits):
    lane = lax.broadcasted_iota(jnp.int32, logits.shape, 1).astype(F32)
    neg = jnp.float32(-jnp.inf)
    big = jnp.float32(1 << 20)
    rmax = lambda a: jnp.max(a, axis=-1, keepdims=True)
    rmin = lambda a: jnp.min(a, axis=-1, keepdims=True)
    rsum = lambda a: jnp.sum(a, axis=-1, keepdims=True)

    gmask = lane < N_EXPERT_GROUPS
    gl = jnp.where(gmask, logits, neg)
    gmax = rmax(gl)
    g_sel = rmin(jnp.where(gl == gmax, lane, big))
    p_sel = 1.0 / rsum(jnp.where(gmask, jnp.exp(gl - gmax), 0.0))

    e_lo = N_EXPERT_GROUPS + g_sel * EXPERTS_PER_GROUP
    emask = (lane >= e_lo) & (lane < e_lo + EXPERTS_PER_GROUP)
    el = jnp.where(emask, logits, neg)
    emax = rmax(el)
    ex = jnp.where(emask, jnp.exp(el - emax), 0.0)
    p_e = ex / rsum(ex)
    pm = jnp.where(emask, p_e, -1.0)
    v1 = rmax(pm)
    i1 = rmin(jnp.where(pm == v1, lane, big))
    pm2 = jnp.where(lane == i1, -1.0, pm)
    v2 = rmax(pm2)
    i2 = rmin(jnp.where(pm2 == v2, lane, big))
    tot = v1 + v2
    gate1 = p_sel * v1 / tot
    gate2 = p_sel * v2 / tot
    comb = jnp.where(lane == i1, gate1, 0.0) + jnp.where(lane == i2, gate2, 0.0)
    gates = jnp.where(lane == i1 - e_lo, gate1, 0.0) + jnp.where(lane == i2 - e_lo, gate2, 0.0)
    return comb, g_sel, gates


def _rmsnorm_rows(h, g):
    return h * lax.rsqrt(jnp.mean(h * h, axis=-1, keepdims=True) + EPS) * g


def _slot_matrix(slot):
    lane = lax.broadcasted_iota(jnp.int32, (slot.shape[0], MOE_SLOTS), 1)
    return jnp.where(lane == slot.astype(jnp.int32), 1.0, 0.0)


def _post_attn_body(x_ref, gm_ref, sb_ref, wo_ref, g2_ref, wr_ref, br_ref,
                    h_ref, route_ref, bucket_ref, gate_ref, cnt_ref):
    h = (x_ref[...]
         + jnp.dot(gm_ref[...], wo_ref[:GM_WIDTH, :], preferred_element_type=F32)
         + jnp.dot(sb_ref[...], wo_ref[GM_WIDTH:, :], preferred_element_type=F32))
    h_ref[...] = h
    hn = _rmsnorm_rows(h, g2_ref[...])
    hi = hn.astype(BF16)
    lo = (hn - hi.astype(F32)).astype(BF16)
    both = jnp.dot(hi, wr_ref[...], preferred_element_type=F32)
    logits = (both[:, :LANES] + jnp.dot(lo, wr_ref[:, :LANES], preferred_element_type=F32)
              + both[:, LANES:]) + br_ref[...]
    comb, g_sel, gates = _route(logits)

    subs = [slice(st * MOE_SUB, (st + 1) * MOE_SUB) for st in range(h.shape[0] // MOE_SUB)]
    lane = lax.broadcasted_iota(jnp.int32, comb.shape, 1).astype(F32)
    r = lax.broadcasted_iota(jnp.int32, (MOE_SUB, MOE_SUB), 0)
    c = lax.broadcasted_iota(jnp.int32, (MOE_SUB, MOE_SUB), 1)
    earlier = jnp.where(r > c, 1.0, 0.0).astype(BF16)
    onehot = jnp.where(lane == g_sel, 1.0, 0.0)
    onehot_b = onehot.astype(BF16)
    before = jnp.concatenate(
        [jnp.dot(earlier, onehot_b[rows], preferred_element_type=F32) for rows in subs], axis=0)
    rank = jnp.sum(onehot * before, axis=-1, keepdims=True)
    slot = jnp.where(rank < MOE_CAP, g_sel * MOE_CAP + rank, -1.0)
    route_ref[...] = jnp.where(lane == SLOT_LANE, slot, comb)
    g1 = gates.astype(BF16)
    g2 = (gates - g1.astype(F32)).astype(BF16)
    g3 = (gates - g1.astype(F32) - g2.astype(F32)).astype(BF16)
    g12 = jnp.concatenate([g1, g2], axis=1)
    for st, rows in enumerate(subs):
        place = _slot_matrix(slot[rows]).T.astype(BF16)
        bucket_ref[st] = jnp.dot(place, hi[rows], preferred_element_type=F32).astype(BF16)
        placed = jnp.dot(place, g12[rows], preferred_element_type=F32)
        gate_ref[st] = (placed[:, :LANES] + placed[:, LANES:]
                        + jnp.dot(place, g3[rows], preferred_element_type=F32))
        cnt_ref[st] = jnp.broadcast_to(jnp.sum(onehot[rows], axis=0, keepdims=True), cnt_ref.shape[1:])


def _moe_ffn_body(long_ref, x_ref, gate_ref, xs_ref, gates_ref, wg_ref, wu_ref, wd_ref, y_ref, ys_ref,
                  wgb_ref, wub_ref, wdb_ref, *, prompt_steps):
    g, s = pl.program_id(0), pl.program_id(1)

    @pl.when(s == 0)
    def _():
        wgb_ref[...] = wg_ref[...].astype(BF16)
        wub_ref[...] = wu_ref[...].astype(BF16)
        wdb_ref[...] = wd_ref[...].astype(BF16)

    def ffn(ins, outs, used):
        x = jnp.concatenate([r[:, 0, :used, :].reshape(-1, D_MODEL) for r, _ in ins], axis=0)
        gates = jnp.concatenate([r[:, 0, :used, :].reshape(-1, LANES) for _, r in ins], axis=0)
        parts = []
        for e in range(EXPERTS_PER_GROUP):
            hg = jnp.dot(x, wgb_ref[e], preferred_element_type=F32)
            hu = jnp.dot(x, wub_ref[e], preferred_element_type=F32)
            parts.append((hg / (1.0 + jnp.exp(-hg)) * hu * gates[:, e:e + 1]).astype(BF16))
        a = jnp.concatenate(parts, axis=-1)
        y = jnp.dot(a, wdb_ref[...].reshape(GROUP_FF, D_MODEL), preferred_element_type=F32)
        row = 0
        for out_ref in outs:
            rows = out_ref.shape[0] * used
            out_ref[:, 0, :used, :] = y[row:row + rows].reshape(out_ref.shape[0], used, D_MODEL)
            if used < MOE_CAP:
                out_ref[:, 0, used:, :] = jnp.zeros((out_ref.shape[0], MOE_CAP - used, D_MODEL), F32)
            row += rows

    def step(used):
        @pl.when(s < prompt_steps - 1)
        def _():
            ffn([(x_ref, gate_ref)], [y_ref], used)

        @pl.when(s == prompt_steps - 1)
        def _():
            ffn([(x_ref, gate_ref), (xs_ref, gates_ref)], [y_ref, ys_ref], used)

    @pl.when(long_ref[g, s] == 0)
    def _():
        step(MOE_SHORT)

    @pl.when(long_ref[g, s] != 0)
    def _():
        step(MOE_CAP)


def _moe_combine_body(h_ref, route_ref, ys_ref, o_ref):
    for st in range(h_ref.shape[0] // MOE_SUB):
        rows = slice(st * MOE_SUB, (st + 1) * MOE_SUB)
        pick = _slot_matrix(route_ref[rows, SLOT_LANE:SLOT_LANE + 1]).astype(BF16)
        ys = ys_ref[st]
        hi = ys.astype(BF16)
        lo = (ys - hi.astype(F32)).astype(BF16)
        o_ref[rows, :] = (h_ref[rows, :] + jnp.dot(pick, hi, preferred_element_type=F32)
                          + jnp.dot(pick, lo, preferred_element_type=F32))


def _moe_dense_body(h_ref, route_ref, g2_ref, wg_ref, wu_ref, wd_ref, o_ref, hn_ref):
    g = pl.program_id(1)

    @pl.when(g == 0)
    def _():
        h = h_ref[...]
        o_ref[...] = h
        hn_ref[...] = _rmsnorm_rows(h, g2_ref[...]).astype(BF16)

    hn = hn_ref[...]
    comb = route_ref[...]
    lane = lax.broadcasted_iota(jnp.int32, comb.shape, 1)
    parts = []
    for e in range(EXPERTS_PER_GROUP):
        gate = jnp.sum(jnp.where(lane == N_EXPERT_GROUPS + g * EXPERTS_PER_GROUP + e, comb, 0.0),
                       axis=-1, keepdims=True)
        hg = jnp.dot(hn, wg_ref[e], preferred_element_type=F32)
        hu = jnp.dot(hn, wu_ref[e], preferred_element_type=F32)
        parts.append((hg / (1.0 + jnp.exp(-hg)) * hu * gate).astype(BF16))
    a = jnp.concatenate(parts, axis=-1)
    o_ref[...] += jnp.dot(a, wd_ref[...].reshape(GROUP_FF, D_MODEL), preferred_element_type=F32)


def _group_spec(a, index):
    return pl.BlockSpec((EXPERTS_PER_GROUP,) + a.shape[1:], index)


def _post_attn(x2d, gm, sb, params, tm):
    n = x2d.shape[0]
    wo, g2, wr, br = params
    assert n % tm == 0 and tm % MOE_SUB == 0, (n, tm)
    n_sub, sub = n // MOE_SUB, tm // MOE_SUB
    row = lambda w: pl.BlockSpec((tm, w), lambda i: (i, 0))
    full = lambda a: pl.BlockSpec(a.shape, lambda i: (0,) * a.ndim)
    per_sub = lambda rows, w: pl.BlockSpec((sub, rows, w), lambda i: (i, 0, 0))
    return pl.pallas_call(
        _post_attn_body,
        grid=(n // tm,),
        in_specs=[row(D_MODEL), row(GM_WIDTH), row(SB_WIDTH),
                  full(wo), full(g2), full(wr), full(br)],
        out_specs=[row(D_MODEL), row(LANES), per_sub(MOE_SLOTS, D_MODEL),
                   per_sub(MOE_SLOTS, LANES), per_sub(SUBLANES, LANES)],
        out_shape=[jax.ShapeDtypeStruct((n, D_MODEL), F32),
                   jax.ShapeDtypeStruct((n, LANES), F32),
                   jax.ShapeDtypeStruct((n_sub, MOE_SLOTS, D_MODEL), BF16),
                   jax.ShapeDtypeStruct((n_sub, MOE_SLOTS, LANES), F32),
                   jax.ShapeDtypeStruct((n_sub, SUBLANES, LANES), F32)],
        compiler_params=pltpu.CompilerParams(
            dimension_semantics=("parallel",), vmem_limit_bytes=VMEM_LIMIT),
        name="post_attn",
    )(x2d, gm, sb, wo, g2, wr, br)


def _out_moe(prompt, sample, params, *, tm_prompt, tm_sample):
    wo, g2, wr, br, wg, wu, wd = params
    routed = [_post_attn(*grp, (wo, g2, wr, br), tm) for grp, tm in
              ((prompt, tm_prompt), (sample, tm_sample))]
    tms = (tm_prompt, tm_sample)
    params_2d = pltpu.CompilerParams(
        dimension_semantics=("parallel", "arbitrary"), vmem_limit_bytes=VMEM_LIMIT)

    def sparse():
        (_, _, bk_p, gt_p, cnt_p), (_, _, bk_s, gt_s, cnt_s) = routed
        n_sub, n_sub_s = bk_p.shape[0], bk_s.shape[0]
        sup = min(MOE_SUPER, n_sub)
        assert n_sub % sup == 0, (n_sub, sup)
        steps = n_sub // sup
        fullest = jnp.max(cnt_p[:, 0, :N_EXPERT_GROUPS].reshape(steps, sup, N_EXPERT_GROUPS), axis=1)
        fullest = fullest.at[steps - 1].max(jnp.max(cnt_s[:, 0, :N_EXPERT_GROUPS], axis=0))
        long_steps = (fullest.T > MOE_SHORT).astype(jnp.int32)
        by_group = lambda a: a.reshape(a.shape[0], N_EXPERT_GROUPS, MOE_CAP, a.shape[-1])
        bucket = lambda w: pl.BlockSpec((sup, 1, MOE_CAP, w), lambda g, s, flags: (s, g, 0, 0))
        bucket_s = lambda w: pl.BlockSpec((n_sub_s, 1, MOE_CAP, w), lambda g, s, flags: (0, g, 0, 0))
        ys = pl.pallas_call(
            functools.partial(_moe_ffn_body, prompt_steps=steps),
            grid_spec=pltpu.PrefetchScalarGridSpec(
                num_scalar_prefetch=1,
                grid=(N_EXPERT_GROUPS, steps),
                in_specs=[bucket(D_MODEL), bucket(LANES), bucket_s(D_MODEL), bucket_s(LANES)]
                         + [_group_spec(w, lambda g, s, flags: (g, 0, 0)) for w in (wg, wu, wd)],
                out_specs=[bucket(D_MODEL), bucket_s(D_MODEL)],
                scratch_shapes=[pltpu.VMEM((EXPERTS_PER_GROUP,) + w.shape[1:], BF16)
                                for w in (wg, wu, wd)]),
            out_shape=[jax.ShapeDtypeStruct((n, N_EXPERT_GROUPS, MOE_CAP, D_MODEL), F32)
                       for n in (n_sub, n_sub_s)],
            compiler_params=params_2d,
            name="moe_ffn",
        )(long_steps, by_group(bk_p), by_group(gt_p), by_group(bk_s), by_group(gt_s), wg, wu, wd)

        def combine(h, route, y, tm):
            n, sub = h.shape[0], tm // MOE_SUB
            row = lambda w: pl.BlockSpec((tm, w), lambda i: (i, 0))
            return pl.pallas_call(
                _moe_combine_body,
                grid=(n // tm,),
                in_specs=[row(D_MODEL), row(LANES),
                          pl.BlockSpec((sub, MOE_SLOTS, D_MODEL), lambda i: (i, 0, 0))],
                out_specs=row(D_MODEL),
                out_shape=jax.ShapeDtypeStruct((n, D_MODEL), F32),
                compiler_params=pltpu.CompilerParams(
                    dimension_semantics=("parallel",), vmem_limit_bytes=VMEM_LIMIT),
                name="moe_combine",
            )(h, route, y.reshape(-1, MOE_SLOTS, D_MODEL))

        return tuple(combine(r[0], r[1], y, tm) for r, y, tm in zip(routed, ys, tms))

    def dense():
        wgb, wub, wdb = (w.astype(BF16) for w in (wg, wu, wd))

        def all_experts(h, route, tm):
            row2 = lambda w: pl.BlockSpec((tm, w), lambda i, g: (i, 0))
            return pl.pallas_call(
                _moe_dense_body,
                grid=(h.shape[0] // tm, N_EXPERT_GROUPS),
                in_specs=[row2(D_MODEL), row2(LANES), pl.BlockSpec(g2.shape, lambda i, g: (0, 0))]
                         + [_group_spec(w, lambda i, g: (g, 0, 0)) for w in (wg, wu, wd)],
                out_specs=row2(D_MODEL),
                out_shape=jax.ShapeDtypeStruct(h.shape, F32),
                scratch_shapes=[pltpu.VMEM((tm, D_MODEL), BF16)],
                compiler_params=params_2d,
                name="moe_dense",
            )(h, route, g2, wgb, wub, wdb)

        return tuple(all_experts(r[0], r[1], tm) for r, tm in zip(routed, tms))

    most = jnp.maximum(jnp.max(routed[0][4]), jnp.max(routed[1][4]))
    return lax.cond(most <= MOE_CAP, sparse, dense)


def _layer(layer, xp, xs, cache_k, cache_v, norm1_g, w_in, gm_v_norm_g, gm_w_s, gm_b_s, q_norm_g,
           k_norm_g, w_out, norm2_g, w_router_group, b_router_group, w_router_expert,
           b_router_expert, w_gate, w_up, w_down):
    b, s, d = xp.shape
    db, n, _ = xs.shape
    head_of = jnp.arange(HEAD_TILE) // SB_HEAD_DIM
    bd = jnp.where(head_of[:, None] == head_of[None, :], 1.0 / SB_HEAD_DIM, 0.0).astype(BF16)
    in_params = (norm1_g[None, :], w_in.astype(BF16), gm_v_norm_g[None, :],
                 jnp.tile(q_norm_g, SB_HEADS)[None, :], jnp.tile(k_norm_g, SB_HEADS)[None, :],
                 gm_w_s, gm_b_s.T, bd)

    wr = jnp.concatenate(
        [w_router_group, jnp.transpose(w_router_expert, (1, 0, 2)).reshape(d, N_EXPERTS)], axis=1)
    wr = jnp.pad(wr, ((0, 0), (0, LANES - wr.shape[1])))
    wr_hi = wr.astype(BF16)
    wr_lo = (wr - wr_hi.astype(F32)).astype(BF16)
    br = jnp.pad(jnp.concatenate([b_router_group, b_router_expert.reshape(-1)]),
                 (0, LANES - N_EXPERT_GROUPS - N_EXPERTS))[None, :]

    moe_params = (w_out.astype(BF16), norm2_g[None, :], jnp.concatenate([wr_hi, wr_lo], axis=1), br,
                  w_gate, w_up, w_down)

    xp2 = xp.reshape(b * s, d)
    gm, kf, vf, qs, kb, vb = _mixer_in(xp2, in_params, tm=min(1024, s), chunk=GM_CHUNK, stream=False,
                                       seq=s)
    shp = (b, s, SB_WIDTH)
    sb = _prompt_attn(qs.reshape(shp), kb, vb.reshape(shp)).reshape(b * s, SB_WIDTH)

    xs2 = xs.reshape(db * n, d)
    gm_s, kf_s, vf_s, q_s, gv_s = _mixer_in(xs2, in_params, tm=db * n, chunk=n, stream=True)
    shs = (db, n, SB_WIDTH)
    frame_minor = lambda c: jnp.transpose(c, (0, 1, 3, 4, 2))
    sb_s = _sample_attn(q_s.reshape(shs), kf_s.reshape(shs), vf_s.reshape(shs),
                        frame_minor(cache_k), frame_minor(cache_v), layer)
    sb_s = sb_s.reshape(db * n, SB_WIDTH).astype(BF16)

    yp, ys = _out_moe((xp2, gm, sb), (xs2, gm_s, sb_s), moe_params,
                      tm_prompt=min(1024, b * s), tm_sample=db * n)
    yp, ys = yp.reshape(b, s, d), ys.reshape(db, n, d)

    heads = (SB_HEADS, SB_HEAD_DIM)
    rows = lambda a: jnp.transpose(a.reshape(b, *heads, s), (0, 3, 1, 2))
    return (yp, ys, rows(kf), rows(vf),
            kf_s.reshape(db, n, *heads), vf_s.reshape(db, n, *heads),
            gv_s.reshape(db, n, GM_GROUPS, GM_GROUP_DIM))


def kernel(x_prompt, x_sample, cache_sb_k, cache_sb_v, norm1_g, w_in, gm_v_norm_g, gm_w_s, gm_b_s, q_norm_g, k_norm_g, w_out, norm2_g, w_router_group, b_router_group, w_router_expert, b_router_expert, w_gate, w_up, w_down):
    depth = w_in.shape[0]
    yp, ys = x_prompt, x_sample
    outs = [[] for _ in range(5)]
    for l in range(depth):
        yp, ys, *rest = _layer(
            l, yp, ys, cache_sb_k, cache_sb_v, norm1_g[l], w_in[l], gm_v_norm_g[l], gm_w_s[l],
            gm_b_s[l], q_norm_g[l], k_norm_g[l], w_out[l], norm2_g[l], w_router_group[l],
            b_router_group[l], w_router_expert[l], b_router_expert[l], w_gate[l], w_up[l],
            w_down[l])
        for acc, r in zip(outs, rest):
            acc.append(r)
    return (yp, ys) + tuple(jnp.stack(o, axis=0) for o in outs)
```

```python
import functools

import jax
import jax.numpy as jnp
from jax import lax
from jax.experimental import pallas as pl
from jax.experimental.pallas import tpu as pltpu

D_MODEL = 1024
GM_WIDTH = 512
GM_GROUPS = 4
GM_GROUP_DIM = 128
GM_CHUNK = 128
SB_WIDTH = 512
SB_HEAD_DIM = 64
SB_HEADS = 8
N_EXPERT_GROUPS = 4
EXPERTS_PER_GROUP = 4
N_EXPERTS = 16
EXPERT_FF = 256
GROUP_FF = EXPERTS_PER_GROUP * EXPERT_FF
EPS = 1e-6

LANES = 128
SUBLANES = 8
HEAD_PAIRS = SB_WIDTH // LANES
HEAD_TILE = 256
SB_BLOCK = 128
SB_WIDE = 2 * SB_BLOCK
SB_FIRST = 3 * SB_BLOCK
SB_PER_STEP = 8
CACHE_BLOCK = 256
MOE_SUB = 256
MOE_CAP = 96
MOE_SHORT = 80
MOE_SLOTS = N_EXPERT_GROUPS * MOE_CAP
MOE_SUPER = 8
SLOT_LANE = 0
COMBINE_AHEAD = 2
COMBINE_RING = COMBINE_AHEAD + 1
VMEM_LIMIT = 56 * 1024 * 1024

LOG2_E = 1.4426950408889634
Q_SCALE = SB_HEAD_DIM ** -0.5 * LOG2_E
SB_UNDERFLOW_BITS = 160.0
MASKED = -1e30

F32 = jnp.float32
BF16 = jnp.bfloat16
_NT = (((1,), (1,)), ((), ()))


def _gelu_tanh(x):
    return 0.5 * x * (1.0 + jnp.tanh(0.7978845608028654 * (x + 0.044715 * (x * x * x))))


def _softplus2(z):
    return jnp.maximum(z, 0.0) + jnp.log2(1.0 + jnp.exp2(-jnp.abs(z)))


def _suffix_matrix(n):
    r = lax.broadcasted_iota(jnp.int32, (n, n), 0)
    c = lax.broadcasted_iota(jnp.int32, (n, n), 1)
    return jnp.where(r > c, 1.0, 0.0).astype(BF16)


def _mixer_in_body(x_ref, g1_ref, w_in_ref, gvg_ref, qg_ref, kg_ref, ws_ref, bst_ref, bd_ref,
                   gm_ref, kf_ref, vf_ref, *rest, chunk, stream):
    x = x_ref[...]
    tm = x.shape[0]
    xn = x * lax.rsqrt(jnp.mean(x * x, axis=-1, keepdims=True) + EPS) * g1_ref[...]
    proj = jnp.dot(xn.astype(BF16), w_in_ref[...], preferred_element_type=F32)

    gu = _gelu_tanh(proj[:, :GM_WIDTH])
    gvr = _gelu_tanh(proj[:, GM_WIDTH:2 * GM_WIDTH])
    gv = gvr * lax.rsqrt(jnp.mean(gvr * gvr, axis=-1, keepdims=True) + EPS) * gvg_ref[...]

    o = 2 * GM_WIDTH
    q = proj[:, o:o + SB_WIDTH]
    k = proj[:, o + SB_WIDTH:o + 2 * SB_WIDTH]
    v = proj[:, o + 2 * SB_WIDTH:]
    bd = bd_ref[...]

    def head_mean_square(a):
        sq = (a * a).astype(BF16)
        return jnp.concatenate(
            [jnp.dot(sq[:, lo:lo + HEAD_TILE], bd, preferred_element_type=F32)
             for lo in range(0, SB_WIDTH, HEAD_TILE)], axis=1)

    qms = head_mean_square(q)
    kms = head_mean_square(k)
    qs = q * lax.rsqrt(qms + EPS) * qg_ref[...] * Q_SCALE
    kn = k * lax.rsqrt(kms + EPS) * kg_ref[...]
    if stream:
        kf_ref[...] = kn
        vf_ref[...] = v
        rest[0][...] = qs
        rest[1][...] = gv
    else:
        kt = kn.T
        kf_ref[...] = kt
        vf_ref[...] = v.T
        qs_ref, kb_ref, vb_ref = rest
        qs_ref[...] = qs.astype(BF16)
        kb_ref[...] = kt.astype(BF16)
        vb_ref[...] = v.astype(BF16)

    r = lax.broadcasted_iota(jnp.int32, (chunk, chunk), 0)
    c = lax.broadcasted_iota(jnp.int32, (chunk, chunk), 1)
    gvb = gv.astype(BF16)
    chunks = [slice(ci * chunk, (ci + 1) * chunk) for ci in range(tm // chunk)]
    for g in range(GM_GROUPS):
        w = jnp.where(r >= c, ws_ref[g, :chunk, :chunk], 0.0).astype(BF16)
        b = bst_ref[:chunk, g:g + 1]
        cols = slice(g * GM_GROUP_DIM, (g + 1) * GM_GROUP_DIM)
        mixed = jnp.dot(w, jnp.concatenate([gvb[rows, cols] for rows in chunks], axis=1),
                        preferred_element_type=F32)
        for ci, rows in enumerate(chunks):
            part = mixed[:, ci * GM_GROUP_DIM:(ci + 1) * GM_GROUP_DIM] + b
            gm_ref[rows, cols] = (gu[rows, cols] * part).astype(BF16)


def _mixer_in(x2d, params, *, tm, chunk, stream, seq=None):
    n = x2d.shape[0]
    assert n % tm == 0 and tm % chunk == 0 and chunk <= GM_CHUNK, (n, tm, chunk)
    full = lambda a: pl.BlockSpec(a.shape, lambda i: (0,) * a.ndim)
    spec = lambda s: pl.BlockSpec((tm,) + s.shape[1:], lambda i: (i,) + (0,) * (len(s.shape) - 1))
    gm = jax.ShapeDtypeStruct((n, GM_WIDTH), BF16)
    if stream:
        rows = jax.ShapeDtypeStruct((n, SB_WIDTH), F32)
        out_shape = [gm, rows, rows,
                     rows,
                     jax.ShapeDtypeStruct((n, GM_WIDTH), F32)]
        out_specs = [spec(s) for s in out_shape]
    else:
        per_seq = seq // tm
        rows_t = jax.ShapeDtypeStruct((n // seq, SB_WIDTH, seq), F32)
        spec_t = pl.BlockSpec((None, SB_WIDTH, tm), lambda i: (i // per_seq, 0, i % per_seq))
        dense = jax.ShapeDtypeStruct((n, SB_WIDTH), BF16)
        dense_t = jax.ShapeDtypeStruct(rows_t.shape, BF16)
        out_shape = [gm, rows_t, rows_t, dense, dense_t, dense]
        out_specs = [spec(gm), spec_t, spec_t, spec(dense), spec_t, spec(dense)]
    return pl.pallas_call(
        functools.partial(_mixer_in_body, chunk=chunk, stream=stream),
        grid=(n // tm,),
        in_specs=[pl.BlockSpec((tm, D_MODEL), lambda i: (i, 0))] + [full(a) for a in params],
        out_specs=out_specs,
        out_shape=out_shape,
        compiler_params=pltpu.CompilerParams(
            dimension_semantics=("parallel",), vmem_limit_bytes=VMEM_LIMIT),
        name="mixer_in_stream" if stream else "mixer_in",
    )(x2d, *params)


def _prompt_attn_body(q_ref, k_ref, v_ref, sfx_ref, sfx_wide_ref, o_ref, *scratch):
    blocks = q_ref.shape[0] // SB_BLOCK

    def one(sub, carry):
        rows = pl.ds(pl.multiple_of(sub * SB_BLOCK, SB_BLOCK), SB_BLOCK)
        _prompt_attn_block(pl.program_id(1) * blocks + sub, q_ref.at[rows], k_ref, v_ref, sfx_ref,
                           sfx_wide_ref, o_ref.at[rows], *scratch)
        return carry

    lax.fori_loop(0, blocks, one, 0)


def _prompt_attn_block(i, q_ref, k_ref, v_ref, sfx_ref, sfx_wide_ref, o_ref, qst_ref, c_ref, acc_ref,
                       least_ref):
    lane = lax.broadcasted_iota(jnp.int32, (SB_BLOCK, LANES), 1)
    first = lane < SB_HEAD_DIM
    for hp in range(HEAD_PAIRS):
        q = q_ref[:, hp * LANES:(hp + 1) * LANES]
        zero = jnp.zeros_like(q)
        qst_ref[hp, :SB_BLOCK, :] = jnp.where(first, q, zero)
        qst_ref[hp, SB_BLOCK:, :] = jnp.where(first, zero, q)
    m = 2 * SB_BLOCK
    suffix = {SB_BLOCK: sfx_ref, SB_WIDE: sfx_wide_ref}

    def sweep(j, keys, causal, fresh):
        rows = pl.ds(pl.multiple_of(j * SB_BLOCK, SB_BLOCK), keys)
        cols = lambda hp: slice(hp * LANES, (hp + 1) * LANES)
        z = jnp.concatenate(
            [jnp.dot(qst_ref[hp], k_ref[cols(hp), rows], preferred_element_type=F32)
             for hp in range(HEAD_PAIRS)], axis=0)
        if causal is not None:
            old = keys - SB_BLOCK
            newest = jnp.where(causal, z[:, old:], MASKED)
            z = newest if old == 0 else jnp.concatenate([z[:, :old], newest], axis=1)
        sp = _softplus2(z)
        c0 = jnp.zeros((HEAD_PAIRS * m, 1), F32) if fresh else c_ref[...]
        spb = sp.astype(BF16)
        if keys == SB_FIRST:
            newest = sp[:, SB_WIDE:]
            later = jnp.concatenate(
                [jnp.dot(spb[:, :SB_WIDE], suffix[SB_WIDE][...], preferred_element_type=F32)
                 + jnp.sum(newest, axis=-1, keepdims=True),
                 jnp.dot(spb[:, SB_WIDE:], suffix[SB_BLOCK][...], preferred_element_type=F32)],
                axis=1) + c0
        else:
            later = jnp.dot(spb, suffix[keys][...], preferred_element_type=F32) + c0
        w = jnp.exp2(z - sp - later)
        wb = w.astype(BF16)
        pv = jnp.concatenate(
            [jnp.dot(wb[hp * m:(hp + 1) * m, :], v_ref[rows, cols(hp)], preferred_element_type=F32)
             for hp in range(HEAD_PAIRS)], axis=0)
        acc_ref[...] = pv if fresh else acc_ref[...] + pv
        c = c0 + jnp.sum(sp, axis=-1, keepdims=True)
        c_ref[...] = c
        least_ref[0] = jnp.min(c)

    below = SB_FIRST // SB_BLOCK - 1
    t = lax.broadcasted_iota(jnp.int32, (HEAD_PAIRS * m, SB_BLOCK), 0) % SB_BLOCK
    s = lax.broadcasted_iota(jnp.int32, (HEAD_PAIRS * m, SB_BLOCK), 1)

    @pl.when(i >= below)
    def _():
        sweep(i - below, SB_FIRST, s < t, True)

    @pl.when(i < below)
    def _():
        sweep(i, SB_BLOCK, s < t, True)

    j0 = jnp.where(i >= below, i - below, i)

    def least_carry():
        return least_ref[0]

    per_wide = SB_WIDE // SB_BLOCK
    n_wide = j0 // per_wide

    def more(state):
        jj, carry = state
        return (jj < n_wide) & (carry < SB_UNDERFLOW_BITS)

    def body(state):
        jj, _ = state
        sweep(j0 - per_wide * (jj + 1), SB_WIDE, None, False)
        return jj + 1, least_carry()

    done, carry = lax.while_loop(more, body, (jnp.int32(0), least_carry()))

    @pl.when((done == n_wide) & (j0 % per_wide == 1) & (carry < SB_UNDERFLOW_BITS))
    def _():
        sweep(0, SB_BLOCK, None, False)

    for hp in range(HEAD_PAIRS):
        lo = hp * m
        o_ref[:, hp * LANES:(hp + 1) * LANES] = jnp.where(
            first, acc_ref[lo:lo + SB_BLOCK, :], acc_ref[lo + SB_BLOCK:lo + m, :]).astype(o_ref.dtype)


def _prompt_attn(qs, kt, vb):
    b, s, _ = qs.shape
    assert s % SB_BLOCK == 0, s
    sfx = tuple(_suffix_matrix(keys) for keys in (SB_BLOCK, SB_WIDE))
    per_step = min(SB_PER_STEP * SB_BLOCK, s)
    assert s % per_step == 0, s
    q_spec = pl.BlockSpec((None, per_step, SB_WIDTH), lambda bi, i: (bi, i, 0))
    kt_spec = pl.BlockSpec((None, SB_WIDTH, s), lambda bi, i: (bi, 0, 0))
    kv_spec = pl.BlockSpec((None, s, SB_WIDTH), lambda bi, i: (bi, 0, 0))
    m = 2 * SB_BLOCK
    return pl.pallas_call(
        _prompt_attn_body,
        grid=(b, s // per_step),
        in_specs=[q_spec, kt_spec, kv_spec] + [pl.BlockSpec(a.shape, lambda bi, i: (0, 0)) for a in sfx],
        out_specs=q_spec,
        out_shape=jax.ShapeDtypeStruct((b, s, SB_WIDTH), BF16),
        scratch_shapes=[pltpu.VMEM((HEAD_PAIRS, m, LANES), BF16),
                        pltpu.VMEM((HEAD_PAIRS * m, 1), F32),
                        pltpu.VMEM((HEAD_PAIRS * m, LANES), F32),
                        pltpu.SMEM((1,), F32)],
        compiler_params=pltpu.CompilerParams(
            dimension_semantics=("parallel", "arbitrary"), vmem_limit_bytes=VMEM_LIMIT),
        name="prompt_attn",
    )(qs, kt, vb, *sfx)


def _sample_attn_body(q_ref, kn_ref, vn_ref, ck_hbm, cv_hbm, o_ref, kbuf, vbuf, sems, c_ref, acc_ref,
                      *, layer):
    b = pl.program_id(0)
    n = q_ref.shape[0]
    m = SB_HEADS * n
    nblk = ck_hbm.shape[-1] // CACHE_BLOCK

    def copies(stream, blk, slot):
        cols = pl.ds(blk * CACHE_BLOCK, CACHE_BLOCK)
        return (pltpu.make_async_copy(ck_hbm.at[layer, stream, :, :, cols], kbuf.at[slot], sems.at[slot, 0]),
                pltpu.make_async_copy(cv_hbm.at[layer, stream, :, :, cols], vbuf.at[slot], sems.at[slot, 1]))

    def start(stream, blk, slot):
        for cp in copies(stream, blk, slot):
            cp.start()

    def wait(stream, blk, slot):
        for cp in copies(stream, blk, slot):
            cp.wait()

    def slot_of(jj):
        return 2 + jj % 2

    @pl.when(b == 0)
    def _():
        start(0, nblk - 1, 0)

    @pl.when(b + 1 < pl.num_programs(0))
    def _():
        start(b + 1, nblk - 1, (b + 1) % 2)

    head = lambda a, h: a[:, h * SB_HEAD_DIM:(h + 1) * SB_HEAD_DIM]
    q_all = q_ref[...].astype(BF16)
    q = [head(q_all, h) for h in range(SB_HEADS)]

    def sweep(slot, newest, suffix, causal, fresh):
        def logits(h):
            z = jnp.dot(q[h], kbuf[slot, h].astype(BF16), preferred_element_type=F32)
            if newest is None:
                return z
            return jnp.concatenate(
                [z, lax.dot_general(q[h], newest[0][h], _NT, preferred_element_type=F32)], axis=1)

        z = jnp.concatenate([logits(h) for h in range(SB_HEADS)], axis=0)
        if causal is not None:
            z = jnp.where(causal, z, MASKED)
        sp = _softplus2(z)
        c0 = jnp.zeros((m, 1), F32) if fresh else c_ref[...]
        later = jnp.dot(sp.astype(BF16), suffix, preferred_element_type=F32) + c0
        w = jnp.exp2(z - sp - later)
        wb = w.astype(BF16)
        pvs = []
        for h in range(SB_HEADS):
            wh = wb[h * n:(h + 1) * n, :]
            pv = lax.dot_general(wh[:, :CACHE_BLOCK], vbuf[slot, h].astype(BF16), _NT,
                                 preferred_element_type=F32)
            if newest is not None:
                pv = pv + jnp.dot(wh[:, CACHE_BLOCK:], newest[1][h], preferred_element_type=F32)
            pvs.append(pv)
        pv = jnp.stack(pvs)
        acc_ref[...] = pv if fresh else acc_ref[...] + pv
        c_ref[...] = c0 + jnp.sum(sp, axis=-1, keepdims=True)

    pad = jnp.zeros((LANES - n, SB_HEAD_DIM), BF16)
    kn_all, vn_all = kn_ref[...].astype(BF16), vn_ref[...].astype(BF16)
    newest = ([jnp.concatenate([head(kn_all, h), pad], axis=0) for h in range(SB_HEADS)],
              [jnp.concatenate([head(vn_all, h), pad], axis=0) for h in range(SB_HEADS)])
    t = lax.broadcasted_iota(jnp.int32, (m, CACHE_BLOCK + LANES), 0) % n
    s = lax.broadcasted_iota(jnp.int32, (m, CACHE_BLOCK + LANES), 1) - CACHE_BLOCK
    wait(b, nblk - 1, b % 2)
    sweep(b % 2, newest, _suffix_matrix(CACHE_BLOCK + LANES), s < t, True)

    suffix = _suffix_matrix(CACHE_BLOCK)

    def more(state):
        jj, carry = state
        return (jj < nblk) & (carry < SB_UNDERFLOW_BITS)

    def body(state):
        jj, _ = state
        blk = nblk - 1 - jj
        slot = slot_of(jj)
        wait(b, blk, slot)

        @pl.when(blk > 0)
        def _():
            start(b, blk - 1, slot_of(jj + 1))

        sweep(slot, None, suffix, None, False)
        return jj + 1, jnp.min(c_ref[...])

    state = (jnp.int32(1), jnp.min(c_ref[...]))

    if nblk > 1:
        @pl.when(more(state))
        def _():
            start(b, nblk - 2, slot_of(1))

    done, _ = lax.while_loop(more, body, state)

    @pl.when((done > 1) & (done < nblk))
    def _():
        wait(b, nblk - 1 - done, slot_of(done))

    for h in range(SB_HEADS):
        o_ref[:, h, :] = acc_ref[h]


def _sample_attn(qs, kn, vn, cache_k, cache_v, layer):
    db, n = qs.shape[:2]
    past = cache_k.shape[-1]
    assert n <= LANES and past >= CACHE_BLOCK and past % CACHE_BLOCK == 0, (n, past)
    m = SB_HEADS * n
    new_spec = pl.BlockSpec((None, n, SB_WIDTH), lambda b: (b, 0, 0))
    hbm = pl.BlockSpec(memory_space=pl.ANY)
    buf = pltpu.VMEM((4, SB_HEADS, SB_HEAD_DIM, CACHE_BLOCK), F32)
    return pl.pallas_call(
        functools.partial(_sample_attn_body, layer=layer),
        grid=(db,),
        in_specs=[new_spec, new_spec, new_spec, hbm, hbm],
        out_specs=pl.BlockSpec((None, n, SB_HEADS, SB_HEAD_DIM), lambda b: (b, 0, 0, 0)),
        out_shape=jax.ShapeDtypeStruct((db, n, SB_HEADS, SB_HEAD_DIM), F32),
        scratch_shapes=[buf, buf, pltpu.SemaphoreType.DMA((4, 2)),
                        pltpu.VMEM((m, 1), F32),
                        pltpu.VMEM((SB_HEADS, n, SB_HEAD_DIM), F32)],
        compiler_params=pltpu.CompilerParams(
            dimension_semantics=("arbitrary",), vmem_limit_bytes=VMEM_LIMIT),
        name="sample_attn",
    )(qs, kn, vn, cache_k, cache_v)


def _route(logits):
    lane = lax.broadcasted_iota(jnp.int32, logits.shape, 1).astype(F32)
    neg = jnp.float32(-jnp.inf)
    big = jnp.float32(1 << 20)
    rmax = lambda a: jnp.max(a, axis=-1, keepdims=True)
    rmin = lambda a: jnp.min(a, axis=-1, keepdims=True)
    rsum = lambda a: jnp.sum(a, axis=-1, keepdims=True)

    gmask = lane < N_EXPERT_GROUPS
    gl = jnp.where(gmask, logits, neg)
    gmax = rmax(gl)
    g_sel = rmin(jnp.where(gl == gmax, lane, big))
    p_sel = 1.0 / rsum(jnp.where(gmask, jnp.exp(gl - gmax), 0.0))

    e_lo = N_EXPERT_GROUPS + g_sel * EXPERTS_PER_GROUP
    emask = (lane >= e_lo) & (lane < e_lo + EXPERTS_PER_GROUP)
    el = jnp.where(emask, logits, neg)
    emax = rmax(el)
    ex = jnp.where(emask, jnp.exp(el - emax), 0.0)
    p_e = ex / rsum(ex)
    pm = jnp.where(emask, p_e, -1.0)
    v1 = rmax(pm)
    i1 = rmin(jnp.where(pm == v1, lane, big))
    pm2 = jnp.where(lane == i1, -1.0, pm)
    v2 = rmax(pm2)
    i2 = rmin(jnp.where(pm2 == v2, lane, big))
    tot = v1 + v2
    gate1 = p_sel * v1 / tot
    gate2 = p_sel * v2 / tot
    comb = jnp.where(lane == i1, gate1, 0.0) + jnp.where(lane == i2, gate2, 0.0)
    gates = jnp.where(lane == i1 - e_lo, gate1, 0.0) + jnp.where(lane == i2 - e_lo, gate2, 0.0)
    return comb, g_sel, gates


def _rmsnorm_rows(h, g):
    return h * lax.rsqrt(jnp.mean(h * h, axis=-1, keepdims=True) + EPS) * g


def _slot_matrix(slot):
    lane = lax.broadcasted_iota(jnp.int32, (slot.shape[0], MOE_SLOTS), 1)
    return jnp.where(lane == slot.astype(jnp.int32), 1.0, 0.0)


def _post_attn_body(x_ref, gm_ref, sb_ref, wo_ref, g2_ref, wr_ref, br_ref,
                    h_ref, route_ref, bucket_ref, gate_ref, cnt_ref):
    h = (x_ref[...]
         + jnp.dot(gm_ref[...], wo_ref[:GM_WIDTH, :], preferred_element_type=F32)
         + jnp.dot(sb_ref[...], wo_ref[GM_WIDTH:, :], preferred_element_type=F32))
    h_ref[...] = h
    hn = _rmsnorm_rows(h, g2_ref[...])
    hi = hn.astype(BF16)
    lo = (hn - hi.astype(F32)).astype(BF16)
    both = jnp.dot(hi, wr_ref[...], preferred_element_type=F32)
    logits = (both[:, :LANES] + jnp.dot(lo, wr_ref[:, :LANES], preferred_element_type=F32)
              + both[:, LANES:]) + br_ref[...]
    comb, g_sel, gates = _route(logits)

    subs = [slice(st * MOE_SUB, (st + 1) * MOE_SUB) for st in range(h.shape[0] // MOE_SUB)]
    lane = lax.broadcasted_iota(jnp.int32, comb.shape, 1).astype(F32)
    r = lax.broadcasted_iota(jnp.int32, (MOE_SUB, MOE_SUB), 0)
    c = lax.broadcasted_iota(jnp.int32, (MOE_SUB, MOE_SUB), 1)
    earlier = jnp.where(r > c, 1.0, 0.0).astype(BF16)
    onehot = jnp.where(lane == g_sel, 1.0, 0.0)
    onehot_b = onehot.astype(BF16)
    before = jnp.concatenate(
        [jnp.dot(earlier, onehot_b[rows], preferred_element_type=F32) for rows in subs], axis=0)
    rank = jnp.sum(onehot * before, axis=-1, keepdims=True)
    slot = jnp.where(rank < MOE_CAP, g_sel * MOE_CAP + rank, -1.0)
    route_ref[...] = jnp.where(lane == SLOT_LANE, slot, comb)
    g1 = gates.astype(BF16)
    g2 = (gates - g1.astype(F32)).astype(BF16)
    g3 = (gates - g1.astype(F32) - g2.astype(F32)).astype(BF16)
    g12 = jnp.concatenate([g1, g2], axis=1)
    for st, rows in enumerate(subs):
        place = _slot_matrix(slot[rows]).T.astype(BF16)
        bucket_ref[st] = jnp.dot(place, hi[rows], preferred_element_type=F32).astype(BF16)
        placed = jnp.dot(place, g12[rows], preferred_element_type=F32)
        gate_ref[st] = (placed[:, :LANES] + placed[:, LANES:]
                        + jnp.dot(place, g3[rows], preferred_element_type=F32))
        cnt_ref[st] = jnp.broadcast_to(jnp.sum(onehot[rows], axis=0, keepdims=True), cnt_ref.shape[1:])


def _moe_ffn_body(long_ref, x_ref, gate_ref, xs_ref, gates_ref, wg_ref, wu_ref, wd_ref, y_ref, ys_ref,
                  wgb_ref, wub_ref, wdb_ref, *, prompt_steps):
    g, s = pl.program_id(0), pl.program_id(1)

    @pl.when(s == 0)
    def _():
        wgb_ref[...] = wg_ref[...].astype(BF16)
        wub_ref[...] = wu_ref[...].astype(BF16)
        wdb_ref[...] = wd_ref[...].astype(BF16)

    def ffn(ins, outs, used):
        x = jnp.concatenate([r[:, 0, :used, :].reshape(-1, D_MODEL) for r, _ in ins], axis=0)
        gates = jnp.concatenate([r[:, 0, :used, :].reshape(-1, LANES) for _, r in ins], axis=0)
        parts = []
        for e in range(EXPERTS_PER_GROUP):
            hg = jnp.dot(x, wgb_ref[e], preferred_element_type=F32)
            hu = jnp.dot(x, wub_ref[e], preferred_element_type=F32)
            parts.append((hg / (1.0 + jnp.exp(-hg)) * hu * gates[:, e:e + 1]).astype(BF16))
        a = jnp.concatenate(parts, axis=-1)
        y = jnp.dot(a, wdb_ref[...].reshape(GROUP_FF, D_MODEL), preferred_element_type=F32)
        row = 0
        for out_ref in outs:
            rows = out_ref.shape[0] * used
            out_ref[:, 0, :used, :] = y[row:row + rows].reshape(out_ref.shape[0], used, D_MODEL)
            if used < MOE_CAP:
                out_ref[:, 0, used:, :] = jnp.zeros((out_ref.shape[0], MOE_CAP - used, D_MODEL), F32)
            row += rows

    def step(used):
        @pl.when(s < prompt_steps - 1)
        def _():
            ffn([(x_ref, gate_ref)], [y_ref], used)

        @pl.when(s == prompt_steps - 1)
        def _():
            ffn([(x_ref, gate_ref), (xs_ref, gates_ref)], [y_ref, ys_ref], used)

    @pl.when(long_ref[g, s] == 0)
    def _():
        step(MOE_SHORT)

    @pl.when(long_ref[g, s] != 0)
    def _():
        step(MOE_CAP)


def _moe_combine_body(h_ref, route_ref, ys_hbm, o_ref, ring, sems):
    i, steps = pl.program_id(0), pl.num_programs(0)
    sub = h_ref.shape[0] // MOE_SUB

    def fetch(step):
        slot = step % COMBINE_RING
        return pltpu.make_async_copy(ys_hbm.at[pl.ds(step * sub, sub)], ring.at[slot], sems.at[slot])

    @pl.when(i == 0)
    def _():
        for ahead in range(COMBINE_AHEAD):
            @pl.when(ahead < steps)
            def _():
                fetch(ahead).start()

    @pl.when(i + COMBINE_AHEAD < steps)
    def _():
        fetch(i + COMBINE_AHEAD).start()

    fetch(i).wait()
    ys_ref = ring.at[i % COMBINE_RING]
    for st in range(sub):
        rows = slice(st * MOE_SUB, (st + 1) * MOE_SUB)
        pick = _slot_matrix(route_ref[rows, SLOT_LANE:SLOT_LANE + 1]).astype(BF16)
        ys = ys_ref[st]
        hi = ys.astype(BF16)
        lo = (ys - hi.astype(F32)).astype(BF16)
        o_ref[rows, :] = (h_ref[rows, :] + jnp.dot(pick, hi, preferred_element_type=F32)
                          + jnp.dot(pick, lo, preferred_element_type=F32))


def _moe_dense_body(h_ref, route_ref, g2_ref, wg_ref, wu_ref, wd_ref, o_ref, hn_ref):
    g = pl.program_id(1)

    @pl.when(g == 0)
    def _():
        h = h_ref[...]
        o_ref[...] = h
        hn_ref[...] = _rmsnorm_rows(h, g2_ref[...]).astype(BF16)

    hn = hn_ref[...]
    comb = route_ref[...]
    lane = lax.broadcasted_iota(jnp.int32, comb.shape, 1)
    parts = []
    for e in range(EXPERTS_PER_GROUP):
        gate = jnp.sum(jnp.where(lane == N_EXPERT_GROUPS + g * EXPERTS_PER_GROUP + e, comb, 0.0),
                       axis=-1, keepdims=True)
        hg = jnp.dot(hn, wg_ref[e], preferred_element_type=F32)
        hu = jnp.dot(hn, wu_ref[e], preferred_element_type=F32)
        parts.append((hg / (1.0 + jnp.exp(-hg)) * hu * gate).astype(BF16))
    a = jnp.concatenate(parts, axis=-1)
    o_ref[...] += jnp.dot(a, wd_ref[...].reshape(GROUP_FF, D_MODEL), preferred_element_type=F32)


def _group_spec(a, index):
    return pl.BlockSpec((EXPERTS_PER_GROUP,) + a.shape[1:], index)


def _post_attn(x2d, gm, sb, params, tm):
    n = x2d.shape[0]
    wo, g2, wr, br = params
    assert n % tm == 0 and tm % MOE_SUB == 0, (n, tm)
    n_sub, sub = n // MOE_SUB, tm // MOE_SUB
    row = lambda w: pl.BlockSpec((tm, w), lambda i: (i, 0))
    full = lambda a: pl.BlockSpec(a.shape, lambda i: (0,) * a.ndim)
    per_sub = lambda rows, w: pl.BlockSpec((sub, rows, w), lambda i: (i, 0, 0))
    return pl.pallas_call(
        _post_attn_body,
        grid=(n // tm,),
        in_specs=[row(D_MODEL), row(GM_WIDTH), row(SB_WIDTH),
                  full(wo), full(g2), full(wr), full(br)],
        out_specs=[row(D_MODEL), row(LANES), per_sub(MOE_SLOTS, D_MODEL),
                   per_sub(MOE_SLOTS, LANES), per_sub(SUBLANES, LANES)],
        out_shape=[jax.ShapeDtypeStruct((n, D_MODEL), F32),
                   jax.ShapeDtypeStruct((n, LANES), F32),
                   jax.ShapeDtypeStruct((n_sub, MOE_SLOTS, D_MODEL), BF16),
                   jax.ShapeDtypeStruct((n_sub, MOE_SLOTS, LANES), F32),
                   jax.ShapeDtypeStruct((n_sub, SUBLANES, LANES), F32)],
        compiler_params=pltpu.CompilerParams(
            dimension_semantics=("parallel",), vmem_limit_bytes=VMEM_LIMIT),
        name="post_attn",
    )(x2d, gm, sb, wo, g2, wr, br)


def _out_moe(prompt, sample, params, *, tm_prompt, tm_sample):
    wo, g2, wr, br, wg, wu, wd = params
    routed = [_post_attn(*grp, (wo, g2, wr, br), tm) for grp, tm in
              ((prompt, tm_prompt), (sample, tm_sample))]
    tms = (tm_prompt, tm_sample)
    params_2d = pltpu.CompilerParams(
        dimension_semantics=("parallel", "arbitrary"), vmem_limit_bytes=VMEM_LIMIT)

    def sparse():
        (_, _, bk_p, gt_p, cnt_p), (_, _, bk_s, gt_s, cnt_s) = routed
        n_sub, n_sub_s = bk_p.shape[0], bk_s.shape[0]
        sup = min(MOE_SUPER, n_sub)
        assert n_sub % sup == 0, (n_sub, sup)
        steps = n_sub // sup
        fullest = jnp.max(cnt_p[:, 0, :N_EXPERT_GROUPS].reshape(steps, sup, N_EXPERT_GROUPS), axis=1)
        fullest = fullest.at[steps - 1].max(jnp.max(cnt_s[:, 0, :N_EXPERT_GROUPS], axis=0))
        long_steps = (fullest.T > MOE_SHORT).astype(jnp.int32)
        by_group = lambda a: a.reshape(a.shape[0], N_EXPERT_GROUPS, MOE_CAP, a.shape[-1])
        bucket = lambda w: pl.BlockSpec((sup, 1, MOE_CAP, w), lambda g, s, flags: (s, g, 0, 0))
        bucket_s = lambda w: pl.BlockSpec((n_sub_s, 1, MOE_CAP, w), lambda g, s, flags: (0, g, 0, 0))
        ys = pl.pallas_call(
            functools.partial(_moe_ffn_body, prompt_steps=steps),
            grid_spec=pltpu.PrefetchScalarGridSpec(
                num_scalar_prefetch=1,
                grid=(N_EXPERT_GROUPS, steps),
                in_specs=[bucket(D_MODEL), bucket(LANES), bucket_s(D_MODEL), bucket_s(LANES)]
                         + [_group_spec(w, lambda g, s, flags: (g, 0, 0)) for w in (wg, wu, wd)],
                out_specs=[bucket(D_MODEL), bucket_s(D_MODEL)],
                scratch_shapes=[pltpu.VMEM((EXPERTS_PER_GROUP,) + w.shape[1:], BF16)
                                for w in (wg, wu, wd)]),
            out_shape=[jax.ShapeDtypeStruct((n, N_EXPERT_GROUPS, MOE_CAP, D_MODEL), F32)
                       for n in (n_sub, n_sub_s)],
            compiler_params=params_2d,
            name="moe_ffn",
        )(long_steps, by_group(bk_p), by_group(gt_p), by_group(bk_s), by_group(gt_s), wg, wu, wd)

        def combine(h, route, y, tm):
            n, sub = h.shape[0], tm // MOE_SUB
            row = lambda w: pl.BlockSpec((tm, w), lambda i: (i, 0))
            return pl.pallas_call(
                _moe_combine_body,
                grid=(n // tm,),
                in_specs=[row(D_MODEL), row(LANES), pl.BlockSpec(memory_space=pl.ANY)],
                out_specs=row(D_MODEL),
                out_shape=jax.ShapeDtypeStruct((n, D_MODEL), F32),
                scratch_shapes=[pltpu.VMEM((COMBINE_RING, sub, MOE_SLOTS, D_MODEL), F32),
                                pltpu.SemaphoreType.DMA((COMBINE_RING,))],
                compiler_params=pltpu.CompilerParams(
                    dimension_semantics=("arbitrary",), vmem_limit_bytes=VMEM_LIMIT),
                name="moe_combine",
            )(h, route, y.reshape(-1, MOE_SLOTS, D_MODEL))

        return tuple(combine(r[0], r[1], y, tm) for r, y, tm in zip(routed, ys, tms))

    def dense():
        wgb, wub, wdb = (w.astype(BF16) for w in (wg, wu, wd))

        def all_experts(h, route, tm):
            row2 = lambda w: pl.BlockSpec((tm, w), lambda i, g: (i, 0))
            return pl.pallas_call(
                _moe_dense_body,
                grid=(h.shape[0] // tm, N_EXPERT_GROUPS),
                in_specs=[row2(D_MODEL), row2(LANES), pl.BlockSpec(g2.shape, lambda i, g: (0, 0))]
                         + [_group_spec(w, lambda i, g: (g, 0, 0)) for w in (wg, wu, wd)],
                out_specs=row2(D_MODEL),
                out_shape=jax.ShapeDtypeStruct(h.shape, F32),
                scratch_shapes=[pltpu.VMEM((tm, D_MODEL), BF16)],
                compiler_params=params_2d,
                name="moe_dense",
            )(h, route, g2, wgb, wub, wdb)

        return tuple(all_experts(r[0], r[1], tm) for r, tm in zip(routed, tms))

    most = jnp.maximum(jnp.max(routed[0][4]), jnp.max(routed[1][4]))
    return lax.cond(most <= MOE_CAP, sparse, dense)


def _layer(layer, xp, xs, cache_k, cache_v, norm1_g, w_in, gm_v_norm_g, gm_w_s, gm_b_s, q_norm_g,
           k_norm_g, w_out, norm2_g, w_router_group, b_router_group, w_router_expert,
           b_router_expert, w_gate, w_up, w_down):
    b, s, d = xp.shape
    db, n, _ = xs.shape
    head_of = jnp.arange(HEAD_TILE) // SB_HEAD_DIM
    bd = jnp.where(head_of[:, None] == head_of[None, :], 1.0 / SB_HEAD_DIM, 0.0).astype(BF16)
    in_params = (norm1_g[None, :], w_in.astype(BF16), gm_v_norm_g[None, :],
                 jnp.tile(q_norm_g, SB_HEADS)[None, :], jnp.tile(k_norm_g, SB_HEADS)[None, :],
                 gm_w_s, gm_b_s.T, bd)

    wr = jnp.concatenate(
        [w_router_group, jnp.transpose(w_router_expert, (1, 0, 2)).reshape(d, N_EXPERTS)], axis=1)
    wr = jnp.pad(wr, ((0, 0), (0, LANES - wr.shape[1])))
    wr_hi = wr.astype(BF16)
    wr_lo = (wr - wr_hi.astype(F32)).astype(BF16)
    br = jnp.pad(jnp.concatenate([b_router_group, b_router_expert.reshape(-1)]),
                 (0, LANES - N_EXPERT_GROUPS - N_EXPERTS))[None, :]

    moe_params = (w_out.astype(BF16), norm2_g[None, :], jnp.concatenate([wr_hi, wr_lo], axis=1), br,
                  w_gate, w_up, w_down)

    xp2 = xp.reshape(b * s, d)
    gm, kf, vf, qs, kb, vb = _mixer_in(xp2, in_params, tm=min(1024, s), chunk=GM_CHUNK, stream=False,
                                       seq=s)
    shp = (b, s, SB_WIDTH)
    sb = _prompt_attn(qs.reshape(shp), kb, vb.reshape(shp)).reshape(b * s, SB_WIDTH)

    xs2 = xs.reshape(db * n, d)
    gm_s, kf_s, vf_s, q_s, gv_s = _mixer_in(xs2, in_params, tm=db * n, chunk=n, stream=True)
    shs = (db, n, SB_WIDTH)
    frame_minor = lambda c: jnp.transpose(c, (0, 1, 3, 4, 2))
    sb_s = _sample_attn(q_s.reshape(shs), kf_s.reshape(shs), vf_s.reshape(shs),
                        frame_minor(cache_k), frame_minor(cache_v), layer)
    sb_s = sb_s.reshape(db * n, SB_WIDTH).astype(BF16)

    yp, ys = _out_moe((xp2, gm, sb), (xs2, gm_s, sb_s), moe_params,
                      tm_prompt=min(1024, b * s), tm_sample=db * n)
    yp, ys = yp.reshape(b, s, d), ys.reshape(db, n, d)

    heads = (SB_HEADS, SB_HEAD_DIM)
    rows = lambda a: jnp.transpose(a.reshape(b, *heads, s), (0, 3, 1, 2))
    return (yp, ys, rows(kf), rows(vf),
            kf_s.reshape(db, n, *heads), vf_s.reshape(db, n, *heads),
            gv_s.reshape(db, n, GM_GROUPS, GM_GROUP_DIM))


def kernel(x_prompt, x_sample, cache_sb_k, cache_sb_v, norm1_g, w_in, gm_v_norm_g, gm_w_s, gm_b_s, q_norm_g, k_norm_g, w_out, norm2_g, w_router_group, b_router_group, w_router_expert, b_router_expert, w_gate, w_up, w_down):
    depth = w_in.shape[0]
    yp, ys = x_prompt, x_sample
    outs = [[] for _ in range(5)]
    for l in range(depth):
        yp, ys, *rest = _layer(
            l, yp, ys, cache_sb_k, cache_sb_v, norm1_g[l], w_in[l], gm_v_norm_g[l], gm_w_s[l],
            gm_b_s[l], q_norm_g[l], k_norm_g[l], w_out[l], norm2_g[l], w_router_group[l],
            b_router_group[l], w_router_expert[l], b_router_expert[l], w_gate[l], w_up[l],
            w_down[l])
        for acc, r in zip(outs, rest):
            acc.append(r)
    return (yp, ys) + tuple(jnp.stack(o, axis=0) for o in outs)
```
